```python
import jax, jax.numpy as jnp
from jax import lax
import numpy as np

D_MODEL = 4096
BATCH = 16
SEQ = 2048
DEPTH = 1

D_MIX = D_MODEL
D_CONV = D_MIX // 2
D_SGU = D_MIX - D_CONV
CONV_GROUPS = 16
SGU_HEADS = 16
SGU_HEAD_DIM = D_SGU // SGU_HEADS
CHUNK = 128
CONV_WIDTH = 31
D_FF = 11008
FFN_CONV_WIDTH = 3
N_MOD = 6
EPS = 1e-6

kernel_name = "hybrid_conformer_conv_sgu_adaln_block"


def rms_norm(x, g):
    xf = x.astype(jnp.float32)
    y = xf * lax.rsqrt(jnp.mean(xf * xf, axis=-1, keepdims=True) + EPS)
    return (y * g.astype(jnp.float32)).astype(x.dtype)


def layer_norm(x, g, b):
    xf = x.astype(jnp.float32)
    mu = jnp.mean(xf, axis=-1, keepdims=True)
    xc = xf - mu
    var = jnp.mean(xc * xc, axis=-1, keepdims=True)
    y = xc * lax.rsqrt(var + EPS) * g.astype(jnp.float32) + b.astype(jnp.float32)
    return y.astype(x.dtype)


def depthwise_conv_centred(x, w, b):
    k = w.shape[0]
    pad = (k - 1) // 2
    y = lax.conv_general_dilated(
        x, w[:, None, :].astype(x.dtype), window_strides=(1,), padding=[(pad, pad)],
        dimension_numbers=("NWC", "WIO", "NWC"), feature_group_count=x.shape[-1])
    return y + b


def modulate(x, g, shift, scale):
    return rms_norm(x, g) * (1 + scale[:, None, :]) + shift[:, None, :]


def _fwd_setup_inputs(seed: int = 0) -> dict:
    key = jax.random.key(seed)
    ks = jax.random.split(key, 24)
    L = DEPTH

    def nrm(k, shape, std):
        return std * jax.random.normal(k, shape, jnp.float32)

    return {
        "x": nrm(ks[0], (BATCH, SEQ, D_MODEL), 1.0),
        "c": nrm(ks[1], (BATCH, D_MODEL), 1.0),
        "w_ada": nrm(ks[2], (L, D_MODEL, N_MOD * D_MODEL), 0.5 * D_MODEL ** -0.5),
        "b_ada": nrm(ks[3], (L, N_MOD * D_MODEL), 0.02),
        "g_mix": 1.0 + nrm(ks[4], (L, D_MODEL), 0.02),
        "w_in": nrm(ks[5], (L, D_MODEL, 2 * D_MIX), D_MODEL ** -0.5),
        "conv_w": nrm(ks[6], (L, CONV_WIDTH, D_CONV), CONV_WIDTH ** -0.5),
        "conv_b": nrm(ks[7], (L, D_CONV), 0.02),
        "conv_ln_g": 1.0 + nrm(ks[8], (L, D_CONV), 0.02),
        "conv_ln_b": nrm(ks[9], (L, D_CONV), 0.02),
        "sgu_ln_g": 1.0 + nrm(ks[10], (L, D_SGU), 0.02),
        "sgu_ln_b": nrm(ks[11], (L, D_SGU), 0.02),
        "sgu_w": nrm(ks[12], (L, SGU_HEADS, CHUNK, CHUNK), CHUNK ** -0.5),
        "sgu_b": 1.0 + nrm(ks[13], (L, SGU_HEADS, CHUNK), 0.02),
        "out_g_conv": 1.0 + nrm(ks[14], (L, D_CONV), 0.02),
        "out_g_sgu": 1.0 + nrm(ks[15], (L, D_SGU), 0.02),
        "w_out": nrm(ks[16], (L, D_MIX, D_MODEL), D_MIX ** -0.5),
        "g_ffn": 1.0 + nrm(ks[17], (L, D_MODEL), 0.02),
        "w_up": nrm(ks[18], (L, D_MODEL, 2 * D_FF), D_MODEL ** -0.5),
        "ffn_conv_w": nrm(ks[19], (L, FFN_CONV_WIDTH, D_FF), FFN_CONV_WIDTH ** -0.5),
        "ffn_conv_b": nrm(ks[20], (L, D_FF), 0.02),
        "w_down": nrm(ks[21], (L, D_FF, D_MODEL), D_FF ** -0.5),
        "g_final": 1.0 + nrm(ks[22], (D_MODEL,), 0.02),
    }


def _fwd_reference(x, c, w_ada, b_ada, g_mix, w_in, conv_w, conv_b, conv_ln_g, conv_ln_b,
              sgu_ln_g, sgu_ln_b, sgu_w, sgu_b, out_g_conv, out_g_sgu, w_out,
              g_ffn, w_up, ffn_conv_w, ffn_conv_b, w_down, g_final):
    bsz, seq, _ = x.shape
    n_chunks = seq // CHUNK
    c_act = jax.nn.silu(c)

    for l in range(DEPTH):
        mod = (c_act @ w_ada[l] + b_ada[l]).reshape(bsz, N_MOD, D_MODEL)
        shift_m, scale_m, gate_m = mod[:, 0], mod[:, 1], mod[:, 2]
        shift_f, scale_f, gate_f = mod[:, 3], mod[:, 4], mod[:, 5]

        h = modulate(x, g_mix[l], shift_m, scale_m)
        proj = h @ w_in[l]
        p_conv = proj[..., :2 * D_CONV]
        p_sgu = proj[..., 2 * D_CONV:]

        a = p_conv[..., :D_CONV] * jax.nn.sigmoid(p_conv[..., D_CONV:])
        a = depthwise_conv_centred(a, conv_w[l], conv_b[l])
        a = jax.nn.silu(layer_norm(a, conv_ln_g[l], conv_ln_b[l]))

        z = jax.nn.gelu(p_sgu, approximate=False)
        u, v = z[..., :D_SGU], z[..., D_SGU:]
        v = layer_norm(v, sgu_ln_g[l], sgu_ln_b[l])
        v = v.reshape(bsz, n_chunks, CHUNK, SGU_HEADS, SGU_HEAD_DIM)
        v = jnp.einsum("hpq,bcqhd->bcphd", sgu_w[l], v) + sgu_b[l].T[:, :, None]
        bgrp = u * v.reshape(bsz, seq, D_SGU)

        y = jnp.concatenate([rms_norm(a, out_g_conv[l]), rms_norm(bgrp, out_g_sgu[l])], axis=-1)
        x = x + gate_m[:, None, :] * (y @ w_out[l])

        h = modulate(x, g_ffn[l], shift_f, scale_f)
        up = h @ w_up[l]
        gte = depthwise_conv_centred(up[..., :D_FF], ffn_conv_w[l], ffn_conv_b[l])
        act = jax.nn.silu(gte) * up[..., D_FF:]
        x = x + gate_f[:, None, :] * (act @ w_down[l])

    return rms_norm(x, g_final)


import jax as _jax
import jax.numpy as _jnp

TWIN_FORMAT = 'train_step'
FWD_PARAMS = ['x', 'c', 'w_ada', 'b_ada', 'g_mix', 'w_in', 'conv_w', 'conv_b', 'conv_ln_g', 'conv_ln_b', 'sgu_ln_g', 'sgu_ln_b', 'sgu_w', 'sgu_b', 'out_g_conv', 'out_g_sgu', 'w_out', 'g_ffn', 'w_up', 'ffn_conv_w', 'ffn_conv_b', 'w_down', 'g_final']
TWIN_WEIGHTS = ['w_ada', 'b_ada', 'g_mix', 'w_in', 'conv_w', 'conv_b', 'conv_ln_g', 'conv_ln_b', 'sgu_ln_g', 'sgu_ln_b', 'sgu_w', 'sgu_b', 'out_g_conv', 'out_g_sgu', 'w_out', 'g_ffn', 'w_up', 'ffn_conv_w', 'ffn_conv_b', 'w_down', 'g_final']
TWIN_DIFF_INPUT = 'x'
TWIN_INPUTS = ['x', 'c', 'w_ada', 'b_ada', 'g_mix', 'w_in', 'conv_w', 'conv_b', 'conv_ln_g', 'conv_ln_b', 'sgu_ln_g', 'sgu_ln_b', 'sgu_w', 'sgu_b', 'out_g_conv', 'out_g_sgu', 'w_out', 'g_ffn', 'w_up', 'ffn_conv_w', 'ffn_conv_b', 'w_down', 'g_final', 'loss_target', 'm_w_ada', 'm_b_ada', 'm_g_mix', 'm_w_in', 'm_conv_w', 'm_conv_b', 'm_conv_ln_g', 'm_conv_ln_b', 'm_sgu_ln_g', 'm_sgu_ln_b', 'm_sgu_w', 'm_sgu_b', 'm_out_g_conv', 'm_out_g_sgu', 'm_w_out', 'm_g_ffn', 'm_w_up', 'm_ffn_conv_w', 'm_ffn_conv_b', 'm_w_down', 'm_g_final', 'v_w_ada', 'v_b_ada', 'v_g_mix', 'v_w_in', 'v_conv_w', 'v_conv_b', 'v_conv_ln_g', 'v_conv_ln_b', 'v_sgu_ln_g', 'v_sgu_ln_b', 'v_sgu_w', 'v_sgu_b', 'v_out_g_conv', 'v_out_g_sgu', 'v_w_out', 'v_g_ffn', 'v_w_up', 'v_ffn_conv_w', 'v_ffn_conv_b', 'v_w_down', 'v_g_final']
TWIN_OUTPUTS = ['loss', 'grad_x', 'grad_w_ada', 'grad_b_ada', 'grad_g_mix', 'grad_w_in', 'grad_conv_w', 'grad_conv_b', 'grad_conv_ln_g', 'grad_conv_ln_b', 'grad_sgu_ln_g', 'grad_sgu_ln_b', 'grad_sgu_w', 'grad_sgu_b', 'grad_out_g_conv', 'grad_out_g_sgu', 'grad_w_out', 'grad_g_ffn', 'grad_w_up', 'grad_ffn_conv_w', 'grad_ffn_conv_b', 'grad_w_down', 'grad_g_final', 'delta_w_ada', 'delta_b_ada', 'delta_g_mix', 'delta_w_in', 'delta_conv_w', 'delta_conv_b', 'delta_conv_ln_g', 'delta_conv_ln_b', 'delta_sgu_ln_g', 'delta_sgu_ln_b', 'delta_sgu_w', 'delta_sgu_b', 'delta_out_g_conv', 'delta_out_g_sgu', 'delta_w_out', 'delta_g_ffn', 'delta_w_up', 'delta_ffn_conv_w', 'delta_ffn_conv_b', 'delta_w_down', 'delta_g_final', 'new_m_w_ada', 'new_m_b_ada', 'new_m_g_mix', 'new_m_w_in', 'new_m_conv_w', 'new_m_conv_b', 'new_m_conv_ln_g', 'new_m_conv_ln_b', 'new_m_sgu_ln_g', 'new_m_sgu_ln_b', 'new_m_sgu_w', 'new_m_sgu_b', 'new_m_out_g_conv', 'new_m_out_g_sgu', 'new_m_w_out', 'new_m_g_ffn', 'new_m_w_up', 'new_m_ffn_conv_w', 'new_m_ffn_conv_b', 'new_m_w_down', 'new_m_g_final', 'new_v_w_ada', 'new_v_b_ada', 'new_v_g_mix', 'new_v_w_in', 'new_v_conv_w', 'new_v_conv_b', 'new_v_conv_ln_g', 'new_v_conv_ln_b', 'new_v_sgu_ln_g', 'new_v_sgu_ln_b', 'new_v_sgu_w', 'new_v_sgu_b', 'new_v_out_g_conv', 'new_v_out_g_sgu', 'new_v_w_out', 'new_v_g_ffn', 'new_v_w_up', 'new_v_ffn_conv_w', 'new_v_ffn_conv_b', 'new_v_w_down', 'new_v_g_final']
TWIN_LEAF_KINDS = {'loss': 'loss', 'grad_x': 'grad_x', 'grad_w_ada': 'grad_w', 'grad_b_ada': 'grad_w', 'grad_g_mix': 'grad_w', 'grad_w_in': 'grad_w', 'grad_conv_w': 'grad_w', 'grad_conv_b': 'grad_w', 'grad_conv_ln_g': 'grad_w', 'grad_conv_ln_b': 'grad_w', 'grad_sgu_ln_g': 'grad_w', 'grad_sgu_ln_b': 'grad_w', 'grad_sgu_w': 'grad_w', 'grad_sgu_b': 'grad_w', 'grad_out_g_conv': 'grad_w', 'grad_out_g_sgu': 'grad_w', 'grad_w_out': 'grad_w', 'grad_g_ffn': 'grad_w', 'grad_w_up': 'grad_w', 'grad_ffn_conv_w': 'grad_w', 'grad_ffn_conv_b': 'grad_w', 'grad_w_down': 'grad_w', 'grad_g_final': 'grad_w', 'delta_w_ada': 'delta_w', 'delta_b_ada': 'delta_w', 'delta_g_mix': 'delta_w', 'delta_w_in': 'delta_w', 'delta_conv_w': 'delta_w', 'delta_conv_b': 'delta_w', 'delta_conv_ln_g': 'delta_w', 'delta_conv_ln_b': 'delta_w', 'delta_sgu_ln_g': 'delta_w', 'delta_sgu_ln_b': 'delta_w', 'delta_sgu_w': 'delta_w', 'delta_sgu_b': 'delta_w', 'delta_out_g_conv': 'delta_w', 'delta_out_g_sgu': 'delta_w', 'delta_w_out': 'delta_w', 'delta_g_ffn': 'delta_w', 'delta_w_up': 'delta_w', 'delta_ffn_conv_w': 'delta_w', 'delta_ffn_conv_b': 'delta_w', 'delta_w_down': 'delta_w', 'delta_g_final': 'delta_w', 'new_m_w_ada': 'new_m', 'new_m_b_ada': 'new_m', 'new_m_g_mix': 'new_m', 'new_m_w_in': 'new_m', 'new_m_conv_w': 'new_m', 'new_m_conv_b': 'new_m', 'new_m_conv_ln_g': 'new_m', 'new_m_conv_ln_b': 'new_m', 'new_m_sgu_ln_g': 'new_m', 'new_m_sgu_ln_b': 'new_m', 'new_m_sgu_w': 'new_m', 'new_m_sgu_b': 'new_m', 'new_m_out_g_conv': 'new_m', 'new_m_out_g_sgu': 'new_m', 'new_m_w_out': 'new_m', 'new_m_g_ffn': 'new_m', 'new_m_w_up': 'new_m', 'new_m_ffn_conv_w': 'new_m', 'new_m_ffn_conv_b': 'new_m', 'new_m_w_down': 'new_m', 'new_m_g_final': 'new_m', 'new_v_w_ada': 'new_v', 'new_v_b_ada': 'new_v', 'new_v_g_mix': 'new_v', 'new_v_w_in': 'new_v', 'new_v_conv_w': 'new_v', 'new_v_conv_b': 'new_v', 'new_v_conv_ln_g': 'new_v', 'new_v_conv_ln_b': 'new_v', 'new_v_sgu_ln_g': 'new_v', 'new_v_sgu_ln_b': 'new_v', 'new_v_sgu_w': 'new_v', 'new_v_sgu_b': 'new_v', 'new_v_out_g_conv': 'new_v', 'new_v_out_g_sgu': 'new_v', 'new_v_w_out': 'new_v', 'new_v_g_ffn': 'new_v', 'new_v_w_up': 'new_v', 'new_v_ffn_conv_w': 'new_v', 'new_v_ffn_conv_b': 'new_v', 'new_v_w_down': 'new_v', 'new_v_g_final': 'new_v'}


def _forward(args):
    return _fwd_reference(*[args[k] for k in FWD_PARAMS])


def _output_shape():
    def fwd():
        inp = _fwd_setup_inputs(0)
        return _fwd_reference(*[inp[k] for k in FWD_PARAMS])
    out = _jax.eval_shape(fwd)
    return out.shape, out.dtype

N_MICROBATCH = 1
ADAM_LR = 0.001
ADAM_B1 = 0.9
ADAM_B2 = 0.999
ADAM_EPS = 1e-08
ADAM_WD = 0.01
ADAM_STEP = 10
PER_EXAMPLE_BATCH_AXIS = {'x': 0, 'c': 0, 'loss_target': 0}
SHARED_INPUTS = []
_WEIGHT_DTYPES = {'w_ada': _jnp.float32, 'b_ada': _jnp.float32, 'g_mix': _jnp.float32, 'w_in': _jnp.float32, 'conv_w': _jnp.float32, 'conv_b': _jnp.float32, 'conv_ln_g': _jnp.float32, 'conv_ln_b': _jnp.float32, 'sgu_ln_g': _jnp.float32, 'sgu_ln_b': _jnp.float32, 'sgu_w': _jnp.float32, 'sgu_b': _jnp.float32, 'out_g_conv': _jnp.float32, 'out_g_sgu': _jnp.float32, 'w_out': _jnp.float32, 'g_ffn': _jnp.float32, 'w_up': _jnp.float32, 'ffn_conv_w': _jnp.float32, 'ffn_conv_b': _jnp.float32, 'w_down': _jnp.float32, 'g_final': _jnp.float32}
MOMENT_SCALE = {'w_ada': 1.569764e-02, 'b_ada': 2.792111e-02, 'g_mix': 1.541901e-02, 'w_in': 1.127062e-02, 'conv_w': 1.342208e-02, 'conv_b': 2.584953e-02, 'conv_ln_g': 1.685874e-02, 'conv_ln_b': 1.549322e-02, 'sgu_ln_g': 9.360412e-03, 'sgu_ln_b': 9.493907e-03, 'sgu_w': 9.173688e-03, 'sgu_b': 9.397298e-03, 'out_g_conv': 1.391784e-02, 'out_g_sgu': 1.331893e-02, 'w_out': 1.327059e-02, 'g_ffn': 1.275119e-02, 'w_up': 5.616281e-03, 'ffn_conv_w': 5.757708e-03, 'ffn_conv_b': 5.034897e-03, 'w_down': 9.055172e-03, 'g_final': 8.009941e+00}


def _to_microbatches(a, axis):
    t = _jnp.moveaxis(a, axis, 0)
    t = t.reshape((N_MICROBATCH, t.shape[0] // N_MICROBATCH) + t.shape[1:])
    return _jnp.moveaxis(t, 1, axis + 1)


def setup_inputs(seed: int = 0) -> dict:
    inp = _fwd_setup_inputs(seed)
    key = _jax.random.fold_in(_jax.random.key(seed), 7919)
    shape, _ = _output_shape()
    out = dict(inp)
    out["loss_target"] = _jax.random.normal(_jax.random.fold_in(key, 0), shape, _jnp.float32)
    for i, name in enumerate(TWIN_WEIGHTS):
        w = inp[name].astype(_jnp.float32)
        if MOMENT_SCALE is None:
            s = _jnp.sqrt(_jnp.mean(_jnp.square(w)) + 1e-30)
        else:
            s = MOMENT_SCALE[name]
        km, kv = _jax.random.split(_jax.random.fold_in(key, i + 1))
        out[name] = w
        out["m_" + name] = s * _jax.random.normal(km, w.shape, _jnp.float32)
        out["v_" + name] = (s * s) * _jax.random.uniform(kv, w.shape, _jnp.float32, 0.5, 1.5)
    if N_MICROBATCH > 1:
        for name, axis in PER_EXAMPLE_BATCH_AXIS.items():
            out[name] = _to_microbatches(out[name], axis)
    return {'x': out['x'], 'c': out['c'], 'w_ada': out['w_ada'], 'b_ada': out['b_ada'], 'g_mix': out['g_mix'], 'w_in': out['w_in'], 'conv_w': out['conv_w'], 'conv_b': out['conv_b'], 'conv_ln_g': out['conv_ln_g'], 'conv_ln_b': out['conv_ln_b'], 'sgu_ln_g': out['sgu_ln_g'], 'sgu_ln_b': out['sgu_ln_b'], 'sgu_w': out['sgu_w'], 'sgu_b': out['sgu_b'], 'out_g_conv': out['out_g_conv'], 'out_g_sgu': out['out_g_sgu'], 'w_out': out['w_out'], 'g_ffn': out['g_ffn'], 'w_up': out['w_up'], 'ffn_conv_w': out['ffn_conv_w'], 'ffn_conv_b': out['ffn_conv_b'], 'w_down': out['w_down'], 'g_final': out['g_final'], 'loss_target': out['loss_target'], 'm_w_ada': out['m_w_ada'], 'm_b_ada': out['m_b_ada'], 'm_g_mix': out['m_g_mix'], 'm_w_in': out['m_w_in'], 'm_conv_w': out['m_conv_w'], 'm_conv_b': out['m_conv_b'], 'm_conv_ln_g': out['m_conv_ln_g'], 'm_conv_ln_b': out['m_conv_ln_b'], 'm_sgu_ln_g': out['m_sgu_ln_g'], 'm_sgu_ln_b': out['m_sgu_ln_b'], 'm_sgu_w': out['m_sgu_w'], 'm_sgu_b': out['m_sgu_b'], 'm_out_g_conv': out['m_out_g_conv'], 'm_out_g_sgu': out['m_out_g_sgu'], 'm_w_out': out['m_w_out'], 'm_g_ffn': out['m_g_ffn'], 'm_w_up': out['m_w_up'], 'm_ffn_conv_w': out['m_ffn_conv_w'], 'm_ffn_conv_b': out['m_ffn_conv_b'], 'm_w_down': out['m_w_down'], 'm_g_final': out['m_g_final'], 'v_w_ada': out['v_w_ada'], 'v_b_ada': out['v_b_ada'], 'v_g_mix': out['v_g_mix'], 'v_w_in': out['v_w_in'], 'v_conv_w': out['v_conv_w'], 'v_conv_b': out['v_conv_b'], 'v_conv_ln_g': out['v_conv_ln_g'], 'v_conv_ln_b': out['v_conv_ln_b'], 'v_sgu_ln_g': out['v_sgu_ln_g'], 'v_sgu_ln_b': out['v_sgu_ln_b'], 'v_sgu_w': out['v_sgu_w'], 'v_sgu_b': out['v_sgu_b'], 'v_out_g_conv': out['v_out_g_conv'], 'v_out_g_sgu': out['v_out_g_sgu'], 'v_w_out': out['v_w_out'], 'v_g_ffn': out['v_g_ffn'], 'v_w_up': out['v_w_up'], 'v_ffn_conv_w': out['v_ffn_conv_w'], 'v_ffn_conv_b': out['v_ffn_conv_b'], 'v_w_down': out['v_w_down'], 'v_g_final': out['v_g_final']}


def _loss(weights, diff, rest, loss_target):
    with _jax.named_scope("forward"):
        args = {**rest, TWIN_DIFF_INPUT: diff, **{k: w.astype(_WEIGHT_DTYPES[k]) for k, w in weights.items()}}
        y = _forward(args)
    with _jax.named_scope("loss_head"):
        err = _jnp.square(y.astype(_jnp.float32) - loss_target)
        return 0.5 * _jnp.sum(_jnp.mean(err, axis=-1)) if err.ndim else 0.5 * err


def _adamw(w, g, m, v):
    m = ADAM_B1 * m + (1.0 - ADAM_B1) * g
    v = ADAM_B2 * v + (1.0 - ADAM_B2) * _jnp.square(g)
    m_hat = m / (1.0 - ADAM_B1 ** ADAM_STEP)
    v_hat = v / (1.0 - ADAM_B2 ** ADAM_STEP)
    delta = -ADAM_LR * (m_hat / (_jnp.sqrt(v_hat) + ADAM_EPS) + ADAM_WD * w)
    return delta, m, v


def reference(x, c, w_ada, b_ada, g_mix, w_in, conv_w, conv_b, conv_ln_g, conv_ln_b, sgu_ln_g, sgu_ln_b, sgu_w, sgu_b, out_g_conv, out_g_sgu, w_out, g_ffn, w_up, ffn_conv_w, ffn_conv_b, w_down, g_final, loss_target, m_w_ada, m_b_ada, m_g_mix, m_w_in, m_conv_w, m_conv_b, m_conv_ln_g, m_conv_ln_b, m_sgu_ln_g, m_sgu_ln_b, m_sgu_w, m_sgu_b, m_out_g_conv, m_out_g_sgu, m_w_out, m_g_ffn, m_w_up, m_ffn_conv_w, m_ffn_conv_b, m_w_down, m_g_final, v_w_ada, v_b_ada, v_g_mix, v_w_in, v_conv_w, v_conv_b, v_conv_ln_g, v_conv_ln_b, v_sgu_ln_g, v_sgu_ln_b, v_sgu_w, v_sgu_b, v_out_g_conv, v_out_g_sgu, v_w_out, v_g_ffn, v_w_up, v_ffn_conv_w, v_ffn_conv_b, v_w_down, v_g_final):
    given = dict(x=x, c=c, w_ada=w_ada, b_ada=b_ada, g_mix=g_mix, w_in=w_in, conv_w=conv_w, conv_b=conv_b, conv_ln_g=conv_ln_g, conv_ln_b=conv_ln_b, sgu_ln_g=sgu_ln_g, sgu_ln_b=sgu_ln_b, sgu_w=sgu_w, sgu_b=sgu_b, out_g_conv=out_g_conv, out_g_sgu=out_g_sgu, w_out=w_out, g_ffn=g_ffn, w_up=w_up, ffn_conv_w=ffn_conv_w, ffn_conv_b=ffn_conv_b, w_down=w_down, g_final=g_final, loss_target=loss_target, m_w_ada=m_w_ada, m_b_ada=m_b_ada, m_g_mix=m_g_mix, m_w_in=m_w_in, m_conv_w=m_conv_w, m_conv_b=m_conv_b, m_conv_ln_g=m_conv_ln_g, m_conv_ln_b=m_conv_ln_b, m_sgu_ln_g=m_sgu_ln_g, m_sgu_ln_b=m_sgu_ln_b, m_sgu_w=m_sgu_w, m_sgu_b=m_sgu_b, m_out_g_conv=m_out_g_conv, m_out_g_sgu=m_out_g_sgu, m_w_out=m_w_out, m_g_ffn=m_g_ffn, m_w_up=m_w_up, m_ffn_conv_w=m_ffn_conv_w, m_ffn_conv_b=m_ffn_conv_b, m_w_down=m_w_down, m_g_final=m_g_final, v_w_ada=v_w_ada, v_b_ada=v_b_ada, v_g_mix=v_g_mix, v_w_in=v_w_in, v_conv_w=v_conv_w, v_conv_b=v_conv_b, v_conv_ln_g=v_conv_ln_g, v_conv_ln_b=v_conv_ln_b, v_sgu_ln_g=v_sgu_ln_g, v_sgu_ln_b=v_sgu_ln_b, v_sgu_w=v_sgu_w, v_sgu_b=v_sgu_b, v_out_g_conv=v_out_g_conv, v_out_g_sgu=v_out_g_sgu, v_w_out=v_w_out, v_g_ffn=v_g_ffn, v_w_up=v_w_up, v_ffn_conv_w=v_ffn_conv_w, v_ffn_conv_b=v_ffn_conv_b, v_w_down=v_w_down, v_g_final=v_g_final)
    weights = {n: given[n] for n in TWIN_WEIGHTS}
    shared = {n: given[n] for n in SHARED_INPUTS}
    per_example = {n: given[n] for n in ['x', 'c']}
    grad_fn = _jax.value_and_grad(_loss, argnums=(0, 1))

    def one_microbatch(ex, loss_target):
        ex = dict(ex)
        diff = ex.pop(TWIN_DIFF_INPUT)
        return grad_fn(weights, diff, {**shared, **ex}, loss_target)

    if N_MICROBATCH == 1:
        loss, (grad_w, grad_x) = one_microbatch(per_example, given["loss_target"])
    else:
        def body(carry, xs):
            loss_sum, grad_sum = carry
            l_k, (gw_k, gx_k) = one_microbatch(xs[0], xs[1])
            with _jax.named_scope("update"):
                return (loss_sum + l_k, _jax.tree.map(_jnp.add, grad_sum, gw_k)), gx_k

        init = (_jnp.zeros((), _jnp.float32), _jax.tree.map(_jnp.zeros_like, weights))
        (loss, grad_w), grad_x = _jax.lax.scan(body, init, (per_example, given["loss_target"]))
    with _jax.named_scope("update"):
        delta_w, new_m, new_v = {}, {}, {}
        for n in TWIN_WEIGHTS:
            delta_w[n], new_m[n], new_v[n] = _adamw(weights[n], grad_w[n], given["m_" + n], given["v_" + n])
    return (loss, grad_x, *[grad_w[n] for n in TWIN_WEIGHTS], *[delta_w[n] for n in TWIN_WEIGHTS],
            *[new_m[n] for n in TWIN_WEIGHTS], *[new_v[n] for n in TWIN_WEIGHTS])
```

```python
import functools

import jax
import jax.numpy as jnp
from jax import lax
from jax.experimental import pallas as pl
from jax.experimental.pallas import tpu as pltpu

EPS = 1e-6
N_DEV = 8
MESH_ID = pl.DeviceIdType.MESH
V7X_VMEM_BYTES = 64 * 1024 * 1024
VMEM_LIMIT = V7X_VMEM_BYTES - 8 * 1024 * 1024
LANES = 128
SUBLANES = 8
HALO = 16
BLOB_ALIGN = SUBLANES * LANES

ADAM_LR = 0.001
ADAM_B1 = 0.9
ADAM_B2 = 0.999
ADAM_EPS = 1e-08
ADAM_WD = 0.01
ADAM_STEP = 10

F32 = jnp.float32
BF16 = jnp.bfloat16


def _cp(sem=None, **kw):
    return pltpu.CompilerParams(dimension_semantics=sem, vmem_limit_bytes=VMEM_LIMIT, **kw)


def _blk(n, pref):
    return pref if n % pref == 0 else n


def _rows_tile(rows, cols, budget=3 << 19):
    best = None
    for t in range(SUBLANES, rows + 1, SUBLANES):
        if rows % t == 0 and t * cols * 4 <= budget:
            best = t
    return best if best is not None else rows


def _sigmoid(z):
    return 1.0 / (1.0 + jnp.exp(-z))


def _gelu(z):
    return z * (lax.erf(z * 0.7071067811865476) + 1.0) * 0.5


def _gelu_grad(z):
    return 0.5 * (1.0 + lax.erf(z * 0.7071067811865476)) + z * jnp.exp(-0.5 * z * z) * 0.3989422804014327


def _mean(z):
    return jnp.mean(z, axis=-1, keepdims=True)


def _colsum(z):
    return jnp.sum(z, axis=0, keepdims=True)


def _my_pos():
    return lax.axis_index("x"), lax.axis_index("y"), lax.axis_index("c")


def _allgather(name, shard, in_vmem):
    r, cdim = shard.shape

    def body(x_ref, out_ref, send_sems, recv_sems, local_sem):
        x, y, c = _my_pos()
        me, sibling = (x, y, c), (x, y, 1 - c)
        chips = [(1 - x, y), (x, 1 - y), (1 - x, 1 - y)]

        def slot(px, py, pc):
            return out_ref.at[4 * px + 2 * py + pc]

        def copy(k, block, to, src=None):
            return pltpu.make_async_remote_copy(
                src_ref=slot(*block) if src is None else src, dst_ref=slot(*block),
                send_sem=send_sems.at[k], recv_sem=recv_sems.at[k], device_id=to, device_id_type=MESH_ID)

        mine = pltpu.make_async_copy(x_ref, slot(*me), local_sem)
        mine.start()
        first = [copy(0, me, sibling, src=x_ref)]
        first += [copy(1 + j, me, (*chip, c), src=x_ref) for j, chip in enumerate(chips)]
        for cp in first:
            cp.start()
        passed = [copy(4 + j, (*chip, c), sibling) for j, chip in enumerate(chips)]
        for j, chip in enumerate(chips):
            copy(1 + j, (*chip, c), me).wait_recv()
            passed[j].start()
        copy(0, sibling, me).wait_recv()
        for j, chip in enumerate(chips):
            copy(4 + j, (*chip, 1 - c), me).wait_recv()
        for cp in first + passed:
            cp.wait_send()
        mine.wait()

    space = pltpu.VMEM if in_vmem else pl.ANY
    return pl.pallas_call(
        body, name=name, out_shape=jax.ShapeDtypeStruct((N_DEV, r, cdim), shard.dtype),
        in_specs=[pl.BlockSpec(memory_space=space)], out_specs=pl.BlockSpec(memory_space=space),
        scratch_shapes=[pltpu.SemaphoreType.DMA((7,)), pltpu.SemaphoreType.DMA((7,)), pltpu.SemaphoreType.DMA],
        compiler_params=pltpu.CompilerParams(vmem_limit_bytes=VMEM_LIMIT),
    )(shard)


def _alltoall(name, blocks):
    _, r, cdim = blocks.shape

    def body(x_ref, out_ref, send_sems, recv_sems, local_sem):
        x, y, c = _my_pos()
        me = 4 * x + 2 * y + c
        mine = pltpu.make_async_copy(x_ref.at[me], out_ref.at[me], local_sem)
        mine.start()
        copies = []
        for k in range(1, N_DEV):
            px = 1 - x if k & 4 else x
            py = 1 - y if k & 2 else y
            pc = 1 - c if k & 1 else c
            peer = 4 * px + 2 * py + pc
            copies.append(pltpu.make_async_remote_copy(
                src_ref=x_ref.at[peer], dst_ref=out_ref.at[me], send_sem=send_sems.at[k - 1],
                recv_sem=recv_sems.at[k - 1], device_id=(px, py, pc), device_id_type=MESH_ID))
        for cp in copies:
            cp.start()
        for cp in copies:
            cp.wait_recv()
        for cp in copies:
            cp.wait_send()
        mine.wait()

    return pl.pallas_call(
        body, name=name, out_shape=jax.ShapeDtypeStruct(blocks.shape, blocks.dtype),
        in_specs=[pl.BlockSpec(memory_space=pl.ANY)], out_specs=pl.BlockSpec(memory_space=pl.ANY),
        scratch_shapes=[pltpu.SemaphoreType.DMA((7,)), pltpu.SemaphoreType.DMA((7,)), pltpu.SemaphoreType.DMA],
    )(blocks)


def _pack(arrays):
    parts = []
    for a in arrays:
        flat = a.reshape(-1).astype(F32)
        pad = (-flat.shape[0]) % BLOB_ALIGN
        parts.append(jnp.pad(flat, (0, pad)) if pad else flat)
    return jnp.concatenate(parts).reshape(-1, LANES)


def _unpack(blob, shapes):
    flat = blob.reshape(-1)
    out, off = [], 0
    for shp in shapes:
        n = 1
        for s in shp:
            n *= s
        out.append(flat[off:off + n].reshape(shp))
        off += n + (-n) % BLOB_ALIGN
    return out


def _cast_bf16(name, w):
    r, cdim = w.shape
    tr = _rows_tile(r, cdim)

    def body(w_ref, o_ref):
        o_ref[...] = w_ref[...].astype(BF16)

    return pl.pallas_call(
        body, name=name, grid=(r // tr,), out_shape=jax.ShapeDtypeStruct(w.shape, BF16),
        in_specs=[pl.BlockSpec((tr, cdim), lambda i: (i, 0))], out_specs=pl.BlockSpec((tr, cdim), lambda i: (i, 0)),
        compiler_params=_cp(("parallel",)))(w)


def _adam_math(w, g, m, v):
    m = ADAM_B1 * m + (1.0 - ADAM_B1) * g
    v = ADAM_B2 * v + (1.0 - ADAM_B2) * (g * g)
    m_hat = m / (1.0 - ADAM_B1 ** ADAM_STEP)
    v_hat = v / (1.0 - ADAM_B2 ** ADAM_STEP)
    delta = -ADAM_LR * (m_hat / (jnp.sqrt(v_hat) + ADAM_EPS) + ADAM_WD * w)
    return delta, m, v


def _adamw(name, w, m, v, *, grad=None, slots=None, ca_t=None, dmod=None):
    r, cdim = w.shape
    tr = _rows_tile(r, cdim)
    row = pl.BlockSpec((tr, cdim), lambda i: (i, 0))
    if grad is not None:
        srcs, src_specs = [grad], [row]
    elif slots is not None:
        srcs, src_specs = [slots], [pl.BlockSpec((N_DEV, tr, cdim), lambda i: (0, i, 0))]
    else:
        srcs = [ca_t, dmod]
        src_specs = [pl.BlockSpec((tr, ca_t.shape[1]), lambda i: (i, 0)), pl.BlockSpec(dmod.shape, lambda i: (0, 0))]
    n_src = len(srcs)

    def body(*refs):
        w_ref, m_ref, v_ref = refs[n_src:n_src + 3]
        g_ref, d_ref, nm_ref, nv_ref = refs[n_src + 3:]
        if grad is not None:
            g = refs[0][...]
        elif slots is not None:
            g = refs[0][0].astype(F32)
            for s in range(1, N_DEV):
                g = g + refs[0][s].astype(F32)
        else:
            g = jnp.dot(refs[0][...], refs[1][...], preferred_element_type=F32)
        delta, nm, nv = _adam_math(w_ref[...], g, m_ref[...], v_ref[...])
        g_ref[...] = g
        d_ref[...] = delta
        nm_ref[...] = nm
        nv_ref[...] = nv

    shp = jax.ShapeDtypeStruct(w.shape, F32)
    return pl.pallas_call(
        body, name=name, grid=(r // tr,), out_shape=(shp,) * 4, in_specs=[*src_specs, row, row, row],
        out_specs=(row,) * 4, compiler_params=_cp(("parallel",)))(*srcs, w, m, v)


def _sum_slots(name, gathered):
    _, r, cdim = gathered.shape
    tr = _rows_tile(r, cdim * N_DEV)

    def body(g_ref, o_ref):
        acc = g_ref[0]
        for s in range(1, N_DEV):
            acc = acc + g_ref[s]
        o_ref[...] = acc

    return pl.pallas_call(
        body, name=name, grid=(r // tr,), out_shape=jax.ShapeDtypeStruct((r, cdim), F32),
        in_specs=[pl.BlockSpec((N_DEV, tr, cdim), lambda i: (0, i, 0))], out_specs=pl.BlockSpec((tr, cdim), lambda i: (i, 0)),
        compiler_params=_cp(("parallel",)))(gathered)


def _col_sums(name, a):
    r, cdim = a.shape

    def body(a_ref, o_ref):
        o_ref[...] = _colsum(a_ref[...])

    return pl.pallas_call(
        body, name=name, out_shape=jax.ShapeDtypeStruct((1, cdim), F32),
        in_specs=[pl.BlockSpec(memory_space=pltpu.VMEM)], out_specs=pl.BlockSpec(memory_space=pltpu.VMEM),
        compiler_params=_cp())(a)


def _mm(name, a, b, *, mode, grid, a_spec, b_spec, acc_shape, outs, out_specs, epilogue, extra=(), extra_specs=()):
    nk = grid[-1]
    dims = {"nn": ((1,), (0,)), "nt": ((1,), (1,)), "tn": ((0,), (0,))}[mode]
    n_extra, n_out = len(extra), len(outs)

    def body(*refs):
        a_ref, b_ref = refs[0], refs[1]
        extra_refs = refs[2:2 + n_extra]
        out_refs = refs[2 + n_extra:2 + n_extra + n_out]
        part = lax.dot_general(a_ref[...], b_ref[...], (dims, ((), ())), preferred_element_type=F32)
        if nk == 1:
            epilogue(part, extra_refs, out_refs)
        else:
            acc = refs[-1]
            k = pl.program_id(len(grid) - 1)

            @pl.when(k == 0)
            def _():
                acc[...] = part

            @pl.when(k > 0)
            def _():
                acc[...] += part

            @pl.when(k == nk - 1)
            def _():
                epilogue(acc[...], extra_refs, out_refs)

    scratch = [pltpu.VMEM(acc_shape, F32)] if nk > 1 else []
    sem = ("parallel",) * (len(grid) - 1) + ("arbitrary",)
    return pl.pallas_call(
        body, name=name, grid=grid, in_specs=[a_spec, b_spec, *extra_specs], out_specs=out_specs, out_shape=outs,
        scratch_shapes=scratch, compiler_params=_cp(sem))(a, b, *extra)


def _store_bf16(acc, extra_refs, out_refs):
    out_refs[0][...] = acc.astype(BF16)


def _store_f32(acc, extra_refs, out_refs):
    out_refs[0][...] = acc


def _residual_epilogue(acc, extra_refs, out_refs):
    x_ref, gate_ref = extra_refs
    out_refs[0][...] = acc
    out_refs[1][...] = x_ref[...] + gate_ref[...] * acc


def kernel(x, c, w_ada, b_ada, g_mix, w_in, conv_w, conv_b, conv_ln_g, conv_ln_b, sgu_ln_g, sgu_ln_b, sgu_w, sgu_b, out_g_conv, out_g_sgu, w_out, g_ffn, w_up, ffn_conv_w, ffn_conv_b, w_down, g_final, loss_target, m_w_ada, m_b_ada, m_g_mix, m_w_in, m_conv_w, m_conv_b, m_conv_ln_g, m_conv_ln_b, m_sgu_ln_g, m_sgu_ln_b, m_sgu_w, m_sgu_b, m_out_g_conv, m_out_g_sgu, m_w_out, m_g_ffn, m_w_up, m_ffn_conv_w, m_ffn_conv_b, m_w_down, m_g_final, v_w_ada, v_b_ada, v_g_mix, v_w_in, v_conv_w, v_conv_b, v_conv_ln_g, v_conv_ln_b, v_sgu_ln_g, v_sgu_ln_b, v_sgu_w, v_sgu_b, v_out_g_conv, v_out_g_sgu, v_w_out, v_g_ffn, v_w_up, v_ffn_conv_w, v_ffn_conv_b, v_w_down, v_g_final):
    bsz, seq, d = x.shape
    t = bsz * seq
    n_batch = bsz * N_DEV
    ada_sh = w_ada.shape[2]
    n_mod = ada_sh * N_DEV // d
    win_sh = w_in.shape[2]
    kc = conv_w.shape[1]
    dc_sh = conv_w.shape[2]
    dc = dc_sh * N_DEV
    heads, ch = sgu_w.shape[1], sgu_w.shape[2]
    hd = dc // heads
    wout_sh = w_out.shape[1]
    cup = w_up.shape[2]
    kf = ffn_conv_w.shape[1]
    cw_sh = ffn_conv_w.shape[2]
    dff = cw_sh * N_DEV
    n_fs = dff // cup
    assert win_sh * N_DEV == 4 * dc and hd == LANES and kc // 2 < HALO and kf == 3 and 2 * n_fs == N_DEV
    assert w_down.shape[1] * 2 == cup and seq % ch == 0

    xi, yi, ci = _my_pos()
    me = 4 * xi + 2 * yi + ci

    tb = min(256, seq)
    tps = seq // tb
    nt = t // tb
    tbd = min(128, seq)
    tpd = seq // tbd
    ntd = t // tbd
    tm = min(512, seq)
    nm = t // tm

    xf = x.reshape(t, d)
    tgt = loss_target.reshape(t, d)

    w_in_g = _allgather("ag_w_in", _cast_bf16("cast_w_in", w_in[0]), False)
    w_out_g = _allgather("ag_w_out", _cast_bf16("cast_w_out", w_out[0]), False)
    w_up_g = _allgather("ag_w_up", _cast_bf16("cast_w_up", w_up[0]), False)
    w_down_g = _allgather("ag_w_down", _cast_bf16("cast_w_down", w_down[0]), False)
    w_out_full = w_out_g.reshape(d, d)
    w_down_s = w_down_g.reshape(n_fs, cup, d)

    small_shapes = [(bsz, d), (kc, dc_sh), (kf, cw_sh)]
    blob0 = _allgather("ag_small_in", _pack([c, conv_w[0], ffn_conv_w[0]]), True)
    per_dev = [_unpack(blob0[s], small_shapes) for s in range(N_DEV)]
    c_all = jnp.concatenate([p[0] for p in per_dev], axis=0)
    conv_w_full = jnp.concatenate([p[1] for p in per_dev], axis=1)
    ffn_w_full = jnp.concatenate([p[2] for p in per_dev], axis=1)
    ffn_w_s = ffn_w_full.reshape(kf, n_fs, cup).transpose(1, 0, 2)
    ffn_b_s = ffn_conv_b.reshape(n_fs, 1, cup)

    b_ada_sh = lax.dynamic_slice(b_ada, (0, me * ada_sh), (1, ada_sh))
    tn_ada = _blk(ada_sh, 512)

    def ada_body(c_ref, w_ref, b_ref, mod_ref, ca_ref):
        cc = c_ref[...]
        ca = (cc * _sigmoid(cc)).astype(BF16)
        ca_ref[...] = ca
        mod_ref[...] = jnp.dot(ca, w_ref[...].astype(BF16), preferred_element_type=F32) + b_ref[...]

    mod_sh, c_act = pl.pallas_call(
        ada_body, name="ada_fwd", grid=(ada_sh // tn_ada,),
        out_shape=(jax.ShapeDtypeStruct((n_batch, ada_sh), F32), jax.ShapeDtypeStruct((n_batch, d), BF16)),
        in_specs=[pl.BlockSpec((n_batch, d), lambda j: (0, 0)), pl.BlockSpec((d, tn_ada), lambda j: (0, j)),
                  pl.BlockSpec((1, tn_ada), lambda j: (0, j))],
        out_specs=(pl.BlockSpec((n_batch, tn_ada), lambda j: (0, j)), pl.BlockSpec((n_batch, d), lambda j: (0, 0))),
        compiler_params=_cp(("arbitrary",)))(c_all, w_ada[0], b_ada_sh)
    mod_all = _allgather("ag_mod", mod_sh, True)
    mod_me = lax.dynamic_slice(mod_all, (0, me * bsz, 0), (N_DEV, bsz, ada_sh))
    mod_me = mod_me.transpose(1, 0, 2).reshape(bsz, n_mod, 1, d)
    shift_m, scale_m, gate_m = mod_me[:, 0], mod_me[:, 1], mod_me[:, 2]
    shift_f, scale_f, gate_f = mod_me[:, 3], mod_me[:, 4], mod_me[:, 5]

    row_d = pl.BlockSpec((tbd, d), lambda i: (i, 0))
    vec_d = pl.BlockSpec((1, d), lambda i: (0, 0))
    seq_d = pl.BlockSpec((None, 1, d), lambda i: (i // tpd, 0, 0))

    def modulate(name, xin, g, shift, scale):
        def body(x_ref, g_ref, sh_ref, sc_ref, h_ref):
            xx = x_ref[...]
            yy = xx * lax.rsqrt(_mean(xx * xx) + EPS)
            h_ref[...] = ((yy * g_ref[...]) * (1.0 + sc_ref[...]) + sh_ref[...]).astype(BF16)

        return pl.pallas_call(
            body, name=name, grid=(ntd,), out_shape=jax.ShapeDtypeStruct((t, d), BF16),
            in_specs=[row_d, vec_d, seq_d, seq_d], out_specs=row_d, compiler_params=_cp(("parallel",)))(xin, g, shift, scale)

    h1 = modulate("mod1_fwd", xf, g_mix, shift_m, scale_m)

    tk_d = _blk(d, 1024)
    proj = _mm(
        "mm_proj", h1, w_in_g, mode="nn", grid=(nm, N_DEV, 1),
        a_spec=pl.BlockSpec((tm, d), lambda i, j, k: (i, 0)),
        b_spec=pl.BlockSpec((None, d, win_sh), lambda i, j, k: (j, 0, 0)),
        acc_shape=(tm, win_sh), outs=(jax.ShapeDtypeStruct((t, N_DEV * win_sh), BF16),),
        out_specs=(pl.BlockSpec((tm, win_sh), lambda i, j, k: (i, j)),), epilogue=_store_bf16)[0]

    hb = tb // HALO
    n_hb = t // HALO

    def main_col(col):
        return pl.BlockSpec((tb, dc), lambda i, col=col: (i, col))

    def chunk_col(col):
        return pl.BlockSpec((ch, dc), lambda i, col=col: (i, col))

    def prev_col(col):
        return pl.BlockSpec((HALO, dc), lambda i, col=col: (jnp.maximum(i * hb - 1, 0), col))

    def next_col(col):
        return pl.BlockSpec((HALO, dc), lambda i, col=col: (jnp.minimum((i + 1) * hb, n_hb - 1), col))

    vec_c = pl.BlockSpec((1, dc), lambda i: (0, 0))
    row_c = pl.BlockSpec((tb, dc), lambda i: (i, 0))
    convw_spec = pl.BlockSpec((kc, dc), lambda i: (0, 0))
    sguw_spec = pl.BlockSpec((heads, ch, ch), lambda i: (0, 0, 0))
    bias_spec = pl.BlockSpec((ch, dc), lambda i: (0, 0))
    n_chunk = tb // ch

    sgu_w_bf = sgu_w[0].astype(BF16)
    sgu_wt_bf = jnp.swapaxes(sgu_w[0], 1, 2).astype(BF16)
    sgu_bias = jnp.repeat(sgu_b[0].T, hd, axis=1)

    def fill_glu_ext(ext, i, pv, pg, pvp, pgp, pvn, pgn):
        first = (i % tps) == 0
        last = (i % tps) == tps - 1

        def glu(v_ref, g_ref):
            return v_ref[...].astype(F32) * _sigmoid(g_ref[...].astype(F32))

        ext[pl.ds(0, HALO), :] = jnp.where(first, 0.0, glu(pvp, pgp))
        ext[pl.ds(HALO, tb), :] = glu(pv, pg)
        ext[pl.ds(HALO + tb, HALO), :] = jnp.where(last, 0.0, glu(pvn, pgn))

    def layer_norm_stats(z):
        mu = _mean(z)
        zc = z - mu
        rstd = lax.rsqrt(_mean(zc * zc) + EPS)
        return zc * rstd, rstd

    def sgu_mix(vs_ref, w_ref, vnb, bias_ref, n_chunks):
        for cc in range(n_chunks):
            for h in range(heads):
                blk = jnp.dot(w_ref[h], vnb[cc * ch:(cc + 1) * ch, h * hd:(h + 1) * hd], preferred_element_type=F32)
                vs_ref[pl.ds(cc * ch, ch), pl.ds(h * hd, hd)] = blk + bias_ref[:, pl.ds(h * hd, hd)]

    def mix_fwd_body(pv, pg, pu, pw, pvp, pgp, pvn, pgn, cw, cb, lg, lb, og, slg, slb, sw, sbias, sog, y_ref, a1_ref, ext, vs_ref):
        i = pl.program_id(0)
        fill_glu_ext(ext, i, pv, pg, pvp, pgp, pvn, pgn)
        pad = kc // 2
        acc = jnp.zeros((tb, dc), F32) + cb[...]
        for k in range(kc):
            acc = acc + cw[pl.ds(k, 1), :] * ext[pl.ds(HALO - pad + k, tb), :]
        a1_ref[...] = acc
        xh, _ = layer_norm_stats(acc)
        a2 = xh * lg[...] + lb[...]
        a3 = a2 * _sigmoid(a2)
        ya = a3 * lax.rsqrt(_mean(a3 * a3) + EPS) * og[...]
        y_ref[:, pl.ds(0, dc)] = ya.astype(BF16)

        u = _gelu(pu[...].astype(F32))
        vv = _gelu(pw[...].astype(F32))
        xhv, _ = layer_norm_stats(vv)
        vn = xhv * slg[...] + slb[...]
        sgu_mix(vs_ref, sw, vn.astype(BF16), sbias, n_chunk)
        bg = u * vs_ref[...]
        yb = bg * lax.rsqrt(_mean(bg * bg) + EPS) * sog[...]
        y_ref[:, pl.ds(dc, dc)] = yb.astype(BF16)

    y, a1 = pl.pallas_call(
        mix_fwd_body, name="mix_fwd", grid=(nt,),
        out_shape=(jax.ShapeDtypeStruct((t, 2 * dc), BF16), jax.ShapeDtypeStruct((t, dc), F32)),
        in_specs=[main_col(0), main_col(1), main_col(2), main_col(3), prev_col(0), prev_col(1), next_col(0), next_col(1),
                  convw_spec, vec_c, vec_c, vec_c, vec_c, vec_c, vec_c, sguw_spec, bias_spec, vec_c],
        out_specs=(pl.BlockSpec((tb, 2 * dc), lambda i: (i, 0)), row_c),
        scratch_shapes=[pltpu.VMEM((tb + 2 * HALO, dc), F32), pltpu.VMEM((tb, dc), F32)],
        compiler_params=_cp(("parallel",)),
    )(proj, proj, proj, proj, proj, proj, proj, proj, conv_w_full, conv_b, conv_ln_g, conv_ln_b, out_g_conv,
      sgu_ln_g, sgu_ln_b, sgu_w_bf, sgu_bias, out_g_sgu)

    tn_d = _blk(d, 1024)
    gate_spec3 = pl.BlockSpec((None, 1, tn_d), lambda i, j, k: (i * tm // seq, 0, j))
    res_spec3 = pl.BlockSpec((tm, tn_d), lambda i, j, k: (i, j))
    td_shape = jax.ShapeDtypeStruct((t, d), F32)

    o_mix, x1 = _mm(
        "mm_out", y, w_out_full, mode="nn", grid=(nm, d // tn_d, 1),
        a_spec=pl.BlockSpec((tm, d), lambda i, j, k: (i, 0)), b_spec=pl.BlockSpec((d, tn_d), lambda i, j, k: (0, j)),
        acc_shape=(tm, tn_d), outs=(td_shape, td_shape), out_specs=(res_spec3, res_spec3),
        epilogue=_residual_epilogue, extra=(xf, gate_m), extra_specs=(res_spec3, gate_spec3))

    h2 = modulate("mod2_fwd", x1, g_ffn, shift_f, scale_f)

    tm_up = tm
    nk_d = d // tk_d
    up = _mm(
        "mm_up", h2, w_up_g, mode="nn", grid=(t // tm_up, N_DEV, nk_d),
        a_spec=pl.BlockSpec((tm_up, tk_d), lambda i, j, k: (i, k)),
        b_spec=pl.BlockSpec((None, tk_d, cup), lambda i, j, k: (j, k, 0)),
        acc_shape=(tm_up, cup), outs=(jax.ShapeDtypeStruct((N_DEV, t, cup), BF16),),
        out_specs=(pl.BlockSpec((None, tm_up, cup), lambda i, j, k: (j, i, 0)),), epilogue=_store_bf16)[0]
    up4 = up.reshape(2, n_fs, t, cup)

    def ffn_halo(which, lo):
        if lo:
            return pl.BlockSpec((None, None, HALO, cup), lambda j, i: (which, j, jnp.maximum(i * hb - 1, 0), 0))
        return pl.BlockSpec((None, None, HALO, cup), lambda j, i: (which, j, jnp.minimum((i + 1) * hb, n_hb - 1), 0))

    pair_spec = pl.BlockSpec((2, None, tb, cup), lambda j, i: (0, j, i, 0))
    fw_spec = pl.BlockSpec((None, kf, cup), lambda j, i: (j, 0, 0))
    fb_spec = pl.BlockSpec((None, 1, cup), lambda j, i: (j, 0, 0))
    act_spec = pl.BlockSpec((None, tb, cup), lambda j, i: (j, i, 0))

    def fill_gate_ext(ext, i, main_f32, prev_ref, next_ref):
        first = (i % tps) == 0
        last = (i % tps) == tps - 1
        ext[pl.ds(0, HALO), :] = jnp.where(first, 0.0, prev_ref[...].astype(F32))
        ext[pl.ds(HALO, tb), :] = main_f32
        ext[pl.ds(HALO + tb, HALO), :] = jnp.where(last, 0.0, next_ref[...].astype(F32))

    def ffn_fwd_body(pair, gp, gn, fw, fb, act_ref, ext):
        i = pl.program_id(1)
        fill_gate_ext(ext, i, pair[0].astype(F32), gp, gn)
        gte = fb[...] + fw[pl.ds(0, 1), :] * ext[pl.ds(HALO - 1, tb), :]
        gte = gte + fw[pl.ds(1, 1), :] * ext[pl.ds(HALO, tb), :]
        gte = gte + fw[pl.ds(2, 1), :] * ext[pl.ds(HALO + 1, tb), :]
        act_ref[...] = (gte * _sigmoid(gte) * pair[1].astype(F32)).astype(BF16)

    act = pl.pallas_call(
        ffn_fwd_body, name="ffn_fwd", grid=(n_fs, nt), out_shape=jax.ShapeDtypeStruct((n_fs, t, cup), BF16),
        in_specs=[pair_spec, ffn_halo(0, True), ffn_halo(0, False), fw_spec, fb_spec], out_specs=act_spec,
        scratch_shapes=[pltpu.VMEM((tb + 2 * HALO, cup), F32)], compiler_params=_cp(("parallel", "parallel")),
    )(up4, up4, up4, ffn_w_s, ffn_b_s)

    gate_spec_f = pl.BlockSpec((None, 1, tn_d), lambda i, j, k: (i * tm // seq, 0, j))
    dn, x2 = _mm(
        "mm_down", act, w_down_s, mode="nn", grid=(nm, d // tn_d, n_fs),
        a_spec=pl.BlockSpec((None, tm, cup), lambda i, j, k: (k, i, 0)),
        b_spec=pl.BlockSpec((None, cup, tn_d), lambda i, j, k: (k, 0, j)),
        acc_shape=(tm, tn_d), outs=(td_shape, td_shape), out_specs=(res_spec3, res_spec3),
        epilogue=_residual_epilogue, extra=(x1, gate_f), extra_specs=(res_spec3, gate_spec_f))

    acc_d = pl.BlockSpec((1, d), lambda i: (0, 0))
    seq_acc = pl.BlockSpec((None, 1, d), lambda i: (i // tpd, 0, 0))

    def head_body(x_ref, t_ref, g_ref, dn_ref, gate_ref, dx_ref, ddn_ref, loss_ref, dg_ref, dgate_ref):
        i = pl.program_id(0)
        xx = x_ref[...]
        rr = lax.rsqrt(_mean(xx * xx) + EPS)
        xn = xx * rr
        err = xn * g_ref[...] - t_ref[...]
        dyf = err * (1.0 / d)
        dxn = dyf * g_ref[...]
        dx = rr * (dxn - xn * _mean(dxn * xn))
        dx_ref[...] = dx
        ddn_ref[...] = (gate_ref[...] * dx).astype(BF16)
        part = 0.5 * jnp.sum(_mean(err * err), axis=0, keepdims=True)

        @pl.when(i == 0)
        def _():
            loss_ref[...] = jnp.zeros_like(loss_ref)
            dg_ref[...] = jnp.zeros_like(dg_ref)

        @pl.when(i % tpd == 0)
        def _():
            dgate_ref[...] = jnp.zeros_like(dgate_ref)

        loss_ref[...] += jnp.broadcast_to(part, loss_ref.shape)
        dg_ref[...] += _colsum(dyf * xn)
        dgate_ref[...] += _colsum(dx * dn_ref[...])

    seq_shape = jax.ShapeDtypeStruct((bsz, 1, d), F32)
    vec_shape = jax.ShapeDtypeStruct((1, d), F32)
    dx2, ddn, loss_part, dg_final, dgate_f = pl.pallas_call(
        head_body, name="loss_head", grid=(ntd,),
        out_shape=(td_shape, jax.ShapeDtypeStruct((t, d), BF16), jax.ShapeDtypeStruct((1, LANES), F32), vec_shape, seq_shape),
        in_specs=[row_d, row_d, vec_d, row_d, seq_d],
        out_specs=(row_d, row_d, pl.BlockSpec((1, LANES), lambda i: (0, 0)), acc_d, seq_acc),
        compiler_params=_cp(("arbitrary",)))(x2, tgt, g_final.reshape(1, d), dn, gate_f)

    dact = _mm(
        "mm_dact", ddn, w_down_s, mode="nt", grid=(t // tm_up, n_fs, nk_d),
        a_spec=pl.BlockSpec((tm_up, tk_d), lambda i, j, k: (i, k)),
        b_spec=pl.BlockSpec((None, cup, tk_d), lambda i, j, k: (j, 0, k)),
        acc_shape=(tm_up, cup), outs=(jax.ShapeDtypeStruct((n_fs, t, cup), BF16),),
        out_specs=(pl.BlockSpec((None, tm_up, cup), lambda i, j, k: (j, i, 0)),), epilogue=_store_bf16)[0]

    tk_t = min(1024, t)
    nk_t = t // tk_t
    tn_gw = _blk(d, 512)
    gw_down = _mm(
        "mm_gw_down", act, ddn, mode="tn", grid=(n_fs, d // tn_gw, nk_t),
        a_spec=pl.BlockSpec((None, tk_t, cup), lambda j, n, k: (j, k, 0)),
        b_spec=pl.BlockSpec((tk_t, tn_gw), lambda j, n, k: (k, n)),
        acc_shape=(cup, tn_gw), outs=(jax.ShapeDtypeStruct((n_fs, cup, d), BF16),),
        out_specs=(pl.BlockSpec((None, cup, tn_gw), lambda j, n, k: (j, 0, n)),), epilogue=_store_bf16)[0]

    ext_e = tb + HALO
    half = HALO // 2

    def ffn_bwd_body(pair, gp, gn, vp, vn, da, dap, dan, fw, fb, dup_ref, dw_ref, db_ref, ext, dg_e):
        i = pl.program_id(1)
        first = (i % tps) == 0
        last = (i % tps) == tps - 1
        fill_gate_ext(ext, i, pair[0].astype(F32), gp, gn)
        w0, w1, w2 = fw[pl.ds(0, 1), :], fw[pl.ds(1, 1), :], fw[pl.ds(2, 1), :]

        def dgte_of(gte, val, dact_v):
            sg = _sigmoid(gte)
            return dact_v * val * (sg * (1.0 + gte * (1.0 - sg))), gte * sg

        def gte_at(lo, n):
            return (fb[...] + w0 * ext[pl.ds(lo - 1, n), :] + w1 * ext[pl.ds(lo, n), :] + w2 * ext[pl.ds(lo + 1, n), :])

        d_lo, _ = dgte_of(gte_at(half, half), vp[...].astype(F32)[half:, :], dap[...].astype(F32)[half:, :])
        dg_e[pl.ds(0, half), :] = jnp.where(first, 0.0, d_lo)
        d_hi, _ = dgte_of(gte_at(HALO + tb, half), vn[...].astype(F32)[:half, :], dan[...].astype(F32)[:half, :])
        dg_e[pl.ds(half + tb, half), :] = jnp.where(last, 0.0, d_hi)
        dact_m = da[...].astype(F32)
        d_mid, silu_m = dgte_of(gte_at(HALO, tb), pair[1].astype(F32), dact_m)
        dg_e[pl.ds(half, tb), :] = d_mid
        dup_ref[1] = (dact_m * silu_m).astype(BF16)
        dgate = w0 * dg_e[pl.ds(half + 1, tb), :] + w1 * d_mid + w2 * dg_e[pl.ds(half - 1, tb), :]
        dup_ref[0] = dgate.astype(BF16)

        @pl.when(i == 0)
        def _():
            dw_ref[...] = jnp.zeros_like(dw_ref)
            db_ref[...] = jnp.zeros_like(db_ref)

        dw_ref[pl.ds(0, 1), :] += _colsum(d_mid * ext[pl.ds(HALO - 1, tb), :])
        dw_ref[pl.ds(1, 1), :] += _colsum(d_mid * ext[pl.ds(HALO, tb), :])
        dw_ref[pl.ds(2, 1), :] += _colsum(d_mid * ext[pl.ds(HALO + 1, tb), :])
        db_ref[...] += _colsum(d_mid)

    def act_halo(lo):
        if lo:
            return pl.BlockSpec((None, HALO, cup), lambda j, i: (j, jnp.maximum(i * hb - 1, 0), 0))
        return pl.BlockSpec((None, HALO, cup), lambda j, i: (j, jnp.minimum((i + 1) * hb, n_hb - 1), 0))

    dup4, g_ffn_w_s, g_ffn_b_s = pl.pallas_call(
        ffn_bwd_body, name="ffn_bwd", grid=(n_fs, nt),
        out_shape=(jax.ShapeDtypeStruct((2, n_fs, t, cup), BF16), jax.ShapeDtypeStruct((n_fs, kf, cup), F32),
                   jax.ShapeDtypeStruct((n_fs, 1, cup), F32)),
        in_specs=[pair_spec, ffn_halo(0, True), ffn_halo(0, False), ffn_halo(1, True), ffn_halo(1, False),
                  act_spec, act_halo(True), act_halo(False), fw_spec, fb_spec],
        out_specs=(pair_spec, fw_spec, fb_spec),
        scratch_shapes=[pltpu.VMEM((tb + 2 * HALO, cup), F32), pltpu.VMEM((ext_e, cup), F32)],
        compiler_params=_cp(("parallel", "arbitrary")),
    )(up4, up4, up4, up4, up4, dact, dact, dact, ffn_w_s, ffn_b_s)
    dup = dup4.reshape(N_DEV, t, cup)

    dh2 = _mm(
        "mm_dh2", dup, w_up_g, mode="nt", grid=(nm, d // tn_d, N_DEV),
        a_spec=pl.BlockSpec((None, tm, cup), lambda i, j, k: (k, i, 0)),
        b_spec=pl.BlockSpec((None, tn_d, cup), lambda i, j, k: (k, j, 0)),
        acc_shape=(tm, tn_d), outs=(td_shape,), out_specs=(res_spec3,), epilogue=_store_f32)[0]

    tm_w = _blk(d, 1024)
    tk_u = min(512, t)
    gw_up = _mm(
        "mm_gw_up", h2, dup, mode="tn", grid=(N_DEV, d // tm_w, t // tk_u),
        a_spec=pl.BlockSpec((tk_u, tm_w), lambda j, i, k: (k, i)),
        b_spec=pl.BlockSpec((None, tk_u, cup), lambda j, i, k: (j, k, 0)),
        acc_shape=(tm_w, cup), outs=(jax.ShapeDtypeStruct((N_DEV, d, cup), BF16),),
        out_specs=(pl.BlockSpec((None, tm_w, cup), lambda j, i, k: (j, i, 0)),), epilogue=_store_bf16)[0]

    def modulate_bwd(name, xin, dh, dres, g, scale, gate=None, branch=None):
        gated = gate is not None

        def body(*refs):
            x_ref, dh_ref, dres_ref, g_ref, sc_ref = refs[:5]
            rest = refs[5:]
            if gated:
                gate_ref, br_ref, dx_ref, dsh_ref, dsc_ref, dg_ref, do_ref, dgate_ref = rest
            else:
                dx_ref, dsh_ref, dsc_ref, dg_ref = rest
            i = pl.program_id(0)
            xx = x_ref[...]
            rr = lax.rsqrt(_mean(xx * xx) + EPS)
            xn = xx * rr
            s1 = 1.0 + sc_ref[...]
            dhh = dh_ref[...]
            dxn = dhh * g_ref[...] * s1
            dx = dres_ref[...] + rr * (dxn - xn * _mean(dxn * xn))
            dx_ref[...] = dx

            @pl.when(i == 0)
            def _():
                dg_ref[...] = jnp.zeros_like(dg_ref)

            @pl.when(i % tpd == 0)
            def _():
                dsh_ref[...] = jnp.zeros_like(dsh_ref)
                dsc_ref[...] = jnp.zeros_like(dsc_ref)
                if gated:
                    dgate_ref[...] = jnp.zeros_like(dgate_ref)

            dsh_ref[...] += _colsum(dhh)
            dsc_ref[...] += _colsum(dhh * (xn * g_ref[...]))
            dg_ref[...] += _colsum(dhh * s1 * xn)
            if gated:
                do_ref[...] = (gate_ref[...] * dx).astype(BF16)
                dgate_ref[...] += _colsum(dx * br_ref[...])

        ins = [xin, dh, dres, g, scale]
        in_specs = [row_d, row_d, row_d, vec_d, seq_d]
        outs = [td_shape, seq_shape, seq_shape, vec_shape]
        out_specs = [row_d, seq_acc, seq_acc, acc_d]
        if gated:
            ins += [gate, branch]
            in_specs += [seq_d, row_d]
            outs += [jax.ShapeDtypeStruct((t, d), BF16), seq_shape]
            out_specs += [row_d, seq_acc]
        return pl.pallas_call(
            body, name=name, grid=(ntd,), out_shape=tuple(outs), in_specs=in_specs, out_specs=tuple(out_specs),
            compiler_params=_cp(("arbitrary",)))(*ins)

    dx1, dshift_f, dscale_f, dg_ffn, d_o, dgate_m = modulate_bwd("mod2_bwd", x1, dh2, dx2, g_ffn, scale_f, gate_m, o_mix)

    dy = _mm(
        "mm_dy", d_o, w_out_full, mode="nt", grid=(nm, d // tn_d, 1),
        a_spec=pl.BlockSpec((tm, d), lambda i, j, k: (i, 0)), b_spec=pl.BlockSpec((tn_d, d), lambda i, j, k: (j, 0)),
        acc_shape=(tm, tn_d), outs=(jax.ShapeDtypeStruct((t, d), BF16),),
        out_specs=(pl.BlockSpec((tm, tn_d), lambda i, j, k: (i, j)),), epilogue=_store_bf16)[0]

    tn_w = _blk(d, 2048)
    gw_out = _mm(
        "mm_gw_out", y, d_o, mode="tn", grid=(d // tm_w, d // tn_w, nk_t),
        a_spec=pl.BlockSpec((tk_t, tm_w), lambda i, j, k: (k, i)), b_spec=pl.BlockSpec((tk_t, tn_w), lambda i, j, k: (k, j)),
        acc_shape=(tm_w, tn_w), outs=(jax.ShapeDtypeStruct((d, d), BF16),),
        out_specs=(pl.BlockSpec((tm_w, tn_w), lambda i, j, k: (i, j)),), epilogue=_store_bf16)[0]

    def conv_norm_bwd_body(dy_ref, a1_ref, lg, lb, og, da1_ref, dog_ref, dlg_ref, dlb_ref):
        i = pl.program_id(0)
        xh, rstd = layer_norm_stats(a1_ref[...])
        a2 = xh * lg[...] + lb[...]
        sg = _sigmoid(a2)
        a3 = a2 * sg
        r3 = lax.rsqrt(_mean(a3 * a3) + EPS)
        n3 = a3 * r3
        dya = dy_ref[...].astype(F32)
        dn3 = dya * og[...]
        da3 = r3 * (dn3 - n3 * _mean(dn3 * n3))
        da2 = da3 * (sg * (1.0 + a2 * (1.0 - sg)))
        dxh = da2 * lg[...]
        da1_ref[...] = rstd * (dxh - _mean(dxh) - xh * _mean(dxh * xh))

        @pl.when(i == 0)
        def _():
            dog_ref[...] = jnp.zeros_like(dog_ref)
            dlg_ref[...] = jnp.zeros_like(dlg_ref)
            dlb_ref[...] = jnp.zeros_like(dlb_ref)

        dog_ref[...] += _colsum(dya * n3)
        dlg_ref[...] += _colsum(da2 * xh)
        dlb_ref[...] += _colsum(da2)

    vecc_shape = jax.ShapeDtypeStruct((1, dc), F32)
    da1, g_og_conv, g_cln_g, g_cln_b = pl.pallas_call(
        conv_norm_bwd_body, name="conv_norm_bwd", grid=(nt,),
        out_shape=(jax.ShapeDtypeStruct((t, dc), F32), vecc_shape, vecc_shape, vecc_shape),
        in_specs=[main_col(0), row_c, vec_c, vec_c, vec_c], out_specs=(row_c, vec_c, vec_c, vec_c),
        compiler_params=_cp(("arbitrary",)))(dy, a1, conv_ln_g, conv_ln_b, out_g_conv)

    def sgu_bwd_body(dy_ref, pu, pw, slg, slb, sw, swt, sbias, sog, dproj_ref, dsog_ref, dslg_ref, dslb_ref, dsw_ref, dsb_ref, vs_ref, dvn_ref):
        i = pl.program_id(0)
        zu = pu[...].astype(F32)
        zv = pw[...].astype(F32)
        u = _gelu(zu)
        vv = _gelu(zv)
        xhv, rstd = layer_norm_stats(vv)
        vnb = (xhv * slg[...] + slb[...]).astype(BF16)
        sgu_mix(vs_ref, sw, vnb, sbias, 1)
        vs = vs_ref[...]
        bg = u * vs
        rb = lax.rsqrt(_mean(bg * bg) + EPS)
        nb = bg * rb
        dyb = dy_ref[...].astype(F32)
        dnb = dyb * sog[...]
        dbg = rb * (dnb - nb * _mean(dnb * nb))
        du = dbg * vs
        dvs = dbg * u
        dvsb = dvs.astype(BF16)

        @pl.when(i == 0)
        def _():
            dsog_ref[...] = jnp.zeros_like(dsog_ref)
            dslg_ref[...] = jnp.zeros_like(dslg_ref)
            dslb_ref[...] = jnp.zeros_like(dslb_ref)
            dsw_ref[...] = jnp.zeros_like(dsw_ref)
            dsb_ref[...] = jnp.zeros_like(dsb_ref)

        dsog_ref[...] += _colsum(dyb * nb)
        dsb_ref[...] += dvs
        for h in range(heads):
            dblk = dvsb[:, h * hd:(h + 1) * hd]
            vblk = vnb[:, h * hd:(h + 1) * hd]
            dsw_ref[h] += lax.dot_general(dblk, vblk, (((1,), (1,)), ((), ())), preferred_element_type=F32)
            dvn_ref[:, pl.ds(h * hd, hd)] = jnp.dot(swt[h], dblk, preferred_element_type=F32)
        dvn = dvn_ref[...]
        dslg_ref[...] += _colsum(dvn * xhv)
        dslb_ref[...] += _colsum(dvn)
        dxh = dvn * slg[...]
        dvv = rstd * (dxh - _mean(dxh) - xhv * _mean(dxh * xhv))
        dproj_ref[:, pl.ds(0, dc)] = (du * _gelu_grad(zu)).astype(BF16)
        dproj_ref[:, pl.ds(dc, dc)] = (dvv * _gelu_grad(zv)).astype(BF16)

    dproj, g_og_sgu, g_sln_g, g_sln_b, g_sgu_w, g_sgu_bias = pl.pallas_call(
        sgu_bwd_body, name="sgu_bwd", grid=(t // ch,),
        out_shape=(jax.ShapeDtypeStruct((t, 4 * dc), BF16), vecc_shape, vecc_shape, vecc_shape,
                   jax.ShapeDtypeStruct((heads, ch, ch), F32), jax.ShapeDtypeStruct((ch, dc), F32)),
        in_specs=[chunk_col(1), chunk_col(2), chunk_col(3), vec_c, vec_c, sguw_spec, sguw_spec, bias_spec, vec_c],
        out_specs=(pl.BlockSpec((ch, 2 * dc), lambda i: (i, 1)), vec_c, vec_c, vec_c, sguw_spec, bias_spec),
        scratch_shapes=[pltpu.VMEM((ch, dc), F32), pltpu.VMEM((ch, dc), F32)],
        compiler_params=_cp(("arbitrary",)),
    )(dy, proj, proj, sgu_ln_g, sgu_ln_b, sgu_w_bf, sgu_wt_bf, sgu_bias, out_g_sgu)

    def sgu_bias_reduce_body(b_ref, o_ref):
        lane = lax.broadcasted_iota(jnp.int32, (ch, LANES), 1)
        res = jnp.zeros((ch, LANES), F32)
        for h in range(heads):
            res = jnp.where(lane == h, jnp.sum(b_ref[:, pl.ds(h * hd, hd)], axis=1, keepdims=True), res)
        o_ref[...] = res

    g_sgu_b_t = pl.pallas_call(
        sgu_bias_reduce_body, name="sgu_bias_reduce", out_shape=jax.ShapeDtypeStruct((ch, LANES), F32),
        in_specs=[pl.BlockSpec(memory_space=pltpu.VMEM)], out_specs=pl.BlockSpec(memory_space=pltpu.VMEM),
        compiler_params=_cp())(g_sgu_bias)
    g_sgu_b = g_sgu_b_t[:, :heads].T

    def conv_bwd_body(da, dap, dan, pv, pg, pvp, pgp, pvn, pgn, cw, dproj_in, dproj_ref, dcw_ref, dcb_ref, ext, dext):
        del dproj_in
        i = pl.program_id(0)
        first = (i % tps) == 0
        last = (i % tps) == tps - 1
        fill_glu_ext(ext, i, pv, pg, pvp, pgp, pvn, pgn)
        da_m = da[...]
        dext[pl.ds(0, HALO), :] = jnp.where(first, 0.0, dap[...])
        dext[pl.ds(HALO, tb), :] = da_m
        dext[pl.ds(HALO + tb, HALO), :] = jnp.where(last, 0.0, dan[...])
        pad = kc // 2

        @pl.when(i == 0)
        def _():
            dcw_ref[...] = jnp.zeros_like(dcw_ref)
            dcb_ref[...] = jnp.zeros_like(dcb_ref)

        da0 = jnp.zeros((tb, dc), F32)
        for k in range(kc):
            da0 = da0 + cw[pl.ds(k, 1), :] * dext[pl.ds(HALO + pad - k, tb), :]
            dcw_ref[pl.ds(k, 1), :] += _colsum(da_m * ext[pl.ds(HALO - pad + k, tb), :])
        dcb_ref[...] += _colsum(da_m)
        vv = pv[...].astype(F32)
        sg = _sigmoid(pg[...].astype(F32))
        dproj_ref[:, pl.ds(0, dc)] = (da0 * sg).astype(BF16)
        dproj_ref[:, pl.ds(dc, dc)] = (da0 * vv * sg * (1.0 - sg)).astype(BF16)

    def halo_rows(lo):
        if lo:
            return pl.BlockSpec((HALO, dc), lambda i: (jnp.maximum(i * hb - 1, 0), 0))
        return pl.BlockSpec((HALO, dc), lambda i: (jnp.minimum((i + 1) * hb, n_hb - 1), 0))

    dproj, g_conv_w, g_conv_b = pl.pallas_call(
        conv_bwd_body, name="conv_bwd", grid=(nt,),
        out_shape=(jax.ShapeDtypeStruct((t, 4 * dc), BF16), jax.ShapeDtypeStruct((kc, dc), F32), vecc_shape),
        in_specs=[row_c, halo_rows(True), halo_rows(False), main_col(0), main_col(1), prev_col(0), prev_col(1),
                  next_col(0), next_col(1), convw_spec, pl.BlockSpec(memory_space=pl.ANY)],
        out_specs=(pl.BlockSpec((tb, 2 * dc), lambda i: (i, 0)), convw_spec, vec_c),
        scratch_shapes=[pltpu.VMEM((tb + 2 * HALO, dc), F32), pltpu.VMEM((tb + 2 * HALO, dc), F32)],
        input_output_aliases={10: 0}, compiler_params=_cp(("arbitrary",)),
    )(da1, da1, da1, proj, proj, proj, proj, proj, proj, conv_w_full, dproj)

    tn_h = _blk(d, 2048)
    dh1 = _mm(
        "mm_dh1", dproj, w_in_g, mode="nt", grid=(nm, d // tn_h, N_DEV),
        a_spec=pl.BlockSpec((tm, win_sh), lambda i, j, k: (i, k)),
        b_spec=pl.BlockSpec((None, tn_h, win_sh), lambda i, j, k: (k, j, 0)),
        acc_shape=(tm, tn_h), outs=(td_shape,), out_specs=(pl.BlockSpec((tm, tn_h), lambda i, j, k: (i, j)),),
        epilogue=_store_f32)[0]

    gw_in = _mm(
        "mm_gw_in", h1, dproj, mode="tn", grid=(N_DEV, d // tm_w, nk_t),
        a_spec=pl.BlockSpec((tk_t, tm_w), lambda j, i, k: (k, i)),
        b_spec=pl.BlockSpec((tk_t, win_sh), lambda j, i, k: (k, j)),
        acc_shape=(tm_w, win_sh), outs=(jax.ShapeDtypeStruct((N_DEV, d, win_sh), BF16),),
        out_specs=(pl.BlockSpec((None, tm_w, win_sh), lambda j, i, k: (j, i, 0)),), epilogue=_store_bf16)[0]

    grad_x, dshift_m, dscale_m, dg_mix = modulate_bwd("mod1_bwd", xf, dh1, dx1, g_mix, scale_m)

    dmod = jnp.concatenate([dshift_m, dscale_m, dgate_m, dshift_f, dscale_f, dgate_f], axis=1)
    g_ffn_w_full = g_ffn_w_s.transpose(1, 0, 2).reshape(kf, dff)
    rep_names = ["g_mix", "conv_b", "conv_ln_g", "conv_ln_b", "sgu_ln_g", "sgu_ln_b", "sgu_w", "sgu_b",
                 "out_g_conv", "out_g_sgu", "g_ffn", "ffn_conv_b", "g_final"]
    rep_w = [g_mix, conv_b, conv_ln_g, conv_ln_b, sgu_ln_g, sgu_ln_b, sgu_w, sgu_b, out_g_conv, out_g_sgu, g_ffn, ffn_conv_b, g_final]
    rep_m = [m_g_mix, m_conv_b, m_conv_ln_g, m_conv_ln_b, m_sgu_ln_g, m_sgu_ln_b, m_sgu_w, m_sgu_b, m_out_g_conv, m_out_g_sgu,
             m_g_ffn, m_ffn_conv_b, m_g_final]
    rep_v = [v_g_mix, v_conv_b, v_conv_ln_g, v_conv_ln_b, v_sgu_ln_g, v_sgu_ln_b, v_sgu_w, v_sgu_b, v_out_g_conv, v_out_g_sgu,
             v_g_ffn, v_ffn_conv_b, v_g_final]
    rep_g = [dg_mix, g_conv_b, g_cln_g, g_cln_b, g_sln_g, g_sln_b, g_sgu_w, g_sgu_b, g_og_conv, g_og_sgu, dg_ffn,
             g_ffn_b_s, dg_final]
    rep_shapes = [w.shape for w in rep_w]
    extra_g = [g_conv_w, g_ffn_w_full, loss_part[:, :1]]
    extra_shapes = [(kc, dc), (kf, dff), (1, 1)]
    sum_blob = _pack(rep_g + extra_g)
    n_sum_rows = sum_blob.shape[0]
    blob1 = jnp.concatenate([sum_blob, _pack([dmod])], axis=0)
    gathered = _allgather("ag_small_grads", blob1, True)
    summed = _sum_slots("sum_small_grads", gathered[:, :n_sum_rows])
    n_rep_rows = _pack(rep_g).shape[0]
    g_conv_w_all, g_ffn_w_all, loss_all = _unpack(summed[n_rep_rows:], extra_shapes)
    loss = loss_all[0, 0]
    dmod_all = gathered[:, n_sum_rows:].reshape(N_DEV, -1)[:, :bsz * n_mod * d].reshape(n_batch, n_mod * d)

    rep_out = _adamw("adamw_small", _pack(rep_w), _pack(rep_m), _pack(rep_v), grad=summed[:n_rep_rows])
    rep_out = [_unpack(o, rep_shapes) for o in rep_out]
    rep = {name: tuple(rep_out[q][p] for q in range(4)) for p, name in enumerate(rep_names)}

    g_conv_w_me = lax.dynamic_slice(g_conv_w_all, (0, me * dc_sh), (kc, dc_sh))
    cw_out = _adamw("adamw_conv_w", conv_w[0], m_conv_w[0], v_conv_w[0], grad=g_conv_w_me)
    g_ffn_w_me = lax.dynamic_slice(g_ffn_w_all, (0, me * cw_sh), (kf, cw_sh))
    fw_out = _adamw("adamw_ffn_conv_w", ffn_conv_w[0], m_ffn_conv_w[0], v_ffn_conv_w[0], grad=g_ffn_w_me)

    g_b_ada = _col_sums("grad_b_ada", dmod_all)
    bada_out = _adamw("adamw_b_ada", b_ada, m_b_ada, v_b_ada, grad=g_b_ada)
    dmod_sh = lax.dynamic_slice(dmod_all, (0, me * ada_sh), (n_batch, ada_sh)).astype(BF16)
    wada_out = _adamw("adamw_w_ada", w_ada[0], m_w_ada[0], v_w_ada[0], ca_t=c_act.T, dmod=dmod_sh)

    def big(name, gw, w, m, v):
        shp = w.shape[1:]
        recv = _alltoall("a2a_" + name, gw.reshape(N_DEV, *shp))
        return _adamw("adamw_" + name, w[0], m[0], v[0], slots=recv)

    win_out = big("w_in", gw_in, w_in, m_w_in, v_w_in)
    wout_out = big("w_out", gw_out, w_out, m_w_out, v_w_out)
    wup_out = big("w_up", gw_up, w_up, m_w_up, v_w_up)
    wdown_out = big("w_down", gw_down, w_down, m_w_down, v_w_down)

    def lead(outs4):
        return tuple(o[None] for o in outs4)

    results = {
        "w_ada": lead(wada_out), "b_ada": bada_out, "w_in": lead(win_out), "conv_w": lead(cw_out),
        "w_out": lead(wout_out), "w_up": lead(wup_out), "ffn_conv_w": lead(fw_out), "w_down": lead(wdown_out),
    }
    results.update(rep)
    order = ["w_ada", "b_ada", "g_mix", "w_in", "conv_w", "conv_b", "conv_ln_g", "conv_ln_b", "sgu_ln_g", "sgu_ln_b",
             "sgu_w", "sgu_b", "out_g_conv", "out_g_sgu", "w_out", "g_ffn", "w_up", "ffn_conv_w", "ffn_conv_b", "w_down", "g_final"]
    out = [loss, grad_x.reshape(bsz, seq, d)]
    for q in range(4):
        out += [results[n][q] for n in order]
    return tuple(out)
```

```python
import functools

import jax
import jax.numpy as jnp
from jax import lax
from jax.experimental import pallas as pl
from jax.experimental.pallas import tpu as pltpu

EPS = 1e-6
N_DEV = 8
MESH_ID = pl.DeviceIdType.MESH
V7X_VMEM_BYTES = 64 * 1024 * 1024
VMEM_LIMIT = V7X_VMEM_BYTES - 8 * 1024 * 1024
LANES = 128
SUBLANES = 8
HALO = 16
BLOB_ALIGN = SUBLANES * LANES

ADAM_LR = 0.001
ADAM_B1 = 0.9
ADAM_B2 = 0.999
ADAM_EPS = 1e-08
ADAM_WD = 0.01
ADAM_STEP = 10

F32 = jnp.float32
BF16 = jnp.bfloat16


def _cp(sem=None, **kw):
    return pltpu.CompilerParams(dimension_semantics=sem, vmem_limit_bytes=VMEM_LIMIT, **kw)


def _blk(n, pref):
    return pref if n % pref == 0 else n


def _rows_tile(rows, cols, budget=3 << 19):
    best = None
    for t in range(SUBLANES, rows + 1, SUBLANES):
        if rows % t == 0 and t * cols * 4 <= budget:
            best = t
    return best if best is not None else rows


def _sigmoid(z):
    return 1.0 / (1.0 + jnp.exp(-z))


def _gelu(z):
    return z * (lax.erf(z * 0.7071067811865476) + 1.0) * 0.5


def _gelu_grad(z):
    return 0.5 * (1.0 + lax.erf(z * 0.7071067811865476)) + z * jnp.exp(-0.5 * z * z) * 0.3989422804014327


def _mean(z):
    return jnp.mean(z, axis=-1, keepdims=True)


def _colsum(z):
    return jnp.sum(z, axis=0, keepdims=True)


def _my_pos():
    return lax.axis_index("x"), lax.axis_index("y"), lax.axis_index("c")


def _allgather(name, shard, in_vmem):
    r, cdim = shard.shape

    def body(x_ref, out_ref, send_sems, recv_sems, local_sem):
        x, y, c = _my_pos()
        me, sibling = (x, y, c), (x, y, 1 - c)
        chips = [(1 - x, y), (x, 1 - y), (1 - x, 1 - y)]

        def slot(px, py, pc):
            return out_ref.at[4 * px + 2 * py + pc]

        def copy(k, block, to, src=None):
            return pltpu.make_async_remote_copy(
                src_ref=slot(*block) if src is None else src, dst_ref=slot(*block),
                send_sem=send_sems.at[k], recv_sem=recv_sems.at[k], device_id=to, device_id_type=MESH_ID)

        mine = pltpu.make_async_copy(x_ref, slot(*me), local_sem)
        mine.start()
        first = [copy(0, me, sibling, src=x_ref)]
        first += [copy(1 + j, me, (*chip, c), src=x_ref) for j, chip in enumerate(chips)]
        for cp in first:
            cp.start()
        passed = [copy(4 + j, (*chip, c), sibling) for j, chip in enumerate(chips)]
        for j, chip in enumerate(chips):
            copy(1 + j, (*chip, c), me).wait_recv()
            passed[j].start()
        copy(0, sibling, me).wait_recv()
        for j, chip in enumerate(chips):
            copy(4 + j, (*chip, 1 - c), me).wait_recv()
        for cp in first + passed:
            cp.wait_send()
        mine.wait()

    space = pltpu.VMEM if in_vmem else pl.ANY
    return pl.pallas_call(
        body, name=name, out_shape=jax.ShapeDtypeStruct((N_DEV, r, cdim), shard.dtype),
        in_specs=[pl.BlockSpec(memory_space=space)], out_specs=pl.BlockSpec(memory_space=space),
        scratch_shapes=[pltpu.SemaphoreType.DMA((7,)), pltpu.SemaphoreType.DMA((7,)), pltpu.SemaphoreType.DMA],
        compiler_params=pltpu.CompilerParams(vmem_limit_bytes=VMEM_LIMIT),
    )(shard)


HBM_SPEC = pl.BlockSpec(memory_space=pltpu.HBM)
SEM_SPEC = pl.BlockSpec(memory_space=pltpu.SEMAPHORE)
ANY_SPEC = pl.BlockSpec(memory_space=pl.ANY)
SPLIT_EFFECT = pltpu.SideEffectType.DATAFLOW_SIDE_EFFECTING
TOKEN_SHAPE = jax.ShapeDtypeStruct((SUBLANES, LANES), F32)


def _hbm(a):
    return pltpu.with_memory_space_constraint(a, pltpu.HBM)


def _peers():
    x, y, c = _my_pos()
    out = []
    for k in range(1, N_DEV):
        px = 1 - x if k & 4 else x
        py = 1 - y if k & 2 else y
        pc = 1 - c if k & 1 else c
        out.append((k, (px, py, pc), 4 * px + 2 * py + pc))
    return out


def _alltoall_start(name, blocks):
    def body(x_ref, land_ref, send_sems, recv_sems, x_thru, land_thru, token):
        del x_thru, land_thru
        x, y, c = _my_pos()
        me = 4 * x + 2 * y + c
        for k, peer, slot in _peers():
            pltpu.make_async_remote_copy(
                src_ref=x_ref.at[slot], dst_ref=land_ref.at[me], send_sem=send_sems.at[k - 1],
                recv_sem=recv_sems.at[k - 1], device_id=peer, device_id_type=MESH_ID).start()
        token[...] = jnp.zeros_like(token)

    sems = pltpu.SemaphoreType.DMA((N_DEV - 1,))
    buf = pltpu.HBM(blocks.shape, blocks.dtype)
    return pl.pallas_call(
        body, name=name, out_shape=(sems, sems, buf, buf, TOKEN_SHAPE), in_specs=(HBM_SPEC, HBM_SPEC),
        out_specs=(SEM_SPEC, SEM_SPEC, HBM_SPEC, HBM_SPEC, pl.BlockSpec(memory_space=pltpu.VMEM)),
        input_output_aliases={0: 2, 1: 3}, compiler_params=pltpu.CompilerParams(has_side_effects=SPLIT_EFFECT),
    )(_hbm(blocks), _hbm(lax.empty(blocks.shape, blocks.dtype)))


def _alltoall_wait(name, send_sems, recv_sems, x_thru, land_thru, after):
    def body(x_ref, land_ref, send_sems, recv_sems, after_ref, x_dead, got_ref):
        del after_ref, x_dead, got_ref
        for k, peer, slot in _peers():
            cp = pltpu.make_async_remote_copy(
                src_ref=x_ref.at[slot], dst_ref=land_ref.at[slot], send_sem=send_sems.at[k - 1],
                recv_sem=recv_sems.at[k - 1], device_id=peer, device_id_type=MESH_ID)
            cp.wait_send()
            cp.wait_recv()

    buf = pltpu.HBM(x_thru.shape, x_thru.dtype)
    return pl.pallas_call(
        body, name=name, out_shape=(buf, buf), in_specs=(HBM_SPEC, HBM_SPEC, SEM_SPEC, SEM_SPEC, ANY_SPEC),
        out_specs=(HBM_SPEC, HBM_SPEC), input_output_aliases={0: 0, 1: 1},
        compiler_params=pltpu.CompilerParams(has_side_effects=SPLIT_EFFECT),
    )(x_thru, land_thru, send_sems, recv_sems, after)


def _gather_start(name, shard, land):
    def body(x_ref, land_ref, send_sems, recv_sems, x_thru, land_thru, token):
        del x_thru, land_thru
        x, y, c = _my_pos()
        me = 4 * x + 2 * y + c
        targets = [(x, y, 1 - c), (1 - x, y, c), (x, 1 - y, c), (1 - x, 1 - y, c)]
        for k, peer in enumerate(targets):
            pltpu.make_async_remote_copy(
                src_ref=x_ref, dst_ref=land_ref.at[me], send_sem=send_sems.at[k], recv_sem=recv_sems.at[k],
                device_id=peer, device_id_type=MESH_ID).start()
        token[...] = jnp.zeros_like(token)

    sems = pltpu.SemaphoreType.DMA((4,))
    return pl.pallas_call(
        body, name=name,
        out_shape=(sems, sems, pltpu.HBM(shard.shape, shard.dtype), pltpu.HBM(land.shape, land.dtype), TOKEN_SHAPE),
        in_specs=(HBM_SPEC, HBM_SPEC),
        out_specs=(SEM_SPEC, SEM_SPEC, HBM_SPEC, HBM_SPEC, pl.BlockSpec(memory_space=pltpu.VMEM)),
        input_output_aliases={0: 2, 1: 3}, compiler_params=pltpu.CompilerParams(has_side_effects=SPLIT_EFFECT),
    )(_hbm(shard), _hbm(land))


def _gather_wait(name, send_sems, recv_sems, x_thru, land_thru, after):
    def body(x_ref, land_ref, send_sems, recv_sems, after_ref, x_dead, got_ref):
        del after_ref, x_dead, got_ref
        x, y, c = _my_pos()
        sources = [(x, y, 1 - c), (1 - x, y, c), (x, 1 - y, c), (1 - x, 1 - y, c)]
        for k, (px, py, pc) in enumerate(sources):
            cp = pltpu.make_async_remote_copy(
                src_ref=x_ref, dst_ref=land_ref.at[4 * px + 2 * py + pc], send_sem=send_sems.at[k],
                recv_sem=recv_sems.at[k], device_id=(px, py, pc), device_id_type=MESH_ID)
            cp.wait_send()
            cp.wait_recv()

    return pl.pallas_call(
        body, name=name, out_shape=(pltpu.HBM(x_thru.shape, x_thru.dtype), pltpu.HBM(land_thru.shape, land_thru.dtype)),
        in_specs=(HBM_SPEC, HBM_SPEC, SEM_SPEC, SEM_SPEC, ANY_SPEC), out_specs=(HBM_SPEC, HBM_SPEC),
        input_output_aliases={0: 0, 1: 1}, compiler_params=pltpu.CompilerParams(has_side_effects=SPLIT_EFFECT),
    )(x_thru, land_thru, send_sems, recv_sems, after)[1]


def _gather_forward(name, land):
    def body(x_ref, out_ref, send_sems, recv_sems):
        x, y, c = _my_pos()
        chips = [(1 - x, y), (x, 1 - y), (1 - x, 1 - y)]
        sends = []
        for j, (px, py) in enumerate(chips):
            mine = 4 * px + 2 * py + c
            cp = pltpu.make_async_remote_copy(
                src_ref=x_ref.at[mine], dst_ref=out_ref.at[mine], send_sem=send_sems.at[j], recv_sem=recv_sems.at[j],
                device_id=(x, y, 1 - c), device_id_type=MESH_ID)
            cp.start()
            sends.append(cp)
        for j, (px, py) in enumerate(chips):
            theirs = 4 * px + 2 * py + (1 - c)
            pltpu.make_async_remote_copy(
                src_ref=x_ref.at[theirs], dst_ref=out_ref.at[theirs], send_sem=send_sems.at[j], recv_sem=recv_sems.at[j],
                device_id=(x, y, 1 - c), device_id_type=MESH_ID).wait_recv()
        for cp in sends:
            cp.wait_send()

    return pl.pallas_call(
        body, name=name, out_shape=jax.ShapeDtypeStruct(land.shape, land.dtype), in_specs=[ANY_SPEC], out_specs=ANY_SPEC,
        scratch_shapes=[pltpu.SemaphoreType.DMA((3,)), pltpu.SemaphoreType.DMA((3,))], input_output_aliases={0: 0},
    )(land)


def _pack(arrays):
    parts = []
    for a in arrays:
        flat = a.reshape(-1).astype(F32)
        pad = (-flat.shape[0]) % BLOB_ALIGN
        parts.append(jnp.pad(flat, (0, pad)) if pad else flat)
    return jnp.concatenate(parts).reshape(-1, LANES)


def _unpack(blob, shapes):
    flat = blob.reshape(-1)
    out, off = [], 0
    for shp in shapes:
        n = 1
        for s in shp:
            n *= s
        out.append(flat[off:off + n].reshape(shp))
        off += n + (-n) % BLOB_ALIGN
    return out


def _cast_bf16(name, w):
    r, cdim = w.shape
    tr = _rows_tile(r, cdim)

    def body(w_ref, o_ref):
        o_ref[...] = w_ref[...].astype(BF16)

    return pl.pallas_call(
        body, name=name, grid=(r // tr,), out_shape=jax.ShapeDtypeStruct(w.shape, BF16),
        in_specs=[pl.BlockSpec((tr, cdim), lambda i: (i, 0))], out_specs=pl.BlockSpec((tr, cdim), lambda i: (i, 0)),
        compiler_params=_cp(("parallel",)))(w)


def _adam_math(w, g, m, v):
    m = ADAM_B1 * m + (1.0 - ADAM_B1) * g
    v = ADAM_B2 * v + (1.0 - ADAM_B2) * (g * g)
    m_hat = m / (1.0 - ADAM_B1 ** ADAM_STEP)
    v_hat = v / (1.0 - ADAM_B2 ** ADAM_STEP)
    delta = -ADAM_LR * (m_hat / (jnp.sqrt(v_hat) + ADAM_EPS) + ADAM_WD * w)
    return delta, m, v


def _adamw(name, w, m, v, *, grad=None, slots=None, own=None, me=None, ca_t=None, dmod=None):
    r, cdim = w.shape
    tr = _rows_tile(r, cdim)
    row = pl.BlockSpec((tr, cdim), lambda i, *_: (i, 0))
    prefetch = []
    if grad is not None:
        srcs, src_specs = [grad], [row]
    elif slots is not None:
        prefetch = [jnp.reshape(me, (1,)).astype(jnp.int32)]
        srcs = [slots, own]
        src_specs = [pl.BlockSpec((N_DEV, tr, cdim), lambda i, me_ref: (0, i, 0)),
                     pl.BlockSpec((None, tr, cdim), lambda i, me_ref: (me_ref[0], i, 0))]
    else:
        srcs = [ca_t, dmod]
        src_specs = [pl.BlockSpec((tr, ca_t.shape[1]), lambda i: (i, 0)), pl.BlockSpec(dmod.shape, lambda i: (0, 0))]
    n_src = len(srcs)
    n_pre = len(prefetch)

    def body(*refs):
        pre, refs = refs[:n_pre], refs[n_pre:]
        w_ref, m_ref, v_ref = refs[n_src:n_src + 3]
        g_ref, d_ref, nm_ref, nv_ref = refs[n_src + 3:]
        if grad is not None:
            g = refs[0][...]
        elif slots is not None:
            mine = pre[0][0]
            own_f = refs[1][...].astype(F32)
            g = jnp.where(mine == 0, own_f, refs[0][0].astype(F32))
            for s in range(1, N_DEV):
                g = g + jnp.where(mine == s, own_f, refs[0][s].astype(F32))
        else:
            g = jnp.dot(refs[0][...], refs[1][...], preferred_element_type=F32)
        delta, nm, nv = _adam_math(w_ref[...], g, m_ref[...], v_ref[...])
        g_ref[...] = g
        d_ref[...] = delta
        nm_ref[...] = nm
        nv_ref[...] = nv

    shp = jax.ShapeDtypeStruct(w.shape, F32)
    grid_spec = pltpu.PrefetchScalarGridSpec(
        num_scalar_prefetch=n_pre, grid=(r // tr,), in_specs=[*src_specs, row, row, row], out_specs=(row,) * 4)
    return pl.pallas_call(
        body, name=name, grid_spec=grid_spec, out_shape=(shp,) * 4, compiler_params=_cp(("parallel",)))(*prefetch, *srcs, w, m, v)


def _sum_slots(name, gathered):
    _, r, cdim = gathered.shape
    tr = _rows_tile(r, cdim * N_DEV)

    def body(g_ref, o_ref):
        acc = g_ref[0]
        for s in range(1, N_DEV):
            acc = acc + g_ref[s]
        o_ref[...] = acc

    return pl.pallas_call(
        body, name=name, grid=(r // tr,), out_shape=jax.ShapeDtypeStruct((r, cdim), F32),
        in_specs=[pl.BlockSpec((N_DEV, tr, cdim), lambda i: (0, i, 0))], out_specs=pl.BlockSpec((tr, cdim), lambda i: (i, 0)),
        compiler_params=_cp(("parallel",)))(gathered)


def _col_sums(name, a):
    r, cdim = a.shape

    def body(a_ref, o_ref):
        o_ref[...] = _colsum(a_ref[...])

    return pl.pallas_call(
        body, name=name, out_shape=jax.ShapeDtypeStruct((1, cdim), F32),
        in_specs=[pl.BlockSpec(memory_space=pltpu.VMEM)], out_specs=pl.BlockSpec(memory_space=pltpu.VMEM),
        compiler_params=_cp())(a)


def _mm(name, a, b, *, mode, grid, a_spec, b_spec, acc_shape, outs, out_specs, epilogue, extra=(), extra_specs=(), after=()):
    nk = grid[-1]
    dims = {"nn": ((1,), (0,)), "nt": ((1,), (1,)), "tn": ((0,), (0,))}[mode]
    n_extra, n_out, n_after = len(extra), len(outs), len(after)

    def body(*refs):
        a_ref, b_ref = refs[0], refs[1]
        extra_refs = refs[2:2 + n_extra]
        out_refs = refs[2 + n_extra + n_after:2 + n_extra + n_after + n_out]
        part = lax.dot_general(a_ref[...], b_ref[...], (dims, ((), ())), preferred_element_type=F32)
        if nk == 1:
            epilogue(part, extra_refs, out_refs)
        else:
            acc = refs[-1]
            k = pl.program_id(len(grid) - 1)

            @pl.when(k == 0)
            def _():
                acc[...] = part

            @pl.when(k > 0)
            def _():
                acc[...] += part

            @pl.when(k == nk - 1)
            def _():
                epilogue(acc[...], extra_refs, out_refs)

    scratch = [pltpu.VMEM(acc_shape, F32)] if nk > 1 else []
    sem = ("parallel",) * (len(grid) - 1) + ("arbitrary",)
    return pl.pallas_call(
        body, name=name, grid=grid, in_specs=[a_spec, b_spec, *extra_specs, *([ANY_SPEC] * n_after)], out_specs=out_specs,
        out_shape=outs, scratch_shapes=scratch, compiler_params=_cp(sem))(a, b, *extra, *after)


def _store_bf16(acc, extra_refs, out_refs):
    out_refs[0][...] = acc.astype(BF16)


def _store_f32(acc, extra_refs, out_refs):
    out_refs[0][...] = acc


def _residual_epilogue(acc, extra_refs, out_refs):
    x_ref, gate_ref = extra_refs
    out_refs[0][...] = acc
    out_refs[1][...] = x_ref[...] + gate_ref[...] * acc


def kernel(x, c, w_ada, b_ada, g_mix, w_in, conv_w, conv_b, conv_ln_g, conv_ln_b, sgu_ln_g, sgu_ln_b, sgu_w, sgu_b, out_g_conv, out_g_sgu, w_out, g_ffn, w_up, ffn_conv_w, ffn_conv_b, w_down, g_final, loss_target, m_w_ada, m_b_ada, m_g_mix, m_w_in, m_conv_w, m_conv_b, m_conv_ln_g, m_conv_ln_b, m_sgu_ln_g, m_sgu_ln_b, m_sgu_w, m_sgu_b, m_out_g_conv, m_out_g_sgu, m_w_out, m_g_ffn, m_w_up, m_ffn_conv_w, m_ffn_conv_b, m_w_down, m_g_final, v_w_ada, v_b_ada, v_g_mix, v_w_in, v_conv_w, v_conv_b, v_conv_ln_g, v_conv_ln_b, v_sgu_ln_g, v_sgu_ln_b, v_sgu_w, v_sgu_b, v_out_g_conv, v_out_g_sgu, v_w_out, v_g_ffn, v_w_up, v_ffn_conv_w, v_ffn_conv_b, v_w_down, v_g_final):
    bsz, seq, d = x.shape
    t = bsz * seq
    n_batch = bsz * N_DEV
    ada_sh = w_ada.shape[2]
    n_mod = ada_sh * N_DEV // d
    win_sh = w_in.shape[2]
    kc = conv_w.shape[1]
    dc_sh = conv_w.shape[2]
    dc = dc_sh * N_DEV
    heads, ch = sgu_w.shape[1], sgu_w.shape[2]
    hd = dc // heads
    wout_sh = w_out.shape[1]
    cup = w_up.shape[2]
    kf = ffn_conv_w.shape[1]
    cw_sh = ffn_conv_w.shape[2]
    dff = cw_sh * N_DEV
    n_fs = dff // cup
    assert win_sh * N_DEV == 4 * dc and hd == LANES and kc // 2 < HALO and kf == 3 and 2 * n_fs == N_DEV
    assert w_down.shape[1] * 2 == cup and seq % ch == 0

    xi, yi, ci = _my_pos()
    me = 4 * xi + 2 * yi + ci

    tb = min(256, seq)
    tps = seq // tb
    nt = t // tb
    tbd = min(128, seq)
    tpd = seq // tbd
    ntd = t // tbd
    tm = min(512, seq)
    nm = t // tm

    xf = x.reshape(t, d)
    tgt = loss_target.reshape(t, d)

    def gather_begin(name, w):
        shard = _cast_bf16("cast_" + name, w[0])
        land = lax.dynamic_update_slice(lax.empty((N_DEV,) + shard.shape, BF16), shard[None], (me, 0, 0))
        return _gather_start("ag1_" + name, shard, land)

    def gather_end(name, started, after):
        send_sems, recv_sems, x_thru, land_thru, _ = started
        land = _gather_wait("ag1w_" + name, send_sems, recv_sems, x_thru, land_thru, after)
        return _gather_forward("ag2_" + name, land)

    ag_in = gather_begin("w_in", w_in)
    ag_out = gather_begin("w_out", w_out)
    ag_up = gather_begin("w_up", w_up)
    ag_down = gather_begin("w_down", w_down)
    started = ag_in[4][0, 0] + ag_out[4][0, 0] + ag_up[4][0, 0] + ag_down[4][0, 0]
    c = c + started

    small_shapes = [(bsz, d), (kc, dc_sh), (kf, cw_sh)]
    blob0 = _allgather("ag_small_in", _pack([c, conv_w[0], ffn_conv_w[0]]), True)
    per_dev = [_unpack(blob0[s], small_shapes) for s in range(N_DEV)]
    c_all = jnp.concatenate([p[0] for p in per_dev], axis=0)
    conv_w_full = jnp.concatenate([p[1] for p in per_dev], axis=1)
    ffn_w_full = jnp.concatenate([p[2] for p in per_dev], axis=1)
    ffn_w_s = ffn_w_full.reshape(kf, n_fs, cup).transpose(1, 0, 2)
    ffn_b_s = ffn_conv_b.reshape(n_fs, 1, cup)

    b_ada_sh = lax.dynamic_slice(b_ada, (0, me * ada_sh), (1, ada_sh))
    tn_ada = _blk(ada_sh, 512)

    def ada_body(c_ref, w_ref, b_ref, mod_ref, ca_ref):
        cc = c_ref[...]
        ca = (cc * _sigmoid(cc)).astype(BF16)
        ca_ref[...] = ca
        mod_ref[...] = jnp.dot(ca, w_ref[...].astype(BF16), preferred_element_type=F32) + b_ref[...]

    mod_sh, c_act = pl.pallas_call(
        ada_body, name="ada_fwd", grid=(ada_sh // tn_ada,),
        out_shape=(jax.ShapeDtypeStruct((n_batch, ada_sh), F32), jax.ShapeDtypeStruct((n_batch, d), BF16)),
        in_specs=[pl.BlockSpec((n_batch, d), lambda j: (0, 0)), pl.BlockSpec((d, tn_ada), lambda j: (0, j)),
                  pl.BlockSpec((1, tn_ada), lambda j: (0, j))],
        out_specs=(pl.BlockSpec((n_batch, tn_ada), lambda j: (0, j)), pl.BlockSpec((n_batch, d), lambda j: (0, 0))),
        compiler_params=_cp(("arbitrary",)))(c_all, w_ada[0], b_ada_sh)
    mod_all = _allgather("ag_mod", mod_sh, True)
    mod_me = lax.dynamic_slice(mod_all, (0, me * bsz, 0), (N_DEV, bsz, ada_sh))
    mod_me = mod_me.transpose(1, 0, 2).reshape(bsz, n_mod, 1, d)
    shift_m, scale_m, gate_m = mod_me[:, 0], mod_me[:, 1], mod_me[:, 2]
    shift_f, scale_f, gate_f = mod_me[:, 3], mod_me[:, 4], mod_me[:, 5]

    row_d = pl.BlockSpec((tbd, d), lambda i: (i, 0))
    vec_d = pl.BlockSpec((1, d), lambda i: (0, 0))
    seq_d = pl.BlockSpec((None, 1, d), lambda i: (i // tpd, 0, 0))

    def modulate(name, xin, g, shift, scale):
        def body(x_ref, g_ref, sh_ref, sc_ref, h_ref):
            xx = x_ref[...]
            yy = xx * lax.rsqrt(_mean(xx * xx) + EPS)
            h_ref[...] = ((yy * g_ref[...]) * (1.0 + sc_ref[...]) + sh_ref[...]).astype(BF16)

        return pl.pallas_call(
            body, name=name, grid=(ntd,), out_shape=jax.ShapeDtypeStruct((t, d), BF16),
            in_specs=[row_d, vec_d, seq_d, seq_d], out_specs=row_d, compiler_params=_cp(("parallel",)))(xin, g, shift, scale)

    h1 = modulate("mod1_fwd", xf, g_mix, shift_m, scale_m)

    tk_d = _blk(d, 1024)
    w_in_g = gather_end("w_in", ag_in, h1)
    proj = _mm(
        "mm_proj", h1, w_in_g, mode="nn", grid=(nm, N_DEV, 1),
        a_spec=pl.BlockSpec((tm, d), lambda i, j, k: (i, 0)),
        b_spec=pl.BlockSpec((None, d, win_sh), lambda i, j, k: (j, 0, 0)),
        acc_shape=(tm, win_sh), outs=(jax.ShapeDtypeStruct((t, N_DEV * win_sh), BF16),),
        out_specs=(pl.BlockSpec((tm, win_sh), lambda i, j, k: (i, j)),), epilogue=_store_bf16)[0]

    hb = tb // HALO
    n_hb = t // HALO

    def main_col(col):
        return pl.BlockSpec((tb, dc), lambda i, col=col: (i, col))

    def chunk_col(col):
        return pl.BlockSpec((ch, dc), lambda i, col=col: (i, col))

    def prev_col(col):
        return pl.BlockSpec((HALO, dc), lambda i, col=col: (jnp.maximum(i * hb - 1, 0), col))

    def next_col(col):
        return pl.BlockSpec((HALO, dc), lambda i, col=col: (jnp.minimum((i + 1) * hb, n_hb - 1), col))

    vec_c = pl.BlockSpec((1, dc), lambda i: (0, 0))
    row_c = pl.BlockSpec((tb, dc), lambda i: (i, 0))
    convw_spec = pl.BlockSpec((kc, dc), lambda i: (0, 0))
    sguw_spec = pl.BlockSpec((heads, ch, ch), lambda i: (0, 0, 0))
    bias_spec = pl.BlockSpec((ch, dc), lambda i: (0, 0))
    n_chunk = tb // ch

    sgu_w_bf = sgu_w[0].astype(BF16)
    sgu_wt_bf = jnp.swapaxes(sgu_w[0], 1, 2).astype(BF16)
    sgu_bias = jnp.repeat(sgu_b[0].T, hd, axis=1)

    def fill_glu_ext(ext, i, pv, pg, pvp, pgp, pvn, pgn):
        first = (i % tps) == 0
        last = (i % tps) == tps - 1

        def glu(v_ref, g_ref):
            return v_ref[...].astype(F32) * _sigmoid(g_ref[...].astype(F32))

        ext[pl.ds(0, HALO), :] = jnp.where(first, 0.0, glu(pvp, pgp))
        ext[pl.ds(HALO, tb), :] = glu(pv, pg)
        ext[pl.ds(HALO + tb, HALO), :] = jnp.where(last, 0.0, glu(pvn, pgn))

    def layer_norm_stats(z):
        mu = _mean(z)
        zc = z - mu
        rstd = lax.rsqrt(_mean(zc * zc) + EPS)
        return zc * rstd, rstd

    def sgu_mix(vs_ref, w_ref, vnb, bias_ref, n_chunks):
        for cc in range(n_chunks):
            for h in range(heads):
                blk = jnp.dot(w_ref[h], vnb[cc * ch:(cc + 1) * ch, h * hd:(h + 1) * hd], preferred_element_type=F32)
                vs_ref[pl.ds(cc * ch, ch), pl.ds(h * hd, hd)] = blk + bias_ref[:, pl.ds(h * hd, hd)]

    def mix_fwd_body(pv, pg, pu, pw, pvp, pgp, pvn, pgn, cw, cb, lg, lb, og, slg, slb, sw, sbias, sog, y_ref, a1_ref, ext, vs_ref):
        i = pl.program_id(0)
        fill_glu_ext(ext, i, pv, pg, pvp, pgp, pvn, pgn)
        pad = kc // 2
        acc = jnp.zeros((tb, dc), F32) + cb[...]
        for k in range(kc):
            acc = acc + cw[pl.ds(k, 1), :] * ext[pl.ds(HALO - pad + k, tb), :]
        a1_ref[...] = acc
        xh, _ = layer_norm_stats(acc)
        a2 = xh * lg[...] + lb[...]
        a3 = a2 * _sigmoid(a2)
        ya = a3 * lax.rsqrt(_mean(a3 * a3) + EPS) * og[...]
        y_ref[:, pl.ds(0, dc)] = ya.astype(BF16)

        u = _gelu(pu[...].astype(F32))
        vv = _gelu(pw[...].astype(F32))
        xhv, _ = layer_norm_stats(vv)
        vn = xhv * slg[...] + slb[...]
        sgu_mix(vs_ref, sw, vn.astype(BF16), sbias, n_chunk)
        bg = u * vs_ref[...]
        yb = bg * lax.rsqrt(_mean(bg * bg) + EPS) * sog[...]
        y_ref[:, pl.ds(dc, dc)] = yb.astype(BF16)

    y, a1 = pl.pallas_call(
        mix_fwd_body, name="mix_fwd", grid=(nt,),
        out_shape=(jax.ShapeDtypeStruct((t, 2 * dc), BF16), jax.ShapeDtypeStruct((t, dc), F32)),
        in_specs=[main_col(0), main_col(1), main_col(2), main_col(3), prev_col(0), prev_col(1), next_col(0), next_col(1),
                  convw_spec, vec_c, vec_c, vec_c, vec_c, vec_c, vec_c, sguw_spec, bias_spec, vec_c],
        out_specs=(pl.BlockSpec((tb, 2 * dc), lambda i: (i, 0)), row_c),
        scratch_shapes=[pltpu.VMEM((tb + 2 * HALO, dc), F32), pltpu.VMEM((tb, dc), F32)],
        compiler_params=_cp(("parallel",)),
    )(proj, proj, proj, proj, proj, proj, proj, proj, conv_w_full, conv_b, conv_ln_g, conv_ln_b, out_g_conv,
      sgu_ln_g, sgu_ln_b, sgu_w_bf, sgu_bias, out_g_sgu)

    tn_d = _blk(d, 1024)
    gate_spec3 = pl.BlockSpec((None, 1, tn_d), lambda i, j, k: (i * tm // seq, 0, j))
    res_spec3 = pl.BlockSpec((tm, tn_d), lambda i, j, k: (i, j))
    td_shape = jax.ShapeDtypeStruct((t, d), F32)

    w_out_full = gather_end("w_out", ag_out, y).reshape(d, d)
    o_mix, x1 = _mm(
        "mm_out", y, w_out_full, mode="nn", grid=(nm, d // tn_d, 1),
        a_spec=pl.BlockSpec((tm, d), lambda i, j, k: (i, 0)), b_spec=pl.BlockSpec((d, tn_d), lambda i, j, k: (0, j)),
        acc_shape=(tm, tn_d), outs=(td_shape, td_shape), out_specs=(res_spec3, res_spec3),
        epilogue=_residual_epilogue, extra=(xf, gate_m), extra_specs=(res_spec3, gate_spec3))

    h2 = modulate("mod2_fwd", x1, g_ffn, shift_f, scale_f)

    tm_up = tm
    nk_d = d // tk_d
    w_up_g = gather_end("w_up", ag_up, h2)
    up = _mm(
        "mm_up", h2, w_up_g, mode="nn", grid=(t // tm_up, N_DEV, nk_d),
        a_spec=pl.BlockSpec((tm_up, tk_d), lambda i, j, k: (i, k)),
        b_spec=pl.BlockSpec((None, tk_d, cup), lambda i, j, k: (j, k, 0)),
        acc_shape=(tm_up, cup), outs=(jax.ShapeDtypeStruct((N_DEV, t, cup), BF16),),
        out_specs=(pl.BlockSpec((None, tm_up, cup), lambda i, j, k: (j, i, 0)),), epilogue=_store_bf16)[0]
    up4 = up.reshape(2, n_fs, t, cup)

    def ffn_halo(which, lo):
        if lo:
            return pl.BlockSpec((None, None, HALO, cup), lambda j, i: (which, j, jnp.maximum(i * hb - 1, 0), 0))
        return pl.BlockSpec((None, None, HALO, cup), lambda j, i: (which, j, jnp.minimum((i + 1) * hb, n_hb - 1), 0))

    pair_spec = pl.BlockSpec((2, None, tb, cup), lambda j, i: (0, j, i, 0))
    fw_spec = pl.BlockSpec((None, kf, cup), lambda j, i: (j, 0, 0))
    fb_spec = pl.BlockSpec((None, 1, cup), lambda j, i: (j, 0, 0))
    act_spec = pl.BlockSpec((None, tb, cup), lambda j, i: (j, i, 0))

    def fill_gate_ext(ext, i, main_f32, prev_ref, next_ref):
        first = (i % tps) == 0
        last = (i % tps) == tps - 1
        ext[pl.ds(0, HALO), :] = jnp.where(first, 0.0, prev_ref[...].astype(F32))
        ext[pl.ds(HALO, tb), :] = main_f32
        ext[pl.ds(HALO + tb, HALO), :] = jnp.where(last, 0.0, next_ref[...].astype(F32))

    def ffn_fwd_body(pair, gp, gn, fw, fb, act_ref, ext):
        i = pl.program_id(1)
        fill_gate_ext(ext, i, pair[0].astype(F32), gp, gn)
        gte = fb[...] + fw[pl.ds(0, 1), :] * ext[pl.ds(HALO - 1, tb), :]
        gte = gte + fw[pl.ds(1, 1), :] * ext[pl.ds(HALO, tb), :]
        gte = gte + fw[pl.ds(2, 1), :] * ext[pl.ds(HALO + 1, tb), :]
        act_ref[...] = (gte * _sigmoid(gte) * pair[1].astype(F32)).astype(BF16)

    act = pl.pallas_call(
        ffn_fwd_body, name="ffn_fwd", grid=(n_fs, nt), out_shape=jax.ShapeDtypeStruct((n_fs, t, cup), BF16),
        in_specs=[pair_spec, ffn_halo(0, True), ffn_halo(0, False), fw_spec, fb_spec], out_specs=act_spec,
        scratch_shapes=[pltpu.VMEM((tb + 2 * HALO, cup), F32)], compiler_params=_cp(("parallel", "parallel")),
    )(up4, up4, up4, ffn_w_s, ffn_b_s)

    w_down_s = gather_end("w_down", ag_down, act).reshape(n_fs, cup, d)
    gate_spec_f = pl.BlockSpec((None, 1, tn_d), lambda i, j, k: (i * tm // seq, 0, j))
    dn, x2 = _mm(
        "mm_down", act, w_down_s, mode="nn", grid=(nm, d // tn_d, n_fs),
        a_spec=pl.BlockSpec((None, tm, cup), lambda i, j, k: (k, i, 0)),
        b_spec=pl.BlockSpec((None, cup, tn_d), lambda i, j, k: (k, 0, j)),
        acc_shape=(tm, tn_d), outs=(td_shape, td_shape), out_specs=(res_spec3, res_spec3),
        epilogue=_residual_epilogue, extra=(x1, gate_f), extra_specs=(res_spec3, gate_spec_f))

    acc_d = pl.BlockSpec((1, d), lambda i: (0, 0))
    seq_acc = pl.BlockSpec((None, 1, d), lambda i: (i // tpd, 0, 0))

    def head_body(x_ref, t_ref, g_ref, dn_ref, gate_ref, dx_ref, ddn_ref, loss_ref, dg_ref, dgate_ref):
        i = pl.program_id(0)
        xx = x_ref[...]
        rr = lax.rsqrt(_mean(xx * xx) + EPS)
        xn = xx * rr
        err = xn * g_ref[...] - t_ref[...]
        dyf = err * (1.0 / d)
        dxn = dyf * g_ref[...]
        dx = rr * (dxn - xn * _mean(dxn * xn))
        dx_ref[...] = dx
        ddn_ref[...] = (gate_ref[...] * dx).astype(BF16)
        part = 0.5 * jnp.sum(_mean(err * err), axis=0, keepdims=True)

        @pl.when(i == 0)
        def _():
            loss_ref[...] = jnp.zeros_like(loss_ref)
            dg_ref[...] = jnp.zeros_like(dg_ref)

        @pl.when(i % tpd == 0)
        def _():
            dgate_ref[...] = jnp.zeros_like(dgate_ref)

        loss_ref[...] += jnp.broadcast_to(part, loss_ref.shape)
        dg_ref[...] += _colsum(dyf * xn)
        dgate_ref[...] += _colsum(dx * dn_ref[...])

    seq_shape = jax.ShapeDtypeStruct((bsz, 1, d), F32)
    vec_shape = jax.ShapeDtypeStruct((1, d), F32)
    dx2, ddn, loss_part, dg_final, dgate_f = pl.pallas_call(
        head_body, name="loss_head", grid=(ntd,),
        out_shape=(td_shape, jax.ShapeDtypeStruct((t, d), BF16), jax.ShapeDtypeStruct((1, LANES), F32), vec_shape, seq_shape),
        in_specs=[row_d, row_d, vec_d, row_d, seq_d],
        out_specs=(row_d, row_d, pl.BlockSpec((1, LANES), lambda i: (0, 0)), acc_d, seq_acc),
        compiler_params=_cp(("arbitrary",)))(x2, tgt, g_final.reshape(1, d), dn, gate_f)

    tk_t = min(1024, t)
    nk_t = t // tk_t
    tn_gw = _blk(d, 512)
    gw_down = _mm(
        "mm_gw_down", act, ddn, mode="tn", grid=(n_fs, d // tn_gw, nk_t),
        a_spec=pl.BlockSpec((None, tk_t, cup), lambda j, n, k: (j, k, 0)),
        b_spec=pl.BlockSpec((tk_t, tn_gw), lambda j, n, k: (k, n)),
        acc_shape=(cup, tn_gw), outs=(jax.ShapeDtypeStruct((n_fs, cup, d), BF16),),
        out_specs=(pl.BlockSpec((None, cup, tn_gw), lambda j, n, k: (j, 0, n)),), epilogue=_store_bf16)[0]

    def exchange_begin(name, gw, w):
        return _alltoall_start("a2a_" + name, gw.reshape(N_DEV, *w.shape[1:]))

    x_down = exchange_begin("w_down", gw_down, w_down)
    dact = _mm(
        "mm_dact", ddn, w_down_s, mode="nt", grid=(t // tm_up, n_fs, nk_d),
        a_spec=pl.BlockSpec((tm_up, tk_d), lambda i, j, k: (i, k)),
        b_spec=pl.BlockSpec((None, cup, tk_d), lambda i, j, k: (j, 0, k)),
        acc_shape=(tm_up, cup), outs=(jax.ShapeDtypeStruct((n_fs, t, cup), BF16),),
        out_specs=(pl.BlockSpec((None, tm_up, cup), lambda i, j, k: (j, i, 0)),), epilogue=_store_bf16, after=(x_down[4],))[0]

    ext_e = tb + HALO
    half = HALO // 2

    def ffn_bwd_body(pair, gp, gn, vp, vn, da, dap, dan, fw, fb, dup_ref, dw_ref, db_ref, ext, dg_e):
        i = pl.program_id(1)
        first = (i % tps) == 0
        last = (i % tps) == tps - 1
        fill_gate_ext(ext, i, pair[0].astype(F32), gp, gn)
        w0, w1, w2 = fw[pl.ds(0, 1), :], fw[pl.ds(1, 1), :], fw[pl.ds(2, 1), :]

        def dgte_of(gte, val, dact_v):
            sg = _sigmoid(gte)
            return dact_v * val * (sg * (1.0 + gte * (1.0 - sg))), gte * sg

        def gte_at(lo, n):
            return (fb[...] + w0 * ext[pl.ds(lo - 1, n), :] + w1 * ext[pl.ds(lo, n), :] + w2 * ext[pl.ds(lo + 1, n), :])

        d_lo, _ = dgte_of(gte_at(half, half), vp[...].astype(F32)[half:, :], dap[...].astype(F32)[half:, :])
        dg_e[pl.ds(0, half), :] = jnp.where(first, 0.0, d_lo)
        d_hi, _ = dgte_of(gte_at(HALO + tb, half), vn[...].astype(F32)[:half, :], dan[...].astype(F32)[:half, :])
        dg_e[pl.ds(half + tb, half), :] = jnp.where(last, 0.0, d_hi)
        dact_m = da[...].astype(F32)
        d_mid, silu_m = dgte_of(gte_at(HALO, tb), pair[1].astype(F32), dact_m)
        dg_e[pl.ds(half, tb), :] = d_mid
        dup_ref[1] = (dact_m * silu_m).astype(BF16)
        dgate = w0 * dg_e[pl.ds(half + 1, tb), :] + w1 * d_mid + w2 * dg_e[pl.ds(half - 1, tb), :]
        dup_ref[0] = dgate.astype(BF16)

        @pl.when(i == 0)
        def _():
            dw_ref[...] = jnp.zeros_like(dw_ref)
            db_ref[...] = jnp.zeros_like(db_ref)

        dw_ref[pl.ds(0, 1), :] += _colsum(d_mid * ext[pl.ds(HALO - 1, tb), :])
        dw_ref[pl.ds(1, 1), :] += _colsum(d_mid * ext[pl.ds(HALO, tb), :])
        dw_ref[pl.ds(2, 1), :] += _colsum(d_mid * ext[pl.ds(HALO + 1, tb), :])
        db_ref[...] += _colsum(d_mid)

    def act_halo(lo):
        if lo:
            return pl.BlockSpec((None, HALO, cup), lambda j, i: (j, jnp.maximum(i * hb - 1, 0), 0))
        return pl.BlockSpec((None, HALO, cup), lambda j, i: (j, jnp.minimum((i + 1) * hb, n_hb - 1), 0))

    dup4, g_ffn_w_s, g_ffn_b_s = pl.pallas_call(
        ffn_bwd_body, name="ffn_bwd", grid=(n_fs, nt),
        out_shape=(jax.ShapeDtypeStruct((2, n_fs, t, cup), BF16), jax.ShapeDtypeStruct((n_fs, kf, cup), F32),
                   jax.ShapeDtypeStruct((n_fs, 1, cup), F32)),
        in_specs=[pair_spec, ffn_halo(0, True), ffn_halo(0, False), ffn_halo(1, True), ffn_halo(1, False),
                  act_spec, act_halo(True), act_halo(False), fw_spec, fb_spec],
        out_specs=(pair_spec, fw_spec, fb_spec),
        scratch_shapes=[pltpu.VMEM((tb + 2 * HALO, cup), F32), pltpu.VMEM((ext_e, cup), F32)],
        compiler_params=_cp(("parallel", "arbitrary")),
    )(up4, up4, up4, up4, up4, dact, dact, dact, ffn_w_s, ffn_b_s)
    dup = dup4.reshape(N_DEV, t, cup)

    tm_w = _blk(d, 1024)
    tk_u = min(512, t)
    gw_up = _mm(
        "mm_gw_up", h2, dup, mode="tn", grid=(N_DEV, d // tm_w, t // tk_u),
        a_spec=pl.BlockSpec((tk_u, tm_w), lambda j, i, k: (k, i)),
        b_spec=pl.BlockSpec((None, tk_u, cup), lambda j, i, k: (j, k, 0)),
        acc_shape=(tm_w, cup), outs=(jax.ShapeDtypeStruct((N_DEV, d, cup), BF16),),
        out_specs=(pl.BlockSpec((None, tm_w, cup), lambda j, i, k: (j, i, 0)),), epilogue=_store_bf16)[0]

    x_up = exchange_begin("w_up", gw_up, w_up)
    dh2 = _mm(
        "mm_dh2", dup, w_up_g, mode="nt", grid=(nm, d // tn_d, N_DEV),
        a_spec=pl.BlockSpec((None, tm, cup), lambda i, j, k: (k, i, 0)),
        b_spec=pl.BlockSpec((None, tn_d, cup), lambda i, j, k: (k, j, 0)),
        acc_shape=(tm, tn_d), outs=(td_shape,), out_specs=(res_spec3,), epilogue=_store_f32, after=(x_up[4],))[0]

    def modulate_bwd(name, xin, dh, dres, g, scale, gate=None, branch=None):
        gated = gate is not None

        def body(*refs):
            x_ref, dh_ref, dres_ref, g_ref, sc_ref = refs[:5]
            rest = refs[5:]
            if gated:
                gate_ref, br_ref, dx_ref, dsh_ref, dsc_ref, dg_ref, do_ref, dgate_ref = rest
            else:
                dx_ref, dsh_ref, dsc_ref, dg_ref = rest
            i = pl.program_id(0)
            xx = x_ref[...]
            rr = lax.rsqrt(_mean(xx * xx) + EPS)
            xn = xx * rr
            s1 = 1.0 + sc_ref[...]
            dhh = dh_ref[...]
            dxn = dhh * g_ref[...] * s1
            dx = dres_ref[...] + rr * (dxn - xn * _mean(dxn * xn))
            dx_ref[...] = dx

            @pl.when(i == 0)
            def _():
                dg_ref[...] = jnp.zeros_like(dg_ref)

            @pl.when(i % tpd == 0)
            def _():
                dsh_ref[...] = jnp.zeros_like(dsh_ref)
                dsc_ref[...] = jnp.zeros_like(dsc_ref)
                if gated:
                    dgate_ref[...] = jnp.zeros_like(dgate_ref)

            dsh_ref[...] += _colsum(dhh)
            dsc_ref[...] += _colsum(dhh * (xn * g_ref[...]))
            dg_ref[...] += _colsum(dhh * s1 * xn)
            if gated:
                do_ref[...] = (gate_ref[...] * dx).astype(BF16)
                dgate_ref[...] += _colsum(dx * br_ref[...])

        ins = [xin, dh, dres, g, scale]
        in_specs = [row_d, row_d, row_d, vec_d, seq_d]
        outs = [td_shape, seq_shape, seq_shape, vec_shape]
        out_specs = [row_d, seq_acc, seq_acc, acc_d]
        if gated:
            ins += [gate, branch]
            in_specs += [seq_d, row_d]
            outs += [jax.ShapeDtypeStruct((t, d), BF16), seq_shape]
            out_specs += [row_d, seq_acc]
        return pl.pallas_call(
            body, name=name, grid=(ntd,), out_shape=tuple(outs), in_specs=in_specs, out_specs=tuple(out_specs),
            compiler_params=_cp(("arbitrary",)))(*ins)

    dx1, dshift_f, dscale_f, dg_ffn, d_o, dgate_m = modulate_bwd("mod2_bwd", x1, dh2, dx2, g_ffn, scale_f, gate_m, o_mix)

    tn_w = _blk(d, 2048)
    gw_out = _mm(
        "mm_gw_out", y, d_o, mode="tn", grid=(d // tm_w, d // tn_w, nk_t),
        a_spec=pl.BlockSpec((tk_t, tm_w), lambda i, j, k: (k, i)), b_spec=pl.BlockSpec((tk_t, tn_w), lambda i, j, k: (k, j)),
        acc_shape=(tm_w, tn_w), outs=(jax.ShapeDtypeStruct((d, d), BF16),),
        out_specs=(pl.BlockSpec((tm_w, tn_w), lambda i, j, k: (i, j)),), epilogue=_store_bf16)[0]

    x_out = exchange_begin("w_out", gw_out, w_out)
    dy = _mm(
        "mm_dy", d_o, w_out_full, mode="nt", grid=(nm, d // tn_d, 1),
        a_spec=pl.BlockSpec((tm, d), lambda i, j, k: (i, 0)), b_spec=pl.BlockSpec((tn_d, d), lambda i, j, k: (j, 0)),
        acc_shape=(tm, tn_d), outs=(jax.ShapeDtypeStruct((t, d), BF16),),
        out_specs=(pl.BlockSpec((tm, tn_d), lambda i, j, k: (i, j)),), epilogue=_store_bf16, after=(x_out[4],))[0]

    def conv_norm_bwd_body(dy_ref, a1_ref, lg, lb, og, da1_ref, dog_ref, dlg_ref, dlb_ref):
        i = pl.program_id(0)
        xh, rstd = layer_norm_stats(a1_ref[...])
        a2 = xh * lg[...] + lb[...]
        sg = _sigmoid(a2)
        a3 = a2 * sg
        r3 = lax.rsqrt(_mean(a3 * a3) + EPS)
        n3 = a3 * r3
        dya = dy_ref[...].astype(F32)
        dn3 = dya * og[...]
        da3 = r3 * (dn3 - n3 * _mean(dn3 * n3))
        da2 = da3 * (sg * (1.0 + a2 * (1.0 - sg)))
        dxh = da2 * lg[...]
        da1_ref[...] = rstd * (dxh - _mean(dxh) - xh * _mean(dxh * xh))

        @pl.when(i == 0)
        def _():
            dog_ref[...] = jnp.zeros_like(dog_ref)
            dlg_ref[...] = jnp.zeros_like(dlg_ref)
            dlb_ref[...] = jnp.zeros_like(dlb_ref)

        dog_ref[...] += _colsum(dya * n3)
        dlg_ref[...] += _colsum(da2 * xh)
        dlb_ref[...] += _colsum(da2)

    vecc_shape = jax.ShapeDtypeStruct((1, dc), F32)
    da1, g_og_conv, g_cln_g, g_cln_b = pl.pallas_call(
        conv_norm_bwd_body, name="conv_norm_bwd", grid=(nt,),
        out_shape=(jax.ShapeDtypeStruct((t, dc), F32), vecc_shape, vecc_shape, vecc_shape),
        in_specs=[main_col(0), row_c, vec_c, vec_c, vec_c], out_specs=(row_c, vec_c, vec_c, vec_c),
        compiler_params=_cp(("arbitrary",)))(dy, a1, conv_ln_g, conv_ln_b, out_g_conv)

    def sgu_bwd_body(dy_ref, pu, pw, slg, slb, sw, swt, sbias, sog, dproj_ref, dsog_ref, dslg_ref, dslb_ref, dsw_ref, dsb_ref, vs_ref, dvn_ref):
        i = pl.program_id(0)
        zu = pu[...].astype(F32)
        zv = pw[...].astype(F32)
        u = _gelu(zu)
        vv = _gelu(zv)
        xhv, rstd = layer_norm_stats(vv)
        vnb = (xhv * slg[...] + slb[...]).astype(BF16)
        sgu_mix(vs_ref, sw, vnb, sbias, 1)
        vs = vs_ref[...]
        bg = u * vs
        rb = lax.rsqrt(_mean(bg * bg) + EPS)
        nb = bg * rb
        dyb = dy_ref[...].astype(F32)
        dnb = dyb * sog[...]
        dbg = rb * (dnb - nb * _mean(dnb * nb))
        du = dbg * vs
        dvs = dbg * u
        dvsb = dvs.astype(BF16)

        @pl.when(i == 0)
        def _():
            dsog_ref[...] = jnp.zeros_like(dsog_ref)
            dslg_ref[...] = jnp.zeros_like(dslg_ref)
            dslb_ref[...] = jnp.zeros_like(dslb_ref)
            dsw_ref[...] = jnp.zeros_like(dsw_ref)
            dsb_ref[...] = jnp.zeros_like(dsb_ref)

        dsog_ref[...] += _colsum(dyb * nb)
        dsb_ref[...] += dvs
        for h in range(heads):
            dblk = dvsb[:, h * hd:(h + 1) * hd]
            vblk = vnb[:, h * hd:(h + 1) * hd]
            dsw_ref[h] += lax.dot_general(dblk, vblk, (((1,), (1,)), ((), ())), preferred_element_type=F32)
            dvn_ref[:, pl.ds(h * hd, hd)] = jnp.dot(swt[h], dblk, preferred_element_type=F32)
        dvn = dvn_ref[...]
        dslg_ref[...] += _colsum(dvn * xhv)
        dslb_ref[...] += _colsum(dvn)
        dxh = dvn * slg[...]
        dvv = rstd * (dxh - _mean(dxh) - xhv * _mean(dxh * xhv))
        dproj_ref[:, pl.ds(0, dc)] = (du * _gelu_grad(zu)).astype(BF16)
        dproj_ref[:, pl.ds(dc, dc)] = (dvv * _gelu_grad(zv)).astype(BF16)

    dproj, g_og_sgu, g_sln_g, g_sln_b, g_sgu_w, g_sgu_bias = pl.pallas_call(
        sgu_bwd_body, name="sgu_bwd", grid=(t // ch,),
        out_shape=(jax.ShapeDtypeStruct((t, 4 * dc), BF16), vecc_shape, vecc_shape, vecc_shape,
                   jax.ShapeDtypeStruct((heads, ch, ch), F32), jax.ShapeDtypeStruct((ch, dc), F32)),
        in_specs=[chunk_col(1), chunk_col(2), chunk_col(3), vec_c, vec_c, sguw_spec, sguw_spec, bias_spec, vec_c],
        out_specs=(pl.BlockSpec((ch, 2 * dc), lambda i: (i, 1)), vec_c, vec_c, vec_c, sguw_spec, bias_spec),
        scratch_shapes=[pltpu.VMEM((ch, dc), F32), pltpu.VMEM((ch, dc), F32)],
        compiler_params=_cp(("arbitrary",)),
    )(dy, proj, proj, sgu_ln_g, sgu_ln_b, sgu_w_bf, sgu_wt_bf, sgu_bias, out_g_sgu)

    def sgu_bias_reduce_body(b_ref, o_ref):
        lane = lax.broadcasted_iota(jnp.int32, (ch, LANES), 1)
        res = jnp.zeros((ch, LANES), F32)
        for h in range(heads):
            res = jnp.where(lane == h, jnp.sum(b_ref[:, pl.ds(h * hd, hd)], axis=1, keepdims=True), res)
        o_ref[...] = res

    g_sgu_b_t = pl.pallas_call(
        sgu_bias_reduce_body, name="sgu_bias_reduce", out_shape=jax.ShapeDtypeStruct((ch, LANES), F32),
        in_specs=[pl.BlockSpec(memory_space=pltpu.VMEM)], out_specs=pl.BlockSpec(memory_space=pltpu.VMEM),
        compiler_params=_cp())(g_sgu_bias)
    g_sgu_b = g_sgu_b_t[:, :heads].T

    def conv_bwd_body(da, dap, dan, pv, pg, pvp, pgp, pvn, pgn, cw, dproj_in, dproj_ref, dcw_ref, dcb_ref, ext, dext):
        del dproj_in
        i = pl.program_id(0)
        first = (i % tps) == 0
        last = (i % tps) == tps - 1
        fill_glu_ext(ext, i, pv, pg, pvp, pgp, pvn, pgn)
        da_m = da[...]
        dext[pl.ds(0, HALO), :] = jnp.where(first, 0.0, dap[...])
        dext[pl.ds(HALO, tb), :] = da_m
        dext[pl.ds(HALO + tb, HALO), :] = jnp.where(last, 0.0, dan[...])
        pad = kc // 2

        @pl.when(i == 0)
        def _():
            dcw_ref[...] = jnp.zeros_like(dcw_ref)
            dcb_ref[...] = jnp.zeros_like(dcb_ref)

        da0 = jnp.zeros((tb, dc), F32)
        for k in range(kc):
            da0 = da0 + cw[pl.ds(k, 1), :] * dext[pl.ds(HALO + pad - k, tb), :]
            dcw_ref[pl.ds(k, 1), :] += _colsum(da_m * ext[pl.ds(HALO - pad + k, tb), :])
        dcb_ref[...] += _colsum(da_m)
        vv = pv[...].astype(F32)
        sg = _sigmoid(pg[...].astype(F32))
        dproj_ref[:, pl.ds(0, dc)] = (da0 * sg).astype(BF16)
        dproj_ref[:, pl.ds(dc, dc)] = (da0 * vv * sg * (1.0 - sg)).astype(BF16)

    def halo_rows(lo):
        if lo:
            return pl.BlockSpec((HALO, dc), lambda i: (jnp.maximum(i * hb - 1, 0), 0))
        return pl.BlockSpec((HALO, dc), lambda i: (jnp.minimum((i + 1) * hb, n_hb - 1), 0))

    dproj, g_conv_w, g_conv_b = pl.pallas_call(
        conv_bwd_body, name="conv_bwd", grid=(nt,),
        out_shape=(jax.ShapeDtypeStruct((t, 4 * dc), BF16), jax.ShapeDtypeStruct((kc, dc), F32), vecc_shape),
        in_specs=[row_c, halo_rows(True), halo_rows(False), main_col(0), main_col(1), prev_col(0), prev_col(1),
                  next_col(0), next_col(1), convw_spec, pl.BlockSpec(memory_space=pl.ANY)],
        out_specs=(pl.BlockSpec((tb, 2 * dc), lambda i: (i, 0)), convw_spec, vec_c),
        scratch_shapes=[pltpu.VMEM((tb + 2 * HALO, dc), F32), pltpu.VMEM((tb + 2 * HALO, dc), F32)],
        input_output_aliases={10: 0}, compiler_params=_cp(("arbitrary",)),
    )(da1, da1, da1, proj, proj, proj, proj, proj, proj, conv_w_full, dproj)

    gw_in = _mm(
        "mm_gw_in", h1, dproj, mode="tn", grid=(N_DEV, d // tm_w, nk_t),
        a_spec=pl.BlockSpec((tk_t, tm_w), lambda j, i, k: (k, i)),
        b_spec=pl.BlockSpec((tk_t, win_sh), lambda j, i, k: (k, j)),
        acc_shape=(tm_w, win_sh), outs=(jax.ShapeDtypeStruct((N_DEV, d, win_sh), BF16),),
        out_specs=(pl.BlockSpec((None, tm_w, win_sh), lambda j, i, k: (j, i, 0)),), epilogue=_store_bf16)[0]

    x_in = exchange_begin("w_in", gw_in, w_in)
    tn_h = _blk(d, 2048)
    dh1 = _mm(
        "mm_dh1", dproj, w_in_g, mode="nt", grid=(nm, d // tn_h, N_DEV),
        a_spec=pl.BlockSpec((tm, win_sh), lambda i, j, k: (i, k)),
        b_spec=pl.BlockSpec((None, tn_h, win_sh), lambda i, j, k: (k, j, 0)),
        acc_shape=(tm, tn_h), outs=(td_shape,), out_specs=(pl.BlockSpec((tm, tn_h), lambda i, j, k: (i, j)),),
        epilogue=_store_f32, after=(x_in[4],))[0]

    grad_x, dshift_m, dscale_m, dg_mix = modulate_bwd("mod1_bwd", xf, dh1, dx1, g_mix, scale_m)

    dmod = jnp.concatenate([dshift_m, dscale_m, dgate_m, dshift_f, dscale_f, dgate_f], axis=1)
    g_ffn_w_full = g_ffn_w_s.transpose(1, 0, 2).reshape(kf, dff)
    rep_names = ["g_mix", "conv_b", "conv_ln_g", "conv_ln_b", "sgu_ln_g", "sgu_ln_b", "sgu_w", "sgu_b",
                 "out_g_conv", "out_g_sgu", "g_ffn", "ffn_conv_b", "g_final"]
    rep_w = [g_mix, conv_b, conv_ln_g, conv_ln_b, sgu_ln_g, sgu_ln_b, sgu_w, sgu_b, out_g_conv, out_g_sgu, g_ffn, ffn_conv_b, g_final]
    rep_m = [m_g_mix, m_conv_b, m_conv_ln_g, m_conv_ln_b, m_sgu_ln_g, m_sgu_ln_b, m_sgu_w, m_sgu_b, m_out_g_conv, m_out_g_sgu,
             m_g_ffn, m_ffn_conv_b, m_g_final]
    rep_v = [v_g_mix, v_conv_b, v_conv_ln_g, v_conv_ln_b, v_sgu_ln_g, v_sgu_ln_b, v_sgu_w, v_sgu_b, v_out_g_conv, v_out_g_sgu,
             v_g_ffn, v_ffn_conv_b, v_g_final]
    rep_g = [dg_mix, g_conv_b, g_cln_g, g_cln_b, g_sln_g, g_sln_b, g_sgu_w, g_sgu_b, g_og_conv, g_og_sgu, dg_ffn,
             g_ffn_b_s, dg_final]
    rep_shapes = [w.shape for w in rep_w]
    extra_g = [g_conv_w, g_ffn_w_full, loss_part[:, :1]]
    extra_shapes = [(kc, dc), (kf, dff), (1, 1)]
    sum_blob = _pack(rep_g + extra_g)
    n_sum_rows = sum_blob.shape[0]
    blob1 = jnp.concatenate([sum_blob, _pack([dmod])], axis=0)
    gathered = _allgather("ag_small_grads", blob1, True)
    summed = _sum_slots("sum_small_grads", gathered[:, :n_sum_rows])
    n_rep_rows = _pack(rep_g).shape[0]
    g_conv_w_all, g_ffn_w_all, loss_all = _unpack(summed[n_rep_rows:], extra_shapes)
    loss = loss_all[0, 0]
    dmod_all = gathered[:, n_sum_rows:].reshape(N_DEV, -1)[:, :bsz * n_mod * d].reshape(n_batch, n_mod * d)

    rep_out = _adamw("adamw_small", _pack(rep_w), _pack(rep_m), _pack(rep_v), grad=summed[:n_rep_rows])
    rep_out = [_unpack(o, rep_shapes) for o in rep_out]
    rep = {name: tuple(rep_out[q][p] for q in range(4)) for p, name in enumerate(rep_names)}

    g_conv_w_me = lax.dynamic_slice(g_conv_w_all, (0, me * dc_sh), (kc, dc_sh))
    cw_out = _adamw("adamw_conv_w", conv_w[0], m_conv_w[0], v_conv_w[0], grad=g_conv_w_me)
    g_ffn_w_me = lax.dynamic_slice(g_ffn_w_all, (0, me * cw_sh), (kf, cw_sh))
    fw_out = _adamw("adamw_ffn_conv_w", ffn_conv_w[0], m_ffn_conv_w[0], v_ffn_conv_w[0], grad=g_ffn_w_me)

    g_b_ada = _col_sums("grad_b_ada", dmod_all)
    bada_out = _adamw("adamw_b_ada", b_ada, m_b_ada, v_b_ada, grad=g_b_ada)
    dmod_sh = lax.dynamic_slice(dmod_all, (0, me * ada_sh), (n_batch, ada_sh)).astype(BF16)
    wada_out = _adamw("adamw_w_ada", w_ada[0], m_w_ada[0], v_w_ada[0], ca_t=c_act.T, dmod=dmod_sh)

    def big(name, started, after, w, m, v):
        send_sems, recv_sems, x_thru, land_thru, _ = started
        mine, recv = _alltoall_wait("a2aw_" + name, send_sems, recv_sems, x_thru, land_thru, after)
        return _adamw("adamw_" + name, w[0], m[0], v[0], slots=recv, own=mine, me=me)

    wdown_out = big("w_down", x_down, wada_out[0], w_down, m_w_down, v_w_down)
    wup_out = big("w_up", x_up, wdown_out[0], w_up, m_w_up, v_w_up)
    wout_out = big("w_out", x_out, wup_out[0], w_out, m_w_out, v_w_out)
    win_out = big("w_in", x_in, wout_out[0], w_in, m_w_in, v_w_in)

    def lead(outs4):
        return tuple(o[None] for o in outs4)

    results = {
        "w_ada": lead(wada_out), "b_ada": bada_out, "w_in": lead(win_out), "conv_w": lead(cw_out),
        "w_out": lead(wout_out), "w_up": lead(wup_out), "ffn_conv_w": lead(fw_out), "w_down": lead(wdown_out),
    }
    results.update(rep)
    order = ["w_ada", "b_ada", "g_mix", "w_in", "conv_w", "conv_b", "conv_ln_g", "conv_ln_b", "sgu_ln_g", "sgu_ln_b",
             "sgu_w", "sgu_b", "out_g_conv", "out_g_sgu", "w_out", "g_ffn", "w_up", "ffn_conv_w", "ffn_conv_b", "w_down", "g_final"]
    out = [loss, grad_x.reshape(bsz, seq, d)]
    for q in range(4):
        out += [results[n][q] for n in order]
    return tuple(out)
```

```python
import functools

import jax
import jax.numpy as jnp
from jax import lax
from jax.experimental import pallas as pl
from jax.experimental.pallas import tpu as pltpu

EPS = 1e-6
N_DEV = 8
MESH_ID = pl.DeviceIdType.MESH
V7X_VMEM_BYTES = 64 * 1024 * 1024
VMEM_LIMIT = V7X_VMEM_BYTES - 8 * 1024 * 1024
LANES = 128
SUBLANES = 8
HALO = 16
BLOB_ALIGN = SUBLANES * LANES

ADAM_LR = 0.001
ADAM_B1 = 0.9
ADAM_B2 = 0.999
ADAM_EPS = 1e-08
ADAM_WD = 0.01
ADAM_STEP = 10

F32 = jnp.float32
BF16 = jnp.bfloat16


def _cp(sem=None, **kw):
    return pltpu.CompilerParams(dimension_semantics=sem, vmem_limit_bytes=VMEM_LIMIT, **kw)


def _blk(n, pref):
    return pref if n % pref == 0 else n


def _rows_tile(rows, cols, budget=3 << 19):
    best = None
    for t in range(SUBLANES, rows + 1, SUBLANES):
        if rows % t == 0 and t * cols * 4 <= budget:
            best = t
    return best if best is not None else rows


def _sigmoid(z):
    return 1.0 / (1.0 + jnp.exp(-z))


def _gelu(z):
    return z * (lax.erf(z * 0.7071067811865476) + 1.0) * 0.5


def _gelu_grad(z):
    return 0.5 * (1.0 + lax.erf(z * 0.7071067811865476)) + z * jnp.exp(-0.5 * z * z) * 0.3989422804014327


def _mean(z):
    return jnp.mean(z, axis=-1, keepdims=True)


def _colsum(z):
    return jnp.sum(z, axis=0, keepdims=True)


def _my_pos():
    return lax.axis_index("x"), lax.axis_index("y"), lax.axis_index("c")


def _allgather(name, shard, in_vmem):
    r, cdim = shard.shape

    def body(x_ref, out_ref, send_sems, recv_sems, local_sem):
        x, y, c = _my_pos()
        me, sibling = (x, y, c), (x, y, 1 - c)
        chips = [(1 - x, y), (x, 1 - y), (1 - x, 1 - y)]

        def slot(px, py, pc):
            return out_ref.at[4 * px + 2 * py + pc]

        def copy(k, block, to, src=None):
            return pltpu.make_async_remote_copy(
                src_ref=slot(*block) if src is None else src, dst_ref=slot(*block),
                send_sem=send_sems.at[k], recv_sem=recv_sems.at[k], device_id=to, device_id_type=MESH_ID)

        mine = pltpu.make_async_copy(x_ref, slot(*me), local_sem)
        mine.start()
        first = [copy(0, me, sibling, src=x_ref)]
        first += [copy(1 + j, me, (*chip, c), src=x_ref) for j, chip in enumerate(chips)]
        for cp in first:
            cp.start()
        passed = [copy(4 + j, (*chip, c), sibling) for j, chip in enumerate(chips)]
        for j, chip in enumerate(chips):
            copy(1 + j, (*chip, c), me).wait_recv()
            passed[j].start()
        copy(0, sibling, me).wait_recv()
        for j, chip in enumerate(chips):
            copy(4 + j, (*chip, 1 - c), me).wait_recv()
        for cp in first + passed:
            cp.wait_send()
        mine.wait()

    space = pltpu.VMEM if in_vmem else pl.ANY
    return pl.pallas_call(
        body, name=name, out_shape=jax.ShapeDtypeStruct((N_DEV, r, cdim), shard.dtype),
        in_specs=[pl.BlockSpec(memory_space=space)], out_specs=pl.BlockSpec(memory_space=space),
        scratch_shapes=[pltpu.SemaphoreType.DMA((7,)), pltpu.SemaphoreType.DMA((7,)), pltpu.SemaphoreType.DMA],
        compiler_params=pltpu.CompilerParams(vmem_limit_bytes=VMEM_LIMIT),
    )(shard)


HBM_SPEC = pl.BlockSpec(memory_space=pltpu.HBM)
SEM_SPEC = pl.BlockSpec(memory_space=pltpu.SEMAPHORE)
ANY_SPEC = pl.BlockSpec(memory_space=pl.ANY)
SPLIT_EFFECT = pltpu.SideEffectType.DATAFLOW_SIDE_EFFECTING
TOKEN_SHAPE = jax.ShapeDtypeStruct((SUBLANES, LANES), F32)


def _hbm(a):
    return pltpu.with_memory_space_constraint(a, pltpu.HBM)


def _peers():
    x, y, c = _my_pos()
    out = []
    for k in range(1, N_DEV):
        px = 1 - x if k & 4 else x
        py = 1 - y if k & 2 else y
        pc = 1 - c if k & 1 else c
        out.append((k, (px, py, pc), 4 * px + 2 * py + pc))
    return out


def _alltoall_start(name, blocks):
    def body(x_ref, land_ref, send_sems, recv_sems, x_thru, land_thru, token):
        del x_thru, land_thru
        x, y, c = _my_pos()
        me = 4 * x + 2 * y + c
        for k, peer, slot in _peers():
            pltpu.make_async_remote_copy(
                src_ref=x_ref.at[slot], dst_ref=land_ref.at[me], send_sem=send_sems.at[k - 1],
                recv_sem=recv_sems.at[k - 1], device_id=peer, device_id_type=MESH_ID).start()
        token[...] = jnp.zeros_like(token)

    sems = pltpu.SemaphoreType.DMA((N_DEV - 1,))
    buf = pltpu.HBM(blocks.shape, blocks.dtype)
    return pl.pallas_call(
        body, name=name, out_shape=(sems, sems, buf, buf, TOKEN_SHAPE), in_specs=(HBM_SPEC, HBM_SPEC),
        out_specs=(SEM_SPEC, SEM_SPEC, HBM_SPEC, HBM_SPEC, pl.BlockSpec(memory_space=pltpu.VMEM)),
        input_output_aliases={0: 2, 1: 3}, compiler_params=pltpu.CompilerParams(has_side_effects=SPLIT_EFFECT),
    )(_hbm(blocks), _hbm(lax.empty(blocks.shape, blocks.dtype)))


def _alltoall_wait(name, send_sems, recv_sems, x_thru, land_thru, after):
    def body(x_ref, land_ref, send_sems, recv_sems, after_ref, x_dead, got_ref):
        del after_ref, x_dead, got_ref
        for k, peer, slot in _peers():
            cp = pltpu.make_async_remote_copy(
                src_ref=x_ref.at[slot], dst_ref=land_ref.at[slot], send_sem=send_sems.at[k - 1],
                recv_sem=recv_sems.at[k - 1], device_id=peer, device_id_type=MESH_ID)
            cp.wait_send()
            cp.wait_recv()

    buf = pltpu.HBM(x_thru.shape, x_thru.dtype)
    return pl.pallas_call(
        body, name=name, out_shape=(buf, buf), in_specs=(HBM_SPEC, HBM_SPEC, SEM_SPEC, SEM_SPEC, ANY_SPEC),
        out_specs=(HBM_SPEC, HBM_SPEC), input_output_aliases={0: 0, 1: 1},
        compiler_params=pltpu.CompilerParams(has_side_effects=SPLIT_EFFECT),
    )(x_thru, land_thru, send_sems, recv_sems, after)


def _gather_start(name, shard, land):
    def body(x_ref, land_ref, send_sems, recv_sems, x_thru, land_thru, token):
        del x_thru, land_thru
        x, y, c = _my_pos()
        me = 4 * x + 2 * y + c
        targets = [(x, y, 1 - c), (1 - x, y, c), (x, 1 - y, c), (1 - x, 1 - y, c)]
        for k, peer in enumerate(targets):
            pltpu.make_async_remote_copy(
                src_ref=x_ref, dst_ref=land_ref.at[me], send_sem=send_sems.at[k], recv_sem=recv_sems.at[k],
                device_id=peer, device_id_type=MESH_ID).start()
        token[...] = jnp.zeros_like(token)

    sems = pltpu.SemaphoreType.DMA((4,))
    return pl.pallas_call(
        body, name=name,
        out_shape=(sems, sems, pltpu.HBM(shard.shape, shard.dtype), pltpu.HBM(land.shape, land.dtype), TOKEN_SHAPE),
        in_specs=(HBM_SPEC, HBM_SPEC),
        out_specs=(SEM_SPEC, SEM_SPEC, HBM_SPEC, HBM_SPEC, pl.BlockSpec(memory_space=pltpu.VMEM)),
        input_output_aliases={0: 2, 1: 3}, compiler_params=pltpu.CompilerParams(has_side_effects=SPLIT_EFFECT),
    )(_hbm(shard), _hbm(land))


def _gather_wait(name, send_sems, recv_sems, x_thru, land_thru, after):
    def body(x_ref, land_ref, send_sems, recv_sems, after_ref, x_dead, got_ref):
        del after_ref, x_dead, got_ref
        x, y, c = _my_pos()
        sources = [(x, y, 1 - c), (1 - x, y, c), (x, 1 - y, c), (1 - x, 1 - y, c)]
        for k, (px, py, pc) in enumerate(sources):
            cp = pltpu.make_async_remote_copy(
                src_ref=x_ref, dst_ref=land_ref.at[4 * px + 2 * py + pc], send_sem=send_sems.at[k],
                recv_sem=recv_sems.at[k], device_id=(px, py, pc), device_id_type=MESH_ID)
            cp.wait_send()
            cp.wait_recv()

    return pl.pallas_call(
        body, name=name, out_shape=(pltpu.HBM(x_thru.shape, x_thru.dtype), pltpu.HBM(land_thru.shape, land_thru.dtype)),
        in_specs=(HBM_SPEC, HBM_SPEC, SEM_SPEC, SEM_SPEC, ANY_SPEC), out_specs=(HBM_SPEC, HBM_SPEC),
        input_output_aliases={0: 0, 1: 1}, compiler_params=pltpu.CompilerParams(has_side_effects=SPLIT_EFFECT),
    )(x_thru, land_thru, send_sems, recv_sems, after)[1]


def _gather_forward(name, land):
    def body(x_ref, out_ref, send_sems, recv_sems):
        x, y, c = _my_pos()
        chips = [(1 - x, y), (x, 1 - y), (1 - x, 1 - y)]
        sends = []
        for j, (px, py) in enumerate(chips):
            mine = 4 * px + 2 * py + c
            cp = pltpu.make_async_remote_copy(
                src_ref=x_ref.at[mine], dst_ref=out_ref.at[mine], send_sem=send_sems.at[j], recv_sem=recv_sems.at[j],
                device_id=(x, y, 1 - c), device_id_type=MESH_ID)
            cp.start()
            sends.append(cp)
        for j, (px, py) in enumerate(chips):
            theirs = 4 * px + 2 * py + (1 - c)
            pltpu.make_async_remote_copy(
                src_ref=x_ref.at[theirs], dst_ref=out_ref.at[theirs], send_sem=send_sems.at[j], recv_sem=recv_sems.at[j],
                device_id=(x, y, 1 - c), device_id_type=MESH_ID).wait_recv()
        for cp in sends:
            cp.wait_send()

    return pl.pallas_call(
        body, name=name, out_shape=jax.ShapeDtypeStruct(land.shape, land.dtype), in_specs=[ANY_SPEC], out_specs=ANY_SPEC,
        scratch_shapes=[pltpu.SemaphoreType.DMA((3,)), pltpu.SemaphoreType.DMA((3,))], input_output_aliases={0: 0},
    )(land)


def _pack(arrays):
    parts = []
    for a in arrays:
        flat = a.reshape(-1).astype(F32)
        pad = (-flat.shape[0]) % BLOB_ALIGN
        parts.append(jnp.pad(flat, (0, pad)) if pad else flat)
    return jnp.concatenate(parts).reshape(-1, LANES)


def _unpack(blob, shapes):
    flat = blob.reshape(-1)
    out, off = [], 0
    for shp in shapes:
        n = 1
        for s in shp:
            n *= s
        out.append(flat[off:off + n].reshape(shp))
        off += n + (-n) % BLOB_ALIGN
    return out


def _cast_bf16(name, w, after):
    r, cdim = w.shape
    tr = _rows_tile(r, cdim)

    def body(w_ref, after_ref, o_ref):
        del after_ref
        o_ref[...] = w_ref[...].astype(BF16)

    return pl.pallas_call(
        body, name=name, grid=(r // tr,), out_shape=jax.ShapeDtypeStruct(w.shape, BF16),
        in_specs=[pl.BlockSpec((tr, cdim), lambda i: (i, 0)), ANY_SPEC], out_specs=pl.BlockSpec((tr, cdim), lambda i: (i, 0)),
        compiler_params=_cp(("parallel",)))(w, after)


def _adam_math(w, g, m, v):
    m = ADAM_B1 * m + (1.0 - ADAM_B1) * g
    v = ADAM_B2 * v + (1.0 - ADAM_B2) * (g * g)
    m_hat = m / (1.0 - ADAM_B1 ** ADAM_STEP)
    v_hat = v / (1.0 - ADAM_B2 ** ADAM_STEP)
    delta = -ADAM_LR * (m_hat / (jnp.sqrt(v_hat) + ADAM_EPS) + ADAM_WD * w)
    return delta, m, v


def _adamw(name, w, m, v, *, grad=None, slots=None, own=None, me=None, ca_t=None, dmod=None):
    r, cdim = w.shape
    tr = _rows_tile(r, cdim)
    row = pl.BlockSpec((tr, cdim), lambda i, *_: (i, 0))
    prefetch = []
    if grad is not None:
        srcs, src_specs = [grad], [row]
    elif slots is not None:
        prefetch = [jnp.reshape(me, (1,)).astype(jnp.int32)]
        srcs = [slots, own]
        src_specs = [pl.BlockSpec((N_DEV, tr, cdim), lambda i, me_ref: (0, i, 0)),
                     pl.BlockSpec((None, tr, cdim), lambda i, me_ref: (me_ref[0], i, 0))]
    else:
        srcs = [ca_t, dmod]
        src_specs = [pl.BlockSpec((tr, ca_t.shape[1]), lambda i: (i, 0)), pl.BlockSpec(dmod.shape, lambda i: (0, 0))]
    n_src = len(srcs)
    n_pre = len(prefetch)

    def body(*refs):
        pre, refs = refs[:n_pre], refs[n_pre:]
        w_ref, m_ref, v_ref = refs[n_src:n_src + 3]
        g_ref, d_ref, nm_ref, nv_ref = refs[n_src + 3:]
        if grad is not None:
            g = refs[0][...]
        elif slots is not None:
            mine = pre[0][0]
            own_f = refs[1][...].astype(F32)
            g = jnp.where(mine == 0, own_f, refs[0][0].astype(F32))
            for s in range(1, N_DEV):
                g = g + jnp.where(mine == s, own_f, refs[0][s].astype(F32))
        else:
            g = jnp.dot(refs[0][...], refs[1][...], preferred_element_type=F32)
        delta, nm, nv = _adam_math(w_ref[...], g, m_ref[...], v_ref[...])
        g_ref[...] = g
        d_ref[...] = delta
        nm_ref[...] = nm
        nv_ref[...] = nv

    shp = jax.ShapeDtypeStruct(w.shape, F32)
    grid_spec = pltpu.PrefetchScalarGridSpec(
        num_scalar_prefetch=n_pre, grid=(r // tr,), in_specs=[*src_specs, row, row, row], out_specs=(row,) * 4)
    return pl.pallas_call(
        body, name=name, grid_spec=grid_spec, out_shape=(shp,) * 4, compiler_params=_cp(("parallel",)))(*prefetch, *srcs, w, m, v)


def _sum_slots(name, gathered):
    _, r, cdim = gathered.shape
    tr = _rows_tile(r, cdim * N_DEV)

    def body(g_ref, o_ref):
        acc = g_ref[0]
        for s in range(1, N_DEV):
            acc = acc + g_ref[s]
        o_ref[...] = acc

    return pl.pallas_call(
        body, name=name, grid=(r // tr,), out_shape=jax.ShapeDtypeStruct((r, cdim), F32),
        in_specs=[pl.BlockSpec((N_DEV, tr, cdim), lambda i: (0, i, 0))], out_specs=pl.BlockSpec((tr, cdim), lambda i: (i, 0)),
        compiler_params=_cp(("parallel",)))(gathered)


def _col_sums(name, a):
    r, cdim = a.shape

    def body(a_ref, o_ref):
        o_ref[...] = _colsum(a_ref[...])

    return pl.pallas_call(
        body, name=name, out_shape=jax.ShapeDtypeStruct((1, cdim), F32),
        in_specs=[pl.BlockSpec(memory_space=pltpu.VMEM)], out_specs=pl.BlockSpec(memory_space=pltpu.VMEM),
        compiler_params=_cp())(a)


def _mm(name, a, b, *, mode, grid, a_spec, b_spec, acc_shape, outs, out_specs, epilogue, extra=(), extra_specs=(), after=()):
    nk = grid[-1]
    dims = {"nn": ((1,), (0,)), "nt": ((1,), (1,)), "tn": ((0,), (0,))}[mode]
    n_extra, n_out, n_after = len(extra), len(outs), len(after)

    def body(*refs):
        a_ref, b_ref = refs[0], refs[1]
        extra_refs = refs[2:2 + n_extra]
        out_refs = refs[2 + n_extra + n_after:2 + n_extra + n_after + n_out]
        def part():
            return lax.dot_general(a_ref[...], b_ref[...], (dims, ((), ())), preferred_element_type=F32)

        if nk == 1:
            epilogue(part(), extra_refs, out_refs)
        else:
            acc = refs[-1]
            k = pl.program_id(len(grid) - 1)

            @pl.when(k == 0)
            def _():
                acc[...] = part()

            @pl.when(jnp.logical_and(k > 0, k < nk - 1))
            def _():
                acc[...] += part()

            @pl.when(k == nk - 1)
            def _():
                epilogue(acc[...] + part(), extra_refs, out_refs)

    scratch = [pltpu.VMEM(acc_shape, F32)] if nk > 1 else []
    sem = ("parallel",) * (len(grid) - 1) + ("arbitrary",)
    return pl.pallas_call(
        body, name=name, grid=grid, in_specs=[a_spec, b_spec, *extra_specs, *([ANY_SPEC] * n_after)], out_specs=out_specs,
        out_shape=outs, scratch_shapes=scratch, compiler_params=_cp(sem))(a, b, *extra, *after)


def _store_bf16(acc, extra_refs, out_refs):
    out_refs[0][...] = acc.astype(BF16)


def _store_f32(acc, extra_refs, out_refs):
    out_refs[0][...] = acc


def _residual_epilogue(acc, extra_refs, out_refs):
    x_ref, gate_ref = extra_refs
    out_refs[0][...] = acc
    out_refs[1][...] = x_ref[...] + gate_ref[...] * acc


def kernel(x, c, w_ada, b_ada, g_mix, w_in, conv_w, conv_b, conv_ln_g, conv_ln_b, sgu_ln_g, sgu_ln_b, sgu_w, sgu_b, out_g_conv, out_g_sgu, w_out, g_ffn, w_up, ffn_conv_w, ffn_conv_b, w_down, g_final, loss_target, m_w_ada, m_b_ada, m_g_mix, m_w_in, m_conv_w, m_conv_b, m_conv_ln_g, m_conv_ln_b, m_sgu_ln_g, m_sgu_ln_b, m_sgu_w, m_sgu_b, m_out_g_conv, m_out_g_sgu, m_w_out, m_g_ffn, m_w_up, m_ffn_conv_w, m_ffn_conv_b, m_w_down, m_g_final, v_w_ada, v_b_ada, v_g_mix, v_w_in, v_conv_w, v_conv_b, v_conv_ln_g, v_conv_ln_b, v_sgu_ln_g, v_sgu_ln_b, v_sgu_w, v_sgu_b, v_out_g_conv, v_out_g_sgu, v_w_out, v_g_ffn, v_w_up, v_ffn_conv_w, v_ffn_conv_b, v_w_down, v_g_final):
    bsz, seq, d = x.shape
    t = bsz * seq
    n_batch = bsz * N_DEV
    ada_sh = w_ada.shape[2]
    n_mod = ada_sh * N_DEV // d
    win_sh = w_in.shape[2]
    kc = conv_w.shape[1]
    dc_sh = conv_w.shape[2]
    dc = dc_sh * N_DEV
    heads, ch = sgu_w.shape[1], sgu_w.shape[2]
    hd = dc // heads
    wout_sh = w_out.shape[1]
    cup = w_up.shape[2]
    kf = ffn_conv_w.shape[1]
    cw_sh = ffn_conv_w.shape[2]
    dff = cw_sh * N_DEV
    n_fs = dff // cup
    assert win_sh * N_DEV == 4 * dc and hd == LANES and kc // 2 < HALO and kf == 3 and 2 * n_fs == N_DEV
    assert w_down.shape[1] * 2 == cup and seq % ch == 0

    xi, yi, ci = _my_pos()
    me = 4 * xi + 2 * yi + ci

    tb = min(256, seq)
    tps = seq // tb
    nt = t // tb
    tbd = min(128, seq)
    tpd = seq // tbd
    ntd = t // tbd
    tm = min(512, seq)
    nm = t // tm

    xf = x.reshape(t, d)
    tgt = loss_target.reshape(t, d)

    small_shapes = [(bsz, d), (kc, dc_sh), (kf, cw_sh)]
    blob0 = _allgather("ag_small_in", _pack([c, conv_w[0], ffn_conv_w[0]]), True)
    per_dev = [_unpack(blob0[s], small_shapes) for s in range(N_DEV)]
    c_all = jnp.concatenate([p[0] for p in per_dev], axis=0)
    conv_w_full = jnp.concatenate([p[1] for p in per_dev], axis=1)
    ffn_w_full = jnp.concatenate([p[2] for p in per_dev], axis=1)
    ffn_w_s = ffn_w_full.reshape(kf, n_fs, cup).transpose(1, 0, 2)
    ffn_b_s = ffn_conv_b.reshape(n_fs, 1, cup)

    b_ada_sh = lax.dynamic_slice(b_ada, (0, me * ada_sh), (1, ada_sh))
    tn_ada = _blk(ada_sh, 512)

    def ada_body(c_ref, w_ref, b_ref, mod_ref, ca_ref):
        cc = c_ref[...]
        ca = (cc * _sigmoid(cc)).astype(BF16)
        ca_ref[...] = ca
        mod_ref[...] = jnp.dot(ca, w_ref[...].astype(BF16), preferred_element_type=F32) + b_ref[...]

    mod_sh, c_act = pl.pallas_call(
        ada_body, name="ada_fwd", grid=(ada_sh // tn_ada,),
        out_shape=(jax.ShapeDtypeStruct((n_batch, ada_sh), F32), jax.ShapeDtypeStruct((n_batch, d), BF16)),
        in_specs=[pl.BlockSpec((n_batch, d), lambda j: (0, 0)), pl.BlockSpec((d, tn_ada), lambda j: (0, j)),
                  pl.BlockSpec((1, tn_ada), lambda j: (0, j))],
        out_specs=(pl.BlockSpec((n_batch, tn_ada), lambda j: (0, j)), pl.BlockSpec((n_batch, d), lambda j: (0, 0))),
        compiler_params=_cp(("arbitrary",)))(c_all, w_ada[0], b_ada_sh)
    mod_all = _allgather("ag_mod", mod_sh, True)
    mod_me = lax.dynamic_slice(mod_all, (0, me * bsz, 0), (N_DEV, bsz, ada_sh))
    mod_me = mod_me.transpose(1, 0, 2).reshape(bsz, n_mod, 1, d)
    shift_m, scale_m, gate_m = mod_me[:, 0], mod_me[:, 1], mod_me[:, 2]
    shift_f, scale_f, gate_f = mod_me[:, 3], mod_me[:, 4], mod_me[:, 5]

    def gather_begin(name, w):
        shard = _cast_bf16("cast_" + name, w[0], mod_all)
        land = lax.dynamic_update_slice(lax.empty((N_DEV,) + shard.shape, BF16), shard[None], (me, 0, 0))
        return _gather_start("ag1_" + name, shard, land)

    def gather_end(name, started, after):
        send_sems, recv_sems, x_thru, land_thru, _ = started
        land = _gather_wait("ag1w_" + name, send_sems, recv_sems, x_thru, land_thru, after)
        return _gather_forward("ag2_" + name, land)

    ag_in = gather_begin("w_in", w_in)
    ag_out = gather_begin("w_out", w_out)
    ag_up = gather_begin("w_up", w_up)
    ag_down = gather_begin("w_down", w_down)
    started = ag_in[4][0, 0] + ag_out[4][0, 0] + ag_up[4][0, 0] + ag_down[4][0, 0]

    row_d = pl.BlockSpec((tbd, d), lambda i: (i, 0))
    vec_d = pl.BlockSpec((1, d), lambda i: (0, 0))
    seq_d = pl.BlockSpec((None, 1, d), lambda i: (i // tpd, 0, 0))

    def modulate(name, xin, g, shift, scale):
        def body(x_ref, g_ref, sh_ref, sc_ref, h_ref):
            xx = x_ref[...]
            yy = xx * lax.rsqrt(_mean(xx * xx) + EPS)
            h_ref[...] = ((yy * g_ref[...]) * (1.0 + sc_ref[...]) + sh_ref[...]).astype(BF16)

        return pl.pallas_call(
            body, name=name, grid=(ntd,), out_shape=jax.ShapeDtypeStruct((t, d), BF16),
            in_specs=[row_d, vec_d, seq_d, seq_d], out_specs=row_d, compiler_params=_cp(("parallel",)))(xin, g, shift, scale)

    h1 = modulate("mod1_fwd", xf, g_mix + started, shift_m, scale_m)

    tk_d = _blk(d, 1024)
    w_in_g = gather_end("w_in", ag_in, h1)
    proj = _mm(
        "mm_proj", h1, w_in_g, mode="nn", grid=(nm, N_DEV, 1),
        a_spec=pl.BlockSpec((tm, d), lambda i, j, k: (i, 0)),
        b_spec=pl.BlockSpec((None, d, win_sh), lambda i, j, k: (j, 0, 0)),
        acc_shape=(tm, win_sh), outs=(jax.ShapeDtypeStruct((t, N_DEV * win_sh), BF16),),
        out_specs=(pl.BlockSpec((tm, win_sh), lambda i, j, k: (i, j)),), epilogue=_store_bf16)[0]

    hb = tb // HALO
    n_hb = t // HALO

    def main_col(col):
        return pl.BlockSpec((tb, dc), lambda i, col=col: (i, col))

    def chunk_col(col):
        return pl.BlockSpec((ch, dc), lambda i, col=col: (i, col))

    def prev_col(col):
        return pl.BlockSpec((HALO, dc), lambda i, col=col: (jnp.maximum(i * hb - 1, 0), col))

    def next_col(col):
        return pl.BlockSpec((HALO, dc), lambda i, col=col: (jnp.minimum((i + 1) * hb, n_hb - 1), col))

    vec_c = pl.BlockSpec((1, dc), lambda i: (0, 0))
    row_c = pl.BlockSpec((tb, dc), lambda i: (i, 0))
    convw_spec = pl.BlockSpec((kc, dc), lambda i: (0, 0))
    sguw_spec = pl.BlockSpec((heads, ch, ch), lambda i: (0, 0, 0))
    bias_spec = pl.BlockSpec((ch, dc), lambda i: (0, 0))
    n_chunk = tb // ch

    sgu_w_bf = sgu_w[0].astype(BF16)
    sgu_wt_bf = jnp.swapaxes(sgu_w[0], 1, 2).astype(BF16)
    sgu_bias = jnp.repeat(sgu_b[0].T, hd, axis=1)

    def fill_glu_ext(ext, i, pv, pg, pvp, pgp, pvn, pgn):
        first = (i % tps) == 0
        last = (i % tps) == tps - 1

        def glu(v_ref, g_ref):
            return v_ref[...].astype(F32) * _sigmoid(g_ref[...].astype(F32))

        ext[pl.ds(0, HALO), :] = jnp.where(first, 0.0, glu(pvp, pgp))
        ext[pl.ds(HALO, tb), :] = glu(pv, pg)
        ext[pl.ds(HALO + tb, HALO), :] = jnp.where(last, 0.0, glu(pvn, pgn))

    def layer_norm_stats(z):
        mu = _mean(z)
        zc = z - mu
        rstd = lax.rsqrt(_mean(zc * zc) + EPS)
        return zc * rstd, rstd

    def sgu_mix(vs_ref, w_ref, vnb, bias_ref, n_chunks):
        for cc in range(n_chunks):
            for h in range(heads):
                blk = jnp.dot(w_ref[h], vnb[cc * ch:(cc + 1) * ch, h * hd:(h + 1) * hd], preferred_element_type=F32)
                vs_ref[pl.ds(cc * ch, ch), pl.ds(h * hd, hd)] = blk + bias_ref[:, pl.ds(h * hd, hd)]

    def mix_fwd_body(pv, pg, pu, pw, pvp, pgp, pvn, pgn, cw, cb, lg, lb, og, slg, slb, sw, sbias, sog, y_ref, a1_ref, ext, vs_ref):
        i = pl.program_id(0)
        fill_glu_ext(ext, i, pv, pg, pvp, pgp, pvn, pgn)
        pad = kc // 2
        acc = jnp.zeros((tb, dc), F32) + cb[...]
        for k in range(kc):
            acc = acc + cw[pl.ds(k, 1), :] * ext[pl.ds(HALO - pad + k, tb), :]
        a1_ref[...] = acc
        xh, _ = layer_norm_stats(acc)
        a2 = xh * lg[...] + lb[...]
        a3 = a2 * _sigmoid(a2)
        ya = a3 * lax.rsqrt(_mean(a3 * a3) + EPS) * og[...]
        y_ref[:, pl.ds(0, dc)] = ya.astype(BF16)

        u = _gelu(pu[...].astype(F32))
        vv = _gelu(pw[...].astype(F32))
        xhv, _ = layer_norm_stats(vv)
        vn = xhv * slg[...] + slb[...]
        sgu_mix(vs_ref, sw, vn.astype(BF16), sbias, n_chunk)
        bg = u * vs_ref[...]
        yb = bg * lax.rsqrt(_mean(bg * bg) + EPS) * sog[...]
        y_ref[:, pl.ds(dc, dc)] = yb.astype(BF16)

    y, a1 = pl.pallas_call(
        mix_fwd_body, name="mix_fwd", grid=(nt,),
        out_shape=(jax.ShapeDtypeStruct((t, 2 * dc), BF16), jax.ShapeDtypeStruct((t, dc), F32)),
        in_specs=[main_col(0), main_col(1), main_col(2), main_col(3), prev_col(0), prev_col(1), next_col(0), next_col(1),
                  convw_spec, vec_c, vec_c, vec_c, vec_c, vec_c, vec_c, sguw_spec, bias_spec, vec_c],
        out_specs=(pl.BlockSpec((tb, 2 * dc), lambda i: (i, 0)), row_c),
        scratch_shapes=[pltpu.VMEM((tb + 2 * HALO, dc), F32), pltpu.VMEM((tb, dc), F32)],
        compiler_params=_cp(("parallel",)),
    )(proj, proj, proj, proj, proj, proj, proj, proj, conv_w_full, conv_b, conv_ln_g, conv_ln_b, out_g_conv,
      sgu_ln_g, sgu_ln_b, sgu_w_bf, sgu_bias, out_g_sgu)

    tn_d = _blk(d, 1024)
    gate_spec3 = pl.BlockSpec((None, 1, tn_d), lambda i, j, k: (i * tm // seq, 0, j))
    res_spec3 = pl.BlockSpec((tm, tn_d), lambda i, j, k: (i, j))
    td_shape = jax.ShapeDtypeStruct((t, d), F32)

    w_out_full = gather_end("w_out", ag_out, y).reshape(d, d)
    o_mix, x1 = _mm(
        "mm_out", y, w_out_full, mode="nn", grid=(nm, d // tn_d, 1),
        a_spec=pl.BlockSpec((tm, d), lambda i, j, k: (i, 0)), b_spec=pl.BlockSpec((d, tn_d), lambda i, j, k: (0, j)),
        acc_shape=(tm, tn_d), outs=(td_shape, td_shape), out_specs=(res_spec3, res_spec3),
        epilogue=_residual_epilogue, extra=(xf, gate_m), extra_specs=(res_spec3, gate_spec3))

    h2 = modulate("mod2_fwd", x1, g_ffn, shift_f, scale_f)

    tm_up = tm
    nk_d = d // tk_d
    w_up_g = gather_end("w_up", ag_up, h2)
    up = _mm(
        "mm_up", h2, w_up_g, mode="nn", grid=(t // tm_up, N_DEV, nk_d),
        a_spec=pl.BlockSpec((tm_up, tk_d), lambda i, j, k: (i, k)),
        b_spec=pl.BlockSpec((None, tk_d, cup), lambda i, j, k: (j, k, 0)),
        acc_shape=(tm_up, cup), outs=(jax.ShapeDtypeStruct((N_DEV, t, cup), BF16),),
        out_specs=(pl.BlockSpec((None, tm_up, cup), lambda i, j, k: (j, i, 0)),), epilogue=_store_bf16)[0]
    up4 = up.reshape(2, n_fs, t, cup)

    def ffn_halo(which, lo):
        if lo:
            return pl.BlockSpec((None, None, HALO, cup), lambda j, i: (which, j, jnp.maximum(i * hb - 1, 0), 0))
        return pl.BlockSpec((None, None, HALO, cup), lambda j, i: (which, j, jnp.minimum((i + 1) * hb, n_hb - 1), 0))

    pair_spec = pl.BlockSpec((2, None, tb, cup), lambda j, i: (0, j, i, 0))
    fw_spec = pl.BlockSpec((None, kf, cup), lambda j, i: (j, 0, 0))
    fb_spec = pl.BlockSpec((None, 1, cup), lambda j, i: (j, 0, 0))
    act_spec = pl.BlockSpec((None, tb, cup), lambda j, i: (j, i, 0))

    def fill_gate_ext(ext, i, main_f32, prev_ref, next_ref):
        first = (i % tps) == 0
        last = (i % tps) == tps - 1
        ext[pl.ds(0, HALO), :] = jnp.where(first, 0.0, prev_ref[...].astype(F32))
        ext[pl.ds(HALO, tb), :] = main_f32
        ext[pl.ds(HALO + tb, HALO), :] = jnp.where(last, 0.0, next_ref[...].astype(F32))

    def ffn_fwd_body(pair, gp, gn, fw, fb, act_ref, ext):
        i = pl.program_id(1)
        fill_gate_ext(ext, i, pair[0].astype(F32), gp, gn)
        gte = fb[...] + fw[pl.ds(0, 1), :] * ext[pl.ds(HALO - 1, tb), :]
        gte = gte + fw[pl.ds(1, 1), :] * ext[pl.ds(HALO, tb), :]
        gte = gte + fw[pl.ds(2, 1), :] * ext[pl.ds(HALO + 1, tb), :]
        act_ref[...] = (gte * _sigmoid(gte) * pair[1].astype(F32)).astype(BF16)

    act = pl.pallas_call(
        ffn_fwd_body, name="ffn_fwd", grid=(n_fs, nt), out_shape=jax.ShapeDtypeStruct((n_fs, t, cup), BF16),
        in_specs=[pair_spec, ffn_halo(0, True), ffn_halo(0, False), fw_spec, fb_spec], out_specs=act_spec,
        scratch_shapes=[pltpu.VMEM((tb + 2 * HALO, cup), F32)], compiler_params=_cp(("parallel", "parallel")),
    )(up4, up4, up4, ffn_w_s, ffn_b_s)

    w_down_s = gather_end("w_down", ag_down, act).reshape(n_fs, cup, d)
    gate_spec_f = pl.BlockSpec((None, 1, tn_d), lambda i, j, k: (i * tm // seq, 0, j))
    dn, x2 = _mm(
        "mm_down", act, w_down_s, mode="nn", grid=(nm, d // tn_d, n_fs),
        a_spec=pl.BlockSpec((None, tm, cup), lambda i, j, k: (k, i, 0)),
        b_spec=pl.BlockSpec((None, cup, tn_d), lambda i, j, k: (k, 0, j)),
        acc_shape=(tm, tn_d), outs=(td_shape, td_shape), out_specs=(res_spec3, res_spec3),
        epilogue=_residual_epilogue, extra=(x1, gate_f), extra_specs=(res_spec3, gate_spec_f))

    acc_d = pl.BlockSpec((1, d), lambda i: (0, 0))
    seq_acc = pl.BlockSpec((None, 1, d), lambda i: (i // tpd, 0, 0))

    def head_body(x_ref, t_ref, g_ref, dn_ref, gate_ref, dx_ref, ddn_ref, loss_ref, dg_ref, dgate_ref):
        i = pl.program_id(0)
        xx = x_ref[...]
        rr = lax.rsqrt(_mean(xx * xx) + EPS)
        xn = xx * rr
        err = xn * g_ref[...] - t_ref[...]
        dyf = err * (1.0 / d)
        dxn = dyf * g_ref[...]
        dx = rr * (dxn - xn * _mean(dxn * xn))
        dx_ref[...] = dx
        ddn_ref[...] = (gate_ref[...] * dx).astype(BF16)
        part = 0.5 * jnp.sum(_mean(err * err), axis=0, keepdims=True)

        @pl.when(i == 0)
        def _():
            loss_ref[...] = jnp.zeros_like(loss_ref)
            dg_ref[...] = jnp.zeros_like(dg_ref)

        @pl.when(i % tpd == 0)
        def _():
            dgate_ref[...] = jnp.zeros_like(dgate_ref)

        loss_ref[...] += jnp.broadcast_to(part, loss_ref.shape)
        dg_ref[...] += _colsum(dyf * xn)
        dgate_ref[...] += _colsum(dx * dn_ref[...])

    seq_shape = jax.ShapeDtypeStruct((bsz, 1, d), F32)
    vec_shape = jax.ShapeDtypeStruct((1, d), F32)
    dx2, ddn, loss_part, dg_final, dgate_f = pl.pallas_call(
        head_body, name="loss_head", grid=(ntd,),
        out_shape=(td_shape, jax.ShapeDtypeStruct((t, d), BF16), jax.ShapeDtypeStruct((1, LANES), F32), vec_shape, seq_shape),
        in_specs=[row_d, row_d, vec_d, row_d, seq_d],
        out_specs=(row_d, row_d, pl.BlockSpec((1, LANES), lambda i: (0, 0)), acc_d, seq_acc),
        compiler_params=_cp(("arbitrary",)))(x2, tgt, g_final.reshape(1, d), dn, gate_f)

    tk_t = min(1024, t)
    nk_t = t // tk_t
    tn_gw = _blk(d, 512)
    gw_down = _mm(
        "mm_gw_down", act, ddn, mode="tn", grid=(n_fs, d // tn_gw, nk_t),
        a_spec=pl.BlockSpec((None, tk_t, cup), lambda j, n, k: (j, k, 0)),
        b_spec=pl.BlockSpec((tk_t, tn_gw), lambda j, n, k: (k, n)),
        acc_shape=(cup, tn_gw), outs=(jax.ShapeDtypeStruct((n_fs, cup, d), BF16),),
        out_specs=(pl.BlockSpec((None, cup, tn_gw), lambda j, n, k: (j, 0, n)),), epilogue=_store_bf16)[0]

    def exchange_begin(name, gw, w):
        return _alltoall_start("a2a_" + name, gw.reshape(N_DEV, *w.shape[1:]))

    x_down = exchange_begin("w_down", gw_down, w_down)
    dact = _mm(
        "mm_dact", ddn, w_down_s, mode="nt", grid=(t // tm_up, n_fs, nk_d),
        a_spec=pl.BlockSpec((tm_up, tk_d), lambda i, j, k: (i, k)),
        b_spec=pl.BlockSpec((None, cup, tk_d), lambda i, j, k: (j, 0, k)),
        acc_shape=(tm_up, cup), outs=(jax.ShapeDtypeStruct((n_fs, t, cup), BF16),),
        out_specs=(pl.BlockSpec((None, tm_up, cup), lambda i, j, k: (j, i, 0)),), epilogue=_store_bf16, after=(x_down[4],))[0]

    ext_e = tb + HALO
    half = HALO // 2

    def ffn_bwd_body(pair, gp, gn, vp, vn, da, dap, dan, fw, fb, dup_ref, dw_ref, db_ref, ext, dg_e):
        i = pl.program_id(1)
        first = (i % tps) == 0
        last = (i % tps) == tps - 1
        fill_gate_ext(ext, i, pair[0].astype(F32), gp, gn)
        w0, w1, w2 = fw[pl.ds(0, 1), :], fw[pl.ds(1, 1), :], fw[pl.ds(2, 1), :]

        def dgte_of(gte, val, dact_v):
            sg = _sigmoid(gte)
            return dact_v * val * (sg * (1.0 + gte * (1.0 - sg))), gte * sg

        def gte_at(lo, n):
            return (fb[...] + w0 * ext[pl.ds(lo - 1, n), :] + w1 * ext[pl.ds(lo, n), :] + w2 * ext[pl.ds(lo + 1, n), :])

        d_lo, _ = dgte_of(gte_at(half, half), vp[...].astype(F32)[half:, :], dap[...].astype(F32)[half:, :])
        dg_e[pl.ds(0, half), :] = jnp.where(first, 0.0, d_lo)
        d_hi, _ = dgte_of(gte_at(HALO + tb, half), vn[...].astype(F32)[:half, :], dan[...].astype(F32)[:half, :])
        dg_e[pl.ds(half + tb, half), :] = jnp.where(last, 0.0, d_hi)
        dact_m = da[...].astype(F32)
        d_mid, silu_m = dgte_of(gte_at(HALO, tb), pair[1].astype(F32), dact_m)
        dg_e[pl.ds(half, tb), :] = d_mid
        dup_ref[1] = (dact_m * silu_m).astype(BF16)
        dgate = w0 * dg_e[pl.ds(half + 1, tb), :] + w1 * d_mid + w2 * dg_e[pl.ds(half - 1, tb), :]
        dup_ref[0] = dgate.astype(BF16)

        @pl.when(i == 0)
        def _():
            dw_ref[...] = jnp.zeros_like(dw_ref)
            db_ref[...] = jnp.zeros_like(db_ref)

        dw_ref[pl.ds(0, 1), :] += _colsum(d_mid * ext[pl.ds(HALO - 1, tb), :])
        dw_ref[pl.ds(1, 1), :] += _colsum(d_mid * ext[pl.ds(HALO, tb), :])
        dw_ref[pl.ds(2, 1), :] += _colsum(d_mid * ext[pl.ds(HALO + 1, tb), :])
        db_ref[...] += _colsum(d_mid)

    def act_halo(lo):
        if lo:
            return pl.BlockSpec((None, HALO, cup), lambda j, i: (j, jnp.maximum(i * hb - 1, 0), 0))
        return pl.BlockSpec((None, HALO, cup), lambda j, i: (j, jnp.minimum((i + 1) * hb, n_hb - 1), 0))

    dup4, g_ffn_w_s, g_ffn_b_s = pl.pallas_call(
        ffn_bwd_body, name="ffn_bwd", grid=(n_fs, nt),
        out_shape=(jax.ShapeDtypeStruct((2, n_fs, t, cup), BF16), jax.ShapeDtypeStruct((n_fs, kf, cup), F32),
                   jax.ShapeDtypeStruct((n_fs, 1, cup), F32)),
        in_specs=[pair_spec, ffn_halo(0, True), ffn_halo(0, False), ffn_halo(1, True), ffn_halo(1, False),
                  act_spec, act_halo(True), act_halo(False), fw_spec, fb_spec],
        out_specs=(pair_spec, fw_spec, fb_spec),
        scratch_shapes=[pltpu.VMEM((tb + 2 * HALO, cup), F32), pltpu.VMEM((ext_e, cup), F32)],
        compiler_params=_cp(("parallel", "arbitrary")),
    )(up4, up4, up4, up4, up4, dact, dact, dact, ffn_w_s, ffn_b_s)
    dup = dup4.reshape(N_DEV, t, cup)

    tm_w = _blk(d, 1024)
    tk_u = min(512, t)
    gw_up = _mm(
        "mm_gw_up", h2, dup, mode="tn", grid=(N_DEV, d // tm_w, t // tk_u),
        a_spec=pl.BlockSpec((tk_u, tm_w), lambda j, i, k: (k, i)),
        b_spec=pl.BlockSpec((None, tk_u, cup), lambda j, i, k: (j, k, 0)),
        acc_shape=(tm_w, cup), outs=(jax.ShapeDtypeStruct((N_DEV, d, cup), BF16),),
        out_specs=(pl.BlockSpec((None, tm_w, cup), lambda j, i, k: (j, i, 0)),), epilogue=_store_bf16)[0]

    x_up = exchange_begin("w_up", gw_up, w_up)
    dh2 = _mm(
        "mm_dh2", dup, w_up_g, mode="nt", grid=(nm, d // tn_d, N_DEV),
        a_spec=pl.BlockSpec((None, tm, cup), lambda i, j, k: (k, i, 0)),
        b_spec=pl.BlockSpec((None, tn_d, cup), lambda i, j, k: (k, j, 0)),
        acc_shape=(tm, tn_d), outs=(td_shape,), out_specs=(res_spec3,), epilogue=_store_f32, after=(x_up[4],))[0]

    def modulate_bwd(name, xin, dh, dres, g, scale, gate=None, branch=None):
        gated = gate is not None

        def body(*refs):
            x_ref, dh_ref, dres_ref, g_ref, sc_ref = refs[:5]
            rest = refs[5:]
            if gated:
                gate_ref, br_ref, dx_ref, dsh_ref, dsc_ref, dg_ref, do_ref, dgate_ref = rest
            else:
                dx_ref, dsh_ref, dsc_ref, dg_ref = rest
            i = pl.program_id(0)
            xx = x_ref[...]
            rr = lax.rsqrt(_mean(xx * xx) + EPS)
            xn = xx * rr
            s1 = 1.0 + sc_ref[...]
            dhh = dh_ref[...]
            dxn = dhh * g_ref[...] * s1
            dx = dres_ref[...] + rr * (dxn - xn * _mean(dxn * xn))
            dx_ref[...] = dx

            @pl.when(i == 0)
            def _():
                dg_ref[...] = jnp.zeros_like(dg_ref)

            @pl.when(i % tpd == 0)
            def _():
                dsh_ref[...] = jnp.zeros_like(dsh_ref)
                dsc_ref[...] = jnp.zeros_like(dsc_ref)
                if gated:
                    dgate_ref[...] = jnp.zeros_like(dgate_ref)

            dsh_ref[...] += _colsum(dhh)
            dsc_ref[...] += _colsum(dhh * (xn * g_ref[...]))
            dg_ref[...] += _colsum(dhh * s1 * xn)
            if gated:
                do_ref[...] = (gate_ref[...] * dx).astype(BF16)
                dgate_ref[...] += _colsum(dx * br_ref[...])

        ins = [xin, dh, dres, g, scale]
        in_specs = [row_d, row_d, row_d, vec_d, seq_d]
        outs = [td_shape, seq_shape, seq_shape, vec_shape]
        out_specs = [row_d, seq_acc, seq_acc, acc_d]
        if gated:
            ins += [gate, branch]
            in_specs += [seq_d, row_d]
            outs += [jax.ShapeDtypeStruct((t, d), BF16), seq_shape]
            out_specs += [row_d, seq_acc]
        return pl.pallas_call(
            body, name=name, grid=(ntd,), out_shape=tuple(outs), in_specs=in_specs, out_specs=tuple(out_specs),
            compiler_params=_cp(("arbitrary",)))(*ins)

    dx1, dshift_f, dscale_f, dg_ffn, d_o, dgate_m = modulate_bwd("mod2_bwd", x1, dh2, dx2, g_ffn, scale_f, gate_m, o_mix)

    tn_w = _blk(d, 2048)
    gw_out = _mm(
        "mm_gw_out", y, d_o, mode="tn", grid=(d // tm_w, d // tn_w, nk_t),
        a_spec=pl.BlockSpec((tk_t, tm_w), lambda i, j, k: (k, i)), b_spec=pl.BlockSpec((tk_t, tn_w), lambda i, j, k: (k, j)),
        acc_shape=(tm_w, tn_w), outs=(jax.ShapeDtypeStruct((d, d), BF16),),
        out_specs=(pl.BlockSpec((tm_w, tn_w), lambda i, j, k: (i, j)),), epilogue=_store_bf16)[0]

    x_out = exchange_begin("w_out", gw_out, w_out)
    dy = _mm(
        "mm_dy", d_o, w_out_full, mode="nt", grid=(nm, d // tn_d, 1),
        a_spec=pl.BlockSpec((tm, d), lambda i, j, k: (i, 0)), b_spec=pl.BlockSpec((tn_d, d), lambda i, j, k: (j, 0)),
        acc_shape=(tm, tn_d), outs=(jax.ShapeDtypeStruct((t, d), BF16),),
        out_specs=(pl.BlockSpec((tm, tn_d), lambda i, j, k: (i, j)),), epilogue=_store_bf16, after=(x_out[4],))[0]

    def conv_norm_bwd_body(dy_ref, a1_ref, lg, lb, og, da1_ref, dog_ref, dlg_ref, dlb_ref):
        i = pl.program_id(0)
        xh, rstd = layer_norm_stats(a1_ref[...])
        a2 = xh * lg[...] + lb[...]
        sg = _sigmoid(a2)
        a3 = a2 * sg
        r3 = lax.rsqrt(_mean(a3 * a3) + EPS)
        n3 = a3 * r3
        dya = dy_ref[...].astype(F32)
        dn3 = dya * og[...]
        da3 = r3 * (dn3 - n3 * _mean(dn3 * n3))
        da2 = da3 * (sg * (1.0 + a2 * (1.0 - sg)))
        dxh = da2 * lg[...]
        da1_ref[...] = rstd * (dxh - _mean(dxh) - xh * _mean(dxh * xh))

        @pl.when(i == 0)
        def _():
            dog_ref[...] = jnp.zeros_like(dog_ref)
            dlg_ref[...] = jnp.zeros_like(dlg_ref)
            dlb_ref[...] = jnp.zeros_like(dlb_ref)

        dog_ref[...] += _colsum(dya * n3)
        dlg_ref[...] += _colsum(da2 * xh)
        dlb_ref[...] += _colsum(da2)

    vecc_shape = jax.ShapeDtypeStruct((1, dc), F32)
    da1, g_og_conv, g_cln_g, g_cln_b = pl.pallas_call(
        conv_norm_bwd_body, name="conv_norm_bwd", grid=(nt,),
        out_shape=(jax.ShapeDtypeStruct((t, dc), F32), vecc_shape, vecc_shape, vecc_shape),
        in_specs=[main_col(0), row_c, vec_c, vec_c, vec_c], out_specs=(row_c, vec_c, vec_c, vec_c),
        compiler_params=_cp(("arbitrary",)))(dy, a1, conv_ln_g, conv_ln_b, out_g_conv)

    def sgu_bwd_body(dy_ref, pu, pw, slg, slb, sw, swt, sbias, sog, dproj_ref, dsog_ref, dslg_ref, dslb_ref, dsw_ref, dsb_ref, vs_ref, dvn_ref):
        i = pl.program_id(0)
        zu = pu[...].astype(F32)
        zv = pw[...].astype(F32)
        u = _gelu(zu)
        vv = _gelu(zv)
        xhv, rstd = layer_norm_stats(vv)
        vnb = (xhv * slg[...] + slb[...]).astype(BF16)
        sgu_mix(vs_ref, sw, vnb, sbias, 1)
        vs = vs_ref[...]
        bg = u * vs
        rb = lax.rsqrt(_mean(bg * bg) + EPS)
        nb = bg * rb
        dyb = dy_ref[...].astype(F32)
        dnb = dyb * sog[...]
        dbg = rb * (dnb - nb * _mean(dnb * nb))
        du = dbg * vs
        dvs = dbg * u
        dvsb = dvs.astype(BF16)

        @pl.when(i == 0)
        def _():
            dsog_ref[...] = jnp.zeros_like(dsog_ref)
            dslg_ref[...] = jnp.zeros_like(dslg_ref)
            dslb_ref[...] = jnp.zeros_like(dslb_ref)
            dsw_ref[...] = jnp.zeros_like(dsw_ref)
            dsb_ref[...] = jnp.zeros_like(dsb_ref)

        dsog_ref[...] += _colsum(dyb * nb)
        dsb_ref[...] += dvs
        for h in range(heads):
            dblk = dvsb[:, h * hd:(h + 1) * hd]
            vblk = vnb[:, h * hd:(h + 1) * hd]
            dsw_ref[h] += lax.dot_general(dblk, vblk, (((1,), (1,)), ((), ())), preferred_element_type=F32)
            dvn_ref[:, pl.ds(h * hd, hd)] = jnp.dot(swt[h], dblk, preferred_element_type=F32)
        dvn = dvn_ref[...]
        dslg_ref[...] += _colsum(dvn * xhv)
        dslb_ref[...] += _colsum(dvn)
        dxh = dvn * slg[...]
        dvv = rstd * (dxh - _mean(dxh) - xhv * _mean(dxh * xhv))
        dproj_ref[:, pl.ds(0, dc)] = (du * _gelu_grad(zu)).astype(BF16)
        dproj_ref[:, pl.ds(dc, dc)] = (dvv * _gelu_grad(zv)).astype(BF16)

    dproj, g_og_sgu, g_sln_g, g_sln_b, g_sgu_w, g_sgu_bias = pl.pallas_call(
        sgu_bwd_body, name="sgu_bwd", grid=(t // ch,),
        out_shape=(jax.ShapeDtypeStruct((t, 4 * dc), BF16), vecc_shape, vecc_shape, vecc_shape,
                   jax.ShapeDtypeStruct((heads, ch, ch), F32), jax.ShapeDtypeStruct((ch, dc), F32)),
        in_specs=[chunk_col(1), chunk_col(2), chunk_col(3), vec_c, vec_c, sguw_spec, sguw_spec, bias_spec, vec_c],
        out_specs=(pl.BlockSpec((ch, 2 * dc), lambda i: (i, 1)), vec_c, vec_c, vec_c, sguw_spec, bias_spec),
        scratch_shapes=[pltpu.VMEM((ch, dc), F32), pltpu.VMEM((ch, dc), F32)],
        compiler_params=_cp(("arbitrary",)),
    )(dy, proj, proj, sgu_ln_g, sgu_ln_b, sgu_w_bf, sgu_wt_bf, sgu_bias, out_g_sgu)

    def sgu_bias_reduce_body(b_ref, o_ref):
        lane = lax.broadcasted_iota(jnp.int32, (ch, LANES), 1)
        res = jnp.zeros((ch, LANES), F32)
        for h in range(heads):
            res = jnp.where(lane == h, jnp.sum(b_ref[:, pl.ds(h * hd, hd)], axis=1, keepdims=True), res)
        o_ref[...] = res

    g_sgu_b_t = pl.pallas_call(
        sgu_bias_reduce_body, name="sgu_bias_reduce", out_shape=jax.ShapeDtypeStruct((ch, LANES), F32),
        in_specs=[pl.BlockSpec(memory_space=pltpu.VMEM)], out_specs=pl.BlockSpec(memory_space=pltpu.VMEM),
        compiler_params=_cp())(g_sgu_bias)
    g_sgu_b = g_sgu_b_t[:, :heads].T

    def conv_bwd_body(da, dap, dan, pv, pg, pvp, pgp, pvn, pgn, cw, dproj_in, dproj_ref, dcw_ref, dcb_ref, ext, dext):
        del dproj_in
        i = pl.program_id(0)
        first = (i % tps) == 0
        last = (i % tps) == tps - 1
        fill_glu_ext(ext, i, pv, pg, pvp, pgp, pvn, pgn)
        da_m = da[...]
        dext[pl.ds(0, HALO), :] = jnp.where(first, 0.0, dap[...])
        dext[pl.ds(HALO, tb), :] = da_m
        dext[pl.ds(HALO + tb, HALO), :] = jnp.where(last, 0.0, dan[...])
        pad = kc // 2

        @pl.when(i == 0)
        def _():
            dcw_ref[...] = jnp.zeros_like(dcw_ref)
            dcb_ref[...] = jnp.zeros_like(dcb_ref)

        da0 = jnp.zeros((tb, dc), F32)
        for k in range(kc):
            da0 = da0 + cw[pl.ds(k, 1), :] * dext[pl.ds(HALO + pad - k, tb), :]
            dcw_ref[pl.ds(k, 1), :] += _colsum(da_m * ext[pl.ds(HALO - pad + k, tb), :])
        dcb_ref[...] += _colsum(da_m)
        vv = pv[...].astype(F32)
        sg = _sigmoid(pg[...].astype(F32))
        dproj_ref[:, pl.ds(0, dc)] = (da0 * sg).astype(BF16)
        dproj_ref[:, pl.ds(dc, dc)] = (da0 * vv * sg * (1.0 - sg)).astype(BF16)

    def halo_rows(lo):
        if lo:
            return pl.BlockSpec((HALO, dc), lambda i: (jnp.maximum(i * hb - 1, 0), 0))
        return pl.BlockSpec((HALO, dc), lambda i: (jnp.minimum((i + 1) * hb, n_hb - 1), 0))

    dproj, g_conv_w, g_conv_b = pl.pallas_call(
        conv_bwd_body, name="conv_bwd", grid=(nt,),
        out_shape=(jax.ShapeDtypeStruct((t, 4 * dc), BF16), jax.ShapeDtypeStruct((kc, dc), F32), vecc_shape),
        in_specs=[row_c, halo_rows(True), halo_rows(False), main_col(0), main_col(1), prev_col(0), prev_col(1),
                  next_col(0), next_col(1), convw_spec, pl.BlockSpec(memory_space=pl.ANY)],
        out_specs=(pl.BlockSpec((tb, 2 * dc), lambda i: (i, 0)), convw_spec, vec_c),
        scratch_shapes=[pltpu.VMEM((tb + 2 * HALO, dc), F32), pltpu.VMEM((tb + 2 * HALO, dc), F32)],
        input_output_aliases={10: 0}, compiler_params=_cp(("arbitrary",)),
    )(da1, da1, da1, proj, proj, proj, proj, proj, proj, conv_w_full, dproj)

    gw_in = _mm(
        "mm_gw_in", h1, dproj, mode="tn", grid=(N_DEV, d // tm_w, nk_t),
        a_spec=pl.BlockSpec((tk_t, tm_w), lambda j, i, k: (k, i)),
        b_spec=pl.BlockSpec((tk_t, win_sh), lambda j, i, k: (k, j)),
        acc_shape=(tm_w, win_sh), outs=(jax.ShapeDtypeStruct((N_DEV, d, win_sh), BF16),),
        out_specs=(pl.BlockSpec((None, tm_w, win_sh), lambda j, i, k: (j, i, 0)),), epilogue=_store_bf16)[0]

    x_in = exchange_begin("w_in", gw_in, w_in)
    tn_h = _blk(d, 2048)
    dh1 = _mm(
        "mm_dh1", dproj, w_in_g, mode="nt", grid=(nm, d // tn_h, N_DEV),
        a_spec=pl.BlockSpec((tm, win_sh), lambda i, j, k: (i, k)),
        b_spec=pl.BlockSpec((None, tn_h, win_sh), lambda i, j, k: (k, j, 0)),
        acc_shape=(tm, tn_h), outs=(td_shape,), out_specs=(pl.BlockSpec((tm, tn_h), lambda i, j, k: (i, j)),),
        epilogue=_store_f32, after=(x_in[4],))[0]

    grad_x, dshift_m, dscale_m, dg_mix = modulate_bwd("mod1_bwd", xf, dh1, dx1, g_mix, scale_m)

    dmod = jnp.concatenate([dshift_m, dscale_m, dgate_m, dshift_f, dscale_f, dgate_f], axis=1)
    g_ffn_w_full = g_ffn_w_s.transpose(1, 0, 2).reshape(kf, dff)
    rep_names = ["g_mix", "conv_b", "conv_ln_g", "conv_ln_b", "sgu_ln_g", "sgu_ln_b", "sgu_w", "sgu_b",
                 "out_g_conv", "out_g_sgu", "g_ffn", "ffn_conv_b", "g_final"]
    rep_w = [g_mix, conv_b, conv_ln_g, conv_ln_b, sgu_ln_g, sgu_ln_b, sgu_w, sgu_b, out_g_conv, out_g_sgu, g_ffn, ffn_conv_b, g_final]
    rep_m = [m_g_mix, m_conv_b, m_conv_ln_g, m_conv_ln_b, m_sgu_ln_g, m_sgu_ln_b, m_sgu_w, m_sgu_b, m_out_g_conv, m_out_g_sgu,
             m_g_ffn, m_ffn_conv_b, m_g_final]
    rep_v = [v_g_mix, v_conv_b, v_conv_ln_g, v_conv_ln_b, v_sgu_ln_g, v_sgu_ln_b, v_sgu_w, v_sgu_b, v_out_g_conv, v_out_g_sgu,
             v_g_ffn, v_ffn_conv_b, v_g_final]
    rep_g = [dg_mix, g_conv_b, g_cln_g, g_cln_b, g_sln_g, g_sln_b, g_sgu_w, g_sgu_b, g_og_conv, g_og_sgu, dg_ffn,
             g_ffn_b_s, dg_final]
    rep_shapes = [w.shape for w in rep_w]
    extra_g = [g_conv_w, g_ffn_w_full, loss_part[:, :1]]
    extra_shapes = [(kc, dc), (kf, dff), (1, 1)]
    sum_blob = _pack(rep_g + extra_g)
    n_sum_rows = sum_blob.shape[0]
    blob1 = jnp.concatenate([sum_blob, _pack([dmod])], axis=0)
    gathered = _allgather("ag_small_grads", blob1, True)
    summed = _sum_slots("sum_small_grads", gathered[:, :n_sum_rows])
    n_rep_rows = _pack(rep_g).shape[0]
    g_conv_w_all, g_ffn_w_all, loss_all = _unpack(summed[n_rep_rows:], extra_shapes)
    loss = loss_all[0, 0]
    dmod_all = gathered[:, n_sum_rows:].reshape(N_DEV, -1)[:, :bsz * n_mod * d].reshape(n_batch, n_mod * d)

    rep_out = _adamw("adamw_small", _pack(rep_w), _pack(rep_m), _pack(rep_v), grad=summed[:n_rep_rows])
    rep_out = [_unpack(o, rep_shapes) for o in rep_out]
    rep = {name: tuple(rep_out[q][p] for q in range(4)) for p, name in enumerate(rep_names)}

    g_conv_w_me = lax.dynamic_slice(g_conv_w_all, (0, me * dc_sh), (kc, dc_sh))
    cw_out = _adamw("adamw_conv_w", conv_w[0], m_conv_w[0], v_conv_w[0], grad=g_conv_w_me)
    g_ffn_w_me = lax.dynamic_slice(g_ffn_w_all, (0, me * cw_sh), (kf, cw_sh))
    fw_out = _adamw("adamw_ffn_conv_w", ffn_conv_w[0], m_ffn_conv_w[0], v_ffn_conv_w[0], grad=g_ffn_w_me)

    g_b_ada = _col_sums("grad_b_ada", dmod_all)
    bada_out = _adamw("adamw_b_ada", b_ada, m_b_ada, v_b_ada, grad=g_b_ada)
    dmod_sh = lax.dynamic_slice(dmod_all, (0, me * ada_sh), (n_batch, ada_sh)).astype(BF16)
    wada_out = _adamw("adamw_w_ada", w_ada[0], m_w_ada[0], v_w_ada[0], ca_t=c_act.T, dmod=dmod_sh)

    def big(name, started, after, w, m, v):
        send_sems, recv_sems, x_thru, land_thru, _ = started
        mine, recv = _alltoall_wait("a2aw_" + name, send_sems, recv_sems, x_thru, land_thru, after)
        return _adamw("adamw_" + name, w[0], m[0], v[0], slots=recv, own=mine, me=me)

    wdown_out = big("w_down", x_down, wada_out[0], w_down, m_w_down, v_w_down)
    wup_out = big("w_up", x_up, wdown_out[0], w_up, m_w_up, v_w_up)
    wout_out = big("w_out", x_out, wup_out[0], w_out, m_w_out, v_w_out)
    win_out = big("w_in", x_in, wout_out[0], w_in, m_w_in, v_w_in)

    def lead(outs4):
        return tuple(o[None] for o in outs4)

    results = {
        "w_ada": lead(wada_out), "b_ada": bada_out, "w_in": lead(win_out), "conv_w": lead(cw_out),
        "w_out": lead(wout_out), "w_up": lead(wup_out), "ffn_conv_w": lead(fw_out), "w_down": lead(wdown_out),
    }
    results.update(rep)
    order = ["w_ada", "b_ada", "g_mix", "w_in", "conv_w", "conv_b", "conv_ln_g", "conv_ln_b", "sgu_ln_g", "sgu_ln_b",
             "sgu_w", "sgu_b", "out_g_conv", "out_g_sgu", "w_out", "g_ffn", "w_up", "ffn_conv_w", "ffn_conv_b", "w_down", "g_final"]
    out = [loss, grad_x.reshape(bsz, seq, d)]
    for q in range(4):
        out += [results[n][q] for n in order]
    return tuple(out)
```

```python
import functools

import jax
import jax.numpy as jnp
from jax import lax
from jax.experimental import pallas as pl
from jax.experimental.pallas import tpu as pltpu

EPS = 1e-6
N_DEV = 8
MESH_ID = pl.DeviceIdType.MESH
V7X_VMEM_BYTES = 64 * 1024 * 1024
VMEM_LIMIT = V7X_VMEM_BYTES - 8 * 1024 * 1024
LANES = 128
SUBLANES = 8
HALO = 16
BLOB_ALIGN = SUBLANES * LANES

ADAM_LR = 0.001
ADAM_B1 = 0.9
ADAM_B2 = 0.999
ADAM_EPS = 1e-08
ADAM_WD = 0.01
ADAM_STEP = 10

F32 = jnp.float32
BF16 = jnp.bfloat16


def _cp(sem=None, **kw):
    return pltpu.CompilerParams(dimension_semantics=sem, vmem_limit_bytes=VMEM_LIMIT, **kw)


def _blk(n, pref):
    return pref if n % pref == 0 else n


def _rows_tile(rows, cols, budget=3 << 19):
    best = None
    for t in range(SUBLANES, rows + 1, SUBLANES):
        if rows % t == 0 and t * cols * 4 <= budget:
            best = t
    return best if best is not None else rows


def _sigmoid(z):
    return 1.0 / (1.0 + jnp.exp(-z))


def _gelu(z):
    return z * (lax.erf(z * 0.7071067811865476) + 1.0) * 0.5


def _gelu_grad(z):
    return 0.5 * (1.0 + lax.erf(z * 0.7071067811865476)) + z * jnp.exp(-0.5 * z * z) * 0.3989422804014327


def _mean(z):
    return jnp.mean(z, axis=-1, keepdims=True)


def _colsum(z):
    return jnp.sum(z, axis=0, keepdims=True)


def _my_pos():
    return lax.axis_index("x"), lax.axis_index("y"), lax.axis_index("c")


def _allgather(name, shard, in_vmem):
    r, cdim = shard.shape

    def body(x_ref, out_ref, send_sems, recv_sems, local_sem):
        x, y, c = _my_pos()
        me, sibling = (x, y, c), (x, y, 1 - c)
        chips = [(1 - x, y), (x, 1 - y), (1 - x, 1 - y)]

        def slot(px, py, pc):
            return out_ref.at[4 * px + 2 * py + pc]

        def copy(k, block, to, src=None):
            return pltpu.make_async_remote_copy(
                src_ref=slot(*block) if src is None else src, dst_ref=slot(*block),
                send_sem=send_sems.at[k], recv_sem=recv_sems.at[k], device_id=to, device_id_type=MESH_ID)

        mine = pltpu.make_async_copy(x_ref, slot(*me), local_sem)
        mine.start()
        first = [copy(0, me, sibling, src=x_ref)]
        first += [copy(1 + j, me, (*chip, c), src=x_ref) for j, chip in enumerate(chips)]
        for cp in first:
            cp.start()
        passed = [copy(4 + j, (*chip, c), sibling) for j, chip in enumerate(chips)]
        for j, chip in enumerate(chips):
            copy(1 + j, (*chip, c), me).wait_recv()
            passed[j].start()
        copy(0, sibling, me).wait_recv()
        for j, chip in enumerate(chips):
            copy(4 + j, (*chip, 1 - c), me).wait_recv()
        for cp in first + passed:
            cp.wait_send()
        mine.wait()

    space = pltpu.VMEM if in_vmem else pl.ANY
    return pl.pallas_call(
        body, name=name, out_shape=jax.ShapeDtypeStruct((N_DEV, r, cdim), shard.dtype),
        in_specs=[pl.BlockSpec(memory_space=space)], out_specs=pl.BlockSpec(memory_space=space),
        scratch_shapes=[pltpu.SemaphoreType.DMA((7,)), pltpu.SemaphoreType.DMA((7,)), pltpu.SemaphoreType.DMA],
        compiler_params=pltpu.CompilerParams(vmem_limit_bytes=VMEM_LIMIT),
    )(shard)


HBM_SPEC = pl.BlockSpec(memory_space=pltpu.HBM)
SEM_SPEC = pl.BlockSpec(memory_space=pltpu.SEMAPHORE)
ANY_SPEC = pl.BlockSpec(memory_space=pl.ANY)
SPLIT_EFFECT = pltpu.SideEffectType.DATAFLOW_SIDE_EFFECTING
TOKEN_SHAPE = jax.ShapeDtypeStruct((SUBLANES, LANES), F32)


def _hbm(a):
    return pltpu.with_memory_space_constraint(a, pltpu.HBM)


def _peers():
    x, y, c = _my_pos()
    out = []
    for k in range(1, N_DEV):
        px = 1 - x if k & 4 else x
        py = 1 - y if k & 2 else y
        pc = 1 - c if k & 1 else c
        out.append((k, (px, py, pc), 4 * px + 2 * py + pc))
    return out


def _alltoall_start(name, blocks):
    def body(x_ref, land_ref, send_sems, recv_sems, x_thru, land_thru, token):
        del x_thru, land_thru
        x, y, c = _my_pos()
        me = 4 * x + 2 * y + c
        for k, peer, slot in _peers():
            pltpu.make_async_remote_copy(
                src_ref=x_ref.at[slot], dst_ref=land_ref.at[me], send_sem=send_sems.at[k - 1],
                recv_sem=recv_sems.at[k - 1], device_id=peer, device_id_type=MESH_ID).start()
        token[...] = jnp.zeros_like(token)

    sems = pltpu.SemaphoreType.DMA((N_DEV - 1,))
    buf = pltpu.HBM(blocks.shape, blocks.dtype)
    return pl.pallas_call(
        body, name=name, out_shape=(sems, sems, buf, buf, TOKEN_SHAPE), in_specs=(HBM_SPEC, HBM_SPEC),
        out_specs=(SEM_SPEC, SEM_SPEC, HBM_SPEC, HBM_SPEC, pl.BlockSpec(memory_space=pltpu.VMEM)),
        input_output_aliases={0: 2, 1: 3}, compiler_params=pltpu.CompilerParams(has_side_effects=SPLIT_EFFECT),
    )(_hbm(blocks), _hbm(lax.empty(blocks.shape, blocks.dtype)))


def _alltoall_wait(name, send_sems, recv_sems, x_thru, land_thru, after):
    def body(x_ref, land_ref, send_sems, recv_sems, after_ref, x_dead, got_ref):
        del after_ref, x_dead, got_ref
        for k, peer, slot in _peers():
            cp = pltpu.make_async_remote_copy(
                src_ref=x_ref.at[slot], dst_ref=land_ref.at[slot], send_sem=send_sems.at[k - 1],
                recv_sem=recv_sems.at[k - 1], device_id=peer, device_id_type=MESH_ID)
            cp.wait_send()
            cp.wait_recv()

    buf = pltpu.HBM(x_thru.shape, x_thru.dtype)
    return pl.pallas_call(
        body, name=name, out_shape=(buf, buf), in_specs=(HBM_SPEC, HBM_SPEC, SEM_SPEC, SEM_SPEC, ANY_SPEC),
        out_specs=(HBM_SPEC, HBM_SPEC), input_output_aliases={0: 0, 1: 1},
        compiler_params=pltpu.CompilerParams(has_side_effects=SPLIT_EFFECT),
    )(x_thru, land_thru, send_sems, recv_sems, after)


def _gather_start(name, shard, land):
    def body(x_ref, land_ref, send_sems, recv_sems, x_thru, land_thru, token):
        del x_thru, land_thru
        x, y, c = _my_pos()
        me = 4 * x + 2 * y + c
        targets = [(x, y, 1 - c), (1 - x, y, c), (x, 1 - y, c), (1 - x, 1 - y, c)]
        for k, peer in enumerate(targets):
            pltpu.make_async_remote_copy(
                src_ref=x_ref, dst_ref=land_ref.at[me], send_sem=send_sems.at[k], recv_sem=recv_sems.at[k],
                device_id=peer, device_id_type=MESH_ID).start()
        token[...] = jnp.zeros_like(token)

    sems = pltpu.SemaphoreType.DMA((4,))
    return pl.pallas_call(
        body, name=name,
        out_shape=(sems, sems, pltpu.HBM(shard.shape, shard.dtype), pltpu.HBM(land.shape, land.dtype), TOKEN_SHAPE),
        in_specs=(HBM_SPEC, HBM_SPEC),
        out_specs=(SEM_SPEC, SEM_SPEC, HBM_SPEC, HBM_SPEC, pl.BlockSpec(memory_space=pltpu.VMEM)),
        input_output_aliases={0: 2, 1: 3}, compiler_params=pltpu.CompilerParams(has_side_effects=SPLIT_EFFECT),
    )(_hbm(shard), _hbm(land))


def _gather_wait(name, send_sems, recv_sems, x_thru, land_thru, after):
    def body(x_ref, land_ref, send_sems, recv_sems, after_ref, x_dead, got_ref):
        del after_ref, x_dead, got_ref
        x, y, c = _my_pos()
        sources = [(x, y, 1 - c), (1 - x, y, c), (x, 1 - y, c), (1 - x, 1 - y, c)]
        for k, (px, py, pc) in enumerate(sources):
            cp = pltpu.make_async_remote_copy(
                src_ref=x_ref, dst_ref=land_ref.at[4 * px + 2 * py + pc], send_sem=send_sems.at[k],
                recv_sem=recv_sems.at[k], device_id=(px, py, pc), device_id_type=MESH_ID)
            cp.wait_send()
            cp.wait_recv()

    return pl.pallas_call(
        body, name=name, out_shape=(pltpu.HBM(x_thru.shape, x_thru.dtype), pltpu.HBM(land_thru.shape, land_thru.dtype)),
        in_specs=(HBM_SPEC, HBM_SPEC, SEM_SPEC, SEM_SPEC, ANY_SPEC), out_specs=(HBM_SPEC, HBM_SPEC),
        input_output_aliases={0: 0, 1: 1}, compiler_params=pltpu.CompilerParams(has_side_effects=SPLIT_EFFECT),
    )(x_thru, land_thru, send_sems, recv_sems, after)[1]


def _gather_forward(name, land):
    def body(x_ref, out_ref, send_sems, recv_sems):
        x, y, c = _my_pos()
        chips = [(1 - x, y), (x, 1 - y), (1 - x, 1 - y)]
        sends = []
        for j, (px, py) in enumerate(chips):
            mine = 4 * px + 2 * py + c
            cp = pltpu.make_async_remote_copy(
                src_ref=x_ref.at[mine], dst_ref=out_ref.at[mine], send_sem=send_sems.at[j], recv_sem=recv_sems.at[j],
                device_id=(x, y, 1 - c), device_id_type=MESH_ID)
            cp.start()
            sends.append(cp)
        for j, (px, py) in enumerate(chips):
            theirs = 4 * px + 2 * py + (1 - c)
            pltpu.make_async_remote_copy(
                src_ref=x_ref.at[theirs], dst_ref=out_ref.at[theirs], send_sem=send_sems.at[j], recv_sem=recv_sems.at[j],
                device_id=(x, y, 1 - c), device_id_type=MESH_ID).wait_recv()
        for cp in sends:
            cp.wait_send()

    return pl.pallas_call(
        body, name=name, out_shape=jax.ShapeDtypeStruct(land.shape, land.dtype), in_specs=[ANY_SPEC], out_specs=ANY_SPEC,
        scratch_shapes=[pltpu.SemaphoreType.DMA((3,)), pltpu.SemaphoreType.DMA((3,))], input_output_aliases={0: 0},
    )(land)


def _pack(arrays):
    parts = []
    for a in arrays:
        flat = a.reshape(-1).astype(F32)
        pad = (-flat.shape[0]) % BLOB_ALIGN
        parts.append(jnp.pad(flat, (0, pad)) if pad else flat)
    return jnp.concatenate(parts).reshape(-1, LANES)


def _unpack(blob, shapes):
    flat = blob.reshape(-1)
    out, off = [], 0
    for shp in shapes:
        n = 1
        for s in shp:
            n *= s
        out.append(flat[off:off + n].reshape(shp))
        off += n + (-n) % BLOB_ALIGN
    return out


def _cast_bf16(name, w, after):
    r, cdim = w.shape
    tr = _rows_tile(r, cdim)

    def body(w_ref, after_ref, o_ref):
        del after_ref
        o_ref[...] = w_ref[...].astype(BF16)

    return pl.pallas_call(
        body, name=name, grid=(r // tr,), out_shape=jax.ShapeDtypeStruct(w.shape, BF16),
        in_specs=[pl.BlockSpec((tr, cdim), lambda i: (i, 0)), ANY_SPEC], out_specs=pl.BlockSpec((tr, cdim), lambda i: (i, 0)),
        compiler_params=_cp(("parallel",)))(w, after)


def _adam_math(w, g, m, v):
    m = ADAM_B1 * m + (1.0 - ADAM_B1) * g
    v = ADAM_B2 * v + (1.0 - ADAM_B2) * (g * g)
    m_hat = m / (1.0 - ADAM_B1 ** ADAM_STEP)
    v_hat = v / (1.0 - ADAM_B2 ** ADAM_STEP)
    delta = -ADAM_LR * (m_hat / (jnp.sqrt(v_hat) + ADAM_EPS) + ADAM_WD * w)
    return delta, m, v


def _adamw(name, w, m, v, *, grad=None, slots=None, own=None, me=None, ca_t=None, dmod=None):
    r, cdim = w.shape
    tr = _rows_tile(r, cdim)
    row = pl.BlockSpec((tr, cdim), lambda i, *_: (i, 0))
    prefetch = []
    if grad is not None:
        srcs, src_specs = [grad], [row]
    elif slots is not None:
        prefetch = [jnp.reshape(me, (1,)).astype(jnp.int32)]
        srcs = [slots, own]
        src_specs = [pl.BlockSpec((N_DEV, tr, cdim), lambda i, me_ref: (0, i, 0)),
                     pl.BlockSpec((None, tr, cdim), lambda i, me_ref: (me_ref[0], i, 0))]
    else:
        srcs = [ca_t, dmod]
        src_specs = [pl.BlockSpec((tr, ca_t.shape[1]), lambda i: (i, 0)), pl.BlockSpec(dmod.shape, lambda i: (0, 0))]
    n_src = len(srcs)
    n_pre = len(prefetch)

    def body(*refs):
        pre, refs = refs[:n_pre], refs[n_pre:]
        w_ref, m_ref, v_ref = refs[n_src:n_src + 3]
        g_ref, d_ref, nm_ref, nv_ref = refs[n_src + 3:]
        if grad is not None:
            g = refs[0][...]
        elif slots is not None:
            mine = pre[0][0]
            own_f = refs[1][...].astype(F32)
            g = jnp.where(mine == 0, own_f, refs[0][0].astype(F32))
            for s in range(1, N_DEV):
                g = g + jnp.where(mine == s, own_f, refs[0][s].astype(F32))
        else:
            g = jnp.dot(refs[0][...], refs[1][...], preferred_element_type=F32)
        delta, nm, nv = _adam_math(w_ref[...], g, m_ref[...], v_ref[...])
        g_ref[...] = g
        d_ref[...] = delta
        nm_ref[...] = nm
        nv_ref[...] = nv

    shp = jax.ShapeDtypeStruct(w.shape, F32)
    grid_spec = pltpu.PrefetchScalarGridSpec(
        num_scalar_prefetch=n_pre, grid=(r // tr,), in_specs=[*src_specs, row, row, row], out_specs=(row,) * 4)
    return pl.pallas_call(
        body, name=name, grid_spec=grid_spec, out_shape=(shp,) * 4, compiler_params=_cp(("parallel",)))(*prefetch, *srcs, w, m, v)


def _sum_slots(name, gathered):
    _, r, cdim = gathered.shape
    tr = _rows_tile(r, cdim * N_DEV)

    def body(g_ref, o_ref):
        acc = g_ref[0]
        for s in range(1, N_DEV):
            acc = acc + g_ref[s]
        o_ref[...] = acc

    return pl.pallas_call(
        body, name=name, grid=(r // tr,), out_shape=jax.ShapeDtypeStruct((r, cdim), F32),
        in_specs=[pl.BlockSpec((N_DEV, tr, cdim), lambda i: (0, i, 0))], out_specs=pl.BlockSpec((tr, cdim), lambda i: (i, 0)),
        compiler_params=_cp(("parallel",)))(gathered)


def _col_sums(name, a):
    r, cdim = a.shape

    def body(a_ref, o_ref):
        o_ref[...] = _colsum(a_ref[...])

    return pl.pallas_call(
        body, name=name, out_shape=jax.ShapeDtypeStruct((1, cdim), F32),
        in_specs=[pl.BlockSpec(memory_space=pltpu.VMEM)], out_specs=pl.BlockSpec(memory_space=pltpu.VMEM),
        compiler_params=_cp())(a)


def _mm(name, a, b, *, mode, grid, a_spec, b_spec, acc_shape, outs, out_specs, epilogue, extra=(), extra_specs=(), after=()):
    nk = grid[-1]
    dims = {"nn": ((1,), (0,)), "nt": ((1,), (1,)), "tn": ((0,), (0,))}[mode]
    n_extra, n_out, n_after = len(extra), len(outs), len(after)

    def body(*refs):
        a_ref, b_ref = refs[0], refs[1]
        extra_refs = refs[2:2 + n_extra]
        out_refs = refs[2 + n_extra + n_after:2 + n_extra + n_after + n_out]
        def part():
            return lax.dot_general(a_ref[...], b_ref[...], (dims, ((), ())), preferred_element_type=F32)

        if nk == 1:
            epilogue(part(), extra_refs, out_refs)
        else:
            acc = refs[-1]
            k = pl.program_id(len(grid) - 1)

            @pl.when(k == 0)
            def _():
                acc[...] = part()

            @pl.when(jnp.logical_and(k > 0, k < nk - 1))
            def _():
                acc[...] += part()

            @pl.when(k == nk - 1)
            def _():
                epilogue(acc[...] + part(), extra_refs, out_refs)

    scratch = [pltpu.VMEM(acc_shape, F32)] if nk > 1 else []
    sem = ("parallel",) * (len(grid) - 1) + ("arbitrary",)
    return pl.pallas_call(
        body, name=name, grid=grid, in_specs=[a_spec, b_spec, *extra_specs, *([ANY_SPEC] * n_after)], out_specs=out_specs,
        out_shape=outs, scratch_shapes=scratch, compiler_params=_cp(sem))(a, b, *extra, *after)


def _store_bf16(acc, extra_refs, out_refs):
    out_refs[0][...] = acc.astype(BF16)


def _store_f32(acc, extra_refs, out_refs):
    out_refs[0][...] = acc


def _residual_epilogue(acc, extra_refs, out_refs):
    x_ref, gate_ref = extra_refs
    out_refs[0][...] = acc
    out_refs[1][...] = x_ref[...] + gate_ref[...] * acc


def kernel(x, c, w_ada, b_ada, g_mix, w_in, conv_w, conv_b, conv_ln_g, conv_ln_b, sgu_ln_g, sgu_ln_b, sgu_w, sgu_b, out_g_conv, out_g_sgu, w_out, g_ffn, w_up, ffn_conv_w, ffn_conv_b, w_down, g_final, loss_target, m_w_ada, m_b_ada, m_g_mix, m_w_in, m_conv_w, m_conv_b, m_conv_ln_g, m_conv_ln_b, m_sgu_ln_g, m_sgu_ln_b, m_sgu_w, m_sgu_b, m_out_g_conv, m_out_g_sgu, m_w_out, m_g_ffn, m_w_up, m_ffn_conv_w, m_ffn_conv_b, m_w_down, m_g_final, v_w_ada, v_b_ada, v_g_mix, v_w_in, v_conv_w, v_conv_b, v_conv_ln_g, v_conv_ln_b, v_sgu_ln_g, v_sgu_ln_b, v_sgu_w, v_sgu_b, v_out_g_conv, v_out_g_sgu, v_w_out, v_g_ffn, v_w_up, v_ffn_conv_w, v_ffn_conv_b, v_w_down, v_g_final):
    bsz, seq, d = x.shape
    t = bsz * seq
    n_batch = bsz * N_DEV
    ada_sh = w_ada.shape[2]
    n_mod = ada_sh * N_DEV // d
    win_sh = w_in.shape[2]
    kc = conv_w.shape[1]
    dc_sh = conv_w.shape[2]
    dc = dc_sh * N_DEV
    heads, ch = sgu_w.shape[1], sgu_w.shape[2]
    hd = dc // heads
    wout_sh = w_out.shape[1]
    cup = w_up.shape[2]
    kf = ffn_conv_w.shape[1]
    cw_sh = ffn_conv_w.shape[2]
    dff = cw_sh * N_DEV
    n_fs = dff // cup
    assert win_sh * N_DEV == 4 * dc and hd == LANES and kc // 2 < HALO and kf == 3 and 2 * n_fs == N_DEV
    assert w_down.shape[1] * 2 == cup and seq % ch == 0

    xi, yi, ci = _my_pos()
    me = 4 * xi + 2 * yi + ci

    tb = min(256, seq)
    tps = seq // tb
    nt = t // tb
    tbd = min(128, seq)
    tpd = seq // tbd
    ntd = t // tbd
    tm = min(512, seq)
    nm = t // tm

    xf = x.reshape(t, d)
    tgt = loss_target.reshape(t, d)

    small_shapes = [(bsz, d), (kc, dc_sh), (kf, cw_sh)]
    blob0 = _allgather("ag_small_in", _pack([c, conv_w[0], ffn_conv_w[0]]), True)
    per_dev = [_unpack(blob0[s], small_shapes) for s in range(N_DEV)]
    c_all = jnp.concatenate([p[0] for p in per_dev], axis=0)
    conv_w_full = jnp.concatenate([p[1] for p in per_dev], axis=1)
    ffn_w_full = jnp.concatenate([p[2] for p in per_dev], axis=1)
    ffn_w_s = ffn_w_full.reshape(kf, n_fs, cup).transpose(1, 0, 2)
    ffn_b_s = ffn_conv_b.reshape(n_fs, 1, cup)

    b_ada_sh = lax.dynamic_slice(b_ada, (0, me * ada_sh), (1, ada_sh))
    tn_ada = _blk(ada_sh, 512)

    def ada_body(c_ref, w_ref, b_ref, mod_ref, ca_ref):
        cc = c_ref[...]
        ca = (cc * _sigmoid(cc)).astype(BF16)
        ca_ref[...] = ca
        mod_ref[...] = jnp.dot(ca, w_ref[...].astype(BF16), preferred_element_type=F32) + b_ref[...]

    mod_sh, c_act = pl.pallas_call(
        ada_body, name="ada_fwd", grid=(ada_sh // tn_ada,),
        out_shape=(jax.ShapeDtypeStruct((n_batch, ada_sh), F32), jax.ShapeDtypeStruct((n_batch, d), BF16)),
        in_specs=[pl.BlockSpec((n_batch, d), lambda j: (0, 0)), pl.BlockSpec((d, tn_ada), lambda j: (0, j)),
                  pl.BlockSpec((1, tn_ada), lambda j: (0, j))],
        out_specs=(pl.BlockSpec((n_batch, tn_ada), lambda j: (0, j)), pl.BlockSpec((n_batch, d), lambda j: (0, 0))),
        compiler_params=_cp(("arbitrary",)))(c_all, w_ada[0], b_ada_sh)
    mod_all = _allgather("ag_mod", mod_sh, True)
    mod_me = lax.dynamic_slice(mod_all, (0, me * bsz, 0), (N_DEV, bsz, ada_sh))
    mod_me = mod_me.transpose(1, 0, 2).reshape(bsz, n_mod, 1, d)
    shift_m, scale_m, gate_m = mod_me[:, 0], mod_me[:, 1], mod_me[:, 2]
    shift_f, scale_f, gate_f = mod_me[:, 3], mod_me[:, 4], mod_me[:, 5]

    def gather_begin(name, w2d):
        shard = _cast_bf16("cast_" + name, w2d, mod_all)
        land = lax.dynamic_update_slice(lax.empty((N_DEV,) + shard.shape, BF16), shard[None], (me, 0, 0))
        return _gather_start("ag1_" + name, shard, land)

    def gather_end(name, started, after):
        send_sems, recv_sems, x_thru, land_thru, _ = started
        land = _gather_wait("ag1w_" + name, send_sems, recv_sems, x_thru, land_thru, after)
        return _gather_forward("ag2_" + name, land)

    w_up_t, m_w_up_t, v_w_up_t = (jnp.swapaxes(a[0], 0, 1) for a in (w_up, m_w_up, v_w_up))
    ag_in = gather_begin("w_in", w_in[0])
    ag_out = gather_begin("w_out", w_out[0])
    ag_up = gather_begin("w_up", w_up_t)
    ag_down = gather_begin("w_down", w_down[0])
    started = ag_in[4][0, 0] + ag_out[4][0, 0] + ag_up[4][0, 0] + ag_down[4][0, 0]

    row_d = pl.BlockSpec((tbd, d), lambda i: (i, 0))
    vec_d = pl.BlockSpec((1, d), lambda i: (0, 0))
    seq_d = pl.BlockSpec((None, 1, d), lambda i: (i // tpd, 0, 0))

    def modulate(name, xin, g, shift, scale):
        def body(x_ref, g_ref, sh_ref, sc_ref, h_ref):
            xx = x_ref[...]
            yy = xx * lax.rsqrt(_mean(xx * xx) + EPS)
            h_ref[...] = ((yy * g_ref[...]) * (1.0 + sc_ref[...]) + sh_ref[...]).astype(BF16)

        return pl.pallas_call(
            body, name=name, grid=(ntd,), out_shape=jax.ShapeDtypeStruct((t, d), BF16),
            in_specs=[row_d, vec_d, seq_d, seq_d], out_specs=row_d, compiler_params=_cp(("parallel",)))(xin, g, shift, scale)

    h1 = modulate("mod1_fwd", xf, g_mix + started, shift_m, scale_m)

    tk_d = _blk(d, 1024)
    w_in_g = gather_end("w_in", ag_in, h1)
    proj = _mm(
        "mm_proj", h1, w_in_g, mode="nn", grid=(nm, N_DEV, 1),
        a_spec=pl.BlockSpec((tm, d), lambda i, j, k: (i, 0)),
        b_spec=pl.BlockSpec((None, d, win_sh), lambda i, j, k: (j, 0, 0)),
        acc_shape=(tm, win_sh), outs=(jax.ShapeDtypeStruct((t, N_DEV * win_sh), BF16),),
        out_specs=(pl.BlockSpec((tm, win_sh), lambda i, j, k: (i, j)),), epilogue=_store_bf16)[0]

    hb = tb // HALO
    n_hb = t // HALO

    def main_col(col):
        return pl.BlockSpec((tb, dc), lambda i, col=col: (i, col))

    def chunk_col(col):
        return pl.BlockSpec((ch, dc), lambda i, col=col: (i, col))

    def prev_col(col):
        return pl.BlockSpec((HALO, dc), lambda i, col=col: (jnp.maximum(i * hb - 1, 0), col))

    def next_col(col):
        return pl.BlockSpec((HALO, dc), lambda i, col=col: (jnp.minimum((i + 1) * hb, n_hb - 1), col))

    vec_c = pl.BlockSpec((1, dc), lambda i: (0, 0))
    row_c = pl.BlockSpec((tb, dc), lambda i: (i, 0))
    convw_spec = pl.BlockSpec((kc, dc), lambda i: (0, 0))
    sguw_spec = pl.BlockSpec((heads, ch, ch), lambda i: (0, 0, 0))
    bias_spec = pl.BlockSpec((ch, dc), lambda i: (0, 0))
    n_chunk = tb // ch

    sgu_w_bf = sgu_w[0].astype(BF16)
    sgu_wt_bf = jnp.swapaxes(sgu_w[0], 1, 2).astype(BF16)
    sgu_bias = jnp.repeat(sgu_b[0].T, hd, axis=1)

    def fill_glu_ext(ext, i, pv, pg, pvp, pgp, pvn, pgn):
        first = (i % tps) == 0
        last = (i % tps) == tps - 1

        def glu(v_ref, g_ref):
            return v_ref[...].astype(F32) * _sigmoid(g_ref[...].astype(F32))

        ext[pl.ds(0, HALO), :] = jnp.where(first, 0.0, glu(pvp, pgp))
        ext[pl.ds(HALO, tb), :] = glu(pv, pg)
        ext[pl.ds(HALO + tb, HALO), :] = jnp.where(last, 0.0, glu(pvn, pgn))

    def layer_norm_stats(z):
        mu = _mean(z)
        zc = z - mu
        rstd = lax.rsqrt(_mean(zc * zc) + EPS)
        return zc * rstd, rstd

    def sgu_mix(vs_ref, w_ref, vnb, bias_ref, n_chunks):
        for cc in range(n_chunks):
            for h in range(heads):
                blk = jnp.dot(w_ref[h], vnb[cc * ch:(cc + 1) * ch, h * hd:(h + 1) * hd], preferred_element_type=F32)
                vs_ref[pl.ds(cc * ch, ch), pl.ds(h * hd, hd)] = blk + bias_ref[:, pl.ds(h * hd, hd)]

    def mix_fwd_body(pv, pg, pu, pw, pvp, pgp, pvn, pgn, cw, cb, lg, lb, og, slg, slb, sw, sbias, sog, y_ref, a1_ref, ext, vs_ref):
        i = pl.program_id(0)
        fill_glu_ext(ext, i, pv, pg, pvp, pgp, pvn, pgn)
        pad = kc // 2
        acc = jnp.zeros((tb, dc), F32) + cb[...]
        for k in range(kc):
            acc = acc + cw[pl.ds(k, 1), :] * ext[pl.ds(HALO - pad + k, tb), :]
        a1_ref[...] = acc
        xh, _ = layer_norm_stats(acc)
        a2 = xh * lg[...] + lb[...]
        a3 = a2 * _sigmoid(a2)
        ya = a3 * lax.rsqrt(_mean(a3 * a3) + EPS) * og[...]
        y_ref[:, pl.ds(0, dc)] = ya.astype(BF16)

        u = _gelu(pu[...].astype(F32))
        vv = _gelu(pw[...].astype(F32))
        xhv, _ = layer_norm_stats(vv)
        vn = xhv * slg[...] + slb[...]
        sgu_mix(vs_ref, sw, vn.astype(BF16), sbias, n_chunk)
        bg = u * vs_ref[...]
        yb = bg * lax.rsqrt(_mean(bg * bg) + EPS) * sog[...]
        y_ref[:, pl.ds(dc, dc)] = yb.astype(BF16)

    y, a1 = pl.pallas_call(
        mix_fwd_body, name="mix_fwd", grid=(nt,),
        out_shape=(jax.ShapeDtypeStruct((t, 2 * dc), BF16), jax.ShapeDtypeStruct((t, dc), F32)),
        in_specs=[main_col(0), main_col(1), main_col(2), main_col(3), prev_col(0), prev_col(1), next_col(0), next_col(1),
                  convw_spec, vec_c, vec_c, vec_c, vec_c, vec_c, vec_c, sguw_spec, bias_spec, vec_c],
        out_specs=(pl.BlockSpec((tb, 2 * dc), lambda i: (i, 0)), row_c),
        scratch_shapes=[pltpu.VMEM((tb + 2 * HALO, dc), F32), pltpu.VMEM((tb, dc), F32)],
        compiler_params=_cp(("parallel",)),
    )(proj, proj, proj, proj, proj, proj, proj, proj, conv_w_full, conv_b, conv_ln_g, conv_ln_b, out_g_conv,
      sgu_ln_g, sgu_ln_b, sgu_w_bf, sgu_bias, out_g_sgu)

    tn_d = _blk(d, 1024)
    gate_spec3 = pl.BlockSpec((None, 1, tn_d), lambda i, j, k: (i * tm // seq, 0, j))
    res_spec3 = pl.BlockSpec((tm, tn_d), lambda i, j, k: (i, j))
    td_shape = jax.ShapeDtypeStruct((t, d), F32)

    w_out_full = gather_end("w_out", ag_out, y).reshape(d, d)
    o_mix, x1 = _mm(
        "mm_out", y, w_out_full, mode="nn", grid=(nm, d // tn_d, 1),
        a_spec=pl.BlockSpec((tm, d), lambda i, j, k: (i, 0)), b_spec=pl.BlockSpec((d, tn_d), lambda i, j, k: (0, j)),
        acc_shape=(tm, tn_d), outs=(td_shape, td_shape), out_specs=(res_spec3, res_spec3),
        epilogue=_residual_epilogue, extra=(xf, gate_m), extra_specs=(res_spec3, gate_spec3))

    h2 = modulate("mod2_fwd", x1, g_ffn, shift_f, scale_f)

    tm_up = tm
    nk_d = d // tk_d
    w_up_g = gather_end("w_up", ag_up, h2)
    up = _mm(
        "mm_up", h2, w_up_g, mode="nt", grid=(t // tm_up, N_DEV, nk_d),
        a_spec=pl.BlockSpec((tm_up, tk_d), lambda i, j, k: (i, k)),
        b_spec=pl.BlockSpec((None, cup, tk_d), lambda i, j, k: (j, 0, k)),
        acc_shape=(tm_up, cup), outs=(jax.ShapeDtypeStruct((N_DEV, t, cup), BF16),),
        out_specs=(pl.BlockSpec((None, tm_up, cup), lambda i, j, k: (j, i, 0)),), epilogue=_store_bf16)[0]
    up4 = up.reshape(2, n_fs, t, cup)

    def ffn_halo(which, lo):
        if lo:
            return pl.BlockSpec((None, None, HALO, cup), lambda j, i: (which, j, jnp.maximum(i * hb - 1, 0), 0))
        return pl.BlockSpec((None, None, HALO, cup), lambda j, i: (which, j, jnp.minimum((i + 1) * hb, n_hb - 1), 0))

    pair_spec = pl.BlockSpec((2, None, tb, cup), lambda j, i: (0, j, i, 0))
    fw_spec = pl.BlockSpec((None, kf, cup), lambda j, i: (j, 0, 0))
    fb_spec = pl.BlockSpec((None, 1, cup), lambda j, i: (j, 0, 0))
    act_spec = pl.BlockSpec((None, tb, cup), lambda j, i: (j, i, 0))

    RC = HALO
    LC = 2 * LANES
    pieces = [(l0, min(LC, cup - l0)) for l0 in range(0, cup, LC)]
    n_rc = tb // RC
    pad = SUBLANES

    def fill_gate_ext(ext, i, cols, gate_ref, prev_ref, next_ref):
        first = (i % tps) == 0
        last = (i % tps) == tps - 1
        lw = cols.size
        ext[pl.ds(0, pad), cols] = jnp.zeros((pad, lw), F32)
        ext[pl.ds(pad, HALO), cols] = jnp.where(first, 0.0, prev_ref[:, cols].astype(F32))
        for rr in range(n_rc):
            ext[pl.ds(pad + HALO + rr * RC, RC), cols] = gate_ref[0, pl.ds(rr * RC, RC), cols].astype(F32)
        ext[pl.ds(pad + HALO + tb, HALO), cols] = jnp.where(last, 0.0, next_ref[:, cols].astype(F32))
        ext[pl.ds(pad + 2 * HALO + tb, pad), cols] = jnp.zeros((pad, lw), F32)

    def tap_rows(fw, fb, cols):
        lw = cols.size
        return [jnp.broadcast_to(fw[pl.ds(k, 1), cols], (RC, lw)) for k in range(kf)] + [jnp.broadcast_to(fb[:, cols], (RC, lw))]

    def ffn_fwd_body(pair, gp, gn, fw, fb, act_ref, ext):
        i = pl.program_id(1)
        for l0, lw in pieces:
            cols = pl.ds(l0, lw)
            fill_gate_ext(ext, i, cols, pair, gp, gn)
            w0, w1, w2, bb = tap_rows(fw, fb, cols)
            for rr in range(n_rc):
                e0 = pad + HALO + rr * RC
                gte = bb + w0 * ext[pl.ds(e0 - 1, RC), cols]
                gte = gte + w1 * ext[pl.ds(e0, RC), cols]
                gte = gte + w2 * ext[pl.ds(e0 + 1, RC), cols]
                val = pair[1, pl.ds(rr * RC, RC), cols].astype(F32)
                act_ref[pl.ds(rr * RC, RC), cols] = (gte * _sigmoid(gte) * val).astype(BF16)

    ext_rows = tb + 2 * HALO + 2 * pad
    act = pl.pallas_call(
        ffn_fwd_body, name="ffn_fwd", grid=(n_fs, nt), out_shape=jax.ShapeDtypeStruct((n_fs, t, cup), BF16),
        in_specs=[pair_spec, ffn_halo(0, True), ffn_halo(0, False), fw_spec, fb_spec], out_specs=act_spec,
        scratch_shapes=[pltpu.VMEM((ext_rows, cup), F32)], compiler_params=_cp(("parallel", "parallel")),
    )(up4, up4, up4, ffn_w_s, ffn_b_s)

    w_down_s = gather_end("w_down", ag_down, act).reshape(n_fs, cup, d)
    gate_spec_f = pl.BlockSpec((None, 1, tn_d), lambda i, j, k: (i * tm // seq, 0, j))
    dn, x2 = _mm(
        "mm_down", act, w_down_s, mode="nn", grid=(nm, d // tn_d, n_fs),
        a_spec=pl.BlockSpec((None, tm, cup), lambda i, j, k: (k, i, 0)),
        b_spec=pl.BlockSpec((None, cup, tn_d), lambda i, j, k: (k, 0, j)),
        acc_shape=(tm, tn_d), outs=(td_shape, td_shape), out_specs=(res_spec3, res_spec3),
        epilogue=_residual_epilogue, extra=(x1, gate_f), extra_specs=(res_spec3, gate_spec_f))

    acc_d = pl.BlockSpec((1, d), lambda i: (0, 0))
    seq_acc = pl.BlockSpec((None, 1, d), lambda i: (i // tpd, 0, 0))

    def head_body(x_ref, t_ref, g_ref, dn_ref, gate_ref, dx_ref, ddn_ref, loss_ref, dg_ref, dgate_ref):
        i = pl.program_id(0)
        xx = x_ref[...]
        rr = lax.rsqrt(_mean(xx * xx) + EPS)
        xn = xx * rr
        err = xn * g_ref[...] - t_ref[...]
        dyf = err * (1.0 / d)
        dxn = dyf * g_ref[...]
        dx = rr * (dxn - xn * _mean(dxn * xn))
        dx_ref[...] = dx
        ddn_ref[...] = (gate_ref[...] * dx).astype(BF16)
        part = 0.5 * jnp.sum(_mean(err * err), axis=0, keepdims=True)

        @pl.when(i == 0)
        def _():
            loss_ref[...] = jnp.zeros_like(loss_ref)
            dg_ref[...] = jnp.zeros_like(dg_ref)

        @pl.when(i % tpd == 0)
        def _():
            dgate_ref[...] = jnp.zeros_like(dgate_ref)

        loss_ref[...] += jnp.broadcast_to(part, loss_ref.shape)
        dg_ref[...] += _colsum(dyf * xn)
        dgate_ref[...] += _colsum(dx * dn_ref[...])

    seq_shape = jax.ShapeDtypeStruct((bsz, 1, d), F32)
    vec_shape = jax.ShapeDtypeStruct((1, d), F32)
    dx2, ddn, loss_part, dg_final, dgate_f = pl.pallas_call(
        head_body, name="loss_head", grid=(ntd,),
        out_shape=(td_shape, jax.ShapeDtypeStruct((t, d), BF16), jax.ShapeDtypeStruct((1, LANES), F32), vec_shape, seq_shape),
        in_specs=[row_d, row_d, vec_d, row_d, seq_d],
        out_specs=(row_d, row_d, pl.BlockSpec((1, LANES), lambda i: (0, 0)), acc_d, seq_acc),
        compiler_params=_cp(("arbitrary",)))(x2, tgt, g_final.reshape(1, d), dn, gate_f)

    tk_t = min(1024, t)
    nk_t = t // tk_t
    tn_gw = _blk(d, 512)
    gw_down = _mm(
        "mm_gw_down", act, ddn, mode="tn", grid=(n_fs, d // tn_gw, nk_t),
        a_spec=pl.BlockSpec((None, tk_t, cup), lambda j, n, k: (j, k, 0)),
        b_spec=pl.BlockSpec((tk_t, tn_gw), lambda j, n, k: (k, n)),
        acc_shape=(cup, tn_gw), outs=(jax.ShapeDtypeStruct((n_fs, cup, d), BF16),),
        out_specs=(pl.BlockSpec((None, cup, tn_gw), lambda j, n, k: (j, 0, n)),), epilogue=_store_bf16)[0]

    def exchange_begin(name, gw):
        return _alltoall_start("a2a_" + name, gw.reshape(N_DEV, gw.size // (N_DEV * gw.shape[-1]), gw.shape[-1]))

    x_down = exchange_begin("w_down", gw_down)
    dact = _mm(
        "mm_dact", ddn, w_down_s, mode="nt", grid=(t // tm_up, n_fs, nk_d),
        a_spec=pl.BlockSpec((tm_up, tk_d), lambda i, j, k: (i, k)),
        b_spec=pl.BlockSpec((None, cup, tk_d), lambda i, j, k: (j, 0, k)),
        acc_shape=(tm_up, cup), outs=(jax.ShapeDtypeStruct((n_fs, t, cup), BF16),),
        out_specs=(pl.BlockSpec((None, tm_up, cup), lambda i, j, k: (j, i, 0)),), epilogue=_store_bf16, after=(x_down[4],))[0]


    def ffn_bwd_body(pair, gp, gn, vp, vn, da, dap, dan, fw, fb, dup_ref, dw_ref, db_ref, ext, dg_e):
        i = pl.program_id(1)
        first = (i % tps) == 0
        last = (i % tps) == tps - 1

        @pl.when(i == 0)
        def _():
            dw_ref[...] = jnp.zeros_like(dw_ref)
            db_ref[...] = jnp.zeros_like(db_ref)

        for l0, lw in pieces:
            cols = pl.ds(l0, lw)
            fill_gate_ext(ext, i, cols, pair, gp, gn)
            w0, w1, w2, bb = tap_rows(fw, fb, cols)
            sums = [jnp.zeros((RC, lw), F32) for _ in range(kf + 1)]
            for cc in range(n_rc + 2):
                e0 = pad + cc * RC
                taps = [ext[pl.ds(e0 - 1, RC), cols], ext[pl.ds(e0, RC), cols], ext[pl.ds(e0 + 1, RC), cols]]
                gte = bb + w0 * taps[0]
                gte = gte + w1 * taps[1]
                gte = gte + w2 * taps[2]
                sg = _sigmoid(gte)
                if cc == 0:
                    val, dact_v = vp[:, cols], dap[:, cols]
                elif cc == n_rc + 1:
                    val, dact_v = vn[:, cols], dan[:, cols]
                else:
                    rows = pl.ds((cc - 1) * RC, RC)
                    val, dact_v = pair[1, rows, cols], da[rows, cols]
                val, dact_v = val.astype(F32), dact_v.astype(F32)
                dgte = dact_v * val * (sg * (1.0 + gte * (1.0 - sg)))
                if cc == 0:
                    dgte = jnp.where(first, 0.0, dgte)
                elif cc == n_rc + 1:
                    dgte = jnp.where(last, 0.0, dgte)
                else:
                    dup_ref[1, rows, cols] = (dact_v * (gte * sg)).astype(BF16)
                    for k in range(kf):
                        sums[k] = sums[k] + dgte * taps[k]
                    sums[kf] = sums[kf] + dgte
                dg_e[pl.ds(cc * RC, RC), cols] = dgte
            for k in range(kf):
                dw_ref[pl.ds(k, 1), cols] += _colsum(sums[k])
            db_ref[:, cols] += _colsum(sums[kf])
            for rr in range(n_rc):
                e0 = HALO + rr * RC
                dgate = w0 * dg_e[pl.ds(e0 + 1, RC), cols] + w1 * dg_e[pl.ds(e0, RC), cols] + w2 * dg_e[pl.ds(e0 - 1, RC), cols]
                dup_ref[0, pl.ds(rr * RC, RC), cols] = dgate.astype(BF16)

    def act_halo(lo):
        if lo:
            return pl.BlockSpec((None, HALO, cup), lambda j, i: (j, jnp.maximum(i * hb - 1, 0), 0))
        return pl.BlockSpec((None, HALO, cup), lambda j, i: (j, jnp.minimum((i + 1) * hb, n_hb - 1), 0))

    dup4, g_ffn_w_s, g_ffn_b_s = pl.pallas_call(
        ffn_bwd_body, name="ffn_bwd", grid=(n_fs, nt),
        out_shape=(jax.ShapeDtypeStruct((2, n_fs, t, cup), BF16), jax.ShapeDtypeStruct((n_fs, kf, cup), F32),
                   jax.ShapeDtypeStruct((n_fs, 1, cup), F32)),
        in_specs=[pair_spec, ffn_halo(0, True), ffn_halo(0, False), ffn_halo(1, True), ffn_halo(1, False),
                  act_spec, act_halo(True), act_halo(False), fw_spec, fb_spec],
        out_specs=(pair_spec, fw_spec, fb_spec),
        scratch_shapes=[pltpu.VMEM((ext_rows, cup), F32), pltpu.VMEM((tb + 2 * HALO, cup), F32)],
        compiler_params=_cp(("parallel", "arbitrary")),
    )(up4, up4, up4, up4, up4, dact, dact, dact, ffn_w_s, ffn_b_s)
    dup = dup4.reshape(N_DEV, t, cup)

    tm_w = _blk(d, 1024)
    gw_up = _mm(
        "mm_gw_up", dup, h2, mode="tn", grid=(N_DEV, d // tn_gw, nk_t),
        a_spec=pl.BlockSpec((None, tk_t, cup), lambda j, n, k: (j, k, 0)),
        b_spec=pl.BlockSpec((tk_t, tn_gw), lambda j, n, k: (k, n)),
        acc_shape=(cup, tn_gw), outs=(jax.ShapeDtypeStruct((N_DEV, cup, d), BF16),),
        out_specs=(pl.BlockSpec((None, cup, tn_gw), lambda j, n, k: (j, 0, n)),), epilogue=_store_bf16)[0]

    x_up = exchange_begin("w_up", gw_up)
    dh2 = _mm(
        "mm_dh2", dup, w_up_g, mode="nn", grid=(nm, d // tn_d, N_DEV),
        a_spec=pl.BlockSpec((None, tm, cup), lambda i, j, k: (k, i, 0)),
        b_spec=pl.BlockSpec((None, cup, tn_d), lambda i, j, k: (k, 0, j)),
        acc_shape=(tm, tn_d), outs=(td_shape,), out_specs=(res_spec3,), epilogue=_store_f32, after=(x_up[4],))[0]

    def modulate_bwd(name, xin, dh, dres, g, scale, gate=None, branch=None):
        gated = gate is not None

        def body(*refs):
            x_ref, dh_ref, dres_ref, g_ref, sc_ref = refs[:5]
            rest = refs[5:]
            if gated:
                gate_ref, br_ref, dx_ref, dsh_ref, dsc_ref, dg_ref, do_ref, dgate_ref = rest
            else:
                dx_ref, dsh_ref, dsc_ref, dg_ref = rest
            i = pl.program_id(0)
            xx = x_ref[...]
            rr = lax.rsqrt(_mean(xx * xx) + EPS)
            xn = xx * rr
            s1 = 1.0 + sc_ref[...]
            dhh = dh_ref[...]
            dxn = dhh * g_ref[...] * s1
            dx = dres_ref[...] + rr * (dxn - xn * _mean(dxn * xn))
            dx_ref[...] = dx

            @pl.when(i == 0)
            def _():
                dg_ref[...] = jnp.zeros_like(dg_ref)

            @pl.when(i % tpd == 0)
            def _():
                dsh_ref[...] = jnp.zeros_like(dsh_ref)
                dsc_ref[...] = jnp.zeros_like(dsc_ref)
                if gated:
                    dgate_ref[...] = jnp.zeros_like(dgate_ref)

            dsh_ref[...] += _colsum(dhh)
            dsc_ref[...] += _colsum(dhh * (xn * g_ref[...]))
            dg_ref[...] += _colsum(dhh * s1 * xn)
            if gated:
                do_ref[...] = (gate_ref[...] * dx).astype(BF16)
                dgate_ref[...] += _colsum(dx * br_ref[...])

        ins = [xin, dh, dres, g, scale]
        in_specs = [row_d, row_d, row_d, vec_d, seq_d]
        outs = [td_shape, seq_shape, seq_shape, vec_shape]
        out_specs = [row_d, seq_acc, seq_acc, acc_d]
        if gated:
            ins += [gate, branch]
            in_specs += [seq_d, row_d]
            outs += [jax.ShapeDtypeStruct((t, d), BF16), seq_shape]
            out_specs += [row_d, seq_acc]
        return pl.pallas_call(
            body, name=name, grid=(ntd,), out_shape=tuple(outs), in_specs=in_specs, out_specs=tuple(out_specs),
            compiler_params=_cp(("arbitrary",)))(*ins)

    dx1, dshift_f, dscale_f, dg_ffn, d_o, dgate_m = modulate_bwd("mod2_bwd", x1, dh2, dx2, g_ffn, scale_f, gate_m, o_mix)

    tn_w = _blk(d, 2048)
    gw_out = _mm(
        "mm_gw_out", y, d_o, mode="tn", grid=(d // tm_w, d // tn_w, nk_t),
        a_spec=pl.BlockSpec((tk_t, tm_w), lambda i, j, k: (k, i)), b_spec=pl.BlockSpec((tk_t, tn_w), lambda i, j, k: (k, j)),
        acc_shape=(tm_w, tn_w), outs=(jax.ShapeDtypeStruct((d, d), BF16),),
        out_specs=(pl.BlockSpec((tm_w, tn_w), lambda i, j, k: (i, j)),), epilogue=_store_bf16)[0]

    x_out = exchange_begin("w_out", gw_out)
    dy = _mm(
        "mm_dy", d_o, w_out_full, mode="nt", grid=(nm, d // tn_d, 1),
        a_spec=pl.BlockSpec((tm, d), lambda i, j, k: (i, 0)), b_spec=pl.BlockSpec((tn_d, d), lambda i, j, k: (j, 0)),
        acc_shape=(tm, tn_d), outs=(jax.ShapeDtypeStruct((t, d), BF16),),
        out_specs=(pl.BlockSpec((tm, tn_d), lambda i, j, k: (i, j)),), epilogue=_store_bf16, after=(x_out[4],))[0]

    def conv_norm_bwd_body(dy_ref, a1_ref, lg, lb, og, da1_ref, dog_ref, dlg_ref, dlb_ref):
        i = pl.program_id(0)
        xh, rstd = layer_norm_stats(a1_ref[...])
        a2 = xh * lg[...] + lb[...]
        sg = _sigmoid(a2)
        a3 = a2 * sg
        r3 = lax.rsqrt(_mean(a3 * a3) + EPS)
        n3 = a3 * r3
        dya = dy_ref[...].astype(F32)
        dn3 = dya * og[...]
        da3 = r3 * (dn3 - n3 * _mean(dn3 * n3))
        da2 = da3 * (sg * (1.0 + a2 * (1.0 - sg)))
        dxh = da2 * lg[...]
        da1_ref[...] = rstd * (dxh - _mean(dxh) - xh * _mean(dxh * xh))

        @pl.when(i == 0)
        def _():
            dog_ref[...] = jnp.zeros_like(dog_ref)
            dlg_ref[...] = jnp.zeros_like(dlg_ref)
            dlb_ref[...] = jnp.zeros_like(dlb_ref)

        dog_ref[...] += _colsum(dya * n3)
        dlg_ref[...] += _colsum(da2 * xh)
        dlb_ref[...] += _colsum(da2)

    vecc_shape = jax.ShapeDtypeStruct((1, dc), F32)
    da1, g_og_conv, g_cln_g, g_cln_b = pl.pallas_call(
        conv_norm_bwd_body, name="conv_norm_bwd", grid=(nt,),
        out_shape=(jax.ShapeDtypeStruct((t, dc), F32), vecc_shape, vecc_shape, vecc_shape),
        in_specs=[main_col(0), row_c, vec_c, vec_c, vec_c], out_specs=(row_c, vec_c, vec_c, vec_c),
        compiler_params=_cp(("arbitrary",)))(dy, a1, conv_ln_g, conv_ln_b, out_g_conv)

    def sgu_bwd_body(dy_ref, pu, pw, slg, slb, sw, swt, sbias, sog, dproj_ref, dsog_ref, dslg_ref, dslb_ref, dsw_ref, dsb_ref, vs_ref, dvn_ref):
        i = pl.program_id(0)
        zu = pu[...].astype(F32)
        zv = pw[...].astype(F32)
        u = _gelu(zu)
        vv = _gelu(zv)
        xhv, rstd = layer_norm_stats(vv)
        vnb = (xhv * slg[...] + slb[...]).astype(BF16)
        sgu_mix(vs_ref, sw, vnb, sbias, 1)
        vs = vs_ref[...]
        bg = u * vs
        rb = lax.rsqrt(_mean(bg * bg) + EPS)
        nb = bg * rb
        dyb = dy_ref[...].astype(F32)
        dnb = dyb * sog[...]
        dbg = rb * (dnb - nb * _mean(dnb * nb))
        du = dbg * vs
        dvs = dbg * u
        dvsb = dvs.astype(BF16)

        @pl.when(i == 0)
        def _():
            dsog_ref[...] = jnp.zeros_like(dsog_ref)
            dslg_ref[...] = jnp.zeros_like(dslg_ref)
            dslb_ref[...] = jnp.zeros_like(dslb_ref)
            dsw_ref[...] = jnp.zeros_like(dsw_ref)
            dsb_ref[...] = jnp.zeros_like(dsb_ref)

        dsog_ref[...] += _colsum(dyb * nb)
        dsb_ref[...] += dvs
        for h in range(heads):
            dblk = dvsb[:, h * hd:(h + 1) * hd]
            vblk = vnb[:, h * hd:(h + 1) * hd]
            dsw_ref[h] += lax.dot_general(dblk, vblk, (((1,), (1,)), ((), ())), preferred_element_type=F32)
            dvn_ref[:, pl.ds(h * hd, hd)] = jnp.dot(swt[h], dblk, preferred_element_type=F32)
        dvn = dvn_ref[...]
        dslg_ref[...] += _colsum(dvn * xhv)
        dslb_ref[...] += _colsum(dvn)
        dxh = dvn * slg[...]
        dvv = rstd * (dxh - _mean(dxh) - xhv * _mean(dxh * xhv))
        dproj_ref[:, pl.ds(0, dc)] = (du * _gelu_grad(zu)).astype(BF16)
        dproj_ref[:, pl.ds(dc, dc)] = (dvv * _gelu_grad(zv)).astype(BF16)

    dproj, g_og_sgu, g_sln_g, g_sln_b, g_sgu_w, g_sgu_bias = pl.pallas_call(
        sgu_bwd_body, name="sgu_bwd", grid=(t // ch,),
        out_shape=(jax.ShapeDtypeStruct((t, 4 * dc), BF16), vecc_shape, vecc_shape, vecc_shape,
                   jax.ShapeDtypeStruct((heads, ch, ch), F32), jax.ShapeDtypeStruct((ch, dc), F32)),
        in_specs=[chunk_col(1), chunk_col(2), chunk_col(3), vec_c, vec_c, sguw_spec, sguw_spec, bias_spec, vec_c],
        out_specs=(pl.BlockSpec((ch, 2 * dc), lambda i: (i, 1)), vec_c, vec_c, vec_c, sguw_spec, bias_spec),
        scratch_shapes=[pltpu.VMEM((ch, dc), F32), pltpu.VMEM((ch, dc), F32)],
        compiler_params=_cp(("arbitrary",)),
    )(dy, proj, proj, sgu_ln_g, sgu_ln_b, sgu_w_bf, sgu_wt_bf, sgu_bias, out_g_sgu)

    def sgu_bias_reduce_body(b_ref, o_ref):
        lane = lax.broadcasted_iota(jnp.int32, (ch, LANES), 1)
        res = jnp.zeros((ch, LANES), F32)
        for h in range(heads):
            res = jnp.where(lane == h, jnp.sum(b_ref[:, pl.ds(h * hd, hd)], axis=1, keepdims=True), res)
        o_ref[...] = res

    g_sgu_b_t = pl.pallas_call(
        sgu_bias_reduce_body, name="sgu_bias_reduce", out_shape=jax.ShapeDtypeStruct((ch, LANES), F32),
        in_specs=[pl.BlockSpec(memory_space=pltpu.VMEM)], out_specs=pl.BlockSpec(memory_space=pltpu.VMEM),
        compiler_params=_cp())(g_sgu_bias)
    g_sgu_b = g_sgu_b_t[:, :heads].T

    def conv_bwd_body(da, dap, dan, pv, pg, pvp, pgp, pvn, pgn, cw, dproj_in, dproj_ref, dcw_ref, dcb_ref, ext, dext):
        del dproj_in
        i = pl.program_id(0)
        first = (i % tps) == 0
        last = (i % tps) == tps - 1
        fill_glu_ext(ext, i, pv, pg, pvp, pgp, pvn, pgn)
        da_m = da[...]
        dext[pl.ds(0, HALO), :] = jnp.where(first, 0.0, dap[...])
        dext[pl.ds(HALO, tb), :] = da_m
        dext[pl.ds(HALO + tb, HALO), :] = jnp.where(last, 0.0, dan[...])
        pad = kc // 2

        @pl.when(i == 0)
        def _():
            dcw_ref[...] = jnp.zeros_like(dcw_ref)
            dcb_ref[...] = jnp.zeros_like(dcb_ref)

        da0 = jnp.zeros((tb, dc), F32)
        for k in range(kc):
            da0 = da0 + cw[pl.ds(k, 1), :] * dext[pl.ds(HALO + pad - k, tb), :]
            dcw_ref[pl.ds(k, 1), :] += _colsum(da_m * ext[pl.ds(HALO - pad + k, tb), :])
        dcb_ref[...] += _colsum(da_m)
        vv = pv[...].astype(F32)
        sg = _sigmoid(pg[...].astype(F32))
        dproj_ref[:, pl.ds(0, dc)] = (da0 * sg).astype(BF16)
        dproj_ref[:, pl.ds(dc, dc)] = (da0 * vv * sg * (1.0 - sg)).astype(BF16)

    def halo_rows(lo):
        if lo:
            return pl.BlockSpec((HALO, dc), lambda i: (jnp.maximum(i * hb - 1, 0), 0))
        return pl.BlockSpec((HALO, dc), lambda i: (jnp.minimum((i + 1) * hb, n_hb - 1), 0))

    dproj, g_conv_w, g_conv_b = pl.pallas_call(
        conv_bwd_body, name="conv_bwd", grid=(nt,),
        out_shape=(jax.ShapeDtypeStruct((t, 4 * dc), BF16), jax.ShapeDtypeStruct((kc, dc), F32), vecc_shape),
        in_specs=[row_c, halo_rows(True), halo_rows(False), main_col(0), main_col(1), prev_col(0), prev_col(1),
                  next_col(0), next_col(1), convw_spec, pl.BlockSpec(memory_space=pl.ANY)],
        out_specs=(pl.BlockSpec((tb, 2 * dc), lambda i: (i, 0)), convw_spec, vec_c),
        scratch_shapes=[pltpu.VMEM((tb + 2 * HALO, dc), F32), pltpu.VMEM((tb + 2 * HALO, dc), F32)],
        input_output_aliases={10: 0}, compiler_params=_cp(("arbitrary",)),
    )(da1, da1, da1, proj, proj, proj, proj, proj, proj, conv_w_full, dproj)

    gw_in = _mm(
        "mm_gw_in", h1, dproj, mode="tn", grid=(N_DEV, d // tm_w, nk_t),
        a_spec=pl.BlockSpec((tk_t, tm_w), lambda j, i, k: (k, i)),
        b_spec=pl.BlockSpec((tk_t, win_sh), lambda j, i, k: (k, j)),
        acc_shape=(tm_w, win_sh), outs=(jax.ShapeDtypeStruct((N_DEV, d, win_sh), BF16),),
        out_specs=(pl.BlockSpec((None, tm_w, win_sh), lambda j, i, k: (j, i, 0)),), epilogue=_store_bf16)[0]

    x_in = exchange_begin("w_in", gw_in)
    tn_h = _blk(d, 2048)
    dh1 = _mm(
        "mm_dh1", dproj, w_in_g, mode="nt", grid=(nm, d // tn_h, N_DEV),
        a_spec=pl.BlockSpec((tm, win_sh), lambda i, j, k: (i, k)),
        b_spec=pl.BlockSpec((None, tn_h, win_sh), lambda i, j, k: (k, j, 0)),
        acc_shape=(tm, tn_h), outs=(td_shape,), out_specs=(pl.BlockSpec((tm, tn_h), lambda i, j, k: (i, j)),),
        epilogue=_store_f32, after=(x_in[4],))[0]

    grad_x, dshift_m, dscale_m, dg_mix = modulate_bwd("mod1_bwd", xf, dh1, dx1, g_mix, scale_m)

    dmod = jnp.concatenate([dshift_m, dscale_m, dgate_m, dshift_f, dscale_f, dgate_f], axis=1)
    g_ffn_w_full = g_ffn_w_s.transpose(1, 0, 2).reshape(kf, dff)
    rep_names = ["g_mix", "conv_b", "conv_ln_g", "conv_ln_b", "sgu_ln_g", "sgu_ln_b", "sgu_w", "sgu_b",
                 "out_g_conv", "out_g_sgu", "g_ffn", "ffn_conv_b", "g_final"]
    rep_w = [g_mix, conv_b, conv_ln_g, conv_ln_b, sgu_ln_g, sgu_ln_b, sgu_w, sgu_b, out_g_conv, out_g_sgu, g_ffn, ffn_conv_b, g_final]
    rep_m = [m_g_mix, m_conv_b, m_conv_ln_g, m_conv_ln_b, m_sgu_ln_g, m_sgu_ln_b, m_sgu_w, m_sgu_b, m_out_g_conv, m_out_g_sgu,
             m_g_ffn, m_ffn_conv_b, m_g_final]
    rep_v = [v_g_mix, v_conv_b, v_conv_ln_g, v_conv_ln_b, v_sgu_ln_g, v_sgu_ln_b, v_sgu_w, v_sgu_b, v_out_g_conv, v_out_g_sgu,
             v_g_ffn, v_ffn_conv_b, v_g_final]
    rep_g = [dg_mix, g_conv_b, g_cln_g, g_cln_b, g_sln_g, g_sln_b, g_sgu_w, g_sgu_b, g_og_conv, g_og_sgu, dg_ffn,
             g_ffn_b_s, dg_final]
    rep_shapes = [w.shape for w in rep_w]
    extra_g = [g_conv_w, g_ffn_w_full, loss_part[:, :1]]
    extra_shapes = [(kc, dc), (kf, dff), (1, 1)]
    sum_blob = _pack(rep_g + extra_g)
    n_sum_rows = sum_blob.shape[0]
    blob1 = jnp.concatenate([sum_blob, _pack([dmod])], axis=0)
    gathered = _allgather("ag_small_grads", blob1, True)
    summed = _sum_slots("sum_small_grads", gathered[:, :n_sum_rows])
    n_rep_rows = _pack(rep_g).shape[0]
    g_conv_w_all, g_ffn_w_all, loss_all = _unpack(summed[n_rep_rows:], extra_shapes)
    loss = loss_all[0, 0]
    dmod_all = gathered[:, n_sum_rows:].reshape(N_DEV, -1)[:, :bsz * n_mod * d].reshape(n_batch, n_mod * d)

    rep_out = _adamw("adamw_small", _pack(rep_w), _pack(rep_m), _pack(rep_v), grad=summed[:n_rep_rows])
    rep_out = [_unpack(o, rep_shapes) for o in rep_out]
    rep = {name: tuple(rep_out[q][p] for q in range(4)) for p, name in enumerate(rep_names)}

    g_conv_w_me = lax.dynamic_slice(g_conv_w_all, (0, me * dc_sh), (kc, dc_sh))
    cw_out = _adamw("adamw_conv_w", conv_w[0], m_conv_w[0], v_conv_w[0], grad=g_conv_w_me)
    g_ffn_w_me = lax.dynamic_slice(g_ffn_w_all, (0, me * cw_sh), (kf, cw_sh))
    fw_out = _adamw("adamw_ffn_conv_w", ffn_conv_w[0], m_ffn_conv_w[0], v_ffn_conv_w[0], grad=g_ffn_w_me)

    g_b_ada = _col_sums("grad_b_ada", dmod_all)
    bada_out = _adamw("adamw_b_ada", b_ada, m_b_ada, v_b_ada, grad=g_b_ada)
    dmod_sh = lax.dynamic_slice(dmod_all, (0, me * ada_sh), (n_batch, ada_sh)).astype(BF16)
    wada_out = _adamw("adamw_w_ada", w_ada[0], m_w_ada[0], v_w_ada[0], ca_t=c_act.T, dmod=dmod_sh)

    def big(name, started, after, w2d, m2d, v2d):
        send_sems, recv_sems, x_thru, land_thru, _ = started
        mine, recv = _alltoall_wait("a2aw_" + name, send_sems, recv_sems, x_thru, land_thru, after)
        return _adamw("adamw_" + name, w2d, m2d, v2d, slots=recv, own=mine, me=me)

    wdown_out = big("w_down", x_down, wada_out[0], w_down[0], m_w_down[0], v_w_down[0])
    wup_out = big("w_up", x_up, wdown_out[0], w_up_t, m_w_up_t, v_w_up_t)
    wout_out = big("w_out", x_out, wup_out[0], w_out[0], m_w_out[0], v_w_out[0])
    win_out = big("w_in", x_in, wout_out[0], w_in[0], m_w_in[0], v_w_in[0])
    wup_out = tuple(jnp.swapaxes(o, 0, 1) for o in wup_out)

    def lead(outs4):
        return tuple(o[None] for o in outs4)

    results = {
        "w_ada": lead(wada_out), "b_ada": bada_out, "w_in": lead(win_out), "conv_w": lead(cw_out),
        "w_out": lead(wout_out), "w_up": lead(wup_out), "ffn_conv_w": lead(fw_out), "w_down": lead(wdown_out),
    }
    results.update(rep)
    order = ["w_ada", "b_ada", "g_mix", "w_in", "conv_w", "conv_b", "conv_ln_g", "conv_ln_b", "sgu_ln_g", "sgu_ln_b",
             "sgu_w", "sgu_b", "out_g_conv", "out_g_sgu", "w_out", "g_ffn", "w_up", "ffn_conv_w", "ffn_conv_b", "w_down", "g_final"]
    out = [loss, grad_x.reshape(bsz, seq, d)]
    for q in range(4):
        out += [results[n][q] for n in order]
    return tuple(out)
```

```python
import functools

import jax
import jax.numpy as jnp
from jax import lax
from jax.experimental import pallas as pl
from jax.experimental.pallas import tpu as pltpu

EPS = 1e-6
N_DEV = 8
MESH_ID = pl.DeviceIdType.MESH
V7X_VMEM_BYTES = 64 * 1024 * 1024
VMEM_LIMIT = V7X_VMEM_BYTES - 8 * 1024 * 1024
LANES = 128
SUBLANES = 8
HALO = 16
BLOB_ALIGN = SUBLANES * LANES

ADAM_LR = 0.001
ADAM_B1 = 0.9
ADAM_B2 = 0.999
ADAM_EPS = 1e-08
ADAM_WD = 0.01
ADAM_STEP = 10

F32 = jnp.float32
BF16 = jnp.bfloat16


def _cp(sem=None, **kw):
    return pltpu.CompilerParams(dimension_semantics=sem, vmem_limit_bytes=VMEM_LIMIT, **kw)


def _blk(n, pref):
    return pref if n % pref == 0 else n


def _rows_tile(rows, cols, budget=3 << 19):
    best = None
    for t in range(SUBLANES, rows + 1, SUBLANES):
        if rows % t == 0 and t * cols * 4 <= budget:
            best = t
    return best if best is not None else rows


def _sigmoid(z):
    return 1.0 / (1.0 + jnp.exp(-z))


def _gelu(z):
    return z * (lax.erf(z * 0.7071067811865476) + 1.0) * 0.5


def _gelu_grad(z):
    return 0.5 * (1.0 + lax.erf(z * 0.7071067811865476)) + z * jnp.exp(-0.5 * z * z) * 0.3989422804014327


def _mean(z):
    return jnp.mean(z, axis=-1, keepdims=True)


def _colsum(z):
    return jnp.sum(z, axis=0, keepdims=True)


def _my_pos():
    return lax.axis_index("x"), lax.axis_index("y"), lax.axis_index("c")


def _allgather(name, shard, in_vmem):
    r, cdim = shard.shape

    def body(x_ref, out_ref, send_sems, recv_sems, local_sem):
        x, y, c = _my_pos()
        me, sibling = (x, y, c), (x, y, 1 - c)
        chips = [(1 - x, y), (x, 1 - y), (1 - x, 1 - y)]

        def slot(px, py, pc):
            return out_ref.at[4 * px + 2 * py + pc]

        def copy(k, block, to, src=None):
            return pltpu.make_async_remote_copy(
                src_ref=slot(*block) if src is None else src, dst_ref=slot(*block),
                send_sem=send_sems.at[k], recv_sem=recv_sems.at[k], device_id=to, device_id_type=MESH_ID)

        mine = pltpu.make_async_copy(x_ref, slot(*me), local_sem)
        mine.start()
        first = [copy(0, me, sibling, src=x_ref)]
        first += [copy(1 + j, me, (*chip, c), src=x_ref) for j, chip in enumerate(chips)]
        for cp in first:
            cp.start()
        passed = [copy(4 + j, (*chip, c), sibling) for j, chip in enumerate(chips)]
        for j, chip in enumerate(chips):
            copy(1 + j, (*chip, c), me).wait_recv()
            passed[j].start()
        copy(0, sibling, me).wait_recv()
        for j, chip in enumerate(chips):
            copy(4 + j, (*chip, 1 - c), me).wait_recv()
        for cp in first + passed:
            cp.wait_send()
        mine.wait()

    space = pltpu.VMEM if in_vmem else pl.ANY
    return pl.pallas_call(
        body, name=name, out_shape=jax.ShapeDtypeStruct((N_DEV, r, cdim), shard.dtype),
        in_specs=[pl.BlockSpec(memory_space=space)], out_specs=pl.BlockSpec(memory_space=space),
        scratch_shapes=[pltpu.SemaphoreType.DMA((7,)), pltpu.SemaphoreType.DMA((7,)), pltpu.SemaphoreType.DMA],
        compiler_params=pltpu.CompilerParams(vmem_limit_bytes=VMEM_LIMIT),
    )(shard)


HBM_SPEC = pl.BlockSpec(memory_space=pltpu.HBM)
SEM_SPEC = pl.BlockSpec(memory_space=pltpu.SEMAPHORE)
ANY_SPEC = pl.BlockSpec(memory_space=pl.ANY)
SPLIT_EFFECT = pltpu.SideEffectType.DATAFLOW_SIDE_EFFECTING
TOKEN_SHAPE = jax.ShapeDtypeStruct((SUBLANES, LANES), F32)


def _hbm(a):
    return pltpu.with_memory_space_constraint(a, pltpu.HBM)


def _peers():
    x, y, c = _my_pos()
    out = []
    for k in range(1, N_DEV):
        px = 1 - x if k & 4 else x
        py = 1 - y if k & 2 else y
        pc = 1 - c if k & 1 else c
        out.append((k, (px, py, pc), 4 * px + 2 * py + pc))
    return out


def _alltoall_start(name, blocks):
    def body(x_ref, land_ref, send_sems, recv_sems, x_thru, land_thru, token):
        del x_thru, land_thru
        x, y, c = _my_pos()
        me = 4 * x + 2 * y + c
        for k, peer, slot in _peers():
            pltpu.make_async_remote_copy(
                src_ref=x_ref.at[slot], dst_ref=land_ref.at[me], send_sem=send_sems.at[k - 1],
                recv_sem=recv_sems.at[k - 1], device_id=peer, device_id_type=MESH_ID).start()
        token[...] = jnp.zeros_like(token)

    sems = pltpu.SemaphoreType.DMA((N_DEV - 1,))
    buf = pltpu.HBM(blocks.shape, blocks.dtype)
    return pl.pallas_call(
        body, name=name, out_shape=(sems, sems, buf, buf, TOKEN_SHAPE), in_specs=(HBM_SPEC, HBM_SPEC),
        out_specs=(SEM_SPEC, SEM_SPEC, HBM_SPEC, HBM_SPEC, pl.BlockSpec(memory_space=pltpu.VMEM)),
        input_output_aliases={0: 2, 1: 3}, compiler_params=pltpu.CompilerParams(has_side_effects=SPLIT_EFFECT),
    )(_hbm(blocks), _hbm(lax.empty(blocks.shape, blocks.dtype)))


def _alltoall_wait(name, send_sems, recv_sems, x_thru, land_thru, after):
    def body(x_ref, land_ref, send_sems, recv_sems, after_ref, x_dead, got_ref):
        del after_ref, x_dead, got_ref
        for k, peer, slot in _peers():
            cp = pltpu.make_async_remote_copy(
                src_ref=x_ref.at[slot], dst_ref=land_ref.at[slot], send_sem=send_sems.at[k - 1],
                recv_sem=recv_sems.at[k - 1], device_id=peer, device_id_type=MESH_ID)
            cp.wait_send()
            cp.wait_recv()

    buf = pltpu.HBM(x_thru.shape, x_thru.dtype)
    return pl.pallas_call(
        body, name=name, out_shape=(buf, buf), in_specs=(HBM_SPEC, HBM_SPEC, SEM_SPEC, SEM_SPEC, ANY_SPEC),
        out_specs=(HBM_SPEC, HBM_SPEC), input_output_aliases={0: 0, 1: 1},
        compiler_params=pltpu.CompilerParams(has_side_effects=SPLIT_EFFECT),
    )(x_thru, land_thru, send_sems, recv_sems, after)


def _gather_start(name, shard, land):
    def body(x_ref, land_ref, send_sems, recv_sems, x_thru, land_thru, token):
        del x_thru, land_thru
        x, y, c = _my_pos()
        me = 4 * x + 2 * y + c
        targets = [(x, y, 1 - c), (1 - x, y, c), (x, 1 - y, c), (1 - x, 1 - y, c)]
        for k, peer in enumerate(targets):
            pltpu.make_async_remote_copy(
                src_ref=x_ref, dst_ref=land_ref.at[me], send_sem=send_sems.at[k], recv_sem=recv_sems.at[k],
                device_id=peer, device_id_type=MESH_ID).start()
        token[...] = jnp.zeros_like(token)

    sems = pltpu.SemaphoreType.DMA((4,))
    return pl.pallas_call(
        body, name=name,
        out_shape=(sems, sems, pltpu.HBM(shard.shape, shard.dtype), pltpu.HBM(land.shape, land.dtype), TOKEN_SHAPE),
        in_specs=(HBM_SPEC, HBM_SPEC),
        out_specs=(SEM_SPEC, SEM_SPEC, HBM_SPEC, HBM_SPEC, pl.BlockSpec(memory_space=pltpu.VMEM)),
        input_output_aliases={0: 2, 1: 3}, compiler_params=pltpu.CompilerParams(has_side_effects=SPLIT_EFFECT),
    )(_hbm(shard), _hbm(land))


def _gather_wait(name, send_sems, recv_sems, x_thru, land_thru, after):
    def body(x_ref, land_ref, send_sems, recv_sems, after_ref, x_dead, got_ref):
        del after_ref, x_dead, got_ref
        x, y, c = _my_pos()
        sources = [(x, y, 1 - c), (1 - x, y, c), (x, 1 - y, c), (1 - x, 1 - y, c)]
        for k, (px, py, pc) in enumerate(sources):
            cp = pltpu.make_async_remote_copy(
                src_ref=x_ref, dst_ref=land_ref.at[4 * px + 2 * py + pc], send_sem=send_sems.at[k],
                recv_sem=recv_sems.at[k], device_id=(px, py, pc), device_id_type=MESH_ID)
            cp.wait_send()
            cp.wait_recv()

    return pl.pallas_call(
        body, name=name, out_shape=(pltpu.HBM(x_thru.shape, x_thru.dtype), pltpu.HBM(land_thru.shape, land_thru.dtype)),
        in_specs=(HBM_SPEC, HBM_SPEC, SEM_SPEC, SEM_SPEC, ANY_SPEC), out_specs=(HBM_SPEC, HBM_SPEC),
        input_output_aliases={0: 0, 1: 1}, compiler_params=pltpu.CompilerParams(has_side_effects=SPLIT_EFFECT),
    )(x_thru, land_thru, send_sems, recv_sems, after)[1]


def _gather_forward(name, land):
    def body(x_ref, out_ref, send_sems, recv_sems):
        x, y, c = _my_pos()
        chips = [(1 - x, y), (x, 1 - y), (1 - x, 1 - y)]
        sends = []
        for j, (px, py) in enumerate(chips):
            mine = 4 * px + 2 * py + c
            cp = pltpu.make_async_remote_copy(
                src_ref=x_ref.at[mine], dst_ref=out_ref.at[mine], send_sem=send_sems.at[j], recv_sem=recv_sems.at[j],
                device_id=(x, y, 1 - c), device_id_type=MESH_ID)
            cp.start()
            sends.append(cp)
        for j, (px, py) in enumerate(chips):
            theirs = 4 * px + 2 * py + (1 - c)
            pltpu.make_async_remote_copy(
                src_ref=x_ref.at[theirs], dst_ref=out_ref.at[theirs], send_sem=send_sems.at[j], recv_sem=recv_sems.at[j],
                device_id=(x, y, 1 - c), device_id_type=MESH_ID).wait_recv()
        for cp in sends:
            cp.wait_send()

    return pl.pallas_call(
        body, name=name, out_shape=jax.ShapeDtypeStruct(land.shape, land.dtype), in_specs=[ANY_SPEC], out_specs=ANY_SPEC,
        scratch_shapes=[pltpu.SemaphoreType.DMA((3,)), pltpu.SemaphoreType.DMA((3,))], input_output_aliases={0: 0},
    )(land)


def _pack(arrays):
    parts = []
    for a in arrays:
        flat = a.reshape(-1).astype(F32)
        pad = (-flat.shape[0]) % BLOB_ALIGN
        parts.append(jnp.pad(flat, (0, pad)) if pad else flat)
    return jnp.concatenate(parts).reshape(-1, LANES)


def _unpack(blob, shapes):
    flat = blob.reshape(-1)
    out, off = [], 0
    for shp in shapes:
        n = 1
        for s in shp:
            n *= s
        out.append(flat[off:off + n].reshape(shp))
        off += n + (-n) % BLOB_ALIGN
    return out


def _cast_bf16(name, w, after):
    r, cdim = w.shape
    tr = _rows_tile(r, cdim)

    def body(w_ref, after_ref, o_ref):
        del after_ref
        o_ref[...] = w_ref[...].astype(BF16)

    return pl.pallas_call(
        body, name=name, grid=(r // tr,), out_shape=jax.ShapeDtypeStruct(w.shape, BF16),
        in_specs=[pl.BlockSpec((tr, cdim), lambda i: (i, 0)), ANY_SPEC], out_specs=pl.BlockSpec((tr, cdim), lambda i: (i, 0)),
        compiler_params=_cp(("parallel",)))(w, after)


def _adam_math(w, g, m, v):
    m = ADAM_B1 * m + (1.0 - ADAM_B1) * g
    v = ADAM_B2 * v + (1.0 - ADAM_B2) * (g * g)
    m_hat = m / (1.0 - ADAM_B1 ** ADAM_STEP)
    v_hat = v / (1.0 - ADAM_B2 ** ADAM_STEP)
    delta = -ADAM_LR * (m_hat / (jnp.sqrt(v_hat) + ADAM_EPS) + ADAM_WD * w)
    return delta, m, v


def _adamw(name, w, m, v, *, grad=None, slots=None, own=None, me=None, ca_t=None, dmod=None):
    r, cdim = w.shape
    tr = _rows_tile(r, cdim)
    row = pl.BlockSpec((tr, cdim), lambda i, *_: (i, 0))
    prefetch = []
    if grad is not None:
        srcs, src_specs = [grad], [row]
    elif slots is not None:
        prefetch = [jnp.reshape(me, (1,)).astype(jnp.int32)]
        srcs = [slots, own]
        src_specs = [pl.BlockSpec((N_DEV, tr, cdim), lambda i, me_ref: (0, i, 0)),
                     pl.BlockSpec((None, tr, cdim), lambda i, me_ref: (me_ref[0], i, 0))]
    else:
        srcs = [ca_t, dmod]
        src_specs = [pl.BlockSpec((tr, ca_t.shape[1]), lambda i: (i, 0)), pl.BlockSpec(dmod.shape, lambda i: (0, 0))]
    n_src = len(srcs)
    n_pre = len(prefetch)

    def body(*refs):
        pre, refs = refs[:n_pre], refs[n_pre:]
        w_ref, m_ref, v_ref = refs[n_src:n_src + 3]
        g_ref, d_ref, nm_ref, nv_ref = refs[n_src + 3:]
        if grad is not None:
            g = refs[0][...]
        elif slots is not None:
            mine = pre[0][0]
            own_f = refs[1][...].astype(F32)
            g = jnp.where(mine == 0, own_f, refs[0][0].astype(F32))
            for s in range(1, N_DEV):
                g = g + jnp.where(mine == s, own_f, refs[0][s].astype(F32))
        else:
            g = jnp.dot(refs[0][...], refs[1][...], preferred_element_type=F32)
        delta, nm, nv = _adam_math(w_ref[...], g, m_ref[...], v_ref[...])
        g_ref[...] = g
        d_ref[...] = delta
        nm_ref[...] = nm
        nv_ref[...] = nv

    shp = jax.ShapeDtypeStruct(w.shape, F32)
    grid_spec = pltpu.PrefetchScalarGridSpec(
        num_scalar_prefetch=n_pre, grid=(r // tr,), in_specs=[*src_specs, row, row, row], out_specs=(row,) * 4)
    return pl.pallas_call(
        body, name=name, grid_spec=grid_spec, out_shape=(shp,) * 4, compiler_params=_cp(("parallel",)))(*prefetch, *srcs, w, m, v)


def _sum_slots(name, gathered):
    _, r, cdim = gathered.shape
    tr = _rows_tile(r, cdim * N_DEV)

    def body(g_ref, o_ref):
        acc = g_ref[0]
        for s in range(1, N_DEV):
            acc = acc + g_ref[s]
        o_ref[...] = acc

    return pl.pallas_call(
        body, name=name, grid=(r // tr,), out_shape=jax.ShapeDtypeStruct((r, cdim), F32),
        in_specs=[pl.BlockSpec((N_DEV, tr, cdim), lambda i: (0, i, 0))], out_specs=pl.BlockSpec((tr, cdim), lambda i: (i, 0)),
        compiler_params=_cp(("parallel",)))(gathered)


def _col_sums(name, a):
    r, cdim = a.shape

    def body(a_ref, o_ref):
        o_ref[...] = _colsum(a_ref[...])

    return pl.pallas_call(
        body, name=name, out_shape=jax.ShapeDtypeStruct((1, cdim), F32),
        in_specs=[pl.BlockSpec(memory_space=pltpu.VMEM)], out_specs=pl.BlockSpec(memory_space=pltpu.VMEM),
        compiler_params=_cp())(a)


def _mm(name, a, b, *, mode, grid, a_spec, b_spec, acc_shape, outs, out_specs, epilogue, extra=(), extra_specs=(), after=()):
    nk = grid[-1]
    dims = {"nn": ((1,), (0,)), "nt": ((1,), (1,)), "tn": ((0,), (0,))}[mode]
    n_extra, n_out, n_after = len(extra), len(outs), len(after)

    def body(*refs):
        a_ref, b_ref = refs[0], refs[1]
        extra_refs = refs[2:2 + n_extra]
        out_refs = refs[2 + n_extra + n_after:2 + n_extra + n_after + n_out]
        def part():
            return lax.dot_general(a_ref[...], b_ref[...], (dims, ((), ())), preferred_element_type=F32)

        if nk == 1:
            epilogue(part(), extra_refs, out_refs)
        else:
            acc = refs[-1]
            k = pl.program_id(len(grid) - 1)

            @pl.when(k == 0)
            def _():
                acc[...] = part()

            @pl.when(jnp.logical_and(k > 0, k < nk - 1))
            def _():
                acc[...] += part()

            @pl.when(k == nk - 1)
            def _():
                epilogue(acc[...] + part(), extra_refs, out_refs)

    scratch = [pltpu.VMEM(acc_shape, F32)] if nk > 1 else []
    sem = ("parallel",) * (len(grid) - 1) + ("arbitrary",)
    return pl.pallas_call(
        body, name=name, grid=grid, in_specs=[a_spec, b_spec, *extra_specs, *([ANY_SPEC] * n_after)], out_specs=out_specs,
        out_shape=outs, scratch_shapes=scratch, compiler_params=_cp(sem))(a, b, *extra, *after)


def _store_bf16(acc, extra_refs, out_refs):
    out_refs[0][...] = acc.astype(BF16)


def _store_f32(acc, extra_refs, out_refs):
    out_refs[0][...] = acc


def _residual_epilogue(acc, extra_refs, out_refs):
    x_ref, gate_ref = extra_refs
    out_refs[0][...] = acc
    out_refs[1][...] = x_ref[...] + gate_ref[...] * acc


def kernel(x, c, w_ada, b_ada, g_mix, w_in, conv_w, conv_b, conv_ln_g, conv_ln_b, sgu_ln_g, sgu_ln_b, sgu_w, sgu_b, out_g_conv, out_g_sgu, w_out, g_ffn, w_up, ffn_conv_w, ffn_conv_b, w_down, g_final, loss_target, m_w_ada, m_b_ada, m_g_mix, m_w_in, m_conv_w, m_conv_b, m_conv_ln_g, m_conv_ln_b, m_sgu_ln_g, m_sgu_ln_b, m_sgu_w, m_sgu_b, m_out_g_conv, m_out_g_sgu, m_w_out, m_g_ffn, m_w_up, m_ffn_conv_w, m_ffn_conv_b, m_w_down, m_g_final, v_w_ada, v_b_ada, v_g_mix, v_w_in, v_conv_w, v_conv_b, v_conv_ln_g, v_conv_ln_b, v_sgu_ln_g, v_sgu_ln_b, v_sgu_w, v_sgu_b, v_out_g_conv, v_out_g_sgu, v_w_out, v_g_ffn, v_w_up, v_ffn_conv_w, v_ffn_conv_b, v_w_down, v_g_final):
    bsz, seq, d = x.shape
    t = bsz * seq
    n_batch = bsz * N_DEV
    ada_sh = w_ada.shape[2]
    n_mod = ada_sh * N_DEV // d
    win_sh = w_in.shape[2]
    kc = conv_w.shape[1]
    dc_sh = conv_w.shape[2]
    dc = dc_sh * N_DEV
    heads, ch = sgu_w.shape[1], sgu_w.shape[2]
    hd = dc // heads
    wout_sh = w_out.shape[1]
    cup = w_up.shape[2]
    kf = ffn_conv_w.shape[1]
    cw_sh = ffn_conv_w.shape[2]
    dff = cw_sh * N_DEV
    n_fs = dff // cup
    assert win_sh * N_DEV == 4 * dc and hd == LANES and kc // 2 < HALO and kf == 3 and 2 * n_fs == N_DEV
    assert w_down.shape[1] * 2 == cup and seq % ch == 0

    xi, yi, ci = _my_pos()
    me = 4 * xi + 2 * yi + ci

    tb = min(256, seq)
    tps = seq // tb
    nt = t // tb
    tbd = min(128, seq)
    tpd = seq // tbd
    ntd = t // tbd
    tm = min(512, seq)
    nm = t // tm

    xf = x.reshape(t, d)
    tgt = loss_target.reshape(t, d)

    small_shapes = [(bsz, d), (kc, dc_sh), (kf, cw_sh)]
    blob0 = _allgather("ag_small_in", _pack([c, conv_w[0], ffn_conv_w[0]]), True)
    per_dev = [_unpack(blob0[s], small_shapes) for s in range(N_DEV)]
    c_all = jnp.concatenate([p[0] for p in per_dev], axis=0)
    conv_w_full = jnp.concatenate([p[1] for p in per_dev], axis=1)
    ffn_w_full = jnp.concatenate([p[2] for p in per_dev], axis=1)
    ffn_w_s = ffn_w_full.reshape(kf, n_fs, cup).transpose(1, 0, 2)
    ffn_b_s = ffn_conv_b.reshape(n_fs, 1, cup)

    b_ada_sh = lax.dynamic_slice(b_ada, (0, me * ada_sh), (1, ada_sh))
    tn_ada = _blk(ada_sh, 512)

    def ada_body(c_ref, w_ref, b_ref, mod_ref, ca_ref):
        cc = c_ref[...]
        ca = (cc * _sigmoid(cc)).astype(BF16)
        ca_ref[...] = ca
        mod_ref[...] = jnp.dot(ca, w_ref[...].astype(BF16), preferred_element_type=F32) + b_ref[...]

    mod_sh, c_act = pl.pallas_call(
        ada_body, name="ada_fwd", grid=(ada_sh // tn_ada,),
        out_shape=(jax.ShapeDtypeStruct((n_batch, ada_sh), F32), jax.ShapeDtypeStruct((n_batch, d), BF16)),
        in_specs=[pl.BlockSpec((n_batch, d), lambda j: (0, 0)), pl.BlockSpec((d, tn_ada), lambda j: (0, j)),
                  pl.BlockSpec((1, tn_ada), lambda j: (0, j))],
        out_specs=(pl.BlockSpec((n_batch, tn_ada), lambda j: (0, j)), pl.BlockSpec((n_batch, d), lambda j: (0, 0))),
        compiler_params=_cp(("arbitrary",)))(c_all, w_ada[0], b_ada_sh)
    mod_all = _allgather("ag_mod", mod_sh, True)
    mod_me = lax.dynamic_slice(mod_all, (0, me * bsz, 0), (N_DEV, bsz, ada_sh))
    mod_me = mod_me.transpose(1, 0, 2).reshape(bsz, n_mod, 1, d)
    shift_m, scale_m, gate_m = mod_me[:, 0], mod_me[:, 1], mod_me[:, 2]
    shift_f, scale_f, gate_f = mod_me[:, 3], mod_me[:, 4], mod_me[:, 5]

    def gather_begin(name, w2d):
        shard = _cast_bf16("cast_" + name, w2d, mod_all)
        land = lax.dynamic_update_slice(lax.empty((N_DEV,) + shard.shape, BF16), shard[None], (me, 0, 0))
        return _gather_start("ag1_" + name, shard, land)

    def gather_end(name, started, after):
        send_sems, recv_sems, x_thru, land_thru, _ = started
        land = _gather_wait("ag1w_" + name, send_sems, recv_sems, x_thru, land_thru, after)
        return _gather_forward("ag2_" + name, land)

    w_up_t, m_w_up_t, v_w_up_t = (jnp.swapaxes(a[0], 0, 1) for a in (w_up, m_w_up, v_w_up))
    ag_in = gather_begin("w_in", w_in[0])
    ag_out = gather_begin("w_out", w_out[0])
    ag_up = gather_begin("w_up", w_up_t)
    ag_down = gather_begin("w_down", w_down[0])
    started = ag_in[4][0, 0] + ag_out[4][0, 0] + ag_up[4][0, 0] + ag_down[4][0, 0]

    row_d = pl.BlockSpec((tbd, d), lambda i: (i, 0))
    vec_d = pl.BlockSpec((1, d), lambda i: (0, 0))
    seq_d = pl.BlockSpec((None, 1, d), lambda i: (i // tpd, 0, 0))

    def modulate(name, xin, g, shift, scale):
        def body(x_ref, g_ref, sh_ref, sc_ref, h_ref):
            xx = x_ref[...]
            yy = xx * lax.rsqrt(_mean(xx * xx) + EPS)
            h_ref[...] = ((yy * g_ref[...]) * (1.0 + sc_ref[...]) + sh_ref[...]).astype(BF16)

        return pl.pallas_call(
            body, name=name, grid=(ntd,), out_shape=jax.ShapeDtypeStruct((t, d), BF16),
            in_specs=[row_d, vec_d, seq_d, seq_d], out_specs=row_d, compiler_params=_cp(("parallel",)))(xin, g, shift, scale)

    h1 = modulate("mod1_fwd", xf, g_mix + started, shift_m, scale_m)

    tk_d = _blk(d, 1024)
    w_in_g = gather_end("w_in", ag_in, h1)
    proj = _mm(
        "mm_proj", h1, w_in_g, mode="nn", grid=(nm, N_DEV, 1),
        a_spec=pl.BlockSpec((tm, d), lambda i, j, k: (i, 0)),
        b_spec=pl.BlockSpec((None, d, win_sh), lambda i, j, k: (j, 0, 0)),
        acc_shape=(tm, win_sh), outs=(jax.ShapeDtypeStruct((t, N_DEV * win_sh), BF16),),
        out_specs=(pl.BlockSpec((tm, win_sh), lambda i, j, k: (i, j)),), epilogue=_store_bf16)[0]

    hb = tb // HALO
    n_hb = t // HALO

    def main_col(col):
        return pl.BlockSpec((tb, dc), lambda i, col=col: (i, col))

    def chunk_col(col):
        return pl.BlockSpec((ch, dc), lambda i, col=col: (i, col))

    def prev_col(col):
        return pl.BlockSpec((HALO, dc), lambda i, col=col: (jnp.maximum(i * hb - 1, 0), col))

    def next_col(col):
        return pl.BlockSpec((HALO, dc), lambda i, col=col: (jnp.minimum((i + 1) * hb, n_hb - 1), col))

    vec_c = pl.BlockSpec((1, dc), lambda i: (0, 0))
    row_c = pl.BlockSpec((tb, dc), lambda i: (i, 0))
    convw_spec = pl.BlockSpec((kc, dc), lambda i: (0, 0))
    sguw_spec = pl.BlockSpec((heads, ch, ch), lambda i: (0, 0, 0))
    bias_spec = pl.BlockSpec((ch, dc), lambda i: (0, 0))
    n_chunk = tb // ch

    sgu_w_bf = sgu_w[0].astype(BF16)
    sgu_wt_bf = jnp.swapaxes(sgu_w[0], 1, 2).astype(BF16)
    sgu_bias = jnp.repeat(sgu_b[0].T, hd, axis=1)

    def fill_glu_ext(ext, i, pv, pg, pvp, pgp, pvn, pgn):
        first = (i % tps) == 0
        last = (i % tps) == tps - 1

        def glu(v_ref, g_ref):
            return v_ref[...].astype(F32) * _sigmoid(g_ref[...].astype(F32))

        ext[pl.ds(0, HALO), :] = jnp.where(first, 0.0, glu(pvp, pgp))
        ext[pl.ds(HALO, tb), :] = glu(pv, pg)
        ext[pl.ds(HALO + tb, HALO), :] = jnp.where(last, 0.0, glu(pvn, pgn))

    def layer_norm_stats(z):
        mu = _mean(z)
        zc = z - mu
        rstd = lax.rsqrt(_mean(zc * zc) + EPS)
        return zc * rstd, rstd

    def sgu_mix(vs_ref, w_ref, vnb, bias_ref, n_chunks):
        for cc in range(n_chunks):
            for h in range(heads):
                blk = jnp.dot(w_ref[h], vnb[cc * ch:(cc + 1) * ch, h * hd:(h + 1) * hd], preferred_element_type=F32)
                vs_ref[pl.ds(cc * ch, ch), pl.ds(h * hd, hd)] = blk + bias_ref[:, pl.ds(h * hd, hd)]

    def mix_fwd_body(pv, pg, pu, pw, pvp, pgp, pvn, pgn, cw, cb, lg, lb, og, slg, slb, sw, sbias, sog, y_ref, a1_ref, ext, vs_ref):
        i = pl.program_id(0)
        fill_glu_ext(ext, i, pv, pg, pvp, pgp, pvn, pgn)
        pad = kc // 2
        acc = jnp.zeros((tb, dc), F32) + cb[...]
        for k in range(kc):
            acc = acc + cw[pl.ds(k, 1), :] * ext[pl.ds(HALO - pad + k, tb), :]
        a1_ref[...] = acc
        xh, _ = layer_norm_stats(acc)
        a2 = xh * lg[...] + lb[...]
        a3 = a2 * _sigmoid(a2)
        ya = a3 * lax.rsqrt(_mean(a3 * a3) + EPS) * og[...]
        y_ref[:, pl.ds(0, dc)] = ya.astype(BF16)

        u = _gelu(pu[...].astype(F32))
        vv = _gelu(pw[...].astype(F32))
        xhv, _ = layer_norm_stats(vv)
        vn = xhv * slg[...] + slb[...]
        sgu_mix(vs_ref, sw, vn.astype(BF16), sbias, n_chunk)
        bg = u * vs_ref[...]
        yb = bg * lax.rsqrt(_mean(bg * bg) + EPS) * sog[...]
        y_ref[:, pl.ds(dc, dc)] = yb.astype(BF16)

    y, a1 = pl.pallas_call(
        mix_fwd_body, name="mix_fwd", grid=(nt,),
        out_shape=(jax.ShapeDtypeStruct((t, 2 * dc), BF16), jax.ShapeDtypeStruct((t, dc), F32)),
        in_specs=[main_col(0), main_col(1), main_col(2), main_col(3), prev_col(0), prev_col(1), next_col(0), next_col(1),
                  convw_spec, vec_c, vec_c, vec_c, vec_c, vec_c, vec_c, sguw_spec, bias_spec, vec_c],
        out_specs=(pl.BlockSpec((tb, 2 * dc), lambda i: (i, 0)), row_c),
        scratch_shapes=[pltpu.VMEM((tb + 2 * HALO, dc), F32), pltpu.VMEM((tb, dc), F32)],
        compiler_params=_cp(("parallel",)),
    )(proj, proj, proj, proj, proj, proj, proj, proj, conv_w_full, conv_b, conv_ln_g, conv_ln_b, out_g_conv,
      sgu_ln_g, sgu_ln_b, sgu_w_bf, sgu_bias, out_g_sgu)

    tn_d = _blk(d, 1024)
    gate_spec3 = pl.BlockSpec((None, 1, tn_d), lambda i, j, k: (i * tm // seq, 0, j))
    res_spec3 = pl.BlockSpec((tm, tn_d), lambda i, j, k: (i, j))
    td_shape = jax.ShapeDtypeStruct((t, d), F32)

    w_out_full = gather_end("w_out", ag_out, y).reshape(d, d)
    o_mix, x1 = _mm(
        "mm_out", y, w_out_full, mode="nn", grid=(nm, d // tn_d, 1),
        a_spec=pl.BlockSpec((tm, d), lambda i, j, k: (i, 0)), b_spec=pl.BlockSpec((d, tn_d), lambda i, j, k: (0, j)),
        acc_shape=(tm, tn_d), outs=(td_shape, td_shape), out_specs=(res_spec3, res_spec3),
        epilogue=_residual_epilogue, extra=(xf, gate_m), extra_specs=(res_spec3, gate_spec3))

    h2 = modulate("mod2_fwd", x1, g_ffn, shift_f, scale_f)

    tm_up = min(1024, seq)
    nk_d = d // tk_d
    w_up_g = gather_end("w_up", ag_up, h2)
    up = _mm(
        "mm_up", h2, w_up_g, mode="nt", grid=(t // tm_up, N_DEV, nk_d),
        a_spec=pl.BlockSpec((tm_up, tk_d), lambda i, j, k: (i, k)),
        b_spec=pl.BlockSpec((None, cup, tk_d), lambda i, j, k: (j, 0, k)),
        acc_shape=(tm_up, cup), outs=(jax.ShapeDtypeStruct((N_DEV, t, cup), BF16),),
        out_specs=(pl.BlockSpec((None, tm_up, cup), lambda i, j, k: (j, i, 0)),), epilogue=_store_bf16)[0]
    up4 = up.reshape(2, n_fs, t, cup)

    def ffn_halo(which, lo):
        if lo:
            return pl.BlockSpec((None, None, HALO, cup), lambda j, i: (which, j, jnp.maximum(i * hb - 1, 0), 0))
        return pl.BlockSpec((None, None, HALO, cup), lambda j, i: (which, j, jnp.minimum((i + 1) * hb, n_hb - 1), 0))

    pair_spec = pl.BlockSpec((2, None, tb, cup), lambda j, i: (0, j, i, 0))
    fw_spec = pl.BlockSpec((None, kf, cup), lambda j, i: (j, 0, 0))
    fb_spec = pl.BlockSpec((None, 1, cup), lambda j, i: (j, 0, 0))
    act_spec = pl.BlockSpec((None, tb, cup), lambda j, i: (j, i, 0))

    RC = HALO
    LC = 2 * LANES
    pieces = [(l0, min(LC, cup - l0)) for l0 in range(0, cup, LC)]
    n_rc = tb // RC
    pad = SUBLANES

    def fill_gate_ext(ext, i, cols, gate_ref, prev_ref, next_ref):
        first = (i % tps) == 0
        last = (i % tps) == tps - 1
        lw = cols.size
        ext[pl.ds(0, pad), cols] = jnp.zeros((pad, lw), F32)
        ext[pl.ds(pad, HALO), cols] = jnp.where(first, 0.0, prev_ref[:, cols].astype(F32))
        for rr in range(n_rc):
            ext[pl.ds(pad + HALO + rr * RC, RC), cols] = gate_ref[0, pl.ds(rr * RC, RC), cols].astype(F32)
        ext[pl.ds(pad + HALO + tb, HALO), cols] = jnp.where(last, 0.0, next_ref[:, cols].astype(F32))
        ext[pl.ds(pad + 2 * HALO + tb, pad), cols] = jnp.zeros((pad, lw), F32)

    def tap_rows(fw, fb, cols):
        lw = cols.size
        return [jnp.broadcast_to(fw[pl.ds(k, 1), cols], (RC, lw)) for k in range(kf)] + [jnp.broadcast_to(fb[:, cols], (RC, lw))]

    def ffn_fwd_body(pair, gp, gn, fw, fb, act_ref, ext):
        i = pl.program_id(1)
        for l0, lw in pieces:
            cols = pl.ds(l0, lw)
            fill_gate_ext(ext, i, cols, pair, gp, gn)
            w0, w1, w2, bb = tap_rows(fw, fb, cols)
            for rr in range(n_rc):
                e0 = pad + HALO + rr * RC
                gte = bb + w0 * ext[pl.ds(e0 - 1, RC), cols]
                gte = gte + w1 * ext[pl.ds(e0, RC), cols]
                gte = gte + w2 * ext[pl.ds(e0 + 1, RC), cols]
                val = pair[1, pl.ds(rr * RC, RC), cols].astype(F32)
                act_ref[pl.ds(rr * RC, RC), cols] = (gte * _sigmoid(gte) * val).astype(BF16)

    ext_rows = tb + 2 * HALO + 2 * pad
    act = pl.pallas_call(
        ffn_fwd_body, name="ffn_fwd", grid=(n_fs, nt), out_shape=jax.ShapeDtypeStruct((n_fs, t, cup), BF16),
        in_specs=[pair_spec, ffn_halo(0, True), ffn_halo(0, False), fw_spec, fb_spec], out_specs=act_spec,
        scratch_shapes=[pltpu.VMEM((ext_rows, cup), F32)], compiler_params=_cp(("parallel", "parallel")),
    )(up4, up4, up4, ffn_w_s, ffn_b_s)

    w_down_s = gather_end("w_down", ag_down, act).reshape(n_fs, cup, d)
    gate_spec_f = pl.BlockSpec((None, 1, tn_d), lambda i, j, k: (i * tm // seq, 0, j))
    dn, x2 = _mm(
        "mm_down", act, w_down_s, mode="nn", grid=(nm, d // tn_d, n_fs),
        a_spec=pl.BlockSpec((None, tm, cup), lambda i, j, k: (k, i, 0)),
        b_spec=pl.BlockSpec((None, cup, tn_d), lambda i, j, k: (k, 0, j)),
        acc_shape=(tm, tn_d), outs=(td_shape, td_shape), out_specs=(res_spec3, res_spec3),
        epilogue=_residual_epilogue, extra=(x1, gate_f), extra_specs=(res_spec3, gate_spec_f))

    acc_d = pl.BlockSpec((1, d), lambda i: (0, 0))
    seq_acc = pl.BlockSpec((None, 1, d), lambda i: (i // tpd, 0, 0))

    def head_body(x_ref, t_ref, g_ref, dn_ref, gate_ref, dx_ref, ddn_ref, loss_ref, dg_ref, dgate_ref):
        i = pl.program_id(0)
        xx = x_ref[...]
        rr = lax.rsqrt(_mean(xx * xx) + EPS)
        xn = xx * rr
        err = xn * g_ref[...] - t_ref[...]
        dyf = err * (1.0 / d)
        dxn = dyf * g_ref[...]
        dx = rr * (dxn - xn * _mean(dxn * xn))
        dx_ref[...] = dx
        ddn_ref[...] = (gate_ref[...] * dx).astype(BF16)
        part = 0.5 * jnp.sum(_mean(err * err), axis=0, keepdims=True)

        @pl.when(i == 0)
        def _():
            loss_ref[...] = jnp.zeros_like(loss_ref)
            dg_ref[...] = jnp.zeros_like(dg_ref)

        @pl.when(i % tpd == 0)
        def _():
            dgate_ref[...] = jnp.zeros_like(dgate_ref)

        loss_ref[...] += jnp.broadcast_to(part, loss_ref.shape)
        dg_ref[...] += _colsum(dyf * xn)
        dgate_ref[...] += _colsum(dx * dn_ref[...])

    seq_shape = jax.ShapeDtypeStruct((bsz, 1, d), F32)
    vec_shape = jax.ShapeDtypeStruct((1, d), F32)
    dx2, ddn, loss_part, dg_final, dgate_f = pl.pallas_call(
        head_body, name="loss_head", grid=(ntd,),
        out_shape=(td_shape, jax.ShapeDtypeStruct((t, d), BF16), jax.ShapeDtypeStruct((1, LANES), F32), vec_shape, seq_shape),
        in_specs=[row_d, row_d, vec_d, row_d, seq_d],
        out_specs=(row_d, row_d, pl.BlockSpec((1, LANES), lambda i: (0, 0)), acc_d, seq_acc),
        compiler_params=_cp(("arbitrary",)))(x2, tgt, g_final.reshape(1, d), dn, gate_f)

    tk_t = min(1024, t)
    nk_t = t // tk_t
    tn_gw = _blk(d, 1024)
    gw_down = _mm(
        "mm_gw_down", act, ddn, mode="tn", grid=(n_fs, d // tn_gw, nk_t),
        a_spec=pl.BlockSpec((None, tk_t, cup), lambda j, n, k: (j, k, 0)),
        b_spec=pl.BlockSpec((tk_t, tn_gw), lambda j, n, k: (k, n)),
        acc_shape=(cup, tn_gw), outs=(jax.ShapeDtypeStruct((n_fs, cup, d), BF16),),
        out_specs=(pl.BlockSpec((None, cup, tn_gw), lambda j, n, k: (j, 0, n)),), epilogue=_store_bf16)[0]

    def exchange_begin(name, gw):
        return _alltoall_start("a2a_" + name, gw.reshape(N_DEV, gw.size // (N_DEV * gw.shape[-1]), gw.shape[-1]))

    x_down = exchange_begin("w_down", gw_down)
    dact = _mm(
        "mm_dact", ddn, w_down_s, mode="nt", grid=(t // tm_up, n_fs, nk_d),
        a_spec=pl.BlockSpec((tm_up, tk_d), lambda i, j, k: (i, k)),
        b_spec=pl.BlockSpec((None, cup, tk_d), lambda i, j, k: (j, 0, k)),
        acc_shape=(tm_up, cup), outs=(jax.ShapeDtypeStruct((n_fs, t, cup), BF16),),
        out_specs=(pl.BlockSpec((None, tm_up, cup), lambda i, j, k: (j, i, 0)),), epilogue=_store_bf16, after=(x_down[4],))[0]


    def ffn_bwd_body(pair, gp, gn, vp, vn, da, dap, dan, fw, fb, dup_ref, dw_ref, db_ref, ext, dg_e):
        i = pl.program_id(1)
        first = (i % tps) == 0
        last = (i % tps) == tps - 1

        @pl.when(i == 0)
        def _():
            dw_ref[...] = jnp.zeros_like(dw_ref)
            db_ref[...] = jnp.zeros_like(db_ref)

        for l0, lw in pieces:
            cols = pl.ds(l0, lw)
            fill_gate_ext(ext, i, cols, pair, gp, gn)
            w0, w1, w2, bb = tap_rows(fw, fb, cols)
            sums = [jnp.zeros((RC, lw), F32) for _ in range(kf + 1)]
            for cc in range(n_rc + 2):
                e0 = pad + cc * RC
                taps = [ext[pl.ds(e0 - 1, RC), cols], ext[pl.ds(e0, RC), cols], ext[pl.ds(e0 + 1, RC), cols]]
                gte = bb + w0 * taps[0]
                gte = gte + w1 * taps[1]
                gte = gte + w2 * taps[2]
                sg = _sigmoid(gte)
                if cc == 0:
                    val, dact_v = vp[:, cols], dap[:, cols]
                elif cc == n_rc + 1:
                    val, dact_v = vn[:, cols], dan[:, cols]
                else:
                    rows = pl.ds((cc - 1) * RC, RC)
                    val, dact_v = pair[1, rows, cols], da[rows, cols]
                val, dact_v = val.astype(F32), dact_v.astype(F32)
                dgte = dact_v * val * (sg * (1.0 + gte * (1.0 - sg)))
                if cc == 0:
                    dgte = jnp.where(first, 0.0, dgte)
                elif cc == n_rc + 1:
                    dgte = jnp.where(last, 0.0, dgte)
                else:
                    dup_ref[1, rows, cols] = (dact_v * (gte * sg)).astype(BF16)
                    for k in range(kf):
                        sums[k] = sums[k] + dgte * taps[k]
                    sums[kf] = sums[kf] + dgte
                dg_e[pl.ds(cc * RC, RC), cols] = dgte
            for k in range(kf):
                dw_ref[pl.ds(k, 1), cols] += _colsum(sums[k])
            db_ref[:, cols] += _colsum(sums[kf])
            for rr in range(n_rc):
                e0 = HALO + rr * RC
                dgate = w0 * dg_e[pl.ds(e0 + 1, RC), cols] + w1 * dg_e[pl.ds(e0, RC), cols] + w2 * dg_e[pl.ds(e0 - 1, RC), cols]
                dup_ref[0, pl.ds(rr * RC, RC), cols] = dgate.astype(BF16)

    def act_halo(lo):
        if lo:
            return pl.BlockSpec((None, HALO, cup), lambda j, i: (j, jnp.maximum(i * hb - 1, 0), 0))
        return pl.BlockSpec((None, HALO, cup), lambda j, i: (j, jnp.minimum((i + 1) * hb, n_hb - 1), 0))

    dup4, g_ffn_w_s, g_ffn_b_s = pl.pallas_call(
        ffn_bwd_body, name="ffn_bwd", grid=(n_fs, nt),
        out_shape=(jax.ShapeDtypeStruct((2, n_fs, t, cup), BF16), jax.ShapeDtypeStruct((n_fs, kf, cup), F32),
                   jax.ShapeDtypeStruct((n_fs, 1, cup), F32)),
        in_specs=[pair_spec, ffn_halo(0, True), ffn_halo(0, False), ffn_halo(1, True), ffn_halo(1, False),
                  act_spec, act_halo(True), act_halo(False), fw_spec, fb_spec],
        out_specs=(pair_spec, fw_spec, fb_spec),
        scratch_shapes=[pltpu.VMEM((ext_rows, cup), F32), pltpu.VMEM((tb + 2 * HALO, cup), F32)],
        compiler_params=_cp(("parallel", "arbitrary")),
    )(up4, up4, up4, up4, up4, dact, dact, dact, ffn_w_s, ffn_b_s)
    dup = dup4.reshape(N_DEV, t, cup)

    tm_w = _blk(d, 1024)
    gw_up = _mm(
        "mm_gw_up", dup, h2, mode="tn", grid=(N_DEV, d // tn_gw, nk_t),
        a_spec=pl.BlockSpec((None, tk_t, cup), lambda j, n, k: (j, k, 0)),
        b_spec=pl.BlockSpec((tk_t, tn_gw), lambda j, n, k: (k, n)),
        acc_shape=(cup, tn_gw), outs=(jax.ShapeDtypeStruct((N_DEV, cup, d), BF16),),
        out_specs=(pl.BlockSpec((None, cup, tn_gw), lambda j, n, k: (j, 0, n)),), epilogue=_store_bf16)[0]

    x_up = exchange_begin("w_up", gw_up)
    dh2 = _mm(
        "mm_dh2", dup, w_up_g, mode="nn", grid=(t // tm_up, d // tn_d, N_DEV),
        a_spec=pl.BlockSpec((None, tm_up, cup), lambda i, j, k: (k, i, 0)),
        b_spec=pl.BlockSpec((None, cup, tn_d), lambda i, j, k: (k, 0, j)),
        acc_shape=(tm_up, tn_d), outs=(td_shape,), out_specs=(pl.BlockSpec((tm_up, tn_d), lambda i, j, k: (i, j)),),
        epilogue=_store_f32, after=(x_up[4],))[0]

    def modulate_bwd(name, xin, dh, dres, g, scale, gate=None, branch=None):
        gated = gate is not None

        def body(*refs):
            x_ref, dh_ref, dres_ref, g_ref, sc_ref = refs[:5]
            rest = refs[5:]
            if gated:
                gate_ref, br_ref, dx_ref, dsh_ref, dsc_ref, dg_ref, do_ref, dgate_ref = rest
            else:
                dx_ref, dsh_ref, dsc_ref, dg_ref = rest
            i = pl.program_id(0)
            xx = x_ref[...]
            rr = lax.rsqrt(_mean(xx * xx) + EPS)
            xn = xx * rr
            s1 = 1.0 + sc_ref[...]
            dhh = dh_ref[...]
            dxn = dhh * g_ref[...] * s1
            dx = dres_ref[...] + rr * (dxn - xn * _mean(dxn * xn))
            dx_ref[...] = dx

            @pl.when(i == 0)
            def _():
                dg_ref[...] = jnp.zeros_like(dg_ref)

            @pl.when(i % tpd == 0)
            def _():
                dsh_ref[...] = jnp.zeros_like(dsh_ref)
                dsc_ref[...] = jnp.zeros_like(dsc_ref)
                if gated:
                    dgate_ref[...] = jnp.zeros_like(dgate_ref)

            dsh_ref[...] += _colsum(dhh)
            dsc_ref[...] += _colsum(dhh * (xn * g_ref[...]))
            dg_ref[...] += _colsum(dhh * s1 * xn)
            if gated:
                do_ref[...] = (gate_ref[...] * dx).astype(BF16)
                dgate_ref[...] += _colsum(dx * br_ref[...])

        ins = [xin, dh, dres, g, scale]
        in_specs = [row_d, row_d, row_d, vec_d, seq_d]
        outs = [td_shape, seq_shape, seq_shape, vec_shape]
        out_specs = [row_d, seq_acc, seq_acc, acc_d]
        if gated:
            ins += [gate, branch]
            in_specs += [seq_d, row_d]
            outs += [jax.ShapeDtypeStruct((t, d), BF16), seq_shape]
            out_specs += [row_d, seq_acc]
        return pl.pallas_call(
            body, name=name, grid=(ntd,), out_shape=tuple(outs), in_specs=in_specs, out_specs=tuple(out_specs),
            compiler_params=_cp(("arbitrary",)))(*ins)

    dx1, dshift_f, dscale_f, dg_ffn, d_o, dgate_m = modulate_bwd("mod2_bwd", x1, dh2, dx2, g_ffn, scale_f, gate_m, o_mix)

    tn_w = _blk(d, 2048)
    gw_out = _mm(
        "mm_gw_out", y, d_o, mode="tn", grid=(d // tm_w, d // tn_w, nk_t),
        a_spec=pl.BlockSpec((tk_t, tm_w), lambda i, j, k: (k, i)), b_spec=pl.BlockSpec((tk_t, tn_w), lambda i, j, k: (k, j)),
        acc_shape=(tm_w, tn_w), outs=(jax.ShapeDtypeStruct((d, d), BF16),),
        out_specs=(pl.BlockSpec((tm_w, tn_w), lambda i, j, k: (i, j)),), epilogue=_store_bf16)[0]

    x_out = exchange_begin("w_out", gw_out)
    dy = _mm(
        "mm_dy", d_o, w_out_full, mode="nt", grid=(nm, d // tn_d, 1),
        a_spec=pl.BlockSpec((tm, d), lambda i, j, k: (i, 0)), b_spec=pl.BlockSpec((tn_d, d), lambda i, j, k: (j, 0)),
        acc_shape=(tm, tn_d), outs=(jax.ShapeDtypeStruct((t, d), BF16),),
        out_specs=(pl.BlockSpec((tm, tn_d), lambda i, j, k: (i, j)),), epilogue=_store_bf16, after=(x_out[4],))[0]

    def conv_norm_bwd_body(dy_ref, a1_ref, lg, lb, og, da1_ref, dog_ref, dlg_ref, dlb_ref):
        i = pl.program_id(0)
        xh, rstd = layer_norm_stats(a1_ref[...])
        a2 = xh * lg[...] + lb[...]
        sg = _sigmoid(a2)
        a3 = a2 * sg
        r3 = lax.rsqrt(_mean(a3 * a3) + EPS)
        n3 = a3 * r3
        dya = dy_ref[...].astype(F32)
        dn3 = dya * og[...]
        da3 = r3 * (dn3 - n3 * _mean(dn3 * n3))
        da2 = da3 * (sg * (1.0 + a2 * (1.0 - sg)))
        dxh = da2 * lg[...]
        da1_ref[...] = rstd * (dxh - _mean(dxh) - xh * _mean(dxh * xh))

        @pl.when(i == 0)
        def _():
            dog_ref[...] = jnp.zeros_like(dog_ref)
            dlg_ref[...] = jnp.zeros_like(dlg_ref)
            dlb_ref[...] = jnp.zeros_like(dlb_ref)

        dog_ref[...] += _colsum(dya * n3)
        dlg_ref[...] += _colsum(da2 * xh)
        dlb_ref[...] += _colsum(da2)

    vecc_shape = jax.ShapeDtypeStruct((1, dc), F32)
    da1, g_og_conv, g_cln_g, g_cln_b = pl.pallas_call(
        conv_norm_bwd_body, name="conv_norm_bwd", grid=(nt,),
        out_shape=(jax.ShapeDtypeStruct((t, dc), F32), vecc_shape, vecc_shape, vecc_shape),
        in_specs=[main_col(0), row_c, vec_c, vec_c, vec_c], out_specs=(row_c, vec_c, vec_c, vec_c),
        compiler_params=_cp(("arbitrary",)))(dy, a1, conv_ln_g, conv_ln_b, out_g_conv)

    def sgu_bwd_body(dy_ref, pu, pw, slg, slb, sw, swt, sbias, sog, dproj_ref, dsog_ref, dslg_ref, dslb_ref, dsw_ref, dsb_ref, vs_ref, dvn_ref):
        i = pl.program_id(0)
        zu = pu[...].astype(F32)
        zv = pw[...].astype(F32)
        u = _gelu(zu)
        vv = _gelu(zv)
        xhv, rstd = layer_norm_stats(vv)
        vnb = (xhv * slg[...] + slb[...]).astype(BF16)
        sgu_mix(vs_ref, sw, vnb, sbias, 1)
        vs = vs_ref[...]
        bg = u * vs
        rb = lax.rsqrt(_mean(bg * bg) + EPS)
        nb = bg * rb
        dyb = dy_ref[...].astype(F32)
        dnb = dyb * sog[...]
        dbg = rb * (dnb - nb * _mean(dnb * nb))
        du = dbg * vs
        dvs = dbg * u
        dvsb = dvs.astype(BF16)

        @pl.when(i == 0)
        def _():
            dsog_ref[...] = jnp.zeros_like(dsog_ref)
            dslg_ref[...] = jnp.zeros_like(dslg_ref)
            dslb_ref[...] = jnp.zeros_like(dslb_ref)
            dsw_ref[...] = jnp.zeros_like(dsw_ref)
            dsb_ref[...] = jnp.zeros_like(dsb_ref)

        dsog_ref[...] += _colsum(dyb * nb)
        dsb_ref[...] += dvs
        for h in range(heads):
            dblk = dvsb[:, h * hd:(h + 1) * hd]
            vblk = vnb[:, h * hd:(h + 1) * hd]
            dsw_ref[h] += lax.dot_general(dblk, vblk, (((1,), (1,)), ((), ())), preferred_element_type=F32)
            dvn_ref[:, pl.ds(h * hd, hd)] = jnp.dot(swt[h], dblk, preferred_element_type=F32)
        dvn = dvn_ref[...]
        dslg_ref[...] += _colsum(dvn * xhv)
        dslb_ref[...] += _colsum(dvn)
        dxh = dvn * slg[...]
        dvv = rstd * (dxh - _mean(dxh) - xhv * _mean(dxh * xhv))
        dproj_ref[:, pl.ds(0, dc)] = (du * _gelu_grad(zu)).astype(BF16)
        dproj_ref[:, pl.ds(dc, dc)] = (dvv * _gelu_grad(zv)).astype(BF16)

    dproj, g_og_sgu, g_sln_g, g_sln_b, g_sgu_w, g_sgu_bias = pl.pallas_call(
        sgu_bwd_body, name="sgu_bwd", grid=(t // ch,),
        out_shape=(jax.ShapeDtypeStruct((t, 4 * dc), BF16), vecc_shape, vecc_shape, vecc_shape,
                   jax.ShapeDtypeStruct((heads, ch, ch), F32), jax.ShapeDtypeStruct((ch, dc), F32)),
        in_specs=[chunk_col(1), chunk_col(2), chunk_col(3), vec_c, vec_c, sguw_spec, sguw_spec, bias_spec, vec_c],
        out_specs=(pl.BlockSpec((ch, 2 * dc), lambda i: (i, 1)), vec_c, vec_c, vec_c, sguw_spec, bias_spec),
        scratch_shapes=[pltpu.VMEM((ch, dc), F32), pltpu.VMEM((ch, dc), F32)],
        compiler_params=_cp(("arbitrary",)),
    )(dy, proj, proj, sgu_ln_g, sgu_ln_b, sgu_w_bf, sgu_wt_bf, sgu_bias, out_g_sgu)

    def sgu_bias_reduce_body(b_ref, o_ref):
        lane = lax.broadcasted_iota(jnp.int32, (ch, LANES), 1)
        res = jnp.zeros((ch, LANES), F32)
        for h in range(heads):
            res = jnp.where(lane == h, jnp.sum(b_ref[:, pl.ds(h * hd, hd)], axis=1, keepdims=True), res)
        o_ref[...] = res

    g_sgu_b_t = pl.pallas_call(
        sgu_bias_reduce_body, name="sgu_bias_reduce", out_shape=jax.ShapeDtypeStruct((ch, LANES), F32),
        in_specs=[pl.BlockSpec(memory_space=pltpu.VMEM)], out_specs=pl.BlockSpec(memory_space=pltpu.VMEM),
        compiler_params=_cp())(g_sgu_bias)
    g_sgu_b = g_sgu_b_t[:, :heads].T

    def conv_bwd_body(da, dap, dan, pv, pg, pvp, pgp, pvn, pgn, cw, dproj_in, dproj_ref, dcw_ref, dcb_ref, ext, dext):
        del dproj_in
        i = pl.program_id(0)
        first = (i % tps) == 0
        last = (i % tps) == tps - 1
        fill_glu_ext(ext, i, pv, pg, pvp, pgp, pvn, pgn)
        da_m = da[...]
        dext[pl.ds(0, HALO), :] = jnp.where(first, 0.0, dap[...])
        dext[pl.ds(HALO, tb), :] = da_m
        dext[pl.ds(HALO + tb, HALO), :] = jnp.where(last, 0.0, dan[...])
        pad = kc // 2

        @pl.when(i == 0)
        def _():
            dcw_ref[...] = jnp.zeros_like(dcw_ref)
            dcb_ref[...] = jnp.zeros_like(dcb_ref)

        da0 = jnp.zeros((tb, dc), F32)
        for k in range(kc):
            da0 = da0 + cw[pl.ds(k, 1), :] * dext[pl.ds(HALO + pad - k, tb), :]
            dcw_ref[pl.ds(k, 1), :] += _colsum(da_m * ext[pl.ds(HALO - pad + k, tb), :])
        dcb_ref[...] += _colsum(da_m)
        vv = pv[...].astype(F32)
        sg = _sigmoid(pg[...].astype(F32))
        dproj_ref[:, pl.ds(0, dc)] = (da0 * sg).astype(BF16)
        dproj_ref[:, pl.ds(dc, dc)] = (da0 * vv * sg * (1.0 - sg)).astype(BF16)

    def halo_rows(lo):
        if lo:
            return pl.BlockSpec((HALO, dc), lambda i: (jnp.maximum(i * hb - 1, 0), 0))
        return pl.BlockSpec((HALO, dc), lambda i: (jnp.minimum((i + 1) * hb, n_hb - 1), 0))

    dproj, g_conv_w, g_conv_b = pl.pallas_call(
        conv_bwd_body, name="conv_bwd", grid=(nt,),
        out_shape=(jax.ShapeDtypeStruct((t, 4 * dc), BF16), jax.ShapeDtypeStruct((kc, dc), F32), vecc_shape),
        in_specs=[row_c, halo_rows(True), halo_rows(False), main_col(0), main_col(1), prev_col(0), prev_col(1),
                  next_col(0), next_col(1), convw_spec, pl.BlockSpec(memory_space=pl.ANY)],
        out_specs=(pl.BlockSpec((tb, 2 * dc), lambda i: (i, 0)), convw_spec, vec_c),
        scratch_shapes=[pltpu.VMEM((tb + 2 * HALO, dc), F32), pltpu.VMEM((tb + 2 * HALO, dc), F32)],
        input_output_aliases={10: 0}, compiler_params=_cp(("arbitrary",)),
    )(da1, da1, da1, proj, proj, proj, proj, proj, proj, conv_w_full, dproj)

    gw_in = _mm(
        "mm_gw_in", h1, dproj, mode="tn", grid=(N_DEV, d // tm_w, nk_t),
        a_spec=pl.BlockSpec((tk_t, tm_w), lambda j, i, k: (k, i)),
        b_spec=pl.BlockSpec((tk_t, win_sh), lambda j, i, k: (k, j)),
        acc_shape=(tm_w, win_sh), outs=(jax.ShapeDtypeStruct((N_DEV, d, win_sh), BF16),),
        out_specs=(pl.BlockSpec((None, tm_w, win_sh), lambda j, i, k: (j, i, 0)),), epilogue=_store_bf16)[0]

    x_in = exchange_begin("w_in", gw_in)
    tn_h = _blk(d, 2048)
    dh1 = _mm(
        "mm_dh1", dproj, w_in_g, mode="nt", grid=(nm, d // tn_h, N_DEV),
        a_spec=pl.BlockSpec((tm, win_sh), lambda i, j, k: (i, k)),
        b_spec=pl.BlockSpec((None, tn_h, win_sh), lambda i, j, k: (k, j, 0)),
        acc_shape=(tm, tn_h), outs=(td_shape,), out_specs=(pl.BlockSpec((tm, tn_h), lambda i, j, k: (i, j)),),
        epilogue=_store_f32, after=(x_in[4],))[0]

    grad_x, dshift_m, dscale_m, dg_mix = modulate_bwd("mod1_bwd", xf, dh1, dx1, g_mix, scale_m)

    dmod = jnp.concatenate([dshift_m, dscale_m, dgate_m, dshift_f, dscale_f, dgate_f], axis=1)
    g_ffn_w_full = g_ffn_w_s.transpose(1, 0, 2).reshape(kf, dff)
    rep_names = ["g_mix", "conv_b", "conv_ln_g", "conv_ln_b", "sgu_ln_g", "sgu_ln_b", "sgu_w", "sgu_b",
                 "out_g_conv", "out_g_sgu", "g_ffn", "ffn_conv_b", "g_final"]
    rep_w = [g_mix, conv_b, conv_ln_g, conv_ln_b, sgu_ln_g, sgu_ln_b, sgu_w, sgu_b, out_g_conv, out_g_sgu, g_ffn, ffn_conv_b, g_final]
    rep_m = [m_g_mix, m_conv_b, m_conv_ln_g, m_conv_ln_b, m_sgu_ln_g, m_sgu_ln_b, m_sgu_w, m_sgu_b, m_out_g_conv, m_out_g_sgu,
             m_g_ffn, m_ffn_conv_b, m_g_final]
    rep_v = [v_g_mix, v_conv_b, v_conv_ln_g, v_conv_ln_b, v_sgu_ln_g, v_sgu_ln_b, v_sgu_w, v_sgu_b, v_out_g_conv, v_out_g_sgu,
             v_g_ffn, v_ffn_conv_b, v_g_final]
    rep_g = [dg_mix, g_conv_b, g_cln_g, g_cln_b, g_sln_g, g_sln_b, g_sgu_w, g_sgu_b, g_og_conv, g_og_sgu, dg_ffn,
             g_ffn_b_s, dg_final]
    rep_shapes = [w.shape for w in rep_w]
    extra_g = [g_conv_w, g_ffn_w_full, loss_part[:, :1]]
    extra_shapes = [(kc, dc), (kf, dff), (1, 1)]
    sum_blob = _pack(rep_g + extra_g)
    n_sum_rows = sum_blob.shape[0]
    blob1 = jnp.concatenate([sum_blob, _pack([dmod])], axis=0)
    small_land = lax.dynamic_update_slice(lax.empty((N_DEV,) + blob1.shape, F32), blob1[None], (me, 0, 0))
    small_started = _gather_start("ag1_small_grads", blob1, small_land)

    def big(name, started, after, w2d, m2d, v2d):
        send_sems, recv_sems, x_thru, land_thru, _ = started
        mine, recv = _alltoall_wait("a2aw_" + name, send_sems, recv_sems, x_thru, land_thru, after)
        return _adamw("adamw_" + name, w2d, m2d, v2d, slots=recv, own=mine, me=me)

    wdown_out = big("w_down", x_down, small_started[4], w_down[0], m_w_down[0], v_w_down[0])
    wup_out = big("w_up", x_up, wdown_out[0], w_up_t, m_w_up_t, v_w_up_t)
    wout_out = big("w_out", x_out, wup_out[0], w_out[0], m_w_out[0], v_w_out[0])

    gathered = gather_end("small_grads", small_started, wout_out[0])
    summed = _sum_slots("sum_small_grads", gathered[:, :n_sum_rows])
    n_rep_rows = _pack(rep_g).shape[0]
    g_conv_w_all, g_ffn_w_all, loss_all = _unpack(summed[n_rep_rows:], extra_shapes)
    loss = loss_all[0, 0]
    dmod_all = gathered[:, n_sum_rows:].reshape(N_DEV, -1)[:, :bsz * n_mod * d].reshape(n_batch, n_mod * d)

    rep_out = _adamw("adamw_small", _pack(rep_w), _pack(rep_m), _pack(rep_v), grad=summed[:n_rep_rows])
    rep_out = [_unpack(o, rep_shapes) for o in rep_out]
    rep = {name: tuple(rep_out[q][p] for q in range(4)) for p, name in enumerate(rep_names)}

    g_conv_w_me = lax.dynamic_slice(g_conv_w_all, (0, me * dc_sh), (kc, dc_sh))
    cw_out = _adamw("adamw_conv_w", conv_w[0], m_conv_w[0], v_conv_w[0], grad=g_conv_w_me)
    g_ffn_w_me = lax.dynamic_slice(g_ffn_w_all, (0, me * cw_sh), (kf, cw_sh))
    fw_out = _adamw("adamw_ffn_conv_w", ffn_conv_w[0], m_ffn_conv_w[0], v_ffn_conv_w[0], grad=g_ffn_w_me)

    g_b_ada = _col_sums("grad_b_ada", dmod_all)
    bada_out = _adamw("adamw_b_ada", b_ada, m_b_ada, v_b_ada, grad=g_b_ada)
    dmod_sh = lax.dynamic_slice(dmod_all, (0, me * ada_sh), (n_batch, ada_sh)).astype(BF16)
    wada_out = _adamw("adamw_w_ada", w_ada[0], m_w_ada[0], v_w_ada[0], ca_t=c_act.T, dmod=dmod_sh)

    win_out = big("w_in", x_in, wada_out[0], w_in[0], m_w_in[0], v_w_in[0])
    wup_out = tuple(jnp.swapaxes(o, 0, 1) for o in wup_out)

    def lead(outs4):
        return tuple(o[None] for o in outs4)

    results = {
        "w_ada": lead(wada_out), "b_ada": bada_out, "w_in": lead(win_out), "conv_w": lead(cw_out),
        "w_out": lead(wout_out), "w_up": lead(wup_out), "ffn_conv_w": lead(fw_out), "w_down": lead(wdown_out),
    }
    results.update(rep)
    order = ["w_ada", "b_ada", "g_mix", "w_in", "conv_w", "conv_b", "conv_ln_g", "conv_ln_b", "sgu_ln_g", "sgu_ln_b",
             "sgu_w", "sgu_b", "out_g_conv", "out_g_sgu", "w_out", "g_ffn", "w_up", "ffn_conv_w", "ffn_conv_b", "w_down", "g_final"]
    out = [loss, grad_x.reshape(bsz, seq, d)]
    for q in range(4):
        out += [results[n][q] for n in order]
    return tuple(out)
```

```python
import functools

import jax
import jax.numpy as jnp
from jax import lax
from jax.experimental import pallas as pl
from jax.experimental.pallas import tpu as pltpu

EPS = 1e-6
N_DEV = 8
MESH_ID = pl.DeviceIdType.MESH
V7X_VMEM_BYTES = 64 * 1024 * 1024
VMEM_LIMIT = V7X_VMEM_BYTES - 8 * 1024 * 1024
LANES = 128
SUBLANES = 8
HALO = 16
BLOB_ALIGN = SUBLANES * LANES

ADAM_LR = 0.001
ADAM_B1 = 0.9
ADAM_B2 = 0.999
ADAM_EPS = 1e-08
ADAM_WD = 0.01
ADAM_STEP = 10

F32 = jnp.float32
BF16 = jnp.bfloat16


def _cp(sem=None, **kw):
    return pltpu.CompilerParams(dimension_semantics=sem, vmem_limit_bytes=VMEM_LIMIT, **kw)


def _blk(n, pref):
    return pref if n % pref == 0 else n


def _rows_tile(rows, cols, budget=3 << 19):
    best = None
    for t in range(SUBLANES, rows + 1, SUBLANES):
        if rows % t == 0 and t * cols * 4 <= budget:
            best = t
    return best if best is not None else rows


def _sigmoid(z):
    return 1.0 / (1.0 + jnp.exp(-z))


def _gelu(z):
    return z * (lax.erf(z * 0.7071067811865476) + 1.0) * 0.5


def _gelu_grad(z):
    return 0.5 * (1.0 + lax.erf(z * 0.7071067811865476)) + z * jnp.exp(-0.5 * z * z) * 0.3989422804014327


def _mean(z):
    return jnp.mean(z, axis=-1, keepdims=True)


def _colsum(z):
    return jnp.sum(z, axis=0, keepdims=True)


def _my_pos():
    return lax.axis_index("x"), lax.axis_index("y"), lax.axis_index("c")


def _allgather(name, shard, in_vmem):
    r, cdim = shard.shape

    def body(x_ref, out_ref, send_sems, recv_sems, local_sem):
        x, y, c = _my_pos()
        me, sibling = (x, y, c), (x, y, 1 - c)
        chips = [(1 - x, y), (x, 1 - y), (1 - x, 1 - y)]

        def slot(px, py, pc):
            return out_ref.at[4 * px + 2 * py + pc]

        def copy(k, block, to, src=None):
            return pltpu.make_async_remote_copy(
                src_ref=slot(*block) if src is None else src, dst_ref=slot(*block),
                send_sem=send_sems.at[k], recv_sem=recv_sems.at[k], device_id=to, device_id_type=MESH_ID)

        mine = pltpu.make_async_copy(x_ref, slot(*me), local_sem)
        mine.start()
        first = [copy(0, me, sibling, src=x_ref)]
        first += [copy(1 + j, me, (*chip, c), src=x_ref) for j, chip in enumerate(chips)]
        for cp in first:
            cp.start()
        passed = [copy(4 + j, (*chip, c), sibling) for j, chip in enumerate(chips)]
        for j, chip in enumerate(chips):
            copy(1 + j, (*chip, c), me).wait_recv()
            passed[j].start()
        copy(0, sibling, me).wait_recv()
        for j, chip in enumerate(chips):
            copy(4 + j, (*chip, 1 - c), me).wait_recv()
        for cp in first + passed:
            cp.wait_send()
        mine.wait()

    space = pltpu.VMEM if in_vmem else pl.ANY
    return pl.pallas_call(
        body, name=name, out_shape=jax.ShapeDtypeStruct((N_DEV, r, cdim), shard.dtype),
        in_specs=[pl.BlockSpec(memory_space=space)], out_specs=pl.BlockSpec(memory_space=space),
        scratch_shapes=[pltpu.SemaphoreType.DMA((7,)), pltpu.SemaphoreType.DMA((7,)), pltpu.SemaphoreType.DMA],
        compiler_params=pltpu.CompilerParams(vmem_limit_bytes=VMEM_LIMIT),
    )(shard)


HBM_SPEC = pl.BlockSpec(memory_space=pltpu.HBM)
SEM_SPEC = pl.BlockSpec(memory_space=pltpu.SEMAPHORE)
ANY_SPEC = pl.BlockSpec(memory_space=pl.ANY)
SPLIT_EFFECT = pltpu.SideEffectType.DATAFLOW_SIDE_EFFECTING
TOKEN_SHAPE = jax.ShapeDtypeStruct((SUBLANES, LANES), F32)


def _hbm(a):
    return pltpu.with_memory_space_constraint(a, pltpu.HBM)


def _peers():
    x, y, c = _my_pos()
    out = []
    for k in range(1, N_DEV):
        px = 1 - x if k & 4 else x
        py = 1 - y if k & 2 else y
        pc = 1 - c if k & 1 else c
        out.append((k, (px, py, pc), 4 * px + 2 * py + pc))
    return out


def _alltoall_start(name, blocks):
    def body(x_ref, land_ref, send_sems, recv_sems, x_thru, land_thru, token):
        del x_thru, land_thru
        x, y, c = _my_pos()
        me = 4 * x + 2 * y + c
        for k, peer, slot in _peers():
            pltpu.make_async_remote_copy(
                src_ref=x_ref.at[slot], dst_ref=land_ref.at[me], send_sem=send_sems.at[k - 1],
                recv_sem=recv_sems.at[k - 1], device_id=peer, device_id_type=MESH_ID).start()
        token[...] = jnp.zeros_like(token)

    sems = pltpu.SemaphoreType.DMA((N_DEV - 1,))
    buf = pltpu.HBM(blocks.shape, blocks.dtype)
    return pl.pallas_call(
        body, name=name, out_shape=(sems, sems, buf, buf, TOKEN_SHAPE), in_specs=(HBM_SPEC, HBM_SPEC),
        out_specs=(SEM_SPEC, SEM_SPEC, HBM_SPEC, HBM_SPEC, pl.BlockSpec(memory_space=pltpu.VMEM)),
        input_output_aliases={0: 2, 1: 3}, compiler_params=pltpu.CompilerParams(has_side_effects=SPLIT_EFFECT),
    )(_hbm(blocks), _hbm(lax.empty(blocks.shape, blocks.dtype)))


def _alltoall_wait(name, send_sems, recv_sems, x_thru, land_thru, after):
    def body(x_ref, land_ref, send_sems, recv_sems, after_ref, x_dead, got_ref):
        del after_ref, x_dead, got_ref
        for k, peer, slot in _peers():
            cp = pltpu.make_async_remote_copy(
                src_ref=x_ref.at[slot], dst_ref=land_ref.at[slot], send_sem=send_sems.at[k - 1],
                recv_sem=recv_sems.at[k - 1], device_id=peer, device_id_type=MESH_ID)
            cp.wait_send()
            cp.wait_recv()

    buf = pltpu.HBM(x_thru.shape, x_thru.dtype)
    return pl.pallas_call(
        body, name=name, out_shape=(buf, buf), in_specs=(HBM_SPEC, HBM_SPEC, SEM_SPEC, SEM_SPEC, ANY_SPEC),
        out_specs=(HBM_SPEC, HBM_SPEC), input_output_aliases={0: 0, 1: 1},
        compiler_params=pltpu.CompilerParams(has_side_effects=SPLIT_EFFECT),
    )(x_thru, land_thru, send_sems, recv_sems, after)


def _gather_start(name, shard, land):
    def body(x_ref, land_ref, sib_send, sib_recv, ici_send, ici_recv, x_thru, land_thru, token):
        del x_thru, land_thru
        x, y, c = _my_pos()
        me = 4 * x + 2 * y + c
        pltpu.make_async_remote_copy(
            src_ref=x_ref, dst_ref=land_ref.at[me], send_sem=sib_send.at[0], recv_sem=sib_recv.at[0],
            device_id=(x, y, 1 - c), device_id_type=MESH_ID).start()
        for k, peer in enumerate([(1 - x, y, c), (x, 1 - y, c), (1 - x, 1 - y, c)]):
            pltpu.make_async_remote_copy(
                src_ref=x_ref, dst_ref=land_ref.at[me], send_sem=ici_send.at[k], recv_sem=ici_recv.at[k],
                device_id=peer, device_id_type=MESH_ID).start()
        token[...] = jnp.zeros_like(token)

    sib, ici = pltpu.SemaphoreType.DMA((1,)), pltpu.SemaphoreType.DMA((3,))
    return pl.pallas_call(
        body, name=name,
        out_shape=(sib, sib, ici, ici, pltpu.HBM(shard.shape, shard.dtype), pltpu.HBM(land.shape, land.dtype), TOKEN_SHAPE),
        in_specs=(HBM_SPEC, HBM_SPEC),
        out_specs=(SEM_SPEC, SEM_SPEC, SEM_SPEC, SEM_SPEC, HBM_SPEC, HBM_SPEC, pl.BlockSpec(memory_space=pltpu.VMEM)),
        input_output_aliases={0: 4, 1: 5}, compiler_params=pltpu.CompilerParams(has_side_effects=SPLIT_EFFECT),
    )(_hbm(shard), _hbm(land))


def _gather_wait(name, send_sems, recv_sems, x_thru, land_thru, after, sibling):
    def body(x_ref, land_ref, send_sems, recv_sems, after_ref, x_dead, got_ref):
        del after_ref, x_dead, got_ref
        x, y, c = _my_pos()
        sources = [(x, y, 1 - c)] if sibling else [(1 - x, y, c), (x, 1 - y, c), (1 - x, 1 - y, c)]
        for k, (px, py, pc) in enumerate(sources):
            cp = pltpu.make_async_remote_copy(
                src_ref=x_ref, dst_ref=land_ref.at[4 * px + 2 * py + pc], send_sem=send_sems.at[k],
                recv_sem=recv_sems.at[k], device_id=(px, py, pc), device_id_type=MESH_ID)
            cp.wait_send()
            cp.wait_recv()

    return pl.pallas_call(
        body, name=name, out_shape=(pltpu.HBM(x_thru.shape, x_thru.dtype), pltpu.HBM(land_thru.shape, land_thru.dtype)),
        in_specs=(HBM_SPEC, HBM_SPEC, SEM_SPEC, SEM_SPEC, ANY_SPEC), out_specs=(HBM_SPEC, HBM_SPEC),
        input_output_aliases={0: 0, 1: 1}, compiler_params=pltpu.CompilerParams(has_side_effects=SPLIT_EFFECT),
    )(x_thru, land_thru, send_sems, recv_sems, after)


def _forward_start(name, land):
    def body(land_ref, send_sems, recv_sems, land_thru):
        del land_thru
        x, y, c = _my_pos()
        for j, (px, py) in enumerate([(1 - x, y), (x, 1 - y), (1 - x, 1 - y)]):
            mine = 4 * px + 2 * py + c
            pltpu.make_async_remote_copy(
                src_ref=land_ref.at[mine], dst_ref=land_ref.at[mine], send_sem=send_sems.at[j], recv_sem=recv_sems.at[j],
                device_id=(x, y, 1 - c), device_id_type=MESH_ID).start()

    sems = pltpu.SemaphoreType.DMA((3,))
    return pl.pallas_call(
        body, name=name, out_shape=(sems, sems, pltpu.HBM(land.shape, land.dtype)), in_specs=(HBM_SPEC,),
        out_specs=(SEM_SPEC, SEM_SPEC, HBM_SPEC), input_output_aliases={0: 2},
        compiler_params=pltpu.CompilerParams(has_side_effects=SPLIT_EFFECT),
    )(_hbm(land))


def _forward_wait(name, send_sems, recv_sems, land_thru, after):
    def body(land_ref, send_sems, recv_sems, after_ref, got_ref):
        del after_ref, got_ref
        x, y, c = _my_pos()
        for j, (px, py) in enumerate([(1 - x, y), (x, 1 - y), (1 - x, 1 - y)]):
            mine, theirs = 4 * px + 2 * py + c, 4 * px + 2 * py + (1 - c)
            cp = pltpu.make_async_remote_copy(
                src_ref=land_ref.at[mine], dst_ref=land_ref.at[theirs], send_sem=send_sems.at[j], recv_sem=recv_sems.at[j],
                device_id=(x, y, 1 - c), device_id_type=MESH_ID)
            cp.wait_send()
            cp.wait_recv()

    return pl.pallas_call(
        body, name=name, out_shape=pltpu.HBM(land_thru.shape, land_thru.dtype),
        in_specs=(HBM_SPEC, SEM_SPEC, SEM_SPEC, ANY_SPEC), out_specs=HBM_SPEC, input_output_aliases={0: 0},
        compiler_params=pltpu.CompilerParams(has_side_effects=SPLIT_EFFECT),
    )(land_thru, send_sems, recv_sems, after)


def _gather_forward(name, land):
    def body(x_ref, out_ref, send_sems, recv_sems):
        x, y, c = _my_pos()
        chips = [(1 - x, y), (x, 1 - y), (1 - x, 1 - y)]
        sends = []
        for j, (px, py) in enumerate(chips):
            mine = 4 * px + 2 * py + c
            cp = pltpu.make_async_remote_copy(
                src_ref=x_ref.at[mine], dst_ref=out_ref.at[mine], send_sem=send_sems.at[j], recv_sem=recv_sems.at[j],
                device_id=(x, y, 1 - c), device_id_type=MESH_ID)
            cp.start()
            sends.append(cp)
        for j, (px, py) in enumerate(chips):
            theirs = 4 * px + 2 * py + (1 - c)
            pltpu.make_async_remote_copy(
                src_ref=x_ref.at[theirs], dst_ref=out_ref.at[theirs], send_sem=send_sems.at[j], recv_sem=recv_sems.at[j],
                device_id=(x, y, 1 - c), device_id_type=MESH_ID).wait_recv()
        for cp in sends:
            cp.wait_send()

    return pl.pallas_call(
        body, name=name, out_shape=jax.ShapeDtypeStruct(land.shape, land.dtype), in_specs=[ANY_SPEC], out_specs=ANY_SPEC,
        scratch_shapes=[pltpu.SemaphoreType.DMA((3,)), pltpu.SemaphoreType.DMA((3,))], input_output_aliases={0: 0},
    )(land)


def _pack(arrays):
    parts = []
    for a in arrays:
        flat = a.reshape(-1).astype(F32)
        pad = (-flat.shape[0]) % BLOB_ALIGN
        parts.append(jnp.pad(flat, (0, pad)) if pad else flat)
    return jnp.concatenate(parts).reshape(-1, LANES)


def _unpack(blob, shapes):
    flat = blob.reshape(-1)
    out, off = [], 0
    for shp in shapes:
        n = 1
        for s in shp:
            n *= s
        out.append(flat[off:off + n].reshape(shp))
        off += n + (-n) % BLOB_ALIGN
    return out


def _cast_bf16(name, w, after):
    r, cdim = w.shape
    tr = _rows_tile(r, cdim)

    def body(w_ref, after_ref, o_ref):
        del after_ref
        o_ref[...] = w_ref[...].astype(BF16)

    return pl.pallas_call(
        body, name=name, grid=(r // tr,), out_shape=jax.ShapeDtypeStruct(w.shape, BF16),
        in_specs=[pl.BlockSpec((tr, cdim), lambda i: (i, 0)), ANY_SPEC], out_specs=pl.BlockSpec((tr, cdim), lambda i: (i, 0)),
        compiler_params=_cp(("parallel",)))(w, after)


def _adam_math(w, g, m, v):
    m = ADAM_B1 * m + (1.0 - ADAM_B1) * g
    v = ADAM_B2 * v + (1.0 - ADAM_B2) * (g * g)
    m_hat = m / (1.0 - ADAM_B1 ** ADAM_STEP)
    v_hat = v / (1.0 - ADAM_B2 ** ADAM_STEP)
    delta = -ADAM_LR * (m_hat / (jnp.sqrt(v_hat) + ADAM_EPS) + ADAM_WD * w)
    return delta, m, v


def _adamw(name, w, m, v, *, grad=None, slots=None, own=None, me=None, ca_t=None, dmod=None):
    r, cdim = w.shape
    tr = _rows_tile(r, cdim)
    row = pl.BlockSpec((tr, cdim), lambda i, *_: (i, 0))
    prefetch = []
    if grad is not None:
        srcs, src_specs = [grad], [row]
    elif slots is not None:
        prefetch = [jnp.reshape(me, (1,)).astype(jnp.int32)]
        srcs = [slots, own]
        src_specs = [pl.BlockSpec((N_DEV, tr, cdim), lambda i, me_ref: (0, i, 0)),
                     pl.BlockSpec((None, tr, cdim), lambda i, me_ref: (me_ref[0], i, 0))]
    else:
        srcs = [ca_t, dmod]
        src_specs = [pl.BlockSpec((tr, ca_t.shape[1]), lambda i: (i, 0)), pl.BlockSpec(dmod.shape, lambda i: (0, 0))]
    n_src = len(srcs)
    n_pre = len(prefetch)

    def body(*refs):
        pre, refs = refs[:n_pre], refs[n_pre:]
        w_ref, m_ref, v_ref = refs[n_src:n_src + 3]
        g_ref, d_ref, nm_ref, nv_ref = refs[n_src + 3:]
        if grad is not None:
            g = refs[0][...]
        elif slots is not None:
            mine = pre[0][0]
            own_f = refs[1][...].astype(F32)
            g = jnp.where(mine == 0, own_f, refs[0][0].astype(F32))
            for s in range(1, N_DEV):
                g = g + jnp.where(mine == s, own_f, refs[0][s].astype(F32))
        else:
            g = jnp.dot(refs[0][...], refs[1][...], preferred_element_type=F32)
        delta, nm, nv = _adam_math(w_ref[...], g, m_ref[...], v_ref[...])
        g_ref[...] = g
        d_ref[...] = delta
        nm_ref[...] = nm
        nv_ref[...] = nv

    shp = jax.ShapeDtypeStruct(w.shape, F32)
    grid_spec = pltpu.PrefetchScalarGridSpec(
        num_scalar_prefetch=n_pre, grid=(r // tr,), in_specs=[*src_specs, row, row, row], out_specs=(row,) * 4)
    return pl.pallas_call(
        body, name=name, grid_spec=grid_spec, out_shape=(shp,) * 4, compiler_params=_cp(("parallel",)))(*prefetch, *srcs, w, m, v)


def _sum_slots(name, gathered):
    _, r, cdim = gathered.shape
    tr = _rows_tile(r, cdim * N_DEV)

    def body(g_ref, o_ref):
        acc = g_ref[0]
        for s in range(1, N_DEV):
            acc = acc + g_ref[s]
        o_ref[...] = acc

    return pl.pallas_call(
        body, name=name, grid=(r // tr,), out_shape=jax.ShapeDtypeStruct((r, cdim), F32),
        in_specs=[pl.BlockSpec((N_DEV, tr, cdim), lambda i: (0, i, 0))], out_specs=pl.BlockSpec((tr, cdim), lambda i: (i, 0)),
        compiler_params=_cp(("parallel",)))(gathered)


def _col_sums(name, a):
    r, cdim = a.shape

    def body(a_ref, o_ref):
        o_ref[...] = _colsum(a_ref[...])

    return pl.pallas_call(
        body, name=name, out_shape=jax.ShapeDtypeStruct((1, cdim), F32),
        in_specs=[pl.BlockSpec(memory_space=pltpu.VMEM)], out_specs=pl.BlockSpec(memory_space=pltpu.VMEM),
        compiler_params=_cp())(a)


def _mm(name, a, b, *, mode, grid, a_spec, b_spec, acc_shape, outs, out_specs, epilogue, extra=(), extra_specs=(), after=(),
        ids=None, prev=None):
    nk = grid[-1]
    dims = {"nn": ((1,), (0,)), "nt": ((1,), (1,)), "tn": ((0,), (0,))}[mode]
    after = tuple(after) + ((prev,) if prev is not None else ())
    n_extra, n_out, n_after = len(extra), len(outs), len(after)
    n_pre = 0 if ids is None else 1

    def body(*refs):
        refs = refs[n_pre:]
        a_ref, b_ref = refs[0], refs[1]
        extra_refs = refs[2:2 + n_extra]
        out_refs = refs[2 + n_extra + n_after:2 + n_extra + n_after + n_out]
        def part():
            return lax.dot_general(a_ref[...], b_ref[...], (dims, ((), ())), preferred_element_type=F32)

        if nk == 1:
            epilogue(part(), extra_refs, out_refs)
        else:
            acc = refs[-1]
            k = pl.program_id(len(grid) - 1)

            @pl.when(k == 0)
            def _():
                acc[...] = part()

            @pl.when(jnp.logical_and(k > 0, k < nk - 1))
            def _():
                acc[...] += part()

            @pl.when(k == nk - 1)
            def _():
                epilogue(acc[...] + part(), extra_refs, out_refs)

    scratch = [pltpu.VMEM(acc_shape, F32)] if nk > 1 else []
    sem = ("parallel",) * (len(grid) - 1) + ("arbitrary",)
    grid_spec = pltpu.PrefetchScalarGridSpec(
        num_scalar_prefetch=n_pre, grid=grid, in_specs=[a_spec, b_spec, *extra_specs, *([ANY_SPEC] * n_after)],
        out_specs=out_specs, scratch_shapes=scratch)
    aliases = {n_pre + 2 + n_extra + n_after - 1: 0} if prev is not None else {}
    pre = () if ids is None else (ids,)
    return pl.pallas_call(
        body, name=name, grid_spec=grid_spec, out_shape=outs, input_output_aliases=aliases,
        compiler_params=_cp(sem))(*pre, a, b, *extra, *after)


def _store_bf16(acc, extra_refs, out_refs):
    out_refs[0][...] = acc.astype(BF16)


def _store_f32(acc, extra_refs, out_refs):
    out_refs[0][...] = acc


def _residual_epilogue(acc, extra_refs, out_refs):
    x_ref, gate_ref = extra_refs
    out_refs[0][...] = acc
    out_refs[1][...] = x_ref[...] + gate_ref[...] * acc


def kernel(x, c, w_ada, b_ada, g_mix, w_in, conv_w, conv_b, conv_ln_g, conv_ln_b, sgu_ln_g, sgu_ln_b, sgu_w, sgu_b, out_g_conv, out_g_sgu, w_out, g_ffn, w_up, ffn_conv_w, ffn_conv_b, w_down, g_final, loss_target, m_w_ada, m_b_ada, m_g_mix, m_w_in, m_conv_w, m_conv_b, m_conv_ln_g, m_conv_ln_b, m_sgu_ln_g, m_sgu_ln_b, m_sgu_w, m_sgu_b, m_out_g_conv, m_out_g_sgu, m_w_out, m_g_ffn, m_w_up, m_ffn_conv_w, m_ffn_conv_b, m_w_down, m_g_final, v_w_ada, v_b_ada, v_g_mix, v_w_in, v_conv_w, v_conv_b, v_conv_ln_g, v_conv_ln_b, v_sgu_ln_g, v_sgu_ln_b, v_sgu_w, v_sgu_b, v_out_g_conv, v_out_g_sgu, v_w_out, v_g_ffn, v_w_up, v_ffn_conv_w, v_ffn_conv_b, v_w_down, v_g_final):
    bsz, seq, d = x.shape
    t = bsz * seq
    n_batch = bsz * N_DEV
    ada_sh = w_ada.shape[2]
    n_mod = ada_sh * N_DEV // d
    win_sh = w_in.shape[2]
    kc = conv_w.shape[1]
    dc_sh = conv_w.shape[2]
    dc = dc_sh * N_DEV
    heads, ch = sgu_w.shape[1], sgu_w.shape[2]
    hd = dc // heads
    wout_sh = w_out.shape[1]
    cup = w_up.shape[2]
    kf = ffn_conv_w.shape[1]
    cw_sh = ffn_conv_w.shape[2]
    dff = cw_sh * N_DEV
    n_fs = dff // cup
    assert win_sh * N_DEV == 4 * dc and hd == LANES and kc // 2 < HALO and kf == 3 and 2 * n_fs == N_DEV
    assert w_down.shape[1] * 2 == cup and seq % ch == 0

    xi, yi, ci = _my_pos()
    me = 4 * xi + 2 * yi + ci

    tb = min(256, seq)
    tps = seq // tb
    nt = t // tb
    tbd = min(128, seq)
    tpd = seq // tbd
    ntd = t // tbd
    tm = min(512, seq)
    nm = t // tm

    xf = x.reshape(t, d)
    tgt = loss_target.reshape(t, d)

    small_shapes = [(bsz, d), (kc, dc_sh), (kf, cw_sh)]
    blob0 = _allgather("ag_small_in", _pack([c, conv_w[0], ffn_conv_w[0]]), True)
    per_dev = [_unpack(blob0[s], small_shapes) for s in range(N_DEV)]
    c_all = jnp.concatenate([p[0] for p in per_dev], axis=0)
    conv_w_full = jnp.concatenate([p[1] for p in per_dev], axis=1)
    ffn_w_full = jnp.concatenate([p[2] for p in per_dev], axis=1)
    ffn_w_s = ffn_w_full.reshape(kf, n_fs, cup).transpose(1, 0, 2)
    ffn_b_s = ffn_conv_b.reshape(n_fs, 1, cup)

    b_ada_sh = lax.dynamic_slice(b_ada, (0, me * ada_sh), (1, ada_sh))
    tn_ada = _blk(ada_sh, 512)

    def ada_body(c_ref, w_ref, b_ref, mod_ref, ca_ref):
        cc = c_ref[...]
        ca = (cc * _sigmoid(cc)).astype(BF16)
        ca_ref[...] = ca
        mod_ref[...] = jnp.dot(ca, w_ref[...].astype(BF16), preferred_element_type=F32) + b_ref[...]

    mod_sh, c_act = pl.pallas_call(
        ada_body, name="ada_fwd", grid=(ada_sh // tn_ada,),
        out_shape=(jax.ShapeDtypeStruct((n_batch, ada_sh), F32), jax.ShapeDtypeStruct((n_batch, d), BF16)),
        in_specs=[pl.BlockSpec((n_batch, d), lambda j: (0, 0)), pl.BlockSpec((d, tn_ada), lambda j: (0, j)),
                  pl.BlockSpec((1, tn_ada), lambda j: (0, j))],
        out_specs=(pl.BlockSpec((n_batch, tn_ada), lambda j: (0, j)), pl.BlockSpec((n_batch, d), lambda j: (0, 0))),
        compiler_params=_cp(("arbitrary",)))(c_all, w_ada[0], b_ada_sh)
    mod_all = _allgather("ag_mod", mod_sh, True)
    mod_me = lax.dynamic_slice(mod_all, (0, me * bsz, 0), (N_DEV, bsz, ada_sh))
    mod_me = mod_me.transpose(1, 0, 2).reshape(bsz, n_mod, 1, d)
    shift_m, scale_m, gate_m = mod_me[:, 0], mod_me[:, 1], mod_me[:, 2]
    shift_f, scale_f, gate_f = mod_me[:, 3], mod_me[:, 4], mod_me[:, 5]

    def gather_begin(name, w2d):
        shard = _cast_bf16("cast_" + name, w2d, mod_all)
        land = lax.dynamic_update_slice(lax.empty((N_DEV,) + shard.shape, BF16), shard[None], (me, 0, 0))
        return _gather_start("ag1_" + name, shard, land)

    def gather_end(name, started, after):
        sib_send, sib_recv, ici_send, ici_recv, x_thru, land_thru, _ = started
        x_thru, land = _gather_wait("ag1s_" + name, sib_send, sib_recv, x_thru, land_thru, after, True)
        _, land = _gather_wait("ag1w_" + name, ici_send, ici_recv, x_thru, land, after, False)
        return _gather_forward("ag2_" + name, land)

    sib = me + 1 - 2 * ci
    flips = [(1 - xi, yi), (xi, 1 - yi), (1 - xi, 1 - yi)]
    ids_near = jnp.stack([me, sib]).astype(jnp.int32)
    ids_ici = jnp.stack([4 * px + 2 * py + ci for px, py in flips]).astype(jnp.int32)
    ids_fwd = jnp.stack([4 * px + 2 * py + 1 - ci for px, py in flips]).astype(jnp.int32)

    def staged_product(name, started, after, part):
        sib_send, sib_recv, ici_send, ici_recv, x_thru, land_thru, _ = started
        x_thru, land = _gather_wait("ag1s_" + name, sib_send, sib_recv, x_thru, land_thru, after, True)
        out = part("near", land, ids_near, None)
        _, land = _gather_wait("ag1w_" + name, ici_send, ici_recv, x_thru, land, out, False)
        fwd_send, fwd_recv, land = _forward_start("ag2s_" + name, land)
        out = part("ici", land, ids_ici, out)
        land = _forward_wait("ag2w_" + name, fwd_send, fwd_recv, land, out)
        out = part("fwd", land, ids_fwd, out)
        return land, out

    w_up_t, m_w_up_t, v_w_up_t = (jnp.swapaxes(a[0], 0, 1) for a in (w_up, m_w_up, v_w_up))
    ag_in = gather_begin("w_in", w_in[0])
    ag_out = gather_begin("w_out", w_out[0])
    ag_up = gather_begin("w_up", w_up_t)
    ag_down = gather_begin("w_down", w_down[0])
    started = ag_in[6][0, 0] + ag_out[6][0, 0] + ag_up[6][0, 0] + ag_down[6][0, 0]

    row_d = pl.BlockSpec((tbd, d), lambda i: (i, 0))
    vec_d = pl.BlockSpec((1, d), lambda i: (0, 0))
    seq_d = pl.BlockSpec((None, 1, d), lambda i: (i // tpd, 0, 0))

    def modulate(name, xin, g, shift, scale):
        def body(x_ref, g_ref, sh_ref, sc_ref, h_ref):
            xx = x_ref[...]
            yy = xx * lax.rsqrt(_mean(xx * xx) + EPS)
            h_ref[...] = ((yy * g_ref[...]) * (1.0 + sc_ref[...]) + sh_ref[...]).astype(BF16)

        return pl.pallas_call(
            body, name=name, grid=(ntd,), out_shape=jax.ShapeDtypeStruct((t, d), BF16),
            in_specs=[row_d, vec_d, seq_d, seq_d], out_specs=row_d, compiler_params=_cp(("parallel",)))(xin, g, shift, scale)

    h1 = modulate("mod1_fwd", xf, g_mix + started, shift_m, scale_m)

    tk_d = _blk(d, 1024)
    def proj_part(tag, land, ids, prev):
        return _mm(
            "mm_proj_" + tag, h1, land, mode="nn", grid=(nm, ids.shape[0], 1),
            a_spec=pl.BlockSpec((tm, d), lambda i, j, k, s: (i, 0)),
            b_spec=pl.BlockSpec((None, d, win_sh), lambda i, j, k, s: (s[j], 0, 0)),
            acc_shape=(tm, win_sh), outs=(jax.ShapeDtypeStruct((t, N_DEV * win_sh), BF16),),
            out_specs=(pl.BlockSpec((tm, win_sh), lambda i, j, k, s: (i, s[j])),), epilogue=_store_bf16,
            ids=ids, prev=prev)[0]

    w_in_g, proj = staged_product("w_in", ag_in, h1, proj_part)

    hb = tb // HALO
    n_hb = t // HALO

    def main_col(col):
        return pl.BlockSpec((tb, dc), lambda i, col=col: (i, col))

    def chunk_col(col):
        return pl.BlockSpec((ch, dc), lambda i, col=col: (i, col))

    def prev_col(col):
        return pl.BlockSpec((HALO, dc), lambda i, col=col: (jnp.maximum(i * hb - 1, 0), col))

    def next_col(col):
        return pl.BlockSpec((HALO, dc), lambda i, col=col: (jnp.minimum((i + 1) * hb, n_hb - 1), col))

    vec_c = pl.BlockSpec((1, dc), lambda i: (0, 0))
    row_c = pl.BlockSpec((tb, dc), lambda i: (i, 0))
    convw_spec = pl.BlockSpec((kc, dc), lambda i: (0, 0))
    sguw_spec = pl.BlockSpec((heads, ch, ch), lambda i: (0, 0, 0))
    bias_spec = pl.BlockSpec((ch, dc), lambda i: (0, 0))
    n_chunk = tb // ch

    sgu_w_bf = sgu_w[0].astype(BF16)
    sgu_wt_bf = jnp.swapaxes(sgu_w[0], 1, 2).astype(BF16)
    sgu_bias = jnp.repeat(sgu_b[0].T, hd, axis=1)

    def fill_glu_ext(ext, i, pv, pg, pvp, pgp, pvn, pgn):
        first = (i % tps) == 0
        last = (i % tps) == tps - 1

        def glu(v_ref, g_ref):
            return v_ref[...].astype(F32) * _sigmoid(g_ref[...].astype(F32))

        ext[pl.ds(0, HALO), :] = jnp.where(first, 0.0, glu(pvp, pgp))
        ext[pl.ds(HALO, tb), :] = glu(pv, pg)
        ext[pl.ds(HALO + tb, HALO), :] = jnp.where(last, 0.0, glu(pvn, pgn))

    def layer_norm_stats(z):
        mu = _mean(z)
        zc = z - mu
        rstd = lax.rsqrt(_mean(zc * zc) + EPS)
        return zc * rstd, rstd

    def sgu_mix(vs_ref, w_ref, vnb, bias_ref, n_chunks):
        for cc in range(n_chunks):
            for h in range(heads):
                blk = jnp.dot(w_ref[h], vnb[cc * ch:(cc + 1) * ch, h * hd:(h + 1) * hd], preferred_element_type=F32)
                vs_ref[pl.ds(cc * ch, ch), pl.ds(h * hd, hd)] = blk + bias_ref[:, pl.ds(h * hd, hd)]

    def mix_fwd_body(pv, pg, pu, pw, pvp, pgp, pvn, pgn, cw, cb, lg, lb, og, slg, slb, sw, sbias, sog, y_ref, a1_ref, ext, vs_ref):
        i = pl.program_id(0)
        fill_glu_ext(ext, i, pv, pg, pvp, pgp, pvn, pgn)
        pad = kc // 2
        acc = jnp.zeros((tb, dc), F32) + cb[...]
        for k in range(kc):
            acc = acc + cw[pl.ds(k, 1), :] * ext[pl.ds(HALO - pad + k, tb), :]
        a1_ref[...] = acc
        xh, _ = layer_norm_stats(acc)
        a2 = xh * lg[...] + lb[...]
        a3 = a2 * _sigmoid(a2)
        ya = a3 * lax.rsqrt(_mean(a3 * a3) + EPS) * og[...]
        y_ref[:, pl.ds(0, dc)] = ya.astype(BF16)

        u = _gelu(pu[...].astype(F32))
        vv = _gelu(pw[...].astype(F32))
        xhv, _ = layer_norm_stats(vv)
        vn = xhv * slg[...] + slb[...]
        sgu_mix(vs_ref, sw, vn.astype(BF16), sbias, n_chunk)
        bg = u * vs_ref[...]
        yb = bg * lax.rsqrt(_mean(bg * bg) + EPS) * sog[...]
        y_ref[:, pl.ds(dc, dc)] = yb.astype(BF16)

    y, a1 = pl.pallas_call(
        mix_fwd_body, name="mix_fwd", grid=(nt,),
        out_shape=(jax.ShapeDtypeStruct((t, 2 * dc), BF16), jax.ShapeDtypeStruct((t, dc), F32)),
        in_specs=[main_col(0), main_col(1), main_col(2), main_col(3), prev_col(0), prev_col(1), next_col(0), next_col(1),
                  convw_spec, vec_c, vec_c, vec_c, vec_c, vec_c, vec_c, sguw_spec, bias_spec, vec_c],
        out_specs=(pl.BlockSpec((tb, 2 * dc), lambda i: (i, 0)), row_c),
        scratch_shapes=[pltpu.VMEM((tb + 2 * HALO, dc), F32), pltpu.VMEM((tb, dc), F32)],
        compiler_params=_cp(("parallel",)),
    )(proj, proj, proj, proj, proj, proj, proj, proj, conv_w_full, conv_b, conv_ln_g, conv_ln_b, out_g_conv,
      sgu_ln_g, sgu_ln_b, sgu_w_bf, sgu_bias, out_g_sgu)

    tn_d = _blk(d, 1024)
    gate_spec3 = pl.BlockSpec((None, 1, tn_d), lambda i, j, k: (i * tm // seq, 0, j))
    res_spec3 = pl.BlockSpec((tm, tn_d), lambda i, j, k: (i, j))
    td_shape = jax.ShapeDtypeStruct((t, d), F32)

    w_out_full = gather_end("w_out", ag_out, y).reshape(d, d)
    o_mix, x1 = _mm(
        "mm_out", y, w_out_full, mode="nn", grid=(nm, d // tn_d, 1),
        a_spec=pl.BlockSpec((tm, d), lambda i, j, k: (i, 0)), b_spec=pl.BlockSpec((d, tn_d), lambda i, j, k: (0, j)),
        acc_shape=(tm, tn_d), outs=(td_shape, td_shape), out_specs=(res_spec3, res_spec3),
        epilogue=_residual_epilogue, extra=(xf, gate_m), extra_specs=(res_spec3, gate_spec3))

    h2 = modulate("mod2_fwd", x1, g_ffn, shift_f, scale_f)

    tm_up = min(1024, seq)
    nk_d = d // tk_d
    def up_part(tag, land, ids, prev):
        return _mm(
            "mm_up_" + tag, h2, land, mode="nt", grid=(t // tm_up, ids.shape[0], nk_d),
            a_spec=pl.BlockSpec((tm_up, tk_d), lambda i, j, k, s: (i, k)),
            b_spec=pl.BlockSpec((None, cup, tk_d), lambda i, j, k, s: (s[j], 0, k)),
            acc_shape=(tm_up, cup), outs=(jax.ShapeDtypeStruct((N_DEV, t, cup), BF16),),
            out_specs=(pl.BlockSpec((None, tm_up, cup), lambda i, j, k, s: (s[j], i, 0)),), epilogue=_store_bf16,
            ids=ids, prev=prev)[0]

    w_up_g, up = staged_product("w_up", ag_up, h2, up_part)
    up4 = up.reshape(2, n_fs, t, cup)

    def ffn_halo(which, lo):
        if lo:
            return pl.BlockSpec((None, None, HALO, cup), lambda j, i: (which, j, jnp.maximum(i * hb - 1, 0), 0))
        return pl.BlockSpec((None, None, HALO, cup), lambda j, i: (which, j, jnp.minimum((i + 1) * hb, n_hb - 1), 0))

    pair_spec = pl.BlockSpec((2, None, tb, cup), lambda j, i: (0, j, i, 0))
    fw_spec = pl.BlockSpec((None, kf, cup), lambda j, i: (j, 0, 0))
    fb_spec = pl.BlockSpec((None, 1, cup), lambda j, i: (j, 0, 0))
    act_spec = pl.BlockSpec((None, tb, cup), lambda j, i: (j, i, 0))

    RC = HALO
    LC = 2 * LANES
    pieces = [(l0, min(LC, cup - l0)) for l0 in range(0, cup, LC)]
    n_rc = tb // RC
    pad = SUBLANES

    def fill_gate_ext(ext, i, cols, gate_ref, prev_ref, next_ref):
        first = (i % tps) == 0
        last = (i % tps) == tps - 1
        lw = cols.size
        ext[pl.ds(0, pad), cols] = jnp.zeros((pad, lw), F32)
        ext[pl.ds(pad, HALO), cols] = jnp.where(first, 0.0, prev_ref[:, cols].astype(F32))
        for rr in range(n_rc):
            ext[pl.ds(pad + HALO + rr * RC, RC), cols] = gate_ref[0, pl.ds(rr * RC, RC), cols].astype(F32)
        ext[pl.ds(pad + HALO + tb, HALO), cols] = jnp.where(last, 0.0, next_ref[:, cols].astype(F32))
        ext[pl.ds(pad + 2 * HALO + tb, pad), cols] = jnp.zeros((pad, lw), F32)

    def tap_rows(fw, fb, cols):
        lw = cols.size
        return [jnp.broadcast_to(fw[pl.ds(k, 1), cols], (RC, lw)) for k in range(kf)] + [jnp.broadcast_to(fb[:, cols], (RC, lw))]

    def ffn_fwd_body(pair, gp, gn, fw, fb, act_ref, ext):
        i = pl.program_id(1)
        for l0, lw in pieces:
            cols = pl.ds(l0, lw)
            fill_gate_ext(ext, i, cols, pair, gp, gn)
            w0, w1, w2, bb = tap_rows(fw, fb, cols)
            for rr in range(n_rc):
                e0 = pad + HALO + rr * RC
                gte = bb + w0 * ext[pl.ds(e0 - 1, RC), cols]
                gte = gte + w1 * ext[pl.ds(e0, RC), cols]
                gte = gte + w2 * ext[pl.ds(e0 + 1, RC), cols]
                val = pair[1, pl.ds(rr * RC, RC), cols].astype(F32)
                act_ref[pl.ds(rr * RC, RC), cols] = (gte * _sigmoid(gte) * val).astype(BF16)

    ext_rows = tb + 2 * HALO + 2 * pad
    act = pl.pallas_call(
        ffn_fwd_body, name="ffn_fwd", grid=(n_fs, nt), out_shape=jax.ShapeDtypeStruct((n_fs, t, cup), BF16),
        in_specs=[pair_spec, ffn_halo(0, True), ffn_halo(0, False), fw_spec, fb_spec], out_specs=act_spec,
        scratch_shapes=[pltpu.VMEM((ext_rows, cup), F32)], compiler_params=_cp(("parallel", "parallel")),
    )(up4, up4, up4, ffn_w_s, ffn_b_s)

    w_down_s = gather_end("w_down", ag_down, act).reshape(n_fs, cup, d)
    gate_spec_f = pl.BlockSpec((None, 1, tn_d), lambda i, j, k: (i * tm // seq, 0, j))
    dn, x2 = _mm(
        "mm_down", act, w_down_s, mode="nn", grid=(nm, d // tn_d, n_fs),
        a_spec=pl.BlockSpec((None, tm, cup), lambda i, j, k: (k, i, 0)),
        b_spec=pl.BlockSpec((None, cup, tn_d), lambda i, j, k: (k, 0, j)),
        acc_shape=(tm, tn_d), outs=(td_shape, td_shape), out_specs=(res_spec3, res_spec3),
        epilogue=_residual_epilogue, extra=(x1, gate_f), extra_specs=(res_spec3, gate_spec_f))

    acc_d = pl.BlockSpec((1, d), lambda i: (0, 0))
    seq_acc = pl.BlockSpec((None, 1, d), lambda i: (i // tpd, 0, 0))

    def head_body(x_ref, t_ref, g_ref, dn_ref, gate_ref, dx_ref, ddn_ref, loss_ref, dg_ref, dgate_ref):
        i = pl.program_id(0)
        xx = x_ref[...]
        rr = lax.rsqrt(_mean(xx * xx) + EPS)
        xn = xx * rr
        err = xn * g_ref[...] - t_ref[...]
        dyf = err * (1.0 / d)
        dxn = dyf * g_ref[...]
        dx = rr * (dxn - xn * _mean(dxn * xn))
        dx_ref[...] = dx
        ddn_ref[...] = (gate_ref[...] * dx).astype(BF16)
        part = 0.5 * jnp.sum(_mean(err * err), axis=0, keepdims=True)

        @pl.when(i == 0)
        def _():
            loss_ref[...] = jnp.zeros_like(loss_ref)
            dg_ref[...] = jnp.zeros_like(dg_ref)

        @pl.when(i % tpd == 0)
        def _():
            dgate_ref[...] = jnp.zeros_like(dgate_ref)

        loss_ref[...] += jnp.broadcast_to(part, loss_ref.shape)
        dg_ref[...] += _colsum(dyf * xn)
        dgate_ref[...] += _colsum(dx * dn_ref[...])

    seq_shape = jax.ShapeDtypeStruct((bsz, 1, d), F32)
    vec_shape = jax.ShapeDtypeStruct((1, d), F32)
    dx2, ddn, loss_part, dg_final, dgate_f = pl.pallas_call(
        head_body, name="loss_head", grid=(ntd,),
        out_shape=(td_shape, jax.ShapeDtypeStruct((t, d), BF16), jax.ShapeDtypeStruct((1, LANES), F32), vec_shape, seq_shape),
        in_specs=[row_d, row_d, vec_d, row_d, seq_d],
        out_specs=(row_d, row_d, pl.BlockSpec((1, LANES), lambda i: (0, 0)), acc_d, seq_acc),
        compiler_params=_cp(("arbitrary",)))(x2, tgt, g_final.reshape(1, d), dn, gate_f)

    tk_t = min(1024, t)
    nk_t = t // tk_t
    tn_gw = _blk(d, 1024)
    gw_down = _mm(
        "mm_gw_down", act, ddn, mode="tn", grid=(n_fs, d // tn_gw, nk_t),
        a_spec=pl.BlockSpec((None, tk_t, cup), lambda j, n, k: (j, k, 0)),
        b_spec=pl.BlockSpec((tk_t, tn_gw), lambda j, n, k: (k, n)),
        acc_shape=(cup, tn_gw), outs=(jax.ShapeDtypeStruct((n_fs, cup, d), BF16),),
        out_specs=(pl.BlockSpec((None, cup, tn_gw), lambda j, n, k: (j, 0, n)),), epilogue=_store_bf16)[0]

    def exchange_begin(name, gw):
        return _alltoall_start("a2a_" + name, gw.reshape(N_DEV, gw.size // (N_DEV * gw.shape[-1]), gw.shape[-1]))

    x_down = exchange_begin("w_down", gw_down)
    dact = _mm(
        "mm_dact", ddn, w_down_s, mode="nt", grid=(t // tm_up, n_fs, nk_d),
        a_spec=pl.BlockSpec((tm_up, tk_d), lambda i, j, k: (i, k)),
        b_spec=pl.BlockSpec((None, cup, tk_d), lambda i, j, k: (j, 0, k)),
        acc_shape=(tm_up, cup), outs=(jax.ShapeDtypeStruct((n_fs, t, cup), BF16),),
        out_specs=(pl.BlockSpec((None, tm_up, cup), lambda i, j, k: (j, i, 0)),), epilogue=_store_bf16, after=(x_down[4],))[0]


    def ffn_bwd_body(pair, gp, gn, vp, vn, da, dap, dan, fw, fb, dup_ref, dw_ref, db_ref, ext, dg_e):
        i = pl.program_id(1)
        first = (i % tps) == 0
        last = (i % tps) == tps - 1

        @pl.when(i == 0)
        def _():
            dw_ref[...] = jnp.zeros_like(dw_ref)
            db_ref[...] = jnp.zeros_like(db_ref)

        for l0, lw in pieces:
            cols = pl.ds(l0, lw)
            fill_gate_ext(ext, i, cols, pair, gp, gn)
            w0, w1, w2, bb = tap_rows(fw, fb, cols)
            sums = [jnp.zeros((RC, lw), F32) for _ in range(kf + 1)]
            for cc in range(n_rc + 2):
                e0 = pad + cc * RC
                taps = [ext[pl.ds(e0 - 1, RC), cols], ext[pl.ds(e0, RC), cols], ext[pl.ds(e0 + 1, RC), cols]]
                gte = bb + w0 * taps[0]
                gte = gte + w1 * taps[1]
                gte = gte + w2 * taps[2]
                sg = _sigmoid(gte)
                if cc == 0:
                    val, dact_v = vp[:, cols], dap[:, cols]
                elif cc == n_rc + 1:
                    val, dact_v = vn[:, cols], dan[:, cols]
                else:
                    rows = pl.ds((cc - 1) * RC, RC)
                    val, dact_v = pair[1, rows, cols], da[rows, cols]
                val, dact_v = val.astype(F32), dact_v.astype(F32)
                dgte = dact_v * val * (sg * (1.0 + gte * (1.0 - sg)))
                if cc == 0:
                    dgte = jnp.where(first, 0.0, dgte)
                elif cc == n_rc + 1:
                    dgte = jnp.where(last, 0.0, dgte)
                else:
                    dup_ref[1, rows, cols] = (dact_v * (gte * sg)).astype(BF16)
                    for k in range(kf):
                        sums[k] = sums[k] + dgte * taps[k]
                    sums[kf] = sums[kf] + dgte
                dg_e[pl.ds(cc * RC, RC), cols] = dgte
            for k in range(kf):
                dw_ref[pl.ds(k, 1), cols] += _colsum(sums[k])
            db_ref[:, cols] += _colsum(sums[kf])
            for rr in range(n_rc):
                e0 = HALO + rr * RC
                dgate = w0 * dg_e[pl.ds(e0 + 1, RC), cols] + w1 * dg_e[pl.ds(e0, RC), cols] + w2 * dg_e[pl.ds(e0 - 1, RC), cols]
                dup_ref[0, pl.ds(rr * RC, RC), cols] = dgate.astype(BF16)

    def act_halo(lo):
        if lo:
            return pl.BlockSpec((None, HALO, cup), lambda j, i: (j, jnp.maximum(i * hb - 1, 0), 0))
        return pl.BlockSpec((None, HALO, cup), lambda j, i: (j, jnp.minimum((i + 1) * hb, n_hb - 1), 0))

    dup4, g_ffn_w_s, g_ffn_b_s = pl.pallas_call(
        ffn_bwd_body, name="ffn_bwd", grid=(n_fs, nt),
        out_shape=(jax.ShapeDtypeStruct((2, n_fs, t, cup), BF16), jax.ShapeDtypeStruct((n_fs, kf, cup), F32),
                   jax.ShapeDtypeStruct((n_fs, 1, cup), F32)),
        in_specs=[pair_spec, ffn_halo(0, True), ffn_halo(0, False), ffn_halo(1, True), ffn_halo(1, False),
                  act_spec, act_halo(True), act_halo(False), fw_spec, fb_spec],
        out_specs=(pair_spec, fw_spec, fb_spec),
        scratch_shapes=[pltpu.VMEM((ext_rows, cup), F32), pltpu.VMEM((tb + 2 * HALO, cup), F32)],
        compiler_params=_cp(("parallel", "arbitrary")),
    )(up4, up4, up4, up4, up4, dact, dact, dact, ffn_w_s, ffn_b_s)
    dup = dup4.reshape(N_DEV, t, cup)

    tm_w = _blk(d, 1024)
    gw_up = _mm(
        "mm_gw_up", dup, h2, mode="tn", grid=(N_DEV, d // tn_gw, nk_t),
        a_spec=pl.BlockSpec((None, tk_t, cup), lambda j, n, k: (j, k, 0)),
        b_spec=pl.BlockSpec((tk_t, tn_gw), lambda j, n, k: (k, n)),
        acc_shape=(cup, tn_gw), outs=(jax.ShapeDtypeStruct((N_DEV, cup, d), BF16),),
        out_specs=(pl.BlockSpec((None, cup, tn_gw), lambda j, n, k: (j, 0, n)),), epilogue=_store_bf16)[0]

    x_up = exchange_begin("w_up", gw_up)
    dh2 = _mm(
        "mm_dh2", dup, w_up_g, mode="nn", grid=(t // tm_up, d // tn_d, N_DEV),
        a_spec=pl.BlockSpec((None, tm_up, cup), lambda i, j, k: (k, i, 0)),
        b_spec=pl.BlockSpec((None, cup, tn_d), lambda i, j, k: (k, 0, j)),
        acc_shape=(tm_up, tn_d), outs=(td_shape,), out_specs=(pl.BlockSpec((tm_up, tn_d), lambda i, j, k: (i, j)),),
        epilogue=_store_f32, after=(x_up[4],))[0]

    def modulate_bwd(name, xin, dh, dres, g, scale, gate=None, branch=None):
        gated = gate is not None

        def body(*refs):
            x_ref, dh_ref, dres_ref, g_ref, sc_ref = refs[:5]
            rest = refs[5:]
            if gated:
                gate_ref, br_ref, dx_ref, dsh_ref, dsc_ref, dg_ref, do_ref, dgate_ref = rest
            else:
                dx_ref, dsh_ref, dsc_ref, dg_ref = rest
            i = pl.program_id(0)
            xx = x_ref[...]
            rr = lax.rsqrt(_mean(xx * xx) + EPS)
            xn = xx * rr
            s1 = 1.0 + sc_ref[...]
            dhh = dh_ref[...]
            dxn = dhh * g_ref[...] * s1
            dx = dres_ref[...] + rr * (dxn - xn * _mean(dxn * xn))
            dx_ref[...] = dx

            @pl.when(i == 0)
            def _():
                dg_ref[...] = jnp.zeros_like(dg_ref)

            @pl.when(i % tpd == 0)
            def _():
                dsh_ref[...] = jnp.zeros_like(dsh_ref)
                dsc_ref[...] = jnp.zeros_like(dsc_ref)
                if gated:
                    dgate_ref[...] = jnp.zeros_like(dgate_ref)

            dsh_ref[...] += _colsum(dhh)
            dsc_ref[...] += _colsum(dhh * (xn * g_ref[...]))
            dg_ref[...] += _colsum(dhh * s1 * xn)
            if gated:
                do_ref[...] = (gate_ref[...] * dx).astype(BF16)
                dgate_ref[...] += _colsum(dx * br_ref[...])

        ins = [xin, dh, dres, g, scale]
        in_specs = [row_d, row_d, row_d, vec_d, seq_d]
        outs = [td_shape, seq_shape, seq_shape, vec_shape]
        out_specs = [row_d, seq_acc, seq_acc, acc_d]
        if gated:
            ins += [gate, branch]
            in_specs += [seq_d, row_d]
            outs += [jax.ShapeDtypeStruct((t, d), BF16), seq_shape]
            out_specs += [row_d, seq_acc]
        return pl.pallas_call(
            body, name=name, grid=(ntd,), out_shape=tuple(outs), in_specs=in_specs, out_specs=tuple(out_specs),
            compiler_params=_cp(("arbitrary",)))(*ins)

    dx1, dshift_f, dscale_f, dg_ffn, d_o, dgate_m = modulate_bwd("mod2_bwd", x1, dh2, dx2, g_ffn, scale_f, gate_m, o_mix)

    tn_w = _blk(d, 2048)
    gw_out = _mm(
        "mm_gw_out", y, d_o, mode="tn", grid=(d // tm_w, d // tn_w, nk_t),
        a_spec=pl.BlockSpec((tk_t, tm_w), lambda i, j, k: (k, i)), b_spec=pl.BlockSpec((tk_t, tn_w), lambda i, j, k: (k, j)),
        acc_shape=(tm_w, tn_w), outs=(jax.ShapeDtypeStruct((d, d), BF16),),
        out_specs=(pl.BlockSpec((tm_w, tn_w), lambda i, j, k: (i, j)),), epilogue=_store_bf16)[0]

    x_out = exchange_begin("w_out", gw_out)
    dy = _mm(
        "mm_dy", d_o, w_out_full, mode="nt", grid=(nm, d // tn_d, 1),
        a_spec=pl.BlockSpec((tm, d), lambda i, j, k: (i, 0)), b_spec=pl.BlockSpec((tn_d, d), lambda i, j, k: (j, 0)),
        acc_shape=(tm, tn_d), outs=(jax.ShapeDtypeStruct((t, d), BF16),),
        out_specs=(pl.BlockSpec((tm, tn_d), lambda i, j, k: (i, j)),), epilogue=_store_bf16, after=(x_out[4],))[0]

    def conv_norm_bwd_body(dy_ref, a1_ref, lg, lb, og, da1_ref, dog_ref, dlg_ref, dlb_ref):
        i = pl.program_id(0)
        xh, rstd = layer_norm_stats(a1_ref[...])
        a2 = xh * lg[...] + lb[...]
        sg = _sigmoid(a2)
        a3 = a2 * sg
        r3 = lax.rsqrt(_mean(a3 * a3) + EPS)
        n3 = a3 * r3
        dya = dy_ref[...].astype(F32)
        dn3 = dya * og[...]
        da3 = r3 * (dn3 - n3 * _mean(dn3 * n3))
        da2 = da3 * (sg * (1.0 + a2 * (1.0 - sg)))
        dxh = da2 * lg[...]
        da1_ref[...] = rstd * (dxh - _mean(dxh) - xh * _mean(dxh * xh))

        @pl.when(i == 0)
        def _():
            dog_ref[...] = jnp.zeros_like(dog_ref)
            dlg_ref[...] = jnp.zeros_like(dlg_ref)
            dlb_ref[...] = jnp.zeros_like(dlb_ref)

        dog_ref[...] += _colsum(dya * n3)
        dlg_ref[...] += _colsum(da2 * xh)
        dlb_ref[...] += _colsum(da2)

    vecc_shape = jax.ShapeDtypeStruct((1, dc), F32)
    da1, g_og_conv, g_cln_g, g_cln_b = pl.pallas_call(
        conv_norm_bwd_body, name="conv_norm_bwd", grid=(nt,),
        out_shape=(jax.ShapeDtypeStruct((t, dc), F32), vecc_shape, vecc_shape, vecc_shape),
        in_specs=[main_col(0), row_c, vec_c, vec_c, vec_c], out_specs=(row_c, vec_c, vec_c, vec_c),
        compiler_params=_cp(("arbitrary",)))(dy, a1, conv_ln_g, conv_ln_b, out_g_conv)

    def sgu_bwd_body(dy_ref, pu, pw, slg, slb, sw, swt, sbias, sog, dproj_ref, dsog_ref, dslg_ref, dslb_ref, dsw_ref, dsb_ref, vs_ref, dvn_ref):
        i = pl.program_id(0)
        zu = pu[...].astype(F32)
        zv = pw[...].astype(F32)
        u = _gelu(zu)
        vv = _gelu(zv)
        xhv, rstd = layer_norm_stats(vv)
        vnb = (xhv * slg[...] + slb[...]).astype(BF16)
        sgu_mix(vs_ref, sw, vnb, sbias, 1)
        vs = vs_ref[...]
        bg = u * vs
        rb = lax.rsqrt(_mean(bg * bg) + EPS)
        nb = bg * rb
        dyb = dy_ref[...].astype(F32)
        dnb = dyb * sog[...]
        dbg = rb * (dnb - nb * _mean(dnb * nb))
        du = dbg * vs
        dvs = dbg * u
        dvsb = dvs.astype(BF16)

        @pl.when(i == 0)
        def _():
            dsog_ref[...] = jnp.zeros_like(dsog_ref)
            dslg_ref[...] = jnp.zeros_like(dslg_ref)
            dslb_ref[...] = jnp.zeros_like(dslb_ref)
            dsw_ref[...] = jnp.zeros_like(dsw_ref)
            dsb_ref[...] = jnp.zeros_like(dsb_ref)

        dsog_ref[...] += _colsum(dyb * nb)
        dsb_ref[...] += dvs
        for h in range(heads):
            dblk = dvsb[:, h * hd:(h + 1) * hd]
            vblk = vnb[:, h * hd:(h + 1) * hd]
            dsw_ref[h] += lax.dot_general(dblk, vblk, (((1,), (1,)), ((), ())), preferred_element_type=F32)
            dvn_ref[:, pl.ds(h * hd, hd)] = jnp.dot(swt[h], dblk, preferred_element_type=F32)
        dvn = dvn_ref[...]
        dslg_ref[...] += _colsum(dvn * xhv)
        dslb_ref[...] += _colsum(dvn)
        dxh = dvn * slg[...]
        dvv = rstd * (dxh - _mean(dxh) - xhv * _mean(dxh * xhv))
        dproj_ref[:, pl.ds(0, dc)] = (du * _gelu_grad(zu)).astype(BF16)
        dproj_ref[:, pl.ds(dc, dc)] = (dvv * _gelu_grad(zv)).astype(BF16)

    dproj, g_og_sgu, g_sln_g, g_sln_b, g_sgu_w, g_sgu_bias = pl.pallas_call(
        sgu_bwd_body, name="sgu_bwd", grid=(t // ch,),
        out_shape=(jax.ShapeDtypeStruct((t, 4 * dc), BF16), vecc_shape, vecc_shape, vecc_shape,
                   jax.ShapeDtypeStruct((heads, ch, ch), F32), jax.ShapeDtypeStruct((ch, dc), F32)),
        in_specs=[chunk_col(1), chunk_col(2), chunk_col(3), vec_c, vec_c, sguw_spec, sguw_spec, bias_spec, vec_c],
        out_specs=(pl.BlockSpec((ch, 2 * dc), lambda i: (i, 1)), vec_c, vec_c, vec_c, sguw_spec, bias_spec),
        scratch_shapes=[pltpu.VMEM((ch, dc), F32), pltpu.VMEM((ch, dc), F32)],
        compiler_params=_cp(("arbitrary",)),
    )(dy, proj, proj, sgu_ln_g, sgu_ln_b, sgu_w_bf, sgu_wt_bf, sgu_bias, out_g_sgu)

    def sgu_bias_reduce_body(b_ref, o_ref):
        lane = lax.broadcasted_iota(jnp.int32, (ch, LANES), 1)
        res = jnp.zeros((ch, LANES), F32)
        for h in range(heads):
            res = jnp.where(lane == h, jnp.sum(b_ref[:, pl.ds(h * hd, hd)], axis=1, keepdims=True), res)
        o_ref[...] = res

    g_sgu_b_t = pl.pallas_call(
        sgu_bias_reduce_body, name="sgu_bias_reduce", out_shape=jax.ShapeDtypeStruct((ch, LANES), F32),
        in_specs=[pl.BlockSpec(memory_space=pltpu.VMEM)], out_specs=pl.BlockSpec(memory_space=pltpu.VMEM),
        compiler_params=_cp())(g_sgu_bias)
    g_sgu_b = g_sgu_b_t[:, :heads].T

    def conv_bwd_body(da, dap, dan, pv, pg, pvp, pgp, pvn, pgn, cw, dproj_in, dproj_ref, dcw_ref, dcb_ref, ext, dext):
        del dproj_in
        i = pl.program_id(0)
        first = (i % tps) == 0
        last = (i % tps) == tps - 1
        fill_glu_ext(ext, i, pv, pg, pvp, pgp, pvn, pgn)
        da_m = da[...]
        dext[pl.ds(0, HALO), :] = jnp.where(first, 0.0, dap[...])
        dext[pl.ds(HALO, tb), :] = da_m
        dext[pl.ds(HALO + tb, HALO), :] = jnp.where(last, 0.0, dan[...])
        pad = kc // 2

        @pl.when(i == 0)
        def _():
            dcw_ref[...] = jnp.zeros_like(dcw_ref)
            dcb_ref[...] = jnp.zeros_like(dcb_ref)

        da0 = jnp.zeros((tb, dc), F32)
        for k in range(kc):
            da0 = da0 + cw[pl.ds(k, 1), :] * dext[pl.ds(HALO + pad - k, tb), :]
            dcw_ref[pl.ds(k, 1), :] += _colsum(da_m * ext[pl.ds(HALO - pad + k, tb), :])
        dcb_ref[...] += _colsum(da_m)
        vv = pv[...].astype(F32)
        sg = _sigmoid(pg[...].astype(F32))
        dproj_ref[:, pl.ds(0, dc)] = (da0 * sg).astype(BF16)
        dproj_ref[:, pl.ds(dc, dc)] = (da0 * vv * sg * (1.0 - sg)).astype(BF16)

    def halo_rows(lo):
        if lo:
            return pl.BlockSpec((HALO, dc), lambda i: (jnp.maximum(i * hb - 1, 0), 0))
        return pl.BlockSpec((HALO, dc), lambda i: (jnp.minimum((i + 1) * hb, n_hb - 1), 0))

    dproj, g_conv_w, g_conv_b = pl.pallas_call(
        conv_bwd_body, name="conv_bwd", grid=(nt,),
        out_shape=(jax.ShapeDtypeStruct((t, 4 * dc), BF16), jax.ShapeDtypeStruct((kc, dc), F32), vecc_shape),
        in_specs=[row_c, halo_rows(True), halo_rows(False), main_col(0), main_col(1), prev_col(0), prev_col(1),
                  next_col(0), next_col(1), convw_spec, pl.BlockSpec(memory_space=pl.ANY)],
        out_specs=(pl.BlockSpec((tb, 2 * dc), lambda i: (i, 0)), convw_spec, vec_c),
        scratch_shapes=[pltpu.VMEM((tb + 2 * HALO, dc), F32), pltpu.VMEM((tb + 2 * HALO, dc), F32)],
        input_output_aliases={10: 0}, compiler_params=_cp(("arbitrary",)),
    )(da1, da1, da1, proj, proj, proj, proj, proj, proj, conv_w_full, dproj)

    tm_wi = _blk(d, 2048)
    gw_in = _mm(
        "mm_gw_in", h1, dproj, mode="tn", grid=(N_DEV, d // tm_wi, nk_t),
        a_spec=pl.BlockSpec((tk_t, tm_wi), lambda j, i, k: (k, i)),
        b_spec=pl.BlockSpec((tk_t, win_sh), lambda j, i, k: (k, j)),
        acc_shape=(tm_wi, win_sh), outs=(jax.ShapeDtypeStruct((N_DEV, d, win_sh), BF16),),
        out_specs=(pl.BlockSpec((None, tm_wi, win_sh), lambda j, i, k: (j, i, 0)),), epilogue=_store_bf16)[0]

    x_in = exchange_begin("w_in", gw_in)
    tn_h = _blk(d, 2048)
    dh1 = _mm(
        "mm_dh1", dproj, w_in_g, mode="nt", grid=(t // tm_up, d // tn_h, N_DEV),
        a_spec=pl.BlockSpec((tm_up, win_sh), lambda i, j, k: (i, k)),
        b_spec=pl.BlockSpec((None, tn_h, win_sh), lambda i, j, k: (k, j, 0)),
        acc_shape=(tm_up, tn_h), outs=(td_shape,), out_specs=(pl.BlockSpec((tm_up, tn_h), lambda i, j, k: (i, j)),),
        epilogue=_store_f32, after=(x_in[4],))[0]

    grad_x, dshift_m, dscale_m, dg_mix = modulate_bwd("mod1_bwd", xf, dh1, dx1, g_mix, scale_m)

    dmod = jnp.concatenate([dshift_m, dscale_m, dgate_m, dshift_f, dscale_f, dgate_f], axis=1)
    g_ffn_w_full = g_ffn_w_s.transpose(1, 0, 2).reshape(kf, dff)
    rep_names = ["g_mix", "conv_b", "conv_ln_g", "conv_ln_b", "sgu_ln_g", "sgu_ln_b", "sgu_w", "sgu_b",
                 "out_g_conv", "out_g_sgu", "g_ffn", "ffn_conv_b", "g_final"]
    rep_w = [g_mix, conv_b, conv_ln_g, conv_ln_b, sgu_ln_g, sgu_ln_b, sgu_w, sgu_b, out_g_conv, out_g_sgu, g_ffn, ffn_conv_b, g_final]
    rep_m = [m_g_mix, m_conv_b, m_conv_ln_g, m_conv_ln_b, m_sgu_ln_g, m_sgu_ln_b, m_sgu_w, m_sgu_b, m_out_g_conv, m_out_g_sgu,
             m_g_ffn, m_ffn_conv_b, m_g_final]
    rep_v = [v_g_mix, v_conv_b, v_conv_ln_g, v_conv_ln_b, v_sgu_ln_g, v_sgu_ln_b, v_sgu_w, v_sgu_b, v_out_g_conv, v_out_g_sgu,
             v_g_ffn, v_ffn_conv_b, v_g_final]
    rep_g = [dg_mix, g_conv_b, g_cln_g, g_cln_b, g_sln_g, g_sln_b, g_sgu_w, g_sgu_b, g_og_conv, g_og_sgu, dg_ffn,
             g_ffn_b_s, dg_final]
    rep_shapes = [w.shape for w in rep_w]
    extra_g = [g_conv_w, g_ffn_w_full, loss_part[:, :1]]
    extra_shapes = [(kc, dc), (kf, dff), (1, 1)]
    sum_blob = _pack(rep_g + extra_g)
    n_sum_rows = sum_blob.shape[0]
    blob1 = jnp.concatenate([sum_blob, _pack([dmod])], axis=0)
    small_land = lax.dynamic_update_slice(lax.empty((N_DEV,) + blob1.shape, F32), blob1[None], (me, 0, 0))
    small_started = _gather_start("ag1_small_grads", blob1, small_land)

    def big(name, started, after, w2d, m2d, v2d):
        send_sems, recv_sems, x_thru, land_thru, _ = started
        mine, recv = _alltoall_wait("a2aw_" + name, send_sems, recv_sems, x_thru, land_thru, after)
        return _adamw("adamw_" + name, w2d, m2d, v2d, slots=recv, own=mine, me=me)

    wdown_out = big("w_down", x_down, small_started[6], w_down[0], m_w_down[0], v_w_down[0])
    wup_out = big("w_up", x_up, wdown_out[0], w_up_t, m_w_up_t, v_w_up_t)
    wout_out = big("w_out", x_out, wup_out[0], w_out[0], m_w_out[0], v_w_out[0])

    gathered = gather_end("small_grads", small_started, wout_out[0])
    summed = _sum_slots("sum_small_grads", gathered[:, :n_sum_rows])
    n_rep_rows = _pack(rep_g).shape[0]
    g_conv_w_all, g_ffn_w_all, loss_all = _unpack(summed[n_rep_rows:], extra_shapes)
    loss = loss_all[0, 0]
    dmod_all = gathered[:, n_sum_rows:].reshape(N_DEV, -1)[:, :bsz * n_mod * d].reshape(n_batch, n_mod * d)

    rep_out = _adamw("adamw_small", _pack(rep_w), _pack(rep_m), _pack(rep_v), grad=summed[:n_rep_rows])
    rep_out = [_unpack(o, rep_shapes) for o in rep_out]
    rep = {name: tuple(rep_out[q][p] for q in range(4)) for p, name in enumerate(rep_names)}

    g_conv_w_me = lax.dynamic_slice(g_conv_w_all, (0, me * dc_sh), (kc, dc_sh))
    cw_out = _adamw("adamw_conv_w", conv_w[0], m_conv_w[0], v_conv_w[0], grad=g_conv_w_me)
    g_ffn_w_me = lax.dynamic_slice(g_ffn_w_all, (0, me * cw_sh), (kf, cw_sh))
    fw_out = _adamw("adamw_ffn_conv_w", ffn_conv_w[0], m_ffn_conv_w[0], v_ffn_conv_w[0], grad=g_ffn_w_me)

    g_b_ada = _col_sums("grad_b_ada", dmod_all)
    bada_out = _adamw("adamw_b_ada", b_ada, m_b_ada, v_b_ada, grad=g_b_ada)
    dmod_sh = lax.dynamic_slice(dmod_all, (0, me * ada_sh), (n_batch, ada_sh)).astype(BF16)
    wada_out = _adamw("adamw_w_ada", w_ada[0], m_w_ada[0], v_w_ada[0], ca_t=c_act.T, dmod=dmod_sh)

    win_out = big("w_in", x_in, wada_out[0], w_in[0], m_w_in[0], v_w_in[0])
    wup_out = tuple(jnp.swapaxes(o, 0, 1) for o in wup_out)

    def lead(outs4):
        return tuple(o[None] for o in outs4)

    results = {
        "w_ada": lead(wada_out), "b_ada": bada_out, "w_in": lead(win_out), "conv_w": lead(cw_out),
        "w_out": lead(wout_out), "w_up": lead(wup_out), "ffn_conv_w": lead(fw_out), "w_down": lead(wdown_out),
    }
    results.update(rep)
    order = ["w_ada", "b_ada", "g_mix", "w_in", "conv_w", "conv_b", "conv_ln_g", "conv_ln_b", "sgu_ln_g", "sgu_ln_b",
             "sgu_w", "sgu_b", "out_g_conv", "out_g_sgu", "w_out", "g_ffn", "w_up", "ffn_conv_w", "ffn_conv_b", "w_down", "g_final"]
    out = [loss, grad_x.reshape(bsz, seq, d)]
    for q in range(4):
        out += [results[n][q] for n in order]
    return tuple(out)
```

```python
import functools

import jax
import jax.numpy as jnp
from jax import lax
from jax.experimental import pallas as pl
from jax.experimental.pallas import tpu as pltpu

EPS = 1e-6
N_DEV = 8
MESH_ID = pl.DeviceIdType.MESH
V7X_VMEM_BYTES = 64 * 1024 * 1024
VMEM_LIMIT = V7X_VMEM_BYTES - 8 * 1024 * 1024
LANES = 128
SUBLANES = 8
HALO = 16
BLOB_ALIGN = SUBLANES * LANES

ADAM_LR = 0.001
ADAM_B1 = 0.9
ADAM_B2 = 0.999
ADAM_EPS = 1e-08
ADAM_WD = 0.01
ADAM_STEP = 10

F32 = jnp.float32
BF16 = jnp.bfloat16


def _cp(sem=None, **kw):
    return pltpu.CompilerParams(dimension_semantics=sem, vmem_limit_bytes=VMEM_LIMIT, **kw)


def _blk(n, pref):
    return pref if n % pref == 0 else n


def _rows_tile(rows, cols, budget=3 << 19):
    best = None
    for t in range(SUBLANES, rows + 1, SUBLANES):
        if rows % t == 0 and t * cols * 4 <= budget:
            best = t
    return best if best is not None else rows


def _sigmoid(z):
    return 1.0 / (1.0 + jnp.exp(-z))


def _gelu(z):
    return z * (lax.erf(z * 0.7071067811865476) + 1.0) * 0.5


def _gelu_grad(z):
    return 0.5 * (1.0 + lax.erf(z * 0.7071067811865476)) + z * jnp.exp(-0.5 * z * z) * 0.3989422804014327


def _mean(z):
    return jnp.mean(z, axis=-1, keepdims=True)


def _colsum(z):
    return jnp.sum(z, axis=0, keepdims=True)


def _my_pos():
    return lax.axis_index("x"), lax.axis_index("y"), lax.axis_index("c")


def _allgather(name, shard, in_vmem):
    r, cdim = shard.shape

    def body(x_ref, out_ref, send_sems, recv_sems, local_sem):
        x, y, c = _my_pos()
        me, sibling = (x, y, c), (x, y, 1 - c)
        chips = [(1 - x, y), (x, 1 - y), (1 - x, 1 - y)]

        def slot(px, py, pc):
            return out_ref.at[4 * px + 2 * py + pc]

        def copy(k, block, to, src=None):
            return pltpu.make_async_remote_copy(
                src_ref=slot(*block) if src is None else src, dst_ref=slot(*block),
                send_sem=send_sems.at[k], recv_sem=recv_sems.at[k], device_id=to, device_id_type=MESH_ID)

        mine = pltpu.make_async_copy(x_ref, slot(*me), local_sem)
        mine.start()
        first = [copy(0, me, sibling, src=x_ref)]
        first += [copy(1 + j, me, (*chip, c), src=x_ref) for j, chip in enumerate(chips)]
        for cp in first:
            cp.start()
        passed = [copy(4 + j, (*chip, c), sibling) for j, chip in enumerate(chips)]
        for j, chip in enumerate(chips):
            copy(1 + j, (*chip, c), me).wait_recv()
            passed[j].start()
        copy(0, sibling, me).wait_recv()
        for j, chip in enumerate(chips):
            copy(4 + j, (*chip, 1 - c), me).wait_recv()
        for cp in first + passed:
            cp.wait_send()
        mine.wait()

    space = pltpu.VMEM if in_vmem else pl.ANY
    return pl.pallas_call(
        body, name=name, out_shape=jax.ShapeDtypeStruct((N_DEV, r, cdim), shard.dtype),
        in_specs=[pl.BlockSpec(memory_space=space)], out_specs=pl.BlockSpec(memory_space=space),
        scratch_shapes=[pltpu.SemaphoreType.DMA((7,)), pltpu.SemaphoreType.DMA((7,)), pltpu.SemaphoreType.DMA],
        compiler_params=pltpu.CompilerParams(vmem_limit_bytes=VMEM_LIMIT),
    )(shard)


HBM_SPEC = pl.BlockSpec(memory_space=pltpu.HBM)
SEM_SPEC = pl.BlockSpec(memory_space=pltpu.SEMAPHORE)
ANY_SPEC = pl.BlockSpec(memory_space=pl.ANY)
SPLIT_EFFECT = pltpu.SideEffectType.DATAFLOW_SIDE_EFFECTING
TOKEN_SHAPE = jax.ShapeDtypeStruct((SUBLANES, LANES), F32)


def _hbm(a):
    return pltpu.with_memory_space_constraint(a, pltpu.HBM)


def _peers():
    x, y, c = _my_pos()
    out = []
    for k in range(1, N_DEV):
        px = 1 - x if k & 4 else x
        py = 1 - y if k & 2 else y
        pc = 1 - c if k & 1 else c
        out.append((k, (px, py, pc), 4 * px + 2 * py + pc))
    return out


def _alltoall_start(name, blocks):
    def body(x_ref, land_ref, send_sems, recv_sems, x_thru, land_thru, token):
        del x_thru, land_thru
        x, y, c = _my_pos()
        me = 4 * x + 2 * y + c
        for k, peer, slot in _peers():
            pltpu.make_async_remote_copy(
                src_ref=x_ref.at[slot], dst_ref=land_ref.at[me], send_sem=send_sems.at[k - 1],
                recv_sem=recv_sems.at[k - 1], device_id=peer, device_id_type=MESH_ID).start()
        token[...] = jnp.zeros_like(token)

    sems = pltpu.SemaphoreType.DMA((N_DEV - 1,))
    buf = pltpu.HBM(blocks.shape, blocks.dtype)
    return pl.pallas_call(
        body, name=name, out_shape=(sems, sems, buf, buf, TOKEN_SHAPE), in_specs=(HBM_SPEC, HBM_SPEC),
        out_specs=(SEM_SPEC, SEM_SPEC, HBM_SPEC, HBM_SPEC, pl.BlockSpec(memory_space=pltpu.VMEM)),
        input_output_aliases={0: 2, 1: 3}, compiler_params=pltpu.CompilerParams(has_side_effects=SPLIT_EFFECT),
    )(_hbm(blocks), _hbm(lax.empty(blocks.shape, blocks.dtype)))


def _alltoall_wait(name, send_sems, recv_sems, x_thru, land_thru, after):
    def body(x_ref, land_ref, send_sems, recv_sems, after_ref, x_dead, got_ref):
        del after_ref, x_dead, got_ref
        for k, peer, slot in _peers():
            cp = pltpu.make_async_remote_copy(
                src_ref=x_ref.at[slot], dst_ref=land_ref.at[slot], send_sem=send_sems.at[k - 1],
                recv_sem=recv_sems.at[k - 1], device_id=peer, device_id_type=MESH_ID)
            cp.wait_send()
            cp.wait_recv()

    buf = pltpu.HBM(x_thru.shape, x_thru.dtype)
    return pl.pallas_call(
        body, name=name, out_shape=(buf, buf), in_specs=(HBM_SPEC, HBM_SPEC, SEM_SPEC, SEM_SPEC, ANY_SPEC),
        out_specs=(HBM_SPEC, HBM_SPEC), input_output_aliases={0: 0, 1: 1},
        compiler_params=pltpu.CompilerParams(has_side_effects=SPLIT_EFFECT),
    )(x_thru, land_thru, send_sems, recv_sems, after)


def _gather_start(name, shard, land):
    def body(x_ref, land_ref, sib_send, sib_recv, ici_send, ici_recv, x_thru, land_thru, token):
        del x_thru, land_thru
        x, y, c = _my_pos()
        me = 4 * x + 2 * y + c
        pltpu.make_async_remote_copy(
            src_ref=x_ref, dst_ref=land_ref.at[me], send_sem=sib_send.at[0], recv_sem=sib_recv.at[0],
            device_id=(x, y, 1 - c), device_id_type=MESH_ID).start()
        for k, peer in enumerate([(1 - x, y, c), (x, 1 - y, c), (1 - x, 1 - y, c)]):
            pltpu.make_async_remote_copy(
                src_ref=x_ref, dst_ref=land_ref.at[me], send_sem=ici_send.at[k], recv_sem=ici_recv.at[k],
                device_id=peer, device_id_type=MESH_ID).start()
        token[...] = jnp.zeros_like(token)

    sib, ici = pltpu.SemaphoreType.DMA((1,)), pltpu.SemaphoreType.DMA((3,))
    return pl.pallas_call(
        body, name=name,
        out_shape=(sib, sib, ici, ici, pltpu.HBM(shard.shape, shard.dtype), pltpu.HBM(land.shape, land.dtype), TOKEN_SHAPE),
        in_specs=(HBM_SPEC, HBM_SPEC),
        out_specs=(SEM_SPEC, SEM_SPEC, SEM_SPEC, SEM_SPEC, HBM_SPEC, HBM_SPEC, pl.BlockSpec(memory_space=pltpu.VMEM)),
        input_output_aliases={0: 4, 1: 5}, compiler_params=pltpu.CompilerParams(has_side_effects=SPLIT_EFFECT),
    )(_hbm(shard), _hbm(land))


def _gather_wait(name, send_sems, recv_sems, x_thru, land_thru, after, sibling):
    def body(x_ref, land_ref, send_sems, recv_sems, after_ref, x_dead, got_ref):
        del after_ref, x_dead, got_ref
        x, y, c = _my_pos()
        sources = [(x, y, 1 - c)] if sibling else [(1 - x, y, c), (x, 1 - y, c), (1 - x, 1 - y, c)]
        for k, (px, py, pc) in enumerate(sources):
            cp = pltpu.make_async_remote_copy(
                src_ref=x_ref, dst_ref=land_ref.at[4 * px + 2 * py + pc], send_sem=send_sems.at[k],
                recv_sem=recv_sems.at[k], device_id=(px, py, pc), device_id_type=MESH_ID)
            cp.wait_send()
            cp.wait_recv()

    return pl.pallas_call(
        body, name=name, out_shape=(pltpu.HBM(x_thru.shape, x_thru.dtype), pltpu.HBM(land_thru.shape, land_thru.dtype)),
        in_specs=(HBM_SPEC, HBM_SPEC, SEM_SPEC, SEM_SPEC, ANY_SPEC), out_specs=(HBM_SPEC, HBM_SPEC),
        input_output_aliases={0: 0, 1: 1}, compiler_params=pltpu.CompilerParams(has_side_effects=SPLIT_EFFECT),
    )(x_thru, land_thru, send_sems, recv_sems, after)


def _forward_start(name, land):
    def body(land_ref, send_sems, recv_sems, land_thru):
        del land_thru
        x, y, c = _my_pos()
        for j, (px, py) in enumerate([(1 - x, y), (x, 1 - y), (1 - x, 1 - y)]):
            mine = 4 * px + 2 * py + c
            pltpu.make_async_remote_copy(
                src_ref=land_ref.at[mine], dst_ref=land_ref.at[mine], send_sem=send_sems.at[j], recv_sem=recv_sems.at[j],
                device_id=(x, y, 1 - c), device_id_type=MESH_ID).start()

    sems = pltpu.SemaphoreType.DMA((3,))
    return pl.pallas_call(
        body, name=name, out_shape=(sems, sems, pltpu.HBM(land.shape, land.dtype)), in_specs=(HBM_SPEC,),
        out_specs=(SEM_SPEC, SEM_SPEC, HBM_SPEC), input_output_aliases={0: 2},
        compiler_params=pltpu.CompilerParams(has_side_effects=SPLIT_EFFECT),
    )(_hbm(land))


def _forward_wait(name, send_sems, recv_sems, land_thru, after):
    def body(land_ref, send_sems, recv_sems, after_ref, got_ref):
        del after_ref, got_ref
        x, y, c = _my_pos()
        for j, (px, py) in enumerate([(1 - x, y), (x, 1 - y), (1 - x, 1 - y)]):
            mine, theirs = 4 * px + 2 * py + c, 4 * px + 2 * py + (1 - c)
            cp = pltpu.make_async_remote_copy(
                src_ref=land_ref.at[mine], dst_ref=land_ref.at[theirs], send_sem=send_sems.at[j], recv_sem=recv_sems.at[j],
                device_id=(x, y, 1 - c), device_id_type=MESH_ID)
            cp.wait_send()
            cp.wait_recv()

    return pl.pallas_call(
        body, name=name, out_shape=pltpu.HBM(land_thru.shape, land_thru.dtype),
        in_specs=(HBM_SPEC, SEM_SPEC, SEM_SPEC, ANY_SPEC), out_specs=HBM_SPEC, input_output_aliases={0: 0},
        compiler_params=pltpu.CompilerParams(has_side_effects=SPLIT_EFFECT),
    )(land_thru, send_sems, recv_sems, after)


def _gather_forward(name, land):
    def body(x_ref, out_ref, send_sems, recv_sems):
        x, y, c = _my_pos()
        chips = [(1 - x, y), (x, 1 - y), (1 - x, 1 - y)]
        sends = []
        for j, (px, py) in enumerate(chips):
            mine = 4 * px + 2 * py + c
            cp = pltpu.make_async_remote_copy(
                src_ref=x_ref.at[mine], dst_ref=out_ref.at[mine], send_sem=send_sems.at[j], recv_sem=recv_sems.at[j],
                device_id=(x, y, 1 - c), device_id_type=MESH_ID)
            cp.start()
            sends.append(cp)
        for j, (px, py) in enumerate(chips):
            theirs = 4 * px + 2 * py + (1 - c)
            pltpu.make_async_remote_copy(
                src_ref=x_ref.at[theirs], dst_ref=out_ref.at[theirs], send_sem=send_sems.at[j], recv_sem=recv_sems.at[j],
                device_id=(x, y, 1 - c), device_id_type=MESH_ID).wait_recv()
        for cp in sends:
            cp.wait_send()

    return pl.pallas_call(
        body, name=name, out_shape=jax.ShapeDtypeStruct(land.shape, land.dtype), in_specs=[ANY_SPEC], out_specs=ANY_SPEC,
        scratch_shapes=[pltpu.SemaphoreType.DMA((3,)), pltpu.SemaphoreType.DMA((3,))], input_output_aliases={0: 0},
    )(land)


def _pack(arrays):
    parts = []
    for a in arrays:
        flat = a.reshape(-1).astype(F32)
        pad = (-flat.shape[0]) % BLOB_ALIGN
        parts.append(jnp.pad(flat, (0, pad)) if pad else flat)
    return jnp.concatenate(parts).reshape(-1, LANES)


def _unpack(blob, shapes):
    flat = blob.reshape(-1)
    out, off = [], 0
    for shp in shapes:
        n = 1
        for s in shp:
            n *= s
        out.append(flat[off:off + n].reshape(shp))
        off += n + (-n) % BLOB_ALIGN
    return out


def _cast_bf16(name, w, after):
    r, cdim = w.shape
    tr = _rows_tile(r, cdim)

    def body(w_ref, after_ref, o_ref):
        del after_ref
        o_ref[...] = w_ref[...].astype(BF16)

    return pl.pallas_call(
        body, name=name, grid=(r // tr,), out_shape=jax.ShapeDtypeStruct(w.shape, BF16),
        in_specs=[pl.BlockSpec((tr, cdim), lambda i: (i, 0)), ANY_SPEC], out_specs=pl.BlockSpec((tr, cdim), lambda i: (i, 0)),
        compiler_params=_cp(("parallel",)))(w, after)


def _adam_math(w, g, m, v):
    m = ADAM_B1 * m + (1.0 - ADAM_B1) * g
    v = ADAM_B2 * v + (1.0 - ADAM_B2) * (g * g)
    m_hat = m / (1.0 - ADAM_B1 ** ADAM_STEP)
    v_hat = v / (1.0 - ADAM_B2 ** ADAM_STEP)
    delta = -ADAM_LR * (m_hat / (jnp.sqrt(v_hat) + ADAM_EPS) + ADAM_WD * w)
    return delta, m, v


def _adamw(name, w, m, v, *, grad=None, slots=None, own=None, me=None, ca_t=None, dmod=None):
    r, cdim = w.shape
    tr = _rows_tile(r, cdim)
    row = pl.BlockSpec((tr, cdim), lambda i, *_: (i, 0))
    prefetch = []
    if grad is not None:
        srcs, src_specs = [grad], [row]
    elif slots is not None:
        prefetch = [jnp.reshape(me, (1,)).astype(jnp.int32)]
        srcs = [slots, own]
        src_specs = [pl.BlockSpec((N_DEV, tr, cdim), lambda i, me_ref: (0, i, 0)),
                     pl.BlockSpec((None, tr, cdim), lambda i, me_ref: (me_ref[0], i, 0))]
    else:
        srcs = [ca_t, dmod]
        src_specs = [pl.BlockSpec((tr, ca_t.shape[1]), lambda i: (i, 0)), pl.BlockSpec(dmod.shape, lambda i: (0, 0))]
    n_src = len(srcs)
    n_pre = len(prefetch)

    def body(*refs):
        pre, refs = refs[:n_pre], refs[n_pre:]
        w_ref, m_ref, v_ref = refs[n_src:n_src + 3]
        g_ref, d_ref, nm_ref, nv_ref = refs[n_src + 3:]
        if grad is not None:
            g = refs[0][...]
        elif slots is not None:
            mine = pre[0][0]
            own_f = refs[1][...].astype(F32)
            g = jnp.where(mine == 0, own_f, refs[0][0].astype(F32))
            for s in range(1, N_DEV):
                g = g + jnp.where(mine == s, own_f, refs[0][s].astype(F32))
        else:
            g = jnp.dot(refs[0][...], refs[1][...], preferred_element_type=F32)
        delta, nm, nv = _adam_math(w_ref[...], g, m_ref[...], v_ref[...])
        g_ref[...] = g
        d_ref[...] = delta
        nm_ref[...] = nm
        nv_ref[...] = nv

    shp = jax.ShapeDtypeStruct(w.shape, F32)
    grid_spec = pltpu.PrefetchScalarGridSpec(
        num_scalar_prefetch=n_pre, grid=(r // tr,), in_specs=[*src_specs, row, row, row], out_specs=(row,) * 4)
    return pl.pallas_call(
        body, name=name, grid_spec=grid_spec, out_shape=(shp,) * 4, compiler_params=_cp(("parallel",)))(*prefetch, *srcs, w, m, v)


def _sum_slots(name, gathered):
    _, r, cdim = gathered.shape
    tr = _rows_tile(r, cdim * N_DEV)

    def body(g_ref, o_ref):
        acc = g_ref[0]
        for s in range(1, N_DEV):
            acc = acc + g_ref[s]
        o_ref[...] = acc

    return pl.pallas_call(
        body, name=name, grid=(r // tr,), out_shape=jax.ShapeDtypeStruct((r, cdim), F32),
        in_specs=[pl.BlockSpec((N_DEV, tr, cdim), lambda i: (0, i, 0))], out_specs=pl.BlockSpec((tr, cdim), lambda i: (i, 0)),
        compiler_params=_cp(("parallel",)))(gathered)


def _col_sums(name, a):
    r, cdim = a.shape

    def body(a_ref, o_ref):
        o_ref[...] = _colsum(a_ref[...])

    return pl.pallas_call(
        body, name=name, out_shape=jax.ShapeDtypeStruct((1, cdim), F32),
        in_specs=[pl.BlockSpec(memory_space=pltpu.VMEM)], out_specs=pl.BlockSpec(memory_space=pltpu.VMEM),
        compiler_params=_cp())(a)


def _mm(name, a, b, *, mode, grid, a_spec, b_spec, acc_shape, outs, out_specs, epilogue, extra=(), extra_specs=(), after=(),
        ids=None, prev=None):
    nk = grid[-1]
    dims = {"nn": ((1,), (0,)), "nt": ((1,), (1,)), "tn": ((0,), (0,))}[mode]
    after = tuple(after) + ((prev,) if prev is not None else ())
    n_extra, n_out, n_after = len(extra), len(outs), len(after)
    n_pre = 0 if ids is None else 1

    def body(*refs):
        refs = refs[n_pre:]
        a_ref, b_ref = refs[0], refs[1]
        extra_refs = refs[2:2 + n_extra]
        out_refs = refs[2 + n_extra + n_after:2 + n_extra + n_after + n_out]
        def part():
            return lax.dot_general(a_ref[...], b_ref[...], (dims, ((), ())), preferred_element_type=F32)

        if nk == 1:
            epilogue(part(), extra_refs, out_refs)
        else:
            acc = refs[-1]
            k = pl.program_id(len(grid) - 1)

            @pl.when(k == 0)
            def _():
                acc[...] = part()

            @pl.when(jnp.logical_and(k > 0, k < nk - 1))
            def _():
                acc[...] += part()

            @pl.when(k == nk - 1)
            def _():
                epilogue(acc[...] + part(), extra_refs, out_refs)

    scratch = [pltpu.VMEM(acc_shape, F32)] if nk > 1 else []
    sem = ("parallel",) * (len(grid) - 1) + ("arbitrary",)
    grid_spec = pltpu.PrefetchScalarGridSpec(
        num_scalar_prefetch=n_pre, grid=grid, in_specs=[a_spec, b_spec, *extra_specs, *([ANY_SPEC] * n_after)],
        out_specs=out_specs, scratch_shapes=scratch)
    aliases = {n_pre + 2 + n_extra + n_after - 1: 0} if prev is not None else {}
    pre = () if ids is None else (ids,)
    return pl.pallas_call(
        body, name=name, grid_spec=grid_spec, out_shape=outs, input_output_aliases=aliases,
        compiler_params=_cp(sem))(*pre, a, b, *extra, *after)


def _store_bf16(acc, extra_refs, out_refs):
    out_refs[0][...] = acc.astype(BF16)


def _store_f32(acc, extra_refs, out_refs):
    out_refs[0][...] = acc


def _residual_epilogue(acc, extra_refs, out_refs):
    x_ref, gate_ref = extra_refs
    out_refs[0][...] = acc
    out_refs[1][...] = x_ref[...] + gate_ref[...] * acc


def kernel(x, c, w_ada, b_ada, g_mix, w_in, conv_w, conv_b, conv_ln_g, conv_ln_b, sgu_ln_g, sgu_ln_b, sgu_w, sgu_b, out_g_conv, out_g_sgu, w_out, g_ffn, w_up, ffn_conv_w, ffn_conv_b, w_down, g_final, loss_target, m_w_ada, m_b_ada, m_g_mix, m_w_in, m_conv_w, m_conv_b, m_conv_ln_g, m_conv_ln_b, m_sgu_ln_g, m_sgu_ln_b, m_sgu_w, m_sgu_b, m_out_g_conv, m_out_g_sgu, m_w_out, m_g_ffn, m_w_up, m_ffn_conv_w, m_ffn_conv_b, m_w_down, m_g_final, v_w_ada, v_b_ada, v_g_mix, v_w_in, v_conv_w, v_conv_b, v_conv_ln_g, v_conv_ln_b, v_sgu_ln_g, v_sgu_ln_b, v_sgu_w, v_sgu_b, v_out_g_conv, v_out_g_sgu, v_w_out, v_g_ffn, v_w_up, v_ffn_conv_w, v_ffn_conv_b, v_w_down, v_g_final):
    bsz, seq, d = x.shape
    t = bsz * seq
    n_batch = bsz * N_DEV
    ada_sh = w_ada.shape[2]
    n_mod = ada_sh * N_DEV // d
    win_sh = w_in.shape[2]
    kc = conv_w.shape[1]
    dc_sh = conv_w.shape[2]
    dc = dc_sh * N_DEV
    heads, ch = sgu_w.shape[1], sgu_w.shape[2]
    hd = dc // heads
    wout_sh = w_out.shape[1]
    cup = w_up.shape[2]
    kf = ffn_conv_w.shape[1]
    cw_sh = ffn_conv_w.shape[2]
    dff = cw_sh * N_DEV
    n_fs = dff // cup
    assert win_sh * N_DEV == 4 * dc and hd == LANES and kc // 2 < HALO and kf == 3 and 2 * n_fs == N_DEV
    assert w_down.shape[1] * 2 == cup and seq % ch == 0

    xi, yi, ci = _my_pos()
    me = 4 * xi + 2 * yi + ci

    tb = min(256, seq)
    tps = seq // tb
    nt = t // tb
    tbd = min(128, seq)
    tpd = seq // tbd
    ntd = t // tbd
    tm = min(512, seq)
    nm = t // tm

    xf = x.reshape(t, d)
    tgt = loss_target.reshape(t, d)

    small_shapes = [(bsz, d), (kc, dc_sh), (kf, cw_sh)]
    blob0 = _allgather("ag_small_in", _pack([c, conv_w[0], ffn_conv_w[0]]), True)
    per_dev = [_unpack(blob0[s], small_shapes) for s in range(N_DEV)]
    c_all = jnp.concatenate([p[0] for p in per_dev], axis=0)
    conv_w_full = jnp.concatenate([p[1] for p in per_dev], axis=1)
    ffn_w_full = jnp.concatenate([p[2] for p in per_dev], axis=1)
    ffn_w_s = ffn_w_full.reshape(kf, n_fs, cup).transpose(1, 0, 2)
    ffn_b_s = ffn_conv_b.reshape(n_fs, 1, cup)

    b_ada_sh = lax.dynamic_slice(b_ada, (0, me * ada_sh), (1, ada_sh))
    tn_ada = _blk(ada_sh, 512)

    def ada_body(c_ref, w_ref, b_ref, mod_ref, ca_ref):
        cc = c_ref[...]
        ca = (cc * _sigmoid(cc)).astype(BF16)
        ca_ref[...] = ca
        mod_ref[...] = jnp.dot(ca, w_ref[...].astype(BF16), preferred_element_type=F32) + b_ref[...]

    mod_sh, c_act = pl.pallas_call(
        ada_body, name="ada_fwd", grid=(ada_sh // tn_ada,),
        out_shape=(jax.ShapeDtypeStruct((n_batch, ada_sh), F32), jax.ShapeDtypeStruct((n_batch, d), BF16)),
        in_specs=[pl.BlockSpec((n_batch, d), lambda j: (0, 0)), pl.BlockSpec((d, tn_ada), lambda j: (0, j)),
                  pl.BlockSpec((1, tn_ada), lambda j: (0, j))],
        out_specs=(pl.BlockSpec((n_batch, tn_ada), lambda j: (0, j)), pl.BlockSpec((n_batch, d), lambda j: (0, 0))),
        compiler_params=_cp(("arbitrary",)))(c_all, w_ada[0], b_ada_sh)
    mod_all = _allgather("ag_mod", mod_sh, True)
    mod_me = lax.dynamic_slice(mod_all, (0, me * bsz, 0), (N_DEV, bsz, ada_sh))
    mod_me = mod_me.transpose(1, 0, 2).reshape(bsz, n_mod, 1, d)
    shift_m, scale_m, gate_m = mod_me[:, 0], mod_me[:, 1], mod_me[:, 2]
    shift_f, scale_f, gate_f = mod_me[:, 3], mod_me[:, 4], mod_me[:, 5]

    def gather_begin(name, w2d):
        shard = _cast_bf16("cast_" + name, w2d, mod_all)
        land = lax.dynamic_update_slice(lax.empty((N_DEV,) + shard.shape, BF16), shard[None], (me, 0, 0))
        return _gather_start("ag1_" + name, shard, land)

    def gather_end(name, started, after):
        sib_send, sib_recv, ici_send, ici_recv, x_thru, land_thru, _ = started
        x_thru, land = _gather_wait("ag1s_" + name, sib_send, sib_recv, x_thru, land_thru, after, True)
        _, land = _gather_wait("ag1w_" + name, ici_send, ici_recv, x_thru, land, after, False)
        return _gather_forward("ag2_" + name, land)

    sib = me + 1 - 2 * ci
    flips = [(1 - xi, yi), (xi, 1 - yi), (1 - xi, 1 - yi)]
    ids_near = jnp.stack([me, sib]).astype(jnp.int32)
    ids_ici = jnp.stack([4 * px + 2 * py + ci for px, py in flips]).astype(jnp.int32)
    ids_fwd = jnp.stack([4 * px + 2 * py + 1 - ci for px, py in flips]).astype(jnp.int32)

    def staged_product(name, started, after, part):
        sib_send, sib_recv, ici_send, ici_recv, x_thru, land_thru, _ = started
        x_thru, land = _gather_wait("ag1s_" + name, sib_send, sib_recv, x_thru, land_thru, after, True)
        out = part("near", land, ids_near, None)
        _, land = _gather_wait("ag1w_" + name, ici_send, ici_recv, x_thru, land, out, False)
        fwd_send, fwd_recv, land = _forward_start("ag2s_" + name, land)
        out = part("ici", land, ids_ici, out)
        land = _forward_wait("ag2w_" + name, fwd_send, fwd_recv, land, out)
        out = part("fwd", land, ids_fwd, out)
        return land, out

    w_up_t, m_w_up_t, v_w_up_t = (jnp.swapaxes(a[0], 0, 1) for a in (w_up, m_w_up, v_w_up))
    ag_in = gather_begin("w_in", w_in[0])
    ag_out = gather_begin("w_out", w_out[0])
    ag_up = gather_begin("w_up", w_up_t)
    ag_down = gather_begin("w_down", w_down[0])
    started = ag_in[6][0, 0] + ag_out[6][0, 0] + ag_up[6][0, 0] + ag_down[6][0, 0]

    row_d = pl.BlockSpec((tbd, d), lambda i: (i, 0))
    vec_d = pl.BlockSpec((1, d), lambda i: (0, 0))
    seq_d = pl.BlockSpec((None, 1, d), lambda i: (i // tpd, 0, 0))

    def modulate(name, xin, g, shift, scale):
        def body(x_ref, g_ref, sh_ref, sc_ref, h_ref):
            xx = x_ref[...]
            yy = xx * lax.rsqrt(_mean(xx * xx) + EPS)
            h_ref[...] = ((yy * g_ref[...]) * (1.0 + sc_ref[...]) + sh_ref[...]).astype(BF16)

        return pl.pallas_call(
            body, name=name, grid=(ntd,), out_shape=jax.ShapeDtypeStruct((t, d), BF16),
            in_specs=[row_d, vec_d, seq_d, seq_d], out_specs=row_d, compiler_params=_cp(("parallel",)))(xin, g, shift, scale)

    h1 = modulate("mod1_fwd", xf, g_mix + started, shift_m, scale_m)

    tk_d = _blk(d, 1024)
    def proj_part(tag, land, ids, prev):
        return _mm(
            "mm_proj_" + tag, h1, land, mode="nn", grid=(nm, ids.shape[0], 1),
            a_spec=pl.BlockSpec((tm, d), lambda i, j, k, s: (i, 0)),
            b_spec=pl.BlockSpec((None, d, win_sh), lambda i, j, k, s: (s[j], 0, 0)),
            acc_shape=(tm, win_sh), outs=(jax.ShapeDtypeStruct((t, N_DEV * win_sh), BF16),),
            out_specs=(pl.BlockSpec((tm, win_sh), lambda i, j, k, s: (i, s[j])),), epilogue=_store_bf16,
            ids=ids, prev=prev)[0]

    w_in_g, proj = staged_product("w_in", ag_in, h1, proj_part)

    hb = tb // HALO
    n_hb = t // HALO

    def main_col(col):
        return pl.BlockSpec((tb, dc), lambda i, col=col: (i, col))

    def chunk_col(col):
        return pl.BlockSpec((ch, dc), lambda i, col=col: (i, col))

    def prev_col(col):
        return pl.BlockSpec((HALO, dc), lambda i, col=col: (jnp.maximum(i * hb - 1, 0), col))

    def next_col(col):
        return pl.BlockSpec((HALO, dc), lambda i, col=col: (jnp.minimum((i + 1) * hb, n_hb - 1), col))

    vec_c = pl.BlockSpec((1, dc), lambda i: (0, 0))
    row_c = pl.BlockSpec((tb, dc), lambda i: (i, 0))
    convw_spec = pl.BlockSpec((kc, dc), lambda i: (0, 0))
    sguw_spec = pl.BlockSpec((heads, ch, ch), lambda i: (0, 0, 0))
    bias_spec = pl.BlockSpec((ch, dc), lambda i: (0, 0))
    n_chunk = tb // ch

    sgu_w_bf = sgu_w[0].astype(BF16)
    sgu_wt_bf = jnp.swapaxes(sgu_w[0], 1, 2).astype(BF16)
    sgu_bias = jnp.repeat(sgu_b[0].T, hd, axis=1)

    CL = 2 * LANES
    conv_pieces = [pl.ds(l0, CL) for l0 in range(0, dc, CL)]
    stage_rows = tb + 2 * HALO
    sh_shape = (SUBLANES, stage_rows + SUBLANES, CL)
    padc = kc // 2
    assert dc % CL == 0 and tb % HALO == 0

    def shift_copies(sh):
        sh[0, pl.ds(stage_rows, SUBLANES), :] = jnp.zeros((SUBLANES, CL), F32)
        for r in range(1, SUBLANES):
            sh[r, pl.ds(0, stage_rows), :] = sh[0, pl.ds(r, stage_rows), :]

    def tap_piece(sh, off, row0):
        r = off % SUBLANES
        return sh[r, pl.ds(off - r + row0, SUBLANES), :]

    def stage_glu(sh, i, cols, pv, pg, pvp, pgp, pvn, pgn):
        first = (i % tps) == 0
        last = (i % tps) == tps - 1

        def glu(v_ref, g_ref, rows):
            return v_ref[rows, cols].astype(F32) * _sigmoid(g_ref[rows, cols].astype(F32))

        whole = pl.ds(0, HALO)
        sh[0, pl.ds(0, HALO), :] = jnp.where(first, 0.0, glu(pvp, pgp, whole))
        for rr in range(tb // HALO):
            sh[0, pl.ds(HALO + rr * HALO, HALO), :] = glu(pv, pg, pl.ds(rr * HALO, HALO))
        sh[0, pl.ds(HALO + tb, HALO), :] = jnp.where(last, 0.0, glu(pvn, pgn, whole))
        shift_copies(sh)

    def stage_taps(wb, cw, cols):
        for k in range(kc):
            wb[k] = jnp.broadcast_to(cw[pl.ds(k, 1), cols], (SUBLANES, CL))

    def layer_norm_stats(z):
        mu = _mean(z)
        zc = z - mu
        rstd = lax.rsqrt(_mean(zc * zc) + EPS)
        return zc * rstd, rstd

    def sgu_mix(vs_ref, w_ref, vnb, bias_ref, n_chunks):
        for cc in range(n_chunks):
            for h in range(heads):
                blk = jnp.dot(w_ref[h], vnb[cc * ch:(cc + 1) * ch, h * hd:(h + 1) * hd], preferred_element_type=F32)
                vs_ref[pl.ds(cc * ch, ch), pl.ds(h * hd, hd)] = blk + bias_ref[:, pl.ds(h * hd, hd)]

    def mix_fwd_body(pv, pg, pu, pw, pvp, pgp, pvn, pgn, cw, cb, lg, lb, og, slg, slb, sw, sbias, sog, y_ref, a1_ref, sh_e, wb, vs_ref):
        i = pl.program_id(0)
        for cols in conv_pieces:
            stage_glu(sh_e, i, cols, pv, pg, pvp, pgp, pvn, pgn)
            stage_taps(wb, cw, cols)
            bias = jnp.broadcast_to(cb[:, cols], (SUBLANES, CL))
            for row0 in range(0, tb, SUBLANES):
                acc = bias
                for k in range(kc):
                    acc = acc + wb[k] * tap_piece(sh_e, HALO - padc + k, row0)
                a1_ref[pl.ds(row0, SUBLANES), cols] = acc
        xh, _ = layer_norm_stats(a1_ref[...])
        a2 = xh * lg[...] + lb[...]
        a3 = a2 * _sigmoid(a2)
        ya = a3 * lax.rsqrt(_mean(a3 * a3) + EPS) * og[...]
        y_ref[:, pl.ds(0, dc)] = ya.astype(BF16)

        u = _gelu(pu[...].astype(F32))
        vv = _gelu(pw[...].astype(F32))
        xhv, _ = layer_norm_stats(vv)
        vn = xhv * slg[...] + slb[...]
        sgu_mix(vs_ref, sw, vn.astype(BF16), sbias, n_chunk)
        bg = u * vs_ref[...]
        yb = bg * lax.rsqrt(_mean(bg * bg) + EPS) * sog[...]
        y_ref[:, pl.ds(dc, dc)] = yb.astype(BF16)

    y, a1 = pl.pallas_call(
        mix_fwd_body, name="mix_fwd", grid=(nt,),
        out_shape=(jax.ShapeDtypeStruct((t, 2 * dc), BF16), jax.ShapeDtypeStruct((t, dc), F32)),
        in_specs=[main_col(0), main_col(1), main_col(2), main_col(3), prev_col(0), prev_col(1), next_col(0), next_col(1),
                  convw_spec, vec_c, vec_c, vec_c, vec_c, vec_c, vec_c, sguw_spec, bias_spec, vec_c],
        out_specs=(pl.BlockSpec((tb, 2 * dc), lambda i: (i, 0)), row_c),
        scratch_shapes=[pltpu.VMEM(sh_shape, F32), pltpu.VMEM((kc, SUBLANES, CL), F32), pltpu.VMEM((tb, dc), F32)],
        compiler_params=_cp(("parallel",)),
    )(proj, proj, proj, proj, proj, proj, proj, proj, conv_w_full, conv_b, conv_ln_g, conv_ln_b, out_g_conv,
      sgu_ln_g, sgu_ln_b, sgu_w_bf, sgu_bias, out_g_sgu)

    tn_d = _blk(d, 1024)
    gate_spec3 = pl.BlockSpec((None, 1, tn_d), lambda i, j, k: (i * tm // seq, 0, j))
    res_spec3 = pl.BlockSpec((tm, tn_d), lambda i, j, k: (i, j))
    td_shape = jax.ShapeDtypeStruct((t, d), F32)

    w_out_full = gather_end("w_out", ag_out, y).reshape(d, d)
    o_mix, x1 = _mm(
        "mm_out", y, w_out_full, mode="nn", grid=(nm, d // tn_d, 1),
        a_spec=pl.BlockSpec((tm, d), lambda i, j, k: (i, 0)), b_spec=pl.BlockSpec((d, tn_d), lambda i, j, k: (0, j)),
        acc_shape=(tm, tn_d), outs=(td_shape, td_shape), out_specs=(res_spec3, res_spec3),
        epilogue=_residual_epilogue, extra=(xf, gate_m), extra_specs=(res_spec3, gate_spec3))

    h2 = modulate("mod2_fwd", x1, g_ffn, shift_f, scale_f)

    tm_up = min(1024, seq)
    nk_d = d // tk_d
    def up_part(tag, land, ids, prev):
        return _mm(
            "mm_up_" + tag, h2, land, mode="nt", grid=(t // tm_up, ids.shape[0], nk_d),
            a_spec=pl.BlockSpec((tm_up, tk_d), lambda i, j, k, s: (i, k)),
            b_spec=pl.BlockSpec((None, cup, tk_d), lambda i, j, k, s: (s[j], 0, k)),
            acc_shape=(tm_up, cup), outs=(jax.ShapeDtypeStruct((N_DEV, t, cup), BF16),),
            out_specs=(pl.BlockSpec((None, tm_up, cup), lambda i, j, k, s: (s[j], i, 0)),), epilogue=_store_bf16,
            ids=ids, prev=prev)[0]

    w_up_g, up = staged_product("w_up", ag_up, h2, up_part)
    up4 = up.reshape(2, n_fs, t, cup)

    def ffn_halo(which, lo):
        if lo:
            return pl.BlockSpec((None, None, HALO, cup), lambda j, i: (which, j, jnp.maximum(i * hb - 1, 0), 0))
        return pl.BlockSpec((None, None, HALO, cup), lambda j, i: (which, j, jnp.minimum((i + 1) * hb, n_hb - 1), 0))

    pair_spec = pl.BlockSpec((2, None, tb, cup), lambda j, i: (0, j, i, 0))
    fw_spec = pl.BlockSpec((None, kf, cup), lambda j, i: (j, 0, 0))
    fb_spec = pl.BlockSpec((None, 1, cup), lambda j, i: (j, 0, 0))
    act_spec = pl.BlockSpec((None, tb, cup), lambda j, i: (j, i, 0))

    RC = HALO
    LC = 2 * LANES
    pieces = [(l0, min(LC, cup - l0)) for l0 in range(0, cup, LC)]
    n_rc = tb // RC
    pad = SUBLANES

    def fill_gate_ext(ext, i, cols, gate_ref, prev_ref, next_ref):
        first = (i % tps) == 0
        last = (i % tps) == tps - 1
        lw = cols.size
        ext[pl.ds(0, pad), cols] = jnp.zeros((pad, lw), F32)
        ext[pl.ds(pad, HALO), cols] = jnp.where(first, 0.0, prev_ref[:, cols].astype(F32))
        for rr in range(n_rc):
            ext[pl.ds(pad + HALO + rr * RC, RC), cols] = gate_ref[0, pl.ds(rr * RC, RC), cols].astype(F32)
        ext[pl.ds(pad + HALO + tb, HALO), cols] = jnp.where(last, 0.0, next_ref[:, cols].astype(F32))
        ext[pl.ds(pad + 2 * HALO + tb, pad), cols] = jnp.zeros((pad, lw), F32)

    def tap_rows(fw, fb, cols):
        lw = cols.size
        return [jnp.broadcast_to(fw[pl.ds(k, 1), cols], (RC, lw)) for k in range(kf)] + [jnp.broadcast_to(fb[:, cols], (RC, lw))]

    def ffn_fwd_body(pair, gp, gn, fw, fb, act_ref, ext):
        i = pl.program_id(1)
        for l0, lw in pieces:
            cols = pl.ds(l0, lw)
            fill_gate_ext(ext, i, cols, pair, gp, gn)
            w0, w1, w2, bb = tap_rows(fw, fb, cols)
            for rr in range(n_rc):
                e0 = pad + HALO + rr * RC
                gte = bb + w0 * ext[pl.ds(e0 - 1, RC), cols]
                gte = gte + w1 * ext[pl.ds(e0, RC), cols]
                gte = gte + w2 * ext[pl.ds(e0 + 1, RC), cols]
                val = pair[1, pl.ds(rr * RC, RC), cols].astype(F32)
                act_ref[pl.ds(rr * RC, RC), cols] = (gte * _sigmoid(gte) * val).astype(BF16)

    ext_rows = tb + 2 * HALO + 2 * pad
    act = pl.pallas_call(
        ffn_fwd_body, name="ffn_fwd", grid=(n_fs, nt), out_shape=jax.ShapeDtypeStruct((n_fs, t, cup), BF16),
        in_specs=[pair_spec, ffn_halo(0, True), ffn_halo(0, False), fw_spec, fb_spec], out_specs=act_spec,
        scratch_shapes=[pltpu.VMEM((ext_rows, cup), F32)], compiler_params=_cp(("parallel", "parallel")),
    )(up4, up4, up4, ffn_w_s, ffn_b_s)

    w_down_s = gather_end("w_down", ag_down, act).reshape(n_fs, cup, d)
    gate_spec_f = pl.BlockSpec((None, 1, tn_d), lambda i, j, k: (i * tm_up // seq, 0, j))
    res_spec_f = pl.BlockSpec((tm_up, tn_d), lambda i, j, k: (i, j))
    dn, x2 = _mm(
        "mm_down", act, w_down_s, mode="nn", grid=(t // tm_up, d // tn_d, n_fs),
        a_spec=pl.BlockSpec((None, tm_up, cup), lambda i, j, k: (k, i, 0)),
        b_spec=pl.BlockSpec((None, cup, tn_d), lambda i, j, k: (k, 0, j)),
        acc_shape=(tm_up, tn_d), outs=(td_shape, td_shape), out_specs=(res_spec_f, res_spec_f),
        epilogue=_residual_epilogue, extra=(x1, gate_f), extra_specs=(res_spec_f, gate_spec_f))

    acc_d = pl.BlockSpec((1, d), lambda i: (0, 0))
    seq_acc = pl.BlockSpec((None, 1, d), lambda i: (i // tpd, 0, 0))

    def head_body(x_ref, t_ref, g_ref, dn_ref, gate_ref, dx_ref, ddn_ref, loss_ref, dg_ref, dgate_ref):
        i = pl.program_id(0)
        xx = x_ref[...]
        rr = lax.rsqrt(_mean(xx * xx) + EPS)
        xn = xx * rr
        err = xn * g_ref[...] - t_ref[...]
        dyf = err * (1.0 / d)
        dxn = dyf * g_ref[...]
        dx = rr * (dxn - xn * _mean(dxn * xn))
        dx_ref[...] = dx
        ddn_ref[...] = (gate_ref[...] * dx).astype(BF16)
        part = 0.5 * jnp.sum(_mean(err * err), axis=0, keepdims=True)

        @pl.when(i == 0)
        def _():
            loss_ref[...] = jnp.zeros_like(loss_ref)
            dg_ref[...] = jnp.zeros_like(dg_ref)

        @pl.when(i % tpd == 0)
        def _():
            dgate_ref[...] = jnp.zeros_like(dgate_ref)

        loss_ref[...] += jnp.broadcast_to(part, loss_ref.shape)
        dg_ref[...] += _colsum(dyf * xn)
        dgate_ref[...] += _colsum(dx * dn_ref[...])

    seq_shape = jax.ShapeDtypeStruct((bsz, 1, d), F32)
    vec_shape = jax.ShapeDtypeStruct((1, d), F32)
    dx2, ddn, loss_part, dg_final, dgate_f = pl.pallas_call(
        head_body, name="loss_head", grid=(ntd,),
        out_shape=(td_shape, jax.ShapeDtypeStruct((t, d), BF16), jax.ShapeDtypeStruct((1, LANES), F32), vec_shape, seq_shape),
        in_specs=[row_d, row_d, vec_d, row_d, seq_d],
        out_specs=(row_d, row_d, pl.BlockSpec((1, LANES), lambda i: (0, 0)), acc_d, seq_acc),
        compiler_params=_cp(("arbitrary",)))(x2, tgt, g_final.reshape(1, d), dn, gate_f)

    tk_t = min(1024, t)
    nk_t = t // tk_t
    tn_gw = _blk(d, 1024)
    gw_down = _mm(
        "mm_gw_down", act, ddn, mode="tn", grid=(n_fs, d // tn_gw, nk_t),
        a_spec=pl.BlockSpec((None, tk_t, cup), lambda j, n, k: (j, k, 0)),
        b_spec=pl.BlockSpec((tk_t, tn_gw), lambda j, n, k: (k, n)),
        acc_shape=(cup, tn_gw), outs=(jax.ShapeDtypeStruct((n_fs, cup, d), BF16),),
        out_specs=(pl.BlockSpec((None, cup, tn_gw), lambda j, n, k: (j, 0, n)),), epilogue=_store_bf16)[0]

    def exchange_begin(name, gw):
        return _alltoall_start("a2a_" + name, gw.reshape(N_DEV, gw.size // (N_DEV * gw.shape[-1]), gw.shape[-1]))

    x_down = exchange_begin("w_down", gw_down)
    dact = _mm(
        "mm_dact", ddn, w_down_s, mode="nt", grid=(t // tm_up, n_fs, nk_d),
        a_spec=pl.BlockSpec((tm_up, tk_d), lambda i, j, k: (i, k)),
        b_spec=pl.BlockSpec((None, cup, tk_d), lambda i, j, k: (j, 0, k)),
        acc_shape=(tm_up, cup), outs=(jax.ShapeDtypeStruct((n_fs, t, cup), BF16),),
        out_specs=(pl.BlockSpec((None, tm_up, cup), lambda i, j, k: (j, i, 0)),), epilogue=_store_bf16, after=(x_down[4],))[0]


    def ffn_bwd_body(pair, gp, gn, vp, vn, da, dap, dan, fw, fb, dup_ref, dw_ref, db_ref, ext, dg_e):
        i = pl.program_id(1)
        first = (i % tps) == 0
        last = (i % tps) == tps - 1

        @pl.when(i == 0)
        def _():
            dw_ref[...] = jnp.zeros_like(dw_ref)
            db_ref[...] = jnp.zeros_like(db_ref)

        for l0, lw in pieces:
            cols = pl.ds(l0, lw)
            fill_gate_ext(ext, i, cols, pair, gp, gn)
            w0, w1, w2, bb = tap_rows(fw, fb, cols)
            sums = [jnp.zeros((RC, lw), F32) for _ in range(kf + 1)]
            for cc in range(n_rc + 2):
                e0 = pad + cc * RC
                taps = [ext[pl.ds(e0 - 1, RC), cols], ext[pl.ds(e0, RC), cols], ext[pl.ds(e0 + 1, RC), cols]]
                gte = bb + w0 * taps[0]
                gte = gte + w1 * taps[1]
                gte = gte + w2 * taps[2]
                sg = _sigmoid(gte)
                if cc == 0:
                    val, dact_v = vp[:, cols], dap[:, cols]
                elif cc == n_rc + 1:
                    val, dact_v = vn[:, cols], dan[:, cols]
                else:
                    rows = pl.ds((cc - 1) * RC, RC)
                    val, dact_v = pair[1, rows, cols], da[rows, cols]
                val, dact_v = val.astype(F32), dact_v.astype(F32)
                dgte = dact_v * val * (sg * (1.0 + gte * (1.0 - sg)))
                if cc == 0:
                    dgte = jnp.where(first, 0.0, dgte)
                elif cc == n_rc + 1:
                    dgte = jnp.where(last, 0.0, dgte)
                else:
                    dup_ref[1, rows, cols] = (dact_v * (gte * sg)).astype(BF16)
                    for k in range(kf):
                        sums[k] = sums[k] + dgte * taps[k]
                    sums[kf] = sums[kf] + dgte
                dg_e[pl.ds(cc * RC, RC), cols] = dgte
            for k in range(kf):
                dw_ref[pl.ds(k, 1), cols] += _colsum(sums[k])
            db_ref[:, cols] += _colsum(sums[kf])
            for rr in range(n_rc):
                e0 = HALO + rr * RC
                dgate = w0 * dg_e[pl.ds(e0 + 1, RC), cols] + w1 * dg_e[pl.ds(e0, RC), cols] + w2 * dg_e[pl.ds(e0 - 1, RC), cols]
                dup_ref[0, pl.ds(rr * RC, RC), cols] = dgate.astype(BF16)

    def act_halo(lo):
        if lo:
            return pl.BlockSpec((None, HALO, cup), lambda j, i: (j, jnp.maximum(i * hb - 1, 0), 0))
        return pl.BlockSpec((None, HALO, cup), lambda j, i: (j, jnp.minimum((i + 1) * hb, n_hb - 1), 0))

    dup4, g_ffn_w_s, g_ffn_b_s = pl.pallas_call(
        ffn_bwd_body, name="ffn_bwd", grid=(n_fs, nt),
        out_shape=(jax.ShapeDtypeStruct((2, n_fs, t, cup), BF16), jax.ShapeDtypeStruct((n_fs, kf, cup), F32),
                   jax.ShapeDtypeStruct((n_fs, 1, cup), F32)),
        in_specs=[pair_spec, ffn_halo(0, True), ffn_halo(0, False), ffn_halo(1, True), ffn_halo(1, False),
                  act_spec, act_halo(True), act_halo(False), fw_spec, fb_spec],
        out_specs=(pair_spec, fw_spec, fb_spec),
        scratch_shapes=[pltpu.VMEM((ext_rows, cup), F32), pltpu.VMEM((tb + 2 * HALO, cup), F32)],
        compiler_params=_cp(("parallel", "arbitrary")),
    )(up4, up4, up4, up4, up4, dact, dact, dact, ffn_w_s, ffn_b_s)
    dup = dup4.reshape(N_DEV, t, cup)

    tm_w = _blk(d, 1024)
    gw_up = _mm(
        "mm_gw_up", dup, h2, mode="tn", grid=(N_DEV, d // tn_gw, nk_t),
        a_spec=pl.BlockSpec((None, tk_t, cup), lambda j, n, k: (j, k, 0)),
        b_spec=pl.BlockSpec((tk_t, tn_gw), lambda j, n, k: (k, n)),
        acc_shape=(cup, tn_gw), outs=(jax.ShapeDtypeStruct((N_DEV, cup, d), BF16),),
        out_specs=(pl.BlockSpec((None, cup, tn_gw), lambda j, n, k: (j, 0, n)),), epilogue=_store_bf16)[0]

    x_up = exchange_begin("w_up", gw_up)
    dh2 = _mm(
        "mm_dh2", dup, w_up_g, mode="nn", grid=(t // tm_up, d // tn_d, N_DEV),
        a_spec=pl.BlockSpec((None, tm_up, cup), lambda i, j, k: (k, i, 0)),
        b_spec=pl.BlockSpec((None, cup, tn_d), lambda i, j, k: (k, 0, j)),
        acc_shape=(tm_up, tn_d), outs=(td_shape,), out_specs=(pl.BlockSpec((tm_up, tn_d), lambda i, j, k: (i, j)),),
        epilogue=_store_f32, after=(x_up[4],))[0]

    def modulate_bwd(name, xin, dh, dres, g, scale, gate=None, branch=None):
        gated = gate is not None

        def body(*refs):
            x_ref, dh_ref, dres_ref, g_ref, sc_ref = refs[:5]
            rest = refs[5:]
            if gated:
                gate_ref, br_ref, dx_ref, dsh_ref, dsc_ref, dg_ref, do_ref, dgate_ref = rest
            else:
                dx_ref, dsh_ref, dsc_ref, dg_ref = rest
            i = pl.program_id(0)
            xx = x_ref[...]
            rr = lax.rsqrt(_mean(xx * xx) + EPS)
            xn = xx * rr
            s1 = 1.0 + sc_ref[...]
            dhh = dh_ref[...]
            dxn = dhh * g_ref[...] * s1
            dx = dres_ref[...] + rr * (dxn - xn * _mean(dxn * xn))
            dx_ref[...] = dx

            @pl.when(i == 0)
            def _():
                dg_ref[...] = jnp.zeros_like(dg_ref)

            @pl.when(i % tpd == 0)
            def _():
                dsh_ref[...] = jnp.zeros_like(dsh_ref)
                dsc_ref[...] = jnp.zeros_like(dsc_ref)
                if gated:
                    dgate_ref[...] = jnp.zeros_like(dgate_ref)

            dsh_ref[...] += _colsum(dhh)
            dsc_ref[...] += _colsum(dhh * (xn * g_ref[...]))
            dg_ref[...] += _colsum(dhh * s1 * xn)
            if gated:
                do_ref[...] = (gate_ref[...] * dx).astype(BF16)
                dgate_ref[...] += _colsum(dx * br_ref[...])

        ins = [xin, dh, dres, g, scale]
        in_specs = [row_d, row_d, row_d, vec_d, seq_d]
        outs = [td_shape, seq_shape, seq_shape, vec_shape]
        out_specs = [row_d, seq_acc, seq_acc, acc_d]
        if gated:
            ins += [gate, branch]
            in_specs += [seq_d, row_d]
            outs += [jax.ShapeDtypeStruct((t, d), BF16), seq_shape]
            out_specs += [row_d, seq_acc]
        return pl.pallas_call(
            body, name=name, grid=(ntd,), out_shape=tuple(outs), in_specs=in_specs, out_specs=tuple(out_specs),
            compiler_params=_cp(("arbitrary",)))(*ins)

    dx1, dshift_f, dscale_f, dg_ffn, d_o, dgate_m = modulate_bwd("mod2_bwd", x1, dh2, dx2, g_ffn, scale_f, gate_m, o_mix)

    tn_w = _blk(d, 2048)
    gw_out = _mm(
        "mm_gw_out", y, d_o, mode="tn", grid=(d // tm_w, d // tn_w, nk_t),
        a_spec=pl.BlockSpec((tk_t, tm_w), lambda i, j, k: (k, i)), b_spec=pl.BlockSpec((tk_t, tn_w), lambda i, j, k: (k, j)),
        acc_shape=(tm_w, tn_w), outs=(jax.ShapeDtypeStruct((d, d), BF16),),
        out_specs=(pl.BlockSpec((tm_w, tn_w), lambda i, j, k: (i, j)),), epilogue=_store_bf16)[0]

    x_out = exchange_begin("w_out", gw_out)
    dy = _mm(
        "mm_dy", d_o, w_out_full, mode="nt", grid=(nm, d // tn_d, 1),
        a_spec=pl.BlockSpec((tm, d), lambda i, j, k: (i, 0)), b_spec=pl.BlockSpec((tn_d, d), lambda i, j, k: (j, 0)),
        acc_shape=(tm, tn_d), outs=(jax.ShapeDtypeStruct((t, d), BF16),),
        out_specs=(pl.BlockSpec((tm, tn_d), lambda i, j, k: (i, j)),), epilogue=_store_bf16, after=(x_out[4],))[0]

    def conv_norm_bwd_body(dy_ref, a1_ref, lg, lb, og, da1_ref, dog_ref, dlg_ref, dlb_ref):
        i = pl.program_id(0)
        xh, rstd = layer_norm_stats(a1_ref[...])
        a2 = xh * lg[...] + lb[...]
        sg = _sigmoid(a2)
        a3 = a2 * sg
        r3 = lax.rsqrt(_mean(a3 * a3) + EPS)
        n3 = a3 * r3
        dya = dy_ref[...].astype(F32)
        dn3 = dya * og[...]
        da3 = r3 * (dn3 - n3 * _mean(dn3 * n3))
        da2 = da3 * (sg * (1.0 + a2 * (1.0 - sg)))
        dxh = da2 * lg[...]
        da1_ref[...] = rstd * (dxh - _mean(dxh) - xh * _mean(dxh * xh))

        @pl.when(i == 0)
        def _():
            dog_ref[...] = jnp.zeros_like(dog_ref)
            dlg_ref[...] = jnp.zeros_like(dlg_ref)
            dlb_ref[...] = jnp.zeros_like(dlb_ref)

        dog_ref[...] += _colsum(dya * n3)
        dlg_ref[...] += _colsum(da2 * xh)
        dlb_ref[...] += _colsum(da2)

    vecc_shape = jax.ShapeDtypeStruct((1, dc), F32)
    da1, g_og_conv, g_cln_g, g_cln_b = pl.pallas_call(
        conv_norm_bwd_body, name="conv_norm_bwd", grid=(nt,),
        out_shape=(jax.ShapeDtypeStruct((t, dc), F32), vecc_shape, vecc_shape, vecc_shape),
        in_specs=[main_col(0), row_c, vec_c, vec_c, vec_c], out_specs=(row_c, vec_c, vec_c, vec_c),
        compiler_params=_cp(("arbitrary",)))(dy, a1, conv_ln_g, conv_ln_b, out_g_conv)

    def sgu_bwd_body(dy_ref, pu, pw, slg, slb, sw, swt, sbias, sog, dproj_ref, dsog_ref, dslg_ref, dslb_ref, dsw_ref, dsb_ref, vs_ref, dvn_ref):
        i = pl.program_id(0)
        zu = pu[...].astype(F32)
        zv = pw[...].astype(F32)
        u = _gelu(zu)
        vv = _gelu(zv)
        xhv, rstd = layer_norm_stats(vv)
        vnb = (xhv * slg[...] + slb[...]).astype(BF16)
        sgu_mix(vs_ref, sw, vnb, sbias, 1)
        vs = vs_ref[...]
        bg = u * vs
        rb = lax.rsqrt(_mean(bg * bg) + EPS)
        nb = bg * rb
        dyb = dy_ref[...].astype(F32)
        dnb = dyb * sog[...]
        dbg = rb * (dnb - nb * _mean(dnb * nb))
        du = dbg * vs
        dvs = dbg * u
        dvsb = dvs.astype(BF16)

        @pl.when(i == 0)
        def _():
            dsog_ref[...] = jnp.zeros_like(dsog_ref)
            dslg_ref[...] = jnp.zeros_like(dslg_ref)
            dslb_ref[...] = jnp.zeros_like(dslb_ref)
            dsw_ref[...] = jnp.zeros_like(dsw_ref)
            dsb_ref[...] = jnp.zeros_like(dsb_ref)

        dsog_ref[...] += _colsum(dyb * nb)
        dsb_ref[...] += dvs
        for h in range(heads):
            dblk = dvsb[:, h * hd:(h + 1) * hd]
            vblk = vnb[:, h * hd:(h + 1) * hd]
            dsw_ref[h] += lax.dot_general(dblk, vblk, (((1,), (1,)), ((), ())), preferred_element_type=F32)
            dvn_ref[:, pl.ds(h * hd, hd)] = jnp.dot(swt[h], dblk, preferred_element_type=F32)
        dvn = dvn_ref[...]
        dslg_ref[...] += _colsum(dvn * xhv)
        dslb_ref[...] += _colsum(dvn)
        dxh = dvn * slg[...]
        dvv = rstd * (dxh - _mean(dxh) - xhv * _mean(dxh * xhv))
        dproj_ref[:, pl.ds(0, dc)] = (du * _gelu_grad(zu)).astype(BF16)
        dproj_ref[:, pl.ds(dc, dc)] = (dvv * _gelu_grad(zv)).astype(BF16)

    dproj, g_og_sgu, g_sln_g, g_sln_b, g_sgu_w, g_sgu_bias = pl.pallas_call(
        sgu_bwd_body, name="sgu_bwd", grid=(t // ch,),
        out_shape=(jax.ShapeDtypeStruct((t, 4 * dc), BF16), vecc_shape, vecc_shape, vecc_shape,
                   jax.ShapeDtypeStruct((heads, ch, ch), F32), jax.ShapeDtypeStruct((ch, dc), F32)),
        in_specs=[chunk_col(1), chunk_col(2), chunk_col(3), vec_c, vec_c, sguw_spec, sguw_spec, bias_spec, vec_c],
        out_specs=(pl.BlockSpec((ch, 2 * dc), lambda i: (i, 1)), vec_c, vec_c, vec_c, sguw_spec, bias_spec),
        scratch_shapes=[pltpu.VMEM((ch, dc), F32), pltpu.VMEM((ch, dc), F32)],
        compiler_params=_cp(("arbitrary",)),
    )(dy, proj, proj, sgu_ln_g, sgu_ln_b, sgu_w_bf, sgu_wt_bf, sgu_bias, out_g_sgu)

    def sgu_bias_reduce_body(b_ref, o_ref):
        lane = lax.broadcasted_iota(jnp.int32, (ch, LANES), 1)
        res = jnp.zeros((ch, LANES), F32)
        for h in range(heads):
            res = jnp.where(lane == h, jnp.sum(b_ref[:, pl.ds(h * hd, hd)], axis=1, keepdims=True), res)
        o_ref[...] = res

    g_sgu_b_t = pl.pallas_call(
        sgu_bias_reduce_body, name="sgu_bias_reduce", out_shape=jax.ShapeDtypeStruct((ch, LANES), F32),
        in_specs=[pl.BlockSpec(memory_space=pltpu.VMEM)], out_specs=pl.BlockSpec(memory_space=pltpu.VMEM),
        compiler_params=_cp())(g_sgu_bias)
    g_sgu_b = g_sgu_b_t[:, :heads].T

    def conv_bwd_body(da, dap, dan, pv, pg, pvp, pgp, pvn, pgn, cw, dproj_in, dproj_ref, dcw_ref, dcb_ref, sh_e, sh_d, wb):
        del dproj_in
        i = pl.program_id(0)
        first = (i % tps) == 0
        last = (i % tps) == tps - 1

        @pl.when(i == 0)
        def _():
            dcw_ref[...] = jnp.zeros_like(dcw_ref)
            dcb_ref[...] = jnp.zeros_like(dcb_ref)

        for cols in conv_pieces:
            stage_glu(sh_e, i, cols, pv, pg, pvp, pgp, pvn, pgn)
            sh_d[0, pl.ds(0, HALO), :] = jnp.where(first, 0.0, dap[:, cols])
            for rr in range(tb // HALO):
                sh_d[0, pl.ds(HALO + rr * HALO, HALO), :] = da[pl.ds(rr * HALO, HALO), cols]
            sh_d[0, pl.ds(HALO + tb, HALO), :] = jnp.where(last, 0.0, dan[:, cols])
            shift_copies(sh_d)
            stage_taps(wb, cw, cols)
            for rr in range(tb // HALO):
                halves = []
                for row0 in (rr * HALO, rr * HALO + SUBLANES):
                    acc = jnp.zeros((SUBLANES, CL), F32)
                    for k in range(kc):
                        acc = acc + wb[k] * tap_piece(sh_d, HALO + padc - k, row0)
                    halves.append(acc)
                da0 = jnp.concatenate(halves, axis=0)
                rows = pl.ds(rr * HALO, HALO)
                vv = pv[rows, cols].astype(F32)
                sg = _sigmoid(pg[rows, cols].astype(F32))
                dproj_ref[rows, cols] = (da0 * sg).astype(BF16)
                dproj_ref[rows, pl.ds(dc + cols.start, CL)] = (da0 * vv * sg * (1.0 - sg)).astype(BF16)
            for k in range(kc):
                acc = jnp.zeros((SUBLANES, CL), F32)
                for row0 in range(0, tb, SUBLANES):
                    acc = acc + da[pl.ds(row0, SUBLANES), cols] * tap_piece(sh_e, HALO - padc + k, row0)
                dcw_ref[pl.ds(k, 1), cols] += _colsum(acc)
            acc = jnp.zeros((SUBLANES, CL), F32)
            for row0 in range(0, tb, SUBLANES):
                acc = acc + da[pl.ds(row0, SUBLANES), cols]
            dcb_ref[:, cols] += _colsum(acc)

    def halo_rows(lo):
        if lo:
            return pl.BlockSpec((HALO, dc), lambda i: (jnp.maximum(i * hb - 1, 0), 0))
        return pl.BlockSpec((HALO, dc), lambda i: (jnp.minimum((i + 1) * hb, n_hb - 1), 0))

    dproj, g_conv_w, g_conv_b = pl.pallas_call(
        conv_bwd_body, name="conv_bwd", grid=(nt,),
        out_shape=(jax.ShapeDtypeStruct((t, 4 * dc), BF16), jax.ShapeDtypeStruct((kc, dc), F32), vecc_shape),
        in_specs=[row_c, halo_rows(True), halo_rows(False), main_col(0), main_col(1), prev_col(0), prev_col(1),
                  next_col(0), next_col(1), convw_spec, pl.BlockSpec(memory_space=pl.ANY)],
        out_specs=(pl.BlockSpec((tb, 2 * dc), lambda i: (i, 0)), convw_spec, vec_c),
        scratch_shapes=[pltpu.VMEM(sh_shape, F32), pltpu.VMEM(sh_shape, F32), pltpu.VMEM((kc, SUBLANES, CL), F32)],
        input_output_aliases={10: 0}, compiler_params=_cp(("arbitrary",)),
    )(da1, da1, da1, proj, proj, proj, proj, proj, proj, conv_w_full, dproj)

    tm_wi = _blk(d, 2048)
    gw_in = _mm(
        "mm_gw_in", h1, dproj, mode="tn", grid=(N_DEV, d // tm_wi, nk_t),
        a_spec=pl.BlockSpec((tk_t, tm_wi), lambda j, i, k: (k, i)),
        b_spec=pl.BlockSpec((tk_t, win_sh), lambda j, i, k: (k, j)),
        acc_shape=(tm_wi, win_sh), outs=(jax.ShapeDtypeStruct((N_DEV, d, win_sh), BF16),),
        out_specs=(pl.BlockSpec((None, tm_wi, win_sh), lambda j, i, k: (j, i, 0)),), epilogue=_store_bf16)[0]

    x_in = exchange_begin("w_in", gw_in)
    tn_h = _blk(d, 2048)
    dh1 = _mm(
        "mm_dh1", dproj, w_in_g, mode="nt", grid=(t // tm_up, d // tn_h, N_DEV),
        a_spec=pl.BlockSpec((tm_up, win_sh), lambda i, j, k: (i, k)),
        b_spec=pl.BlockSpec((None, tn_h, win_sh), lambda i, j, k: (k, j, 0)),
        acc_shape=(tm_up, tn_h), outs=(td_shape,), out_specs=(pl.BlockSpec((tm_up, tn_h), lambda i, j, k: (i, j)),),
        epilogue=_store_f32, after=(x_in[4],))[0]

    grad_x, dshift_m, dscale_m, dg_mix = modulate_bwd("mod1_bwd", xf, dh1, dx1, g_mix, scale_m)

    dmod = jnp.concatenate([dshift_m, dscale_m, dgate_m, dshift_f, dscale_f, dgate_f], axis=1)
    g_ffn_w_full = g_ffn_w_s.transpose(1, 0, 2).reshape(kf, dff)
    rep_names = ["g_mix", "conv_b", "conv_ln_g", "conv_ln_b", "sgu_ln_g", "sgu_ln_b", "sgu_w", "sgu_b",
                 "out_g_conv", "out_g_sgu", "g_ffn", "ffn_conv_b", "g_final"]
    rep_w = [g_mix, conv_b, conv_ln_g, conv_ln_b, sgu_ln_g, sgu_ln_b, sgu_w, sgu_b, out_g_conv, out_g_sgu, g_ffn, ffn_conv_b, g_final]
    rep_m = [m_g_mix, m_conv_b, m_conv_ln_g, m_conv_ln_b, m_sgu_ln_g, m_sgu_ln_b, m_sgu_w, m_sgu_b, m_out_g_conv, m_out_g_sgu,
             m_g_ffn, m_ffn_conv_b, m_g_final]
    rep_v = [v_g_mix, v_conv_b, v_conv_ln_g, v_conv_ln_b, v_sgu_ln_g, v_sgu_ln_b, v_sgu_w, v_sgu_b, v_out_g_conv, v_out_g_sgu,
             v_g_ffn, v_ffn_conv_b, v_g_final]
    rep_g = [dg_mix, g_conv_b, g_cln_g, g_cln_b, g_sln_g, g_sln_b, g_sgu_w, g_sgu_b, g_og_conv, g_og_sgu, dg_ffn,
             g_ffn_b_s, dg_final]
    rep_shapes = [w.shape for w in rep_w]
    extra_g = [g_conv_w, g_ffn_w_full, loss_part[:, :1]]
    extra_shapes = [(kc, dc), (kf, dff), (1, 1)]
    sum_blob = _pack(rep_g + extra_g)
    n_sum_rows = sum_blob.shape[0]
    blob1 = jnp.concatenate([sum_blob, _pack([dmod])], axis=0)
    small_land = lax.dynamic_update_slice(lax.empty((N_DEV,) + blob1.shape, F32), blob1[None], (me, 0, 0))
    small_started = _gather_start("ag1_small_grads", blob1, small_land)

    def big(name, started, after, w2d, m2d, v2d):
        send_sems, recv_sems, x_thru, land_thru, _ = started
        mine, recv = _alltoall_wait("a2aw_" + name, send_sems, recv_sems, x_thru, land_thru, after)
        return _adamw("adamw_" + name, w2d, m2d, v2d, slots=recv, own=mine, me=me)

    wdown_out = big("w_down", x_down, small_started[6], w_down[0], m_w_down[0], v_w_down[0])
    wup_out = big("w_up", x_up, wdown_out[0], w_up_t, m_w_up_t, v_w_up_t)
    wout_out = big("w_out", x_out, wup_out[0], w_out[0], m_w_out[0], v_w_out[0])

    gathered = gather_end("small_grads", small_started, wout_out[0])
    summed = _sum_slots("sum_small_grads", gathered[:, :n_sum_rows])
    n_rep_rows = _pack(rep_g).shape[0]
    g_conv_w_all, g_ffn_w_all, loss_all = _unpack(summed[n_rep_rows:], extra_shapes)
    loss = loss_all[0, 0]
    dmod_all = gathered[:, n_sum_rows:].reshape(N_DEV, -1)[:, :bsz * n_mod * d].reshape(n_batch, n_mod * d)

    rep_out = _adamw("adamw_small", _pack(rep_w), _pack(rep_m), _pack(rep_v), grad=summed[:n_rep_rows])
    rep_out = [_unpack(o, rep_shapes) for o in rep_out]
    rep = {name: tuple(rep_out[q][p] for q in range(4)) for p, name in enumerate(rep_names)}

    g_conv_w_me = lax.dynamic_slice(g_conv_w_all, (0, me * dc_sh), (kc, dc_sh))
    cw_out = _adamw("adamw_conv_w", conv_w[0], m_conv_w[0], v_conv_w[0], grad=g_conv_w_me)
    g_ffn_w_me = lax.dynamic_slice(g_ffn_w_all, (0, me * cw_sh), (kf, cw_sh))
    fw_out = _adamw("adamw_ffn_conv_w", ffn_conv_w[0], m_ffn_conv_w[0], v_ffn_conv_w[0], grad=g_ffn_w_me)

    g_b_ada = _col_sums("grad_b_ada", dmod_all)
    bada_out = _adamw("adamw_b_ada", b_ada, m_b_ada, v_b_ada, grad=g_b_ada)
    dmod_sh = lax.dynamic_slice(dmod_all, (0, me * ada_sh), (n_batch, ada_sh)).astype(BF16)
    wada_out = _adamw("adamw_w_ada", w_ada[0], m_w_ada[0], v_w_ada[0], ca_t=c_act.T, dmod=dmod_sh)

    win_out = big("w_in", x_in, wada_out[0], w_in[0], m_w_in[0], v_w_in[0])
    wup_out = tuple(jnp.swapaxes(o, 0, 1) for o in wup_out)

    def lead(outs4):
        return tuple(o[None] for o in outs4)

    results = {
        "w_ada": lead(wada_out), "b_ada": bada_out, "w_in": lead(win_out), "conv_w": lead(cw_out),
        "w_out": lead(wout_out), "w_up": lead(wup_out), "ffn_conv_w": lead(fw_out), "w_down": lead(wdown_out),
    }
    results.update(rep)
    order = ["w_ada", "b_ada", "g_mix", "w_in", "conv_w", "conv_b", "conv_ln_g", "conv_ln_b", "sgu_ln_g", "sgu_ln_b",
             "sgu_w", "sgu_b", "out_g_conv", "out_g_sgu", "w_out", "g_ffn", "w_up", "ffn_conv_w", "ffn_conv_b", "w_down", "g_final"]
    out = [loss, grad_x.reshape(bsz, seq, d)]
    for q in range(4):
        out += [results[n][q] for n in order]
    return tuple(out)
```

```python
import functools

import jax
import jax.numpy as jnp
from jax import lax
from jax.experimental import pallas as pl
from jax.experimental.pallas import tpu as pltpu

EPS = 1e-6
N_DEV = 8
MESH_ID = pl.DeviceIdType.MESH
V7X_VMEM_BYTES = 64 * 1024 * 1024
VMEM_LIMIT = V7X_VMEM_BYTES - 8 * 1024 * 1024
LANES = 128
SUBLANES = 8
HALO = 16
BLOB_ALIGN = SUBLANES * LANES

ADAM_LR = 0.001
ADAM_B1 = 0.9
ADAM_B2 = 0.999
ADAM_EPS = 1e-08
ADAM_WD = 0.01
ADAM_STEP = 10

F32 = jnp.float32
BF16 = jnp.bfloat16


def _cp(sem=None, **kw):
    return pltpu.CompilerParams(dimension_semantics=sem, vmem_limit_bytes=VMEM_LIMIT, **kw)


def _blk(n, pref):
    return pref if n % pref == 0 else n


def _rows_tile(rows, cols, budget=3 << 19):
    best = None
    for t in range(SUBLANES, rows + 1, SUBLANES):
        if rows % t == 0 and t * cols * 4 <= budget:
            best = t
    return best if best is not None else rows


def _sigmoid(z):
    return 1.0 / (1.0 + jnp.exp(-z))


def _gelu(z):
    return z * (lax.erf(z * 0.7071067811865476) + 1.0) * 0.5


def _gelu_grad(z):
    return 0.5 * (1.0 + lax.erf(z * 0.7071067811865476)) + z * jnp.exp(-0.5 * z * z) * 0.3989422804014327


def _mean(z):
    return jnp.mean(z, axis=-1, keepdims=True)


def _colsum(z):
    return jnp.sum(z, axis=0, keepdims=True)


def _my_pos():
    return lax.axis_index("x"), lax.axis_index("y"), lax.axis_index("c")


def _allgather(name, shard, in_vmem):
    r, cdim = shard.shape

    def body(x_ref, out_ref, send_sems, recv_sems, local_sem):
        x, y, c = _my_pos()
        me, sibling = (x, y, c), (x, y, 1 - c)
        chips = [(1 - x, y), (x, 1 - y), (1 - x, 1 - y)]

        def slot(px, py, pc):
            return out_ref.at[4 * px + 2 * py + pc]

        def copy(k, block, to, src=None):
            return pltpu.make_async_remote_copy(
                src_ref=slot(*block) if src is None else src, dst_ref=slot(*block),
                send_sem=send_sems.at[k], recv_sem=recv_sems.at[k], device_id=to, device_id_type=MESH_ID)

        mine = pltpu.make_async_copy(x_ref, slot(*me), local_sem)
        mine.start()
        first = [copy(0, me, sibling, src=x_ref)]
        first += [copy(1 + j, me, (*chip, c), src=x_ref) for j, chip in enumerate(chips)]
        for cp in first:
            cp.start()
        passed = [copy(4 + j, (*chip, c), sibling) for j, chip in enumerate(chips)]
        for j, chip in enumerate(chips):
            copy(1 + j, (*chip, c), me).wait_recv()
            passed[j].start()
        copy(0, sibling, me).wait_recv()
        for j, chip in enumerate(chips):
            copy(4 + j, (*chip, 1 - c), me).wait_recv()
        for cp in first + passed:
            cp.wait_send()
        mine.wait()

    space = pltpu.VMEM if in_vmem else pl.ANY
    return pl.pallas_call(
        body, name=name, out_shape=jax.ShapeDtypeStruct((N_DEV, r, cdim), shard.dtype),
        in_specs=[pl.BlockSpec(memory_space=space)], out_specs=pl.BlockSpec(memory_space=space),
        scratch_shapes=[pltpu.SemaphoreType.DMA((7,)), pltpu.SemaphoreType.DMA((7,)), pltpu.SemaphoreType.DMA],
        compiler_params=pltpu.CompilerParams(vmem_limit_bytes=VMEM_LIMIT),
    )(shard)


HBM_SPEC = pl.BlockSpec(memory_space=pltpu.HBM)
SEM_SPEC = pl.BlockSpec(memory_space=pltpu.SEMAPHORE)
ANY_SPEC = pl.BlockSpec(memory_space=pl.ANY)
SPLIT_EFFECT = pltpu.SideEffectType.DATAFLOW_SIDE_EFFECTING
TOKEN_SHAPE = jax.ShapeDtypeStruct((SUBLANES, LANES), F32)


def _hbm(a):
    return pltpu.with_memory_space_constraint(a, pltpu.HBM)


def _peers(plane):
    x, y, c = _my_pos()
    out = []
    for k in range(1, 4 if plane else N_DEV):
        bits = k << 1 if plane else k
        px = 1 - x if bits & 4 else x
        py = 1 - y if bits & 2 else y
        pc = 1 - c if bits & 1 else c
        if plane:
            out.append((k, (px, py, pc), 2 * px + py, 2 * x + y))
        else:
            out.append((k, (px, py, pc), 4 * px + 2 * py + pc, 4 * x + 2 * y + c))
    return out


def _alltoall_start(name, blocks, plane=False):
    def body(x_ref, land_ref, send_sems, recv_sems, x_thru, land_thru, token):
        del x_thru, land_thru
        for k, peer, slot, mine in _peers(plane):
            pltpu.make_async_remote_copy(
                src_ref=x_ref.at[slot], dst_ref=land_ref.at[mine], send_sem=send_sems.at[k - 1],
                recv_sem=recv_sems.at[k - 1], device_id=peer, device_id_type=MESH_ID).start()
        token[...] = jnp.zeros_like(token)

    sems = pltpu.SemaphoreType.DMA((blocks.shape[0] - 1,))
    buf = pltpu.HBM(blocks.shape, blocks.dtype)
    return pl.pallas_call(
        body, name=name, out_shape=(sems, sems, buf, buf, TOKEN_SHAPE), in_specs=(HBM_SPEC, HBM_SPEC),
        out_specs=(SEM_SPEC, SEM_SPEC, HBM_SPEC, HBM_SPEC, pl.BlockSpec(memory_space=pltpu.VMEM)),
        input_output_aliases={0: 2, 1: 3}, compiler_params=pltpu.CompilerParams(has_side_effects=SPLIT_EFFECT),
    )(_hbm(blocks), _hbm(lax.empty(blocks.shape, blocks.dtype)))


def _alltoall_wait(name, send_sems, recv_sems, x_thru, land_thru, after, plane=False):
    def body(x_ref, land_ref, send_sems, recv_sems, after_ref, x_dead, got_ref):
        del after_ref, x_dead, got_ref
        for k, peer, slot, _ in _peers(plane):
            cp = pltpu.make_async_remote_copy(
                src_ref=x_ref.at[slot], dst_ref=land_ref.at[slot], send_sem=send_sems.at[k - 1],
                recv_sem=recv_sems.at[k - 1], device_id=peer, device_id_type=MESH_ID)
            cp.wait_send()
            cp.wait_recv()

    buf = pltpu.HBM(x_thru.shape, x_thru.dtype)
    return pl.pallas_call(
        body, name=name, out_shape=(buf, buf), in_specs=(HBM_SPEC, HBM_SPEC, SEM_SPEC, SEM_SPEC, ANY_SPEC),
        out_specs=(HBM_SPEC, HBM_SPEC), input_output_aliases={0: 0, 1: 1},
        compiler_params=pltpu.CompilerParams(has_side_effects=SPLIT_EFFECT),
    )(x_thru, land_thru, send_sems, recv_sems, after)


def _pair_start(name, blocks):
    _, r, cdim = blocks.shape

    def body(x_ref, land_ref, send_sems, recv_sems, x_thru, land_thru, token):
        del x_thru, land_thru
        x, y, c = _my_pos()
        for q in range(4):
            pltpu.make_async_remote_copy(
                src_ref=x_ref.at[2 * q + 1 - c], dst_ref=land_ref.at[q], send_sem=send_sems.at[q],
                recv_sem=recv_sems.at[q], device_id=(x, y, 1 - c), device_id_type=MESH_ID).start()
        token[...] = jnp.zeros_like(token)

    sems = pltpu.SemaphoreType.DMA((4,))
    land = pltpu.HBM((4, r, cdim), blocks.dtype)
    return pl.pallas_call(
        body, name=name, out_shape=(sems, sems, pltpu.HBM(blocks.shape, blocks.dtype), land, TOKEN_SHAPE),
        in_specs=(HBM_SPEC, HBM_SPEC),
        out_specs=(SEM_SPEC, SEM_SPEC, HBM_SPEC, HBM_SPEC, pl.BlockSpec(memory_space=pltpu.VMEM)),
        input_output_aliases={0: 2, 1: 3}, compiler_params=pltpu.CompilerParams(has_side_effects=SPLIT_EFFECT),
    )(_hbm(blocks), _hbm(lax.empty((4, r, cdim), blocks.dtype)))


def _pair_wait(name, send_sems, recv_sems, x_thru, land_thru, after):
    def body(x_ref, land_ref, send_sems, recv_sems, after_ref, x_dead, got_ref):
        del after_ref, x_dead, got_ref
        x, y, c = _my_pos()
        for q in range(4):
            cp = pltpu.make_async_remote_copy(
                src_ref=x_ref.at[2 * q + 1 - c], dst_ref=land_ref.at[q], send_sem=send_sems.at[q],
                recv_sem=recv_sems.at[q], device_id=(x, y, 1 - c), device_id_type=MESH_ID)
            cp.wait_send()
            cp.wait_recv()

    return pl.pallas_call(
        body, name=name, out_shape=(pltpu.HBM(x_thru.shape, x_thru.dtype), pltpu.HBM(land_thru.shape, land_thru.dtype)),
        in_specs=(HBM_SPEC, HBM_SPEC, SEM_SPEC, SEM_SPEC, ANY_SPEC), out_specs=(HBM_SPEC, HBM_SPEC),
        input_output_aliases={0: 0, 1: 1}, compiler_params=pltpu.CompilerParams(has_side_effects=SPLIT_EFFECT),
    )(x_thru, land_thru, send_sems, recv_sems, after)


def _pair_add(name, blocks, from_sibling, core):
    _, r, cdim = blocks.shape
    tr = _rows_tile(r, cdim)

    def body(c_ref, mine_ref, sib_ref, o_ref):
        del c_ref
        o_ref[...] = (mine_ref[...].astype(F32) + sib_ref[...].astype(F32)).astype(BF16)

    grid_spec = pltpu.PrefetchScalarGridSpec(
        num_scalar_prefetch=1, grid=(4, r // tr),
        in_specs=[pl.BlockSpec((None, tr, cdim), lambda q, i, c_ref: (2 * q + c_ref[0], i, 0)),
                  pl.BlockSpec((None, tr, cdim), lambda q, i, c_ref: (q, i, 0))],
        out_specs=pl.BlockSpec((None, tr, cdim), lambda q, i, c_ref: (q, i, 0)))
    return pl.pallas_call(
        body, name=name, grid_spec=grid_spec, out_shape=jax.ShapeDtypeStruct((4, r, cdim), BF16),
        compiler_params=_cp(("parallel", "parallel")))(jnp.reshape(core, (1,)).astype(jnp.int32), blocks, from_sibling)


def _gather_start(name, shard, land):
    def body(x_ref, land_ref, sib_send, sib_recv, ici_send, ici_recv, x_thru, land_thru, token):
        del x_thru, land_thru
        x, y, c = _my_pos()
        me = 4 * x + 2 * y + c
        pltpu.make_async_remote_copy(
            src_ref=x_ref, dst_ref=land_ref.at[me], send_sem=sib_send.at[0], recv_sem=sib_recv.at[0],
            device_id=(x, y, 1 - c), device_id_type=MESH_ID).start()
        for k, peer in enumerate([(1 - x, y, c), (x, 1 - y, c), (1 - x, 1 - y, c)]):
            pltpu.make_async_remote_copy(
                src_ref=x_ref, dst_ref=land_ref.at[me], send_sem=ici_send.at[k], recv_sem=ici_recv.at[k],
                device_id=peer, device_id_type=MESH_ID).start()
        token[...] = jnp.zeros_like(token)

    sib, ici = pltpu.SemaphoreType.DMA((1,)), pltpu.SemaphoreType.DMA((3,))
    return pl.pallas_call(
        body, name=name,
        out_shape=(sib, sib, ici, ici, pltpu.HBM(shard.shape, shard.dtype), pltpu.HBM(land.shape, land.dtype), TOKEN_SHAPE),
        in_specs=(HBM_SPEC, HBM_SPEC),
        out_specs=(SEM_SPEC, SEM_SPEC, SEM_SPEC, SEM_SPEC, HBM_SPEC, HBM_SPEC, pl.BlockSpec(memory_space=pltpu.VMEM)),
        input_output_aliases={0: 4, 1: 5}, compiler_params=pltpu.CompilerParams(has_side_effects=SPLIT_EFFECT),
    )(_hbm(shard), _hbm(land))


def _gather_wait(name, send_sems, recv_sems, x_thru, land_thru, after, sibling):
    def body(x_ref, land_ref, send_sems, recv_sems, after_ref, x_dead, got_ref):
        del after_ref, x_dead, got_ref
        x, y, c = _my_pos()
        sources = [(x, y, 1 - c)] if sibling else [(1 - x, y, c), (x, 1 - y, c), (1 - x, 1 - y, c)]
        for k, (px, py, pc) in enumerate(sources):
            cp = pltpu.make_async_remote_copy(
                src_ref=x_ref, dst_ref=land_ref.at[4 * px + 2 * py + pc], send_sem=send_sems.at[k],
                recv_sem=recv_sems.at[k], device_id=(px, py, pc), device_id_type=MESH_ID)
            cp.wait_send()
            cp.wait_recv()

    return pl.pallas_call(
        body, name=name, out_shape=(pltpu.HBM(x_thru.shape, x_thru.dtype), pltpu.HBM(land_thru.shape, land_thru.dtype)),
        in_specs=(HBM_SPEC, HBM_SPEC, SEM_SPEC, SEM_SPEC, ANY_SPEC), out_specs=(HBM_SPEC, HBM_SPEC),
        input_output_aliases={0: 0, 1: 1}, compiler_params=pltpu.CompilerParams(has_side_effects=SPLIT_EFFECT),
    )(x_thru, land_thru, send_sems, recv_sems, after)


def _forward_start(name, land):
    def body(land_ref, send_sems, recv_sems, land_thru):
        del land_thru
        x, y, c = _my_pos()
        for j, (px, py) in enumerate([(1 - x, y), (x, 1 - y), (1 - x, 1 - y)]):
            mine = 4 * px + 2 * py + c
            pltpu.make_async_remote_copy(
                src_ref=land_ref.at[mine], dst_ref=land_ref.at[mine], send_sem=send_sems.at[j], recv_sem=recv_sems.at[j],
                device_id=(x, y, 1 - c), device_id_type=MESH_ID).start()

    sems = pltpu.SemaphoreType.DMA((3,))
    return pl.pallas_call(
        body, name=name, out_shape=(sems, sems, pltpu.HBM(land.shape, land.dtype)), in_specs=(HBM_SPEC,),
        out_specs=(SEM_SPEC, SEM_SPEC, HBM_SPEC), input_output_aliases={0: 2},
        compiler_params=pltpu.CompilerParams(has_side_effects=SPLIT_EFFECT),
    )(_hbm(land))


def _forward_wait(name, send_sems, recv_sems, land_thru, after):
    def body(land_ref, send_sems, recv_sems, after_ref, got_ref):
        del after_ref, got_ref
        x, y, c = _my_pos()
        for j, (px, py) in enumerate([(1 - x, y), (x, 1 - y), (1 - x, 1 - y)]):
            mine, theirs = 4 * px + 2 * py + c, 4 * px + 2 * py + (1 - c)
            cp = pltpu.make_async_remote_copy(
                src_ref=land_ref.at[mine], dst_ref=land_ref.at[theirs], send_sem=send_sems.at[j], recv_sem=recv_sems.at[j],
                device_id=(x, y, 1 - c), device_id_type=MESH_ID)
            cp.wait_send()
            cp.wait_recv()

    return pl.pallas_call(
        body, name=name, out_shape=pltpu.HBM(land_thru.shape, land_thru.dtype),
        in_specs=(HBM_SPEC, SEM_SPEC, SEM_SPEC, ANY_SPEC), out_specs=HBM_SPEC, input_output_aliases={0: 0},
        compiler_params=pltpu.CompilerParams(has_side_effects=SPLIT_EFFECT),
    )(land_thru, send_sems, recv_sems, after)


def _gather_forward(name, land):
    def body(x_ref, out_ref, send_sems, recv_sems):
        x, y, c = _my_pos()
        chips = [(1 - x, y), (x, 1 - y), (1 - x, 1 - y)]
        sends = []
        for j, (px, py) in enumerate(chips):
            mine = 4 * px + 2 * py + c
            cp = pltpu.make_async_remote_copy(
                src_ref=x_ref.at[mine], dst_ref=out_ref.at[mine], send_sem=send_sems.at[j], recv_sem=recv_sems.at[j],
                device_id=(x, y, 1 - c), device_id_type=MESH_ID)
            cp.start()
            sends.append(cp)
        for j, (px, py) in enumerate(chips):
            theirs = 4 * px + 2 * py + (1 - c)
            pltpu.make_async_remote_copy(
                src_ref=x_ref.at[theirs], dst_ref=out_ref.at[theirs], send_sem=send_sems.at[j], recv_sem=recv_sems.at[j],
                device_id=(x, y, 1 - c), device_id_type=MESH_ID).wait_recv()
        for cp in sends:
            cp.wait_send()

    return pl.pallas_call(
        body, name=name, out_shape=jax.ShapeDtypeStruct(land.shape, land.dtype), in_specs=[ANY_SPEC], out_specs=ANY_SPEC,
        scratch_shapes=[pltpu.SemaphoreType.DMA((3,)), pltpu.SemaphoreType.DMA((3,))], input_output_aliases={0: 0},
    )(land)


def _pack(arrays):
    parts = []
    for a in arrays:
        flat = a.reshape(-1).astype(F32)
        pad = (-flat.shape[0]) % BLOB_ALIGN
        parts.append(jnp.pad(flat, (0, pad)) if pad else flat)
    return jnp.concatenate(parts).reshape(-1, LANES)


def _unpack(blob, shapes):
    flat = blob.reshape(-1)
    out, off = [], 0
    for shp in shapes:
        n = 1
        for s in shp:
            n *= s
        out.append(flat[off:off + n].reshape(shp))
        off += n + (-n) % BLOB_ALIGN
    return out


def _cast_bf16(name, w, after):
    r, cdim = w.shape
    tr = _rows_tile(r, cdim)

    def body(w_ref, after_ref, o_ref):
        del after_ref
        o_ref[...] = w_ref[...].astype(BF16)

    return pl.pallas_call(
        body, name=name, grid=(r // tr,), out_shape=jax.ShapeDtypeStruct(w.shape, BF16),
        in_specs=[pl.BlockSpec((tr, cdim), lambda i: (i, 0)), ANY_SPEC], out_specs=pl.BlockSpec((tr, cdim), lambda i: (i, 0)),
        compiler_params=_cp(("parallel",)))(w, after)


def _adam_math(w, g, m, v):
    m = ADAM_B1 * m + (1.0 - ADAM_B1) * g
    v = ADAM_B2 * v + (1.0 - ADAM_B2) * (g * g)
    m_hat = m / (1.0 - ADAM_B1 ** ADAM_STEP)
    v_hat = v / (1.0 - ADAM_B2 ** ADAM_STEP)
    delta = -ADAM_LR * (m_hat / (jnp.sqrt(v_hat) + ADAM_EPS) + ADAM_WD * w)
    return delta, m, v


def _adamw(name, w, m, v, *, grad=None, slots=None, own=None, me=None, ca_t=None, dmod=None):
    r, cdim = w.shape
    tr = _rows_tile(r, cdim)
    row = pl.BlockSpec((tr, cdim), lambda i, *_: (i, 0))
    prefetch = []
    if grad is not None:
        srcs, src_specs = [grad], [row]
    elif slots is not None:
        prefetch = [jnp.reshape(me, (1,)).astype(jnp.int32)]
        srcs = [slots, own]
        src_specs = [pl.BlockSpec((slots.shape[0], tr, cdim), lambda i, me_ref: (0, i, 0)),
                     pl.BlockSpec((None, tr, cdim), lambda i, me_ref: (me_ref[0], i, 0))]
    else:
        srcs = [ca_t, dmod]
        src_specs = [pl.BlockSpec((tr, ca_t.shape[1]), lambda i: (i, 0)), pl.BlockSpec(dmod.shape, lambda i: (0, 0))]
    n_src = len(srcs)
    n_pre = len(prefetch)

    def body(*refs):
        pre, refs = refs[:n_pre], refs[n_pre:]
        w_ref, m_ref, v_ref = refs[n_src:n_src + 3]
        g_ref, d_ref, nm_ref, nv_ref = refs[n_src + 3:]
        if grad is not None:
            g = refs[0][...]
        elif slots is not None:
            mine = pre[0][0]
            own_f = refs[1][...].astype(F32)
            g = jnp.where(mine == 0, own_f, refs[0][0].astype(F32))
            for s in range(1, slots.shape[0]):
                g = g + jnp.where(mine == s, own_f, refs[0][s].astype(F32))
        else:
            g = jnp.dot(refs[0][...], refs[1][...], preferred_element_type=F32)
        delta, nm, nv = _adam_math(w_ref[...], g, m_ref[...], v_ref[...])
        g_ref[...] = g
        d_ref[...] = delta
        nm_ref[...] = nm
        nv_ref[...] = nv

    shp = jax.ShapeDtypeStruct(w.shape, F32)
    grid_spec = pltpu.PrefetchScalarGridSpec(
        num_scalar_prefetch=n_pre, grid=(r // tr,), in_specs=[*src_specs, row, row, row], out_specs=(row,) * 4)
    return pl.pallas_call(
        body, name=name, grid_spec=grid_spec, out_shape=(shp,) * 4, compiler_params=_cp(("parallel",)))(*prefetch, *srcs, w, m, v)


def _sum_slots(name, gathered):
    _, r, cdim = gathered.shape
    tr = _rows_tile(r, cdim * N_DEV)

    def body(g_ref, o_ref):
        acc = g_ref[0]
        for s in range(1, N_DEV):
            acc = acc + g_ref[s]
        o_ref[...] = acc

    return pl.pallas_call(
        body, name=name, grid=(r // tr,), out_shape=jax.ShapeDtypeStruct((r, cdim), F32),
        in_specs=[pl.BlockSpec((N_DEV, tr, cdim), lambda i: (0, i, 0))], out_specs=pl.BlockSpec((tr, cdim), lambda i: (i, 0)),
        compiler_params=_cp(("parallel",)))(gathered)


def _col_sums(name, a):
    r, cdim = a.shape

    def body(a_ref, o_ref):
        o_ref[...] = _colsum(a_ref[...])

    return pl.pallas_call(
        body, name=name, out_shape=jax.ShapeDtypeStruct((1, cdim), F32),
        in_specs=[pl.BlockSpec(memory_space=pltpu.VMEM)], out_specs=pl.BlockSpec(memory_space=pltpu.VMEM),
        compiler_params=_cp())(a)


def _mm(name, a, b, *, mode, grid, a_spec, b_spec, acc_shape, outs, out_specs, epilogue, extra=(), extra_specs=(), after=(),
        ids=None, prev=None):
    nk = grid[-1]
    dims = {"nn": ((1,), (0,)), "nt": ((1,), (1,)), "tn": ((0,), (0,))}[mode]
    after = tuple(after) + ((prev,) if prev is not None else ())
    n_extra, n_out, n_after = len(extra), len(outs), len(after)
    n_pre = 0 if ids is None else 1

    def body(*refs):
        refs = refs[n_pre:]
        a_ref, b_ref = refs[0], refs[1]
        extra_refs = refs[2:2 + n_extra]
        out_refs = refs[2 + n_extra + n_after:2 + n_extra + n_after + n_out]
        def part():
            return lax.dot_general(a_ref[...], b_ref[...], (dims, ((), ())), preferred_element_type=F32)

        if nk == 1:
            epilogue(part(), extra_refs, out_refs)
        else:
            acc = refs[-1]
            k = pl.program_id(len(grid) - 1)

            @pl.when(k == 0)
            def _():
                acc[...] = part()

            @pl.when(jnp.logical_and(k > 0, k < nk - 1))
            def _():
                acc[...] += part()

            @pl.when(k == nk - 1)
            def _():
                epilogue(acc[...] + part(), extra_refs, out_refs)

    scratch = [pltpu.VMEM(acc_shape, F32)] if nk > 1 else []
    sem = ("parallel",) * (len(grid) - 1) + ("arbitrary",)
    grid_spec = pltpu.PrefetchScalarGridSpec(
        num_scalar_prefetch=n_pre, grid=grid, in_specs=[a_spec, b_spec, *extra_specs, *([ANY_SPEC] * n_after)],
        out_specs=out_specs, scratch_shapes=scratch)
    aliases = {n_pre + 2 + n_extra + n_after - 1: 0} if prev is not None else {}
    pre = () if ids is None else (ids,)
    return pl.pallas_call(
        body, name=name, grid_spec=grid_spec, out_shape=outs, input_output_aliases=aliases,
        compiler_params=_cp(sem))(*pre, a, b, *extra, *after)


def _store_bf16(acc, extra_refs, out_refs):
    out_refs[0][...] = acc.astype(BF16)


def _store_f32(acc, extra_refs, out_refs):
    out_refs[0][...] = acc


def _residual_epilogue(acc, extra_refs, out_refs):
    x_ref, gate_ref = extra_refs
    out_refs[0][...] = acc
    out_refs[1][...] = x_ref[...] + gate_ref[...] * acc


def kernel(x, c, w_ada, b_ada, g_mix, w_in, conv_w, conv_b, conv_ln_g, conv_ln_b, sgu_ln_g, sgu_ln_b, sgu_w, sgu_b, out_g_conv, out_g_sgu, w_out, g_ffn, w_up, ffn_conv_w, ffn_conv_b, w_down, g_final, loss_target, m_w_ada, m_b_ada, m_g_mix, m_w_in, m_conv_w, m_conv_b, m_conv_ln_g, m_conv_ln_b, m_sgu_ln_g, m_sgu_ln_b, m_sgu_w, m_sgu_b, m_out_g_conv, m_out_g_sgu, m_w_out, m_g_ffn, m_w_up, m_ffn_conv_w, m_ffn_conv_b, m_w_down, m_g_final, v_w_ada, v_b_ada, v_g_mix, v_w_in, v_conv_w, v_conv_b, v_conv_ln_g, v_conv_ln_b, v_sgu_ln_g, v_sgu_ln_b, v_sgu_w, v_sgu_b, v_out_g_conv, v_out_g_sgu, v_w_out, v_g_ffn, v_w_up, v_ffn_conv_w, v_ffn_conv_b, v_w_down, v_g_final):
    bsz, seq, d = x.shape
    t = bsz * seq
    n_batch = bsz * N_DEV
    ada_sh = w_ada.shape[2]
    n_mod = ada_sh * N_DEV // d
    win_sh = w_in.shape[2]
    kc = conv_w.shape[1]
    dc_sh = conv_w.shape[2]
    dc = dc_sh * N_DEV
    heads, ch = sgu_w.shape[1], sgu_w.shape[2]
    hd = dc // heads
    wout_sh = w_out.shape[1]
    cup = w_up.shape[2]
    kf = ffn_conv_w.shape[1]
    cw_sh = ffn_conv_w.shape[2]
    dff = cw_sh * N_DEV
    n_fs = dff // cup
    assert win_sh * N_DEV == 4 * dc and hd == LANES and kc // 2 < HALO and kf == 3 and 2 * n_fs == N_DEV
    assert w_down.shape[1] * 2 == cup and seq % ch == 0

    xi, yi, ci = _my_pos()
    me = 4 * xi + 2 * yi + ci

    tb = min(256, seq)
    tps = seq // tb
    nt = t // tb
    tbd = min(128, seq)
    tpd = seq // tbd
    ntd = t // tbd
    tm = min(512, seq)
    nm = t // tm

    xf = x.reshape(t, d)
    tgt = loss_target.reshape(t, d)

    small_shapes = [(bsz, d), (kc, dc_sh), (kf, cw_sh)]
    blob0 = _allgather("ag_small_in", _pack([c, conv_w[0], ffn_conv_w[0]]), True)
    per_dev = [_unpack(blob0[s], small_shapes) for s in range(N_DEV)]
    c_all = jnp.concatenate([p[0] for p in per_dev], axis=0)
    conv_w_full = jnp.concatenate([p[1] for p in per_dev], axis=1)
    ffn_w_full = jnp.concatenate([p[2] for p in per_dev], axis=1)
    ffn_w_s = ffn_w_full.reshape(kf, n_fs, cup).transpose(1, 0, 2)
    ffn_b_s = ffn_conv_b.reshape(n_fs, 1, cup)

    b_ada_sh = lax.dynamic_slice(b_ada, (0, me * ada_sh), (1, ada_sh))
    tn_ada = _blk(ada_sh, 512)

    def ada_body(c_ref, w_ref, b_ref, mod_ref, ca_ref):
        cc = c_ref[...]
        ca = (cc * _sigmoid(cc)).astype(BF16)
        ca_ref[...] = ca
        mod_ref[...] = jnp.dot(ca, w_ref[...].astype(BF16), preferred_element_type=F32) + b_ref[...]

    mod_sh, c_act = pl.pallas_call(
        ada_body, name="ada_fwd", grid=(ada_sh // tn_ada,),
        out_shape=(jax.ShapeDtypeStruct((n_batch, ada_sh), F32), jax.ShapeDtypeStruct((n_batch, d), BF16)),
        in_specs=[pl.BlockSpec((n_batch, d), lambda j: (0, 0)), pl.BlockSpec((d, tn_ada), lambda j: (0, j)),
                  pl.BlockSpec((1, tn_ada), lambda j: (0, j))],
        out_specs=(pl.BlockSpec((n_batch, tn_ada), lambda j: (0, j)), pl.BlockSpec((n_batch, d), lambda j: (0, 0))),
        compiler_params=_cp(("arbitrary",)))(c_all, w_ada[0], b_ada_sh)
    mod_all = _allgather("ag_mod", mod_sh, True)
    mod_me = lax.dynamic_slice(mod_all, (0, me * bsz, 0), (N_DEV, bsz, ada_sh))
    mod_me = mod_me.transpose(1, 0, 2).reshape(bsz, n_mod, 1, d)
    shift_m, scale_m, gate_m = mod_me[:, 0], mod_me[:, 1], mod_me[:, 2]
    shift_f, scale_f, gate_f = mod_me[:, 3], mod_me[:, 4], mod_me[:, 5]

    def gather_begin(name, w2d):
        shard = _cast_bf16("cast_" + name, w2d, mod_all)
        land = lax.dynamic_update_slice(lax.empty((N_DEV,) + shard.shape, BF16), shard[None], (me, 0, 0))
        return _gather_start("ag1_" + name, shard, land)

    def gather_end(name, started, after):
        sib_send, sib_recv, ici_send, ici_recv, x_thru, land_thru, _ = started
        x_thru, land = _gather_wait("ag1s_" + name, sib_send, sib_recv, x_thru, land_thru, after, True)
        _, land = _gather_wait("ag1w_" + name, ici_send, ici_recv, x_thru, land, after, False)
        return _gather_forward("ag2_" + name, land)

    sib = me + 1 - 2 * ci
    flips = [(1 - xi, yi), (xi, 1 - yi), (1 - xi, 1 - yi)]
    ids_near = jnp.stack([me, sib]).astype(jnp.int32)
    ids_ici = jnp.stack([4 * px + 2 * py + ci for px, py in flips]).astype(jnp.int32)
    ids_fwd = jnp.stack([4 * px + 2 * py + 1 - ci for px, py in flips]).astype(jnp.int32)

    def staged_product(name, started, after, part):
        sib_send, sib_recv, ici_send, ici_recv, x_thru, land_thru, _ = started
        x_thru, land = _gather_wait("ag1s_" + name, sib_send, sib_recv, x_thru, land_thru, after, True)
        out = part("near", land, ids_near, None)
        _, land = _gather_wait("ag1w_" + name, ici_send, ici_recv, x_thru, land, out, False)
        fwd_send, fwd_recv, land = _forward_start("ag2s_" + name, land)
        out = part("ici", land, ids_ici, out)
        land = _forward_wait("ag2w_" + name, fwd_send, fwd_recv, land, out)
        out = part("fwd", land, ids_fwd, out)
        return land, out

    w_up_t, m_w_up_t, v_w_up_t = (jnp.swapaxes(a[0], 0, 1) for a in (w_up, m_w_up, v_w_up))
    ag_in = gather_begin("w_in", w_in[0])
    ag_out = gather_begin("w_out", w_out[0])
    ag_up = gather_begin("w_up", w_up_t)
    ag_down = gather_begin("w_down", w_down[0])
    started = ag_in[6][0, 0] + ag_out[6][0, 0] + ag_up[6][0, 0] + ag_down[6][0, 0]

    row_d = pl.BlockSpec((tbd, d), lambda i: (i, 0))
    vec_d = pl.BlockSpec((1, d), lambda i: (0, 0))
    seq_d = pl.BlockSpec((None, 1, d), lambda i: (i // tpd, 0, 0))

    def modulate(name, xin, g, shift, scale):
        def body(x_ref, g_ref, sh_ref, sc_ref, h_ref):
            xx = x_ref[...]
            yy = xx * lax.rsqrt(_mean(xx * xx) + EPS)
            h_ref[...] = ((yy * g_ref[...]) * (1.0 + sc_ref[...]) + sh_ref[...]).astype(BF16)

        return pl.pallas_call(
            body, name=name, grid=(ntd,), out_shape=jax.ShapeDtypeStruct((t, d), BF16),
            in_specs=[row_d, vec_d, seq_d, seq_d], out_specs=row_d, compiler_params=_cp(("parallel",)))(xin, g, shift, scale)

    h1 = modulate("mod1_fwd", xf, g_mix + started, shift_m, scale_m)

    tk_d = _blk(d, 1024)
    def proj_part(tag, land, ids, prev):
        return _mm(
            "mm_proj_" + tag, h1, land, mode="nn", grid=(nm, ids.shape[0], 1),
            a_spec=pl.BlockSpec((tm, d), lambda i, j, k, s: (i, 0)),
            b_spec=pl.BlockSpec((None, d, win_sh), lambda i, j, k, s: (s[j], 0, 0)),
            acc_shape=(tm, win_sh), outs=(jax.ShapeDtypeStruct((t, N_DEV * win_sh), BF16),),
            out_specs=(pl.BlockSpec((tm, win_sh), lambda i, j, k, s: (i, s[j])),), epilogue=_store_bf16,
            ids=ids, prev=prev)[0]

    w_in_g, proj = staged_product("w_in", ag_in, h1, proj_part)

    hb = tb // HALO
    n_hb = t // HALO

    def main_col(col):
        return pl.BlockSpec((tb, dc), lambda i, col=col: (i, col))

    def chunk_col(col):
        return pl.BlockSpec((ch, dc), lambda i, col=col: (i, col))

    def prev_col(col):
        return pl.BlockSpec((HALO, dc), lambda i, col=col: (jnp.maximum(i * hb - 1, 0), col))

    def next_col(col):
        return pl.BlockSpec((HALO, dc), lambda i, col=col: (jnp.minimum((i + 1) * hb, n_hb - 1), col))

    vec_c = pl.BlockSpec((1, dc), lambda i: (0, 0))
    row_c = pl.BlockSpec((tb, dc), lambda i: (i, 0))
    convw_spec = pl.BlockSpec((kc, dc), lambda i: (0, 0))
    sguw_spec = pl.BlockSpec((heads, ch, ch), lambda i: (0, 0, 0))
    bias_spec = pl.BlockSpec((ch, dc), lambda i: (0, 0))
    n_chunk = tb // ch

    sgu_w_bf = sgu_w[0].astype(BF16)
    sgu_wt_bf = jnp.swapaxes(sgu_w[0], 1, 2).astype(BF16)
    sgu_bias = jnp.repeat(sgu_b[0].T, hd, axis=1)

    CL = 2 * LANES
    conv_pieces = [pl.ds(l0, CL) for l0 in range(0, dc, CL)]
    stage_rows = tb + 2 * HALO
    sh_shape = (SUBLANES, stage_rows + SUBLANES, CL)
    padc = kc // 2
    assert dc % CL == 0 and tb % HALO == 0

    def shift_copies(sh):
        sh[0, pl.ds(stage_rows, SUBLANES), :] = jnp.zeros((SUBLANES, CL), F32)
        for r in range(1, SUBLANES):
            sh[r, pl.ds(0, stage_rows), :] = sh[0, pl.ds(r, stage_rows), :]

    def tap_piece(sh, off, row0):
        r = off % SUBLANES
        return sh[r, pl.ds(off - r + row0, SUBLANES), :]

    def stage_glu(sh, i, cols, pv, pg, pvp, pgp, pvn, pgn):
        first = (i % tps) == 0
        last = (i % tps) == tps - 1

        def glu(v_ref, g_ref, rows):
            return v_ref[rows, cols].astype(F32) * _sigmoid(g_ref[rows, cols].astype(F32))

        whole = pl.ds(0, HALO)
        sh[0, pl.ds(0, HALO), :] = jnp.where(first, 0.0, glu(pvp, pgp, whole))
        for rr in range(tb // HALO):
            sh[0, pl.ds(HALO + rr * HALO, HALO), :] = glu(pv, pg, pl.ds(rr * HALO, HALO))
        sh[0, pl.ds(HALO + tb, HALO), :] = jnp.where(last, 0.0, glu(pvn, pgn, whole))
        shift_copies(sh)

    def stage_taps(wb, cw, cols):
        for k in range(kc):
            wb[k] = jnp.broadcast_to(cw[pl.ds(k, 1), cols], (SUBLANES, CL))

    def layer_norm_stats(z):
        mu = _mean(z)
        zc = z - mu
        rstd = lax.rsqrt(_mean(zc * zc) + EPS)
        return zc * rstd, rstd

    def sgu_mix(vs_ref, w_ref, vnb, bias_ref, n_chunks):
        for cc in range(n_chunks):
            for h in range(heads):
                blk = jnp.dot(w_ref[h], vnb[cc * ch:(cc + 1) * ch, h * hd:(h + 1) * hd], preferred_element_type=F32)
                vs_ref[pl.ds(cc * ch, ch), pl.ds(h * hd, hd)] = blk + bias_ref[:, pl.ds(h * hd, hd)]

    def mix_fwd_body(pv, pg, pu, pw, pvp, pgp, pvn, pgn, cw, cb, lg, lb, og, slg, slb, sw, sbias, sog, y_ref, a1_ref, sh_e, wb, vs_ref):
        i = pl.program_id(0)
        for cols in conv_pieces:
            stage_glu(sh_e, i, cols, pv, pg, pvp, pgp, pvn, pgn)
            stage_taps(wb, cw, cols)
            bias = jnp.broadcast_to(cb[:, cols], (SUBLANES, CL))
            for row0 in range(0, tb, SUBLANES):
                acc = bias
                for k in range(kc):
                    acc = acc + wb[k] * tap_piece(sh_e, HALO - padc + k, row0)
                a1_ref[pl.ds(row0, SUBLANES), cols] = acc
        xh, _ = layer_norm_stats(a1_ref[...])
        a2 = xh * lg[...] + lb[...]
        a3 = a2 * _sigmoid(a2)
        ya = a3 * lax.rsqrt(_mean(a3 * a3) + EPS) * og[...]
        y_ref[:, pl.ds(0, dc)] = ya.astype(BF16)

        u = _gelu(pu[...].astype(F32))
        vv = _gelu(pw[...].astype(F32))
        xhv, _ = layer_norm_stats(vv)
        vn = xhv * slg[...] + slb[...]
        sgu_mix(vs_ref, sw, vn.astype(BF16), sbias, n_chunk)
        bg = u * vs_ref[...]
        yb = bg * lax.rsqrt(_mean(bg * bg) + EPS) * sog[...]
        y_ref[:, pl.ds(dc, dc)] = yb.astype(BF16)

    y, a1 = pl.pallas_call(
        mix_fwd_body, name="mix_fwd", grid=(nt,),
        out_shape=(jax.ShapeDtypeStruct((t, 2 * dc), BF16), jax.ShapeDtypeStruct((t, dc), F32)),
        in_specs=[main_col(0), main_col(1), main_col(2), main_col(3), prev_col(0), prev_col(1), next_col(0), next_col(1),
                  convw_spec, vec_c, vec_c, vec_c, vec_c, vec_c, vec_c, sguw_spec, bias_spec, vec_c],
        out_specs=(pl.BlockSpec((tb, 2 * dc), lambda i: (i, 0)), row_c),
        scratch_shapes=[pltpu.VMEM(sh_shape, F32), pltpu.VMEM((kc, SUBLANES, CL), F32), pltpu.VMEM((tb, dc), F32)],
        compiler_params=_cp(("parallel",)),
    )(proj, proj, proj, proj, proj, proj, proj, proj, conv_w_full, conv_b, conv_ln_g, conv_ln_b, out_g_conv,
      sgu_ln_g, sgu_ln_b, sgu_w_bf, sgu_bias, out_g_sgu)

    tn_d = _blk(d, 1024)
    gate_spec3 = pl.BlockSpec((None, 1, tn_d), lambda i, j, k: (i * tm // seq, 0, j))
    res_spec3 = pl.BlockSpec((tm, tn_d), lambda i, j, k: (i, j))
    td_shape = jax.ShapeDtypeStruct((t, d), F32)

    w_out_full = gather_end("w_out", ag_out, y).reshape(d, d)
    o_mix, x1 = _mm(
        "mm_out", y, w_out_full, mode="nn", grid=(nm, d // tn_d, 1),
        a_spec=pl.BlockSpec((tm, d), lambda i, j, k: (i, 0)), b_spec=pl.BlockSpec((d, tn_d), lambda i, j, k: (0, j)),
        acc_shape=(tm, tn_d), outs=(td_shape, td_shape), out_specs=(res_spec3, res_spec3),
        epilogue=_residual_epilogue, extra=(xf, gate_m), extra_specs=(res_spec3, gate_spec3))

    h2 = modulate("mod2_fwd", x1, g_ffn, shift_f, scale_f)

    tm_up = min(1024, seq)
    nk_d = d // tk_d
    def up_part(tag, land, ids, prev):
        return _mm(
            "mm_up_" + tag, h2, land, mode="nt", grid=(t // tm_up, ids.shape[0], nk_d),
            a_spec=pl.BlockSpec((tm_up, tk_d), lambda i, j, k, s: (i, k)),
            b_spec=pl.BlockSpec((None, cup, tk_d), lambda i, j, k, s: (s[j], 0, k)),
            acc_shape=(tm_up, cup), outs=(jax.ShapeDtypeStruct((N_DEV, t, cup), BF16),),
            out_specs=(pl.BlockSpec((None, tm_up, cup), lambda i, j, k, s: (s[j], i, 0)),), epilogue=_store_bf16,
            ids=ids, prev=prev)[0]

    w_up_g, up = staged_product("w_up", ag_up, h2, up_part)
    up4 = up.reshape(2, n_fs, t, cup)

    def ffn_halo(which, lo):
        if lo:
            return pl.BlockSpec((None, None, HALO, cup), lambda j, i: (which, j, jnp.maximum(i * hb - 1, 0), 0))
        return pl.BlockSpec((None, None, HALO, cup), lambda j, i: (which, j, jnp.minimum((i + 1) * hb, n_hb - 1), 0))

    pair_spec = pl.BlockSpec((2, None, tb, cup), lambda j, i: (0, j, i, 0))
    fw_spec = pl.BlockSpec((None, kf, cup), lambda j, i: (j, 0, 0))
    fb_spec = pl.BlockSpec((None, 1, cup), lambda j, i: (j, 0, 0))
    act_spec = pl.BlockSpec((None, tb, cup), lambda j, i: (j, i, 0))

    RC = HALO
    LC = 2 * LANES
    pieces = [(l0, min(LC, cup - l0)) for l0 in range(0, cup, LC)]
    n_rc = tb // RC
    pad = SUBLANES

    def fill_gate_ext(ext, i, cols, gate_ref, prev_ref, next_ref):
        first = (i % tps) == 0
        last = (i % tps) == tps - 1
        lw = cols.size
        ext[pl.ds(0, pad), cols] = jnp.zeros((pad, lw), F32)
        ext[pl.ds(pad, HALO), cols] = jnp.where(first, 0.0, prev_ref[:, cols].astype(F32))
        for rr in range(n_rc):
            ext[pl.ds(pad + HALO + rr * RC, RC), cols] = gate_ref[0, pl.ds(rr * RC, RC), cols].astype(F32)
        ext[pl.ds(pad + HALO + tb, HALO), cols] = jnp.where(last, 0.0, next_ref[:, cols].astype(F32))
        ext[pl.ds(pad + 2 * HALO + tb, pad), cols] = jnp.zeros((pad, lw), F32)

    def tap_rows(fw, fb, cols):
        lw = cols.size
        return [jnp.broadcast_to(fw[pl.ds(k, 1), cols], (RC, lw)) for k in range(kf)] + [jnp.broadcast_to(fb[:, cols], (RC, lw))]

    def ffn_fwd_body(pair, gp, gn, fw, fb, act_ref, ext):
        i = pl.program_id(1)
        for l0, lw in pieces:
            cols = pl.ds(l0, lw)
            fill_gate_ext(ext, i, cols, pair, gp, gn)
            w0, w1, w2, bb = tap_rows(fw, fb, cols)
            for rr in range(n_rc):
                e0 = pad + HALO + rr * RC
                gte = bb + w0 * ext[pl.ds(e0 - 1, RC), cols]
                gte = gte + w1 * ext[pl.ds(e0, RC), cols]
                gte = gte + w2 * ext[pl.ds(e0 + 1, RC), cols]
                val = pair[1, pl.ds(rr * RC, RC), cols].astype(F32)
                act_ref[pl.ds(rr * RC, RC), cols] = (gte * _sigmoid(gte) * val).astype(BF16)

    ext_rows = tb + 2 * HALO + 2 * pad
    act = pl.pallas_call(
        ffn_fwd_body, name="ffn_fwd", grid=(n_fs, nt), out_shape=jax.ShapeDtypeStruct((n_fs, t, cup), BF16),
        in_specs=[pair_spec, ffn_halo(0, True), ffn_halo(0, False), fw_spec, fb_spec], out_specs=act_spec,
        scratch_shapes=[pltpu.VMEM((ext_rows, cup), F32)], compiler_params=_cp(("parallel", "parallel")),
    )(up4, up4, up4, ffn_w_s, ffn_b_s)

    w_down_s = gather_end("w_down", ag_down, act).reshape(n_fs, cup, d)
    gate_spec_f = pl.BlockSpec((None, 1, tn_d), lambda i, j, k: (i * tm_up // seq, 0, j))
    res_spec_f = pl.BlockSpec((tm_up, tn_d), lambda i, j, k: (i, j))
    dn, x2 = _mm(
        "mm_down", act, w_down_s, mode="nn", grid=(t // tm_up, d // tn_d, n_fs),
        a_spec=pl.BlockSpec((None, tm_up, cup), lambda i, j, k: (k, i, 0)),
        b_spec=pl.BlockSpec((None, cup, tn_d), lambda i, j, k: (k, 0, j)),
        acc_shape=(tm_up, tn_d), outs=(td_shape, td_shape), out_specs=(res_spec_f, res_spec_f),
        epilogue=_residual_epilogue, extra=(x1, gate_f), extra_specs=(res_spec_f, gate_spec_f))

    acc_d = pl.BlockSpec((1, d), lambda i: (0, 0))
    seq_acc = pl.BlockSpec((None, 1, d), lambda i: (i // tpd, 0, 0))

    def head_body(x_ref, t_ref, g_ref, dn_ref, gate_ref, dx_ref, ddn_ref, loss_ref, dg_ref, dgate_ref):
        i = pl.program_id(0)
        xx = x_ref[...]
        rr = lax.rsqrt(_mean(xx * xx) + EPS)
        xn = xx * rr
        err = xn * g_ref[...] - t_ref[...]
        dyf = err * (1.0 / d)
        dxn = dyf * g_ref[...]
        dx = rr * (dxn - xn * _mean(dxn * xn))
        dx_ref[...] = dx
        ddn_ref[...] = (gate_ref[...] * dx).astype(BF16)
        part = 0.5 * jnp.sum(_mean(err * err), axis=0, keepdims=True)

        @pl.when(i == 0)
        def _():
            loss_ref[...] = jnp.zeros_like(loss_ref)
            dg_ref[...] = jnp.zeros_like(dg_ref)

        @pl.when(i % tpd == 0)
        def _():
            dgate_ref[...] = jnp.zeros_like(dgate_ref)

        loss_ref[...] += jnp.broadcast_to(part, loss_ref.shape)
        dg_ref[...] += _colsum(dyf * xn)
        dgate_ref[...] += _colsum(dx * dn_ref[...])

    seq_shape = jax.ShapeDtypeStruct((bsz, 1, d), F32)
    vec_shape = jax.ShapeDtypeStruct((1, d), F32)
    dx2, ddn, loss_part, dg_final, dgate_f = pl.pallas_call(
        head_body, name="loss_head", grid=(ntd,),
        out_shape=(td_shape, jax.ShapeDtypeStruct((t, d), BF16), jax.ShapeDtypeStruct((1, LANES), F32), vec_shape, seq_shape),
        in_specs=[row_d, row_d, vec_d, row_d, seq_d],
        out_specs=(row_d, row_d, pl.BlockSpec((1, LANES), lambda i: (0, 0)), acc_d, seq_acc),
        compiler_params=_cp(("arbitrary",)))(x2, tgt, g_final.reshape(1, d), dn, gate_f)

    tk_t = min(1024, t)
    nk_t = t // tk_t
    tn_gw = _blk(d, 1024)
    gw_down = _mm(
        "mm_gw_down", act, ddn, mode="tn", grid=(n_fs, d // tn_gw, nk_t),
        a_spec=pl.BlockSpec((None, tk_t, cup), lambda j, n, k: (j, k, 0)),
        b_spec=pl.BlockSpec((tk_t, tn_gw), lambda j, n, k: (k, n)),
        acc_shape=(cup, tn_gw), outs=(jax.ShapeDtypeStruct((n_fs, cup, d), BF16),),
        out_specs=(pl.BlockSpec((None, cup, tn_gw), lambda j, n, k: (j, 0, n)),), epilogue=_store_bf16)[0]

    def as_blocks(gw):
        return gw.reshape(N_DEV, gw.size // (N_DEV * gw.shape[-1]), gw.shape[-1])

    def exchange_begin(name, gw):
        return _alltoall_start("a2a_" + name, as_blocks(gw))

    def pair_begin(name, gw):
        return _pair_start("p2a_" + name, as_blocks(gw))

    def pair_end(name, started, after):
        send_sems, recv_sems, x_thru, land_thru, _ = started
        mine, theirs = _pair_wait("p2w_" + name, send_sems, recv_sems, x_thru, land_thru, after)
        summed_pairs = _pair_add("pair_add_" + name, mine, theirs, ci)
        plane = _alltoall_start("a2a_" + name, summed_pairs, plane=True)
        return plane, plane[4][0, 0]

    p_down = pair_begin("w_down", gw_down)
    dact = _mm(
        "mm_dact", ddn, w_down_s, mode="nt", grid=(t // tm_up, n_fs, nk_d),
        a_spec=pl.BlockSpec((tm_up, tk_d), lambda i, j, k: (i, k)),
        b_spec=pl.BlockSpec((None, cup, tk_d), lambda i, j, k: (j, 0, k)),
        acc_shape=(tm_up, cup), outs=(jax.ShapeDtypeStruct((n_fs, t, cup), BF16),),
        out_specs=(pl.BlockSpec((None, tm_up, cup), lambda i, j, k: (j, i, 0)),), epilogue=_store_bf16, after=(p_down[4],))[0]
    x_down, down_started = pair_end("w_down", p_down, dact)


    def ffn_bwd_body(pair, gp, gn, vp, vn, da, dap, dan, fw, fb, dup_ref, dw_ref, db_ref, ext, dg_e):
        i = pl.program_id(1)
        first = (i % tps) == 0
        last = (i % tps) == tps - 1

        @pl.when(i == 0)
        def _():
            dw_ref[...] = jnp.zeros_like(dw_ref)
            db_ref[...] = jnp.zeros_like(db_ref)

        for l0, lw in pieces:
            cols = pl.ds(l0, lw)
            fill_gate_ext(ext, i, cols, pair, gp, gn)
            w0, w1, w2, bb = tap_rows(fw, fb, cols)
            sums = [jnp.zeros((RC, lw), F32) for _ in range(kf + 1)]
            for cc in range(n_rc + 2):
                e0 = pad + cc * RC
                taps = [ext[pl.ds(e0 - 1, RC), cols], ext[pl.ds(e0, RC), cols], ext[pl.ds(e0 + 1, RC), cols]]
                gte = bb + w0 * taps[0]
                gte = gte + w1 * taps[1]
                gte = gte + w2 * taps[2]
                sg = _sigmoid(gte)
                if cc == 0:
                    val, dact_v = vp[:, cols], dap[:, cols]
                elif cc == n_rc + 1:
                    val, dact_v = vn[:, cols], dan[:, cols]
                else:
                    rows = pl.ds((cc - 1) * RC, RC)
                    val, dact_v = pair[1, rows, cols], da[rows, cols]
                val, dact_v = val.astype(F32), dact_v.astype(F32)
                dgte = dact_v * val * (sg * (1.0 + gte * (1.0 - sg)))
                if cc == 0:
                    dgte = jnp.where(first, 0.0, dgte)
                elif cc == n_rc + 1:
                    dgte = jnp.where(last, 0.0, dgte)
                else:
                    dup_ref[1, rows, cols] = (dact_v * (gte * sg)).astype(BF16)
                    for k in range(kf):
                        sums[k] = sums[k] + dgte * taps[k]
                    sums[kf] = sums[kf] + dgte
                dg_e[pl.ds(cc * RC, RC), cols] = dgte
            for k in range(kf):
                dw_ref[pl.ds(k, 1), cols] += _colsum(sums[k])
            db_ref[:, cols] += _colsum(sums[kf])
            for rr in range(n_rc):
                e0 = HALO + rr * RC
                dgate = w0 * dg_e[pl.ds(e0 + 1, RC), cols] + w1 * dg_e[pl.ds(e0, RC), cols] + w2 * dg_e[pl.ds(e0 - 1, RC), cols]
                dup_ref[0, pl.ds(rr * RC, RC), cols] = dgate.astype(BF16)

    def act_halo(lo):
        if lo:
            return pl.BlockSpec((None, HALO, cup), lambda j, i: (j, jnp.maximum(i * hb - 1, 0), 0))
        return pl.BlockSpec((None, HALO, cup), lambda j, i: (j, jnp.minimum((i + 1) * hb, n_hb - 1), 0))

    dup4, g_ffn_w_s, g_ffn_b_s = pl.pallas_call(
        ffn_bwd_body, name="ffn_bwd", grid=(n_fs, nt),
        out_shape=(jax.ShapeDtypeStruct((2, n_fs, t, cup), BF16), jax.ShapeDtypeStruct((n_fs, kf, cup), F32),
                   jax.ShapeDtypeStruct((n_fs, 1, cup), F32)),
        in_specs=[pair_spec, ffn_halo(0, True), ffn_halo(0, False), ffn_halo(1, True), ffn_halo(1, False),
                  act_spec, act_halo(True), act_halo(False), fw_spec, fb_spec],
        out_specs=(pair_spec, fw_spec, fb_spec),
        scratch_shapes=[pltpu.VMEM((ext_rows, cup), F32), pltpu.VMEM((tb + 2 * HALO, cup), F32)],
        compiler_params=_cp(("parallel", "arbitrary")),
    )(up4, up4, up4, up4, up4, dact, dact, dact, ffn_w_s, ffn_b_s + down_started)
    dup = dup4.reshape(N_DEV, t, cup)

    tm_w = _blk(d, 1024)
    gw_up = _mm(
        "mm_gw_up", dup, h2, mode="tn", grid=(N_DEV, d // tn_gw, nk_t),
        a_spec=pl.BlockSpec((None, tk_t, cup), lambda j, n, k: (j, k, 0)),
        b_spec=pl.BlockSpec((tk_t, tn_gw), lambda j, n, k: (k, n)),
        acc_shape=(cup, tn_gw), outs=(jax.ShapeDtypeStruct((N_DEV, cup, d), BF16),),
        out_specs=(pl.BlockSpec((None, cup, tn_gw), lambda j, n, k: (j, 0, n)),), epilogue=_store_bf16)[0]

    p_up = pair_begin("w_up", gw_up)
    dh2 = _mm(
        "mm_dh2", dup, w_up_g, mode="nn", grid=(t // tm_up, d // tn_d, N_DEV),
        a_spec=pl.BlockSpec((None, tm_up, cup), lambda i, j, k: (k, i, 0)),
        b_spec=pl.BlockSpec((None, cup, tn_d), lambda i, j, k: (k, 0, j)),
        acc_shape=(tm_up, tn_d), outs=(td_shape,), out_specs=(pl.BlockSpec((tm_up, tn_d), lambda i, j, k: (i, j)),),
        epilogue=_store_f32, after=(p_up[4],))[0]
    x_up, up_started = pair_end("w_up", p_up, dh2)

    def modulate_bwd(name, xin, dh, dres, g, scale, gate=None, branch=None):
        gated = gate is not None

        def body(*refs):
            x_ref, dh_ref, dres_ref, g_ref, sc_ref = refs[:5]
            rest = refs[5:]
            if gated:
                gate_ref, br_ref, dx_ref, dsh_ref, dsc_ref, dg_ref, do_ref, dgate_ref = rest
            else:
                dx_ref, dsh_ref, dsc_ref, dg_ref = rest
            i = pl.program_id(0)
            xx = x_ref[...]
            rr = lax.rsqrt(_mean(xx * xx) + EPS)
            xn = xx * rr
            s1 = 1.0 + sc_ref[...]
            dhh = dh_ref[...]
            dxn = dhh * g_ref[...] * s1
            dx = dres_ref[...] + rr * (dxn - xn * _mean(dxn * xn))
            dx_ref[...] = dx

            @pl.when(i == 0)
            def _():
                dg_ref[...] = jnp.zeros_like(dg_ref)

            @pl.when(i % tpd == 0)
            def _():
                dsh_ref[...] = jnp.zeros_like(dsh_ref)
                dsc_ref[...] = jnp.zeros_like(dsc_ref)
                if gated:
                    dgate_ref[...] = jnp.zeros_like(dgate_ref)

            dsh_ref[...] += _colsum(dhh)
            dsc_ref[...] += _colsum(dhh * (xn * g_ref[...]))
            dg_ref[...] += _colsum(dhh * s1 * xn)
            if gated:
                do_ref[...] = (gate_ref[...] * dx).astype(BF16)
                dgate_ref[...] += _colsum(dx * br_ref[...])

        ins = [xin, dh, dres, g, scale]
        in_specs = [row_d, row_d, row_d, vec_d, seq_d]
        outs = [td_shape, seq_shape, seq_shape, vec_shape]
        out_specs = [row_d, seq_acc, seq_acc, acc_d]
        if gated:
            ins += [gate, branch]
            in_specs += [seq_d, row_d]
            outs += [jax.ShapeDtypeStruct((t, d), BF16), seq_shape]
            out_specs += [row_d, seq_acc]
        return pl.pallas_call(
            body, name=name, grid=(ntd,), out_shape=tuple(outs), in_specs=in_specs, out_specs=tuple(out_specs),
            compiler_params=_cp(("arbitrary",)))(*ins)

    dx1, dshift_f, dscale_f, dg_ffn, d_o, dgate_m = modulate_bwd(
        "mod2_bwd", x1, dh2, dx2, g_ffn + up_started, scale_f, gate_m, o_mix)

    tn_w = _blk(d, 2048)
    gw_out = _mm(
        "mm_gw_out", y, d_o, mode="tn", grid=(d // tm_w, d // tn_w, nk_t),
        a_spec=pl.BlockSpec((tk_t, tm_w), lambda i, j, k: (k, i)), b_spec=pl.BlockSpec((tk_t, tn_w), lambda i, j, k: (k, j)),
        acc_shape=(tm_w, tn_w), outs=(jax.ShapeDtypeStruct((d, d), BF16),),
        out_specs=(pl.BlockSpec((tm_w, tn_w), lambda i, j, k: (i, j)),), epilogue=_store_bf16)[0]

    x_out = exchange_begin("w_out", gw_out)
    dy = _mm(
        "mm_dy", d_o, w_out_full, mode="nt", grid=(nm, d // tn_d, 1),
        a_spec=pl.BlockSpec((tm, d), lambda i, j, k: (i, 0)), b_spec=pl.BlockSpec((tn_d, d), lambda i, j, k: (j, 0)),
        acc_shape=(tm, tn_d), outs=(jax.ShapeDtypeStruct((t, d), BF16),),
        out_specs=(pl.BlockSpec((tm, tn_d), lambda i, j, k: (i, j)),), epilogue=_store_bf16, after=(x_out[4],))[0]

    def conv_norm_bwd_body(dy_ref, a1_ref, lg, lb, og, da1_ref, dog_ref, dlg_ref, dlb_ref):
        i = pl.program_id(0)
        xh, rstd = layer_norm_stats(a1_ref[...])
        a2 = xh * lg[...] + lb[...]
        sg = _sigmoid(a2)
        a3 = a2 * sg
        r3 = lax.rsqrt(_mean(a3 * a3) + EPS)
        n3 = a3 * r3
        dya = dy_ref[...].astype(F32)
        dn3 = dya * og[...]
        da3 = r3 * (dn3 - n3 * _mean(dn3 * n3))
        da2 = da3 * (sg * (1.0 + a2 * (1.0 - sg)))
        dxh = da2 * lg[...]
        da1_ref[...] = rstd * (dxh - _mean(dxh) - xh * _mean(dxh * xh))

        @pl.when(i == 0)
        def _():
            dog_ref[...] = jnp.zeros_like(dog_ref)
            dlg_ref[...] = jnp.zeros_like(dlg_ref)
            dlb_ref[...] = jnp.zeros_like(dlb_ref)

        dog_ref[...] += _colsum(dya * n3)
        dlg_ref[...] += _colsum(da2 * xh)
        dlb_ref[...] += _colsum(da2)

    vecc_shape = jax.ShapeDtypeStruct((1, dc), F32)
    da1, g_og_conv, g_cln_g, g_cln_b = pl.pallas_call(
        conv_norm_bwd_body, name="conv_norm_bwd", grid=(nt,),
        out_shape=(jax.ShapeDtypeStruct((t, dc), F32), vecc_shape, vecc_shape, vecc_shape),
        in_specs=[main_col(0), row_c, vec_c, vec_c, vec_c], out_specs=(row_c, vec_c, vec_c, vec_c),
        compiler_params=_cp(("arbitrary",)))(dy, a1, conv_ln_g, conv_ln_b, out_g_conv)

    def sgu_bwd_body(dy_ref, pu, pw, slg, slb, sw, swt, sbias, sog, dproj_ref, dsog_ref, dslg_ref, dslb_ref, dsw_ref, dsb_ref, vs_ref, dvn_ref):
        i = pl.program_id(0)
        zu = pu[...].astype(F32)
        zv = pw[...].astype(F32)
        u = _gelu(zu)
        vv = _gelu(zv)
        xhv, rstd = layer_norm_stats(vv)
        vnb = (xhv * slg[...] + slb[...]).astype(BF16)
        sgu_mix(vs_ref, sw, vnb, sbias, 1)
        vs = vs_ref[...]
        bg = u * vs
        rb = lax.rsqrt(_mean(bg * bg) + EPS)
        nb = bg * rb
        dyb = dy_ref[...].astype(F32)
        dnb = dyb * sog[...]
        dbg = rb * (dnb - nb * _mean(dnb * nb))
        du = dbg * vs
        dvs = dbg * u
        dvsb = dvs.astype(BF16)

        @pl.when(i == 0)
        def _():
            dsog_ref[...] = jnp.zeros_like(dsog_ref)
            dslg_ref[...] = jnp.zeros_like(dslg_ref)
            dslb_ref[...] = jnp.zeros_like(dslb_ref)
            dsw_ref[...] = jnp.zeros_like(dsw_ref)
            dsb_ref[...] = jnp.zeros_like(dsb_ref)

        dsog_ref[...] += _colsum(dyb * nb)
        dsb_ref[...] += dvs
        for h in range(heads):
            dblk = dvsb[:, h * hd:(h + 1) * hd]
            vblk = vnb[:, h * hd:(h + 1) * hd]
            dsw_ref[h] += lax.dot_general(dblk, vblk, (((1,), (1,)), ((), ())), preferred_element_type=F32)
            dvn_ref[:, pl.ds(h * hd, hd)] = jnp.dot(swt[h], dblk, preferred_element_type=F32)
        dvn = dvn_ref[...]
        dslg_ref[...] += _colsum(dvn * xhv)
        dslb_ref[...] += _colsum(dvn)
        dxh = dvn * slg[...]
        dvv = rstd * (dxh - _mean(dxh) - xhv * _mean(dxh * xhv))
        dproj_ref[:, pl.ds(0, dc)] = (du * _gelu_grad(zu)).astype(BF16)
        dproj_ref[:, pl.ds(dc, dc)] = (dvv * _gelu_grad(zv)).astype(BF16)

    dproj, g_og_sgu, g_sln_g, g_sln_b, g_sgu_w, g_sgu_bias = pl.pallas_call(
        sgu_bwd_body, name="sgu_bwd", grid=(t // ch,),
        out_shape=(jax.ShapeDtypeStruct((t, 4 * dc), BF16), vecc_shape, vecc_shape, vecc_shape,
                   jax.ShapeDtypeStruct((heads, ch, ch), F32), jax.ShapeDtypeStruct((ch, dc), F32)),
        in_specs=[chunk_col(1), chunk_col(2), chunk_col(3), vec_c, vec_c, sguw_spec, sguw_spec, bias_spec, vec_c],
        out_specs=(pl.BlockSpec((ch, 2 * dc), lambda i: (i, 1)), vec_c, vec_c, vec_c, sguw_spec, bias_spec),
        scratch_shapes=[pltpu.VMEM((ch, dc), F32), pltpu.VMEM((ch, dc), F32)],
        compiler_params=_cp(("arbitrary",)),
    )(dy, proj, proj, sgu_ln_g, sgu_ln_b, sgu_w_bf, sgu_wt_bf, sgu_bias, out_g_sgu)

    def sgu_bias_reduce_body(b_ref, o_ref):
        lane = lax.broadcasted_iota(jnp.int32, (ch, LANES), 1)
        res = jnp.zeros((ch, LANES), F32)
        for h in range(heads):
            res = jnp.where(lane == h, jnp.sum(b_ref[:, pl.ds(h * hd, hd)], axis=1, keepdims=True), res)
        o_ref[...] = res

    g_sgu_b_t = pl.pallas_call(
        sgu_bias_reduce_body, name="sgu_bias_reduce", out_shape=jax.ShapeDtypeStruct((ch, LANES), F32),
        in_specs=[pl.BlockSpec(memory_space=pltpu.VMEM)], out_specs=pl.BlockSpec(memory_space=pltpu.VMEM),
        compiler_params=_cp())(g_sgu_bias)
    g_sgu_b = g_sgu_b_t[:, :heads].T

    def conv_bwd_body(da, dap, dan, pv, pg, pvp, pgp, pvn, pgn, cw, dproj_in, dproj_ref, dcw_ref, dcb_ref, sh_e, sh_d, wb):
        del dproj_in
        i = pl.program_id(0)
        first = (i % tps) == 0
        last = (i % tps) == tps - 1

        @pl.when(i == 0)
        def _():
            dcw_ref[...] = jnp.zeros_like(dcw_ref)
            dcb_ref[...] = jnp.zeros_like(dcb_ref)

        for cols in conv_pieces:
            stage_glu(sh_e, i, cols, pv, pg, pvp, pgp, pvn, pgn)
            sh_d[0, pl.ds(0, HALO), :] = jnp.where(first, 0.0, dap[:, cols])
            for rr in range(tb // HALO):
                sh_d[0, pl.ds(HALO + rr * HALO, HALO), :] = da[pl.ds(rr * HALO, HALO), cols]
            sh_d[0, pl.ds(HALO + tb, HALO), :] = jnp.where(last, 0.0, dan[:, cols])
            shift_copies(sh_d)
            stage_taps(wb, cw, cols)
            for rr in range(tb // HALO):
                halves = []
                for row0 in (rr * HALO, rr * HALO + SUBLANES):
                    acc = jnp.zeros((SUBLANES, CL), F32)
                    for k in range(kc):
                        acc = acc + wb[k] * tap_piece(sh_d, HALO + padc - k, row0)
                    halves.append(acc)
                da0 = jnp.concatenate(halves, axis=0)
                rows = pl.ds(rr * HALO, HALO)
                vv = pv[rows, cols].astype(F32)
                sg = _sigmoid(pg[rows, cols].astype(F32))
                dproj_ref[rows, cols] = (da0 * sg).astype(BF16)
                dproj_ref[rows, pl.ds(dc + cols.start, CL)] = (da0 * vv * sg * (1.0 - sg)).astype(BF16)
            for k in range(kc):
                acc = jnp.zeros((SUBLANES, CL), F32)
                for row0 in range(0, tb, SUBLANES):
                    acc = acc + da[pl.ds(row0, SUBLANES), cols] * tap_piece(sh_e, HALO - padc + k, row0)
                dcw_ref[pl.ds(k, 1), cols] += _colsum(acc)
            acc = jnp.zeros((SUBLANES, CL), F32)
            for row0 in range(0, tb, SUBLANES):
                acc = acc + da[pl.ds(row0, SUBLANES), cols]
            dcb_ref[:, cols] += _colsum(acc)

    def halo_rows(lo):
        if lo:
            return pl.BlockSpec((HALO, dc), lambda i: (jnp.maximum(i * hb - 1, 0), 0))
        return pl.BlockSpec((HALO, dc), lambda i: (jnp.minimum((i + 1) * hb, n_hb - 1), 0))

    dproj, g_conv_w, g_conv_b = pl.pallas_call(
        conv_bwd_body, name="conv_bwd", grid=(nt,),
        out_shape=(jax.ShapeDtypeStruct((t, 4 * dc), BF16), jax.ShapeDtypeStruct((kc, dc), F32), vecc_shape),
        in_specs=[row_c, halo_rows(True), halo_rows(False), main_col(0), main_col(1), prev_col(0), prev_col(1),
                  next_col(0), next_col(1), convw_spec, pl.BlockSpec(memory_space=pl.ANY)],
        out_specs=(pl.BlockSpec((tb, 2 * dc), lambda i: (i, 0)), convw_spec, vec_c),
        scratch_shapes=[pltpu.VMEM(sh_shape, F32), pltpu.VMEM(sh_shape, F32), pltpu.VMEM((kc, SUBLANES, CL), F32)],
        input_output_aliases={10: 0}, compiler_params=_cp(("arbitrary",)),
    )(da1, da1, da1, proj, proj, proj, proj, proj, proj, conv_w_full, dproj)

    tm_wi = _blk(d, 2048)
    gw_in = _mm(
        "mm_gw_in", h1, dproj, mode="tn", grid=(N_DEV, d // tm_wi, nk_t),
        a_spec=pl.BlockSpec((tk_t, tm_wi), lambda j, i, k: (k, i)),
        b_spec=pl.BlockSpec((tk_t, win_sh), lambda j, i, k: (k, j)),
        acc_shape=(tm_wi, win_sh), outs=(jax.ShapeDtypeStruct((N_DEV, d, win_sh), BF16),),
        out_specs=(pl.BlockSpec((None, tm_wi, win_sh), lambda j, i, k: (j, i, 0)),), epilogue=_store_bf16)[0]

    x_in = exchange_begin("w_in", gw_in)
    tn_h = _blk(d, 2048)
    dh1 = _mm(
        "mm_dh1", dproj, w_in_g, mode="nt", grid=(t // tm_up, d // tn_h, N_DEV),
        a_spec=pl.BlockSpec((tm_up, win_sh), lambda i, j, k: (i, k)),
        b_spec=pl.BlockSpec((None, tn_h, win_sh), lambda i, j, k: (k, j, 0)),
        acc_shape=(tm_up, tn_h), outs=(td_shape,), out_specs=(pl.BlockSpec((tm_up, tn_h), lambda i, j, k: (i, j)),),
        epilogue=_store_f32, after=(x_in[4],))[0]

    grad_x, dshift_m, dscale_m, dg_mix = modulate_bwd("mod1_bwd", xf, dh1, dx1, g_mix, scale_m)

    dmod = jnp.concatenate([dshift_m, dscale_m, dgate_m, dshift_f, dscale_f, dgate_f], axis=1)
    g_ffn_w_full = g_ffn_w_s.transpose(1, 0, 2).reshape(kf, dff)
    rep_names = ["g_mix", "conv_b", "conv_ln_g", "conv_ln_b", "sgu_ln_g", "sgu_ln_b", "sgu_w", "sgu_b",
                 "out_g_conv", "out_g_sgu", "g_ffn", "ffn_conv_b", "g_final"]
    rep_w = [g_mix, conv_b, conv_ln_g, conv_ln_b, sgu_ln_g, sgu_ln_b, sgu_w, sgu_b, out_g_conv, out_g_sgu, g_ffn, ffn_conv_b, g_final]
    rep_m = [m_g_mix, m_conv_b, m_conv_ln_g, m_conv_ln_b, m_sgu_ln_g, m_sgu_ln_b, m_sgu_w, m_sgu_b, m_out_g_conv, m_out_g_sgu,
             m_g_ffn, m_ffn_conv_b, m_g_final]
    rep_v = [v_g_mix, v_conv_b, v_conv_ln_g, v_conv_ln_b, v_sgu_ln_g, v_sgu_ln_b, v_sgu_w, v_sgu_b, v_out_g_conv, v_out_g_sgu,
             v_g_ffn, v_ffn_conv_b, v_g_final]
    rep_g = [dg_mix, g_conv_b, g_cln_g, g_cln_b, g_sln_g, g_sln_b, g_sgu_w, g_sgu_b, g_og_conv, g_og_sgu, dg_ffn,
             g_ffn_b_s, dg_final]
    rep_shapes = [w.shape for w in rep_w]
    extra_g = [g_conv_w, g_ffn_w_full, loss_part[:, :1]]
    extra_shapes = [(kc, dc), (kf, dff), (1, 1)]
    sum_blob = _pack(rep_g + extra_g)
    n_sum_rows = sum_blob.shape[0]
    blob1 = jnp.concatenate([sum_blob, _pack([dmod])], axis=0)
    small_land = lax.dynamic_update_slice(lax.empty((N_DEV,) + blob1.shape, F32), blob1[None], (me, 0, 0))
    small_started = _gather_start("ag1_small_grads", blob1, small_land)

    def big(name, started, after, w2d, m2d, v2d, plane=False):
        send_sems, recv_sems, x_thru, land_thru, _ = started
        mine, recv = _alltoall_wait("a2aw_" + name, send_sems, recv_sems, x_thru, land_thru, after, plane)
        return _adamw("adamw_" + name, w2d, m2d, v2d, slots=recv, own=mine, me=2 * xi + yi if plane else me)

    wdown_out = big("w_down", x_down, small_started[6], w_down[0], m_w_down[0], v_w_down[0], True)
    wup_out = big("w_up", x_up, wdown_out[0], w_up_t, m_w_up_t, v_w_up_t, True)
    wout_out = big("w_out", x_out, wup_out[0], w_out[0], m_w_out[0], v_w_out[0])

    gathered = gather_end("small_grads", small_started, wout_out[0])
    summed = _sum_slots("sum_small_grads", gathered[:, :n_sum_rows])
    n_rep_rows = _pack(rep_g).shape[0]
    g_conv_w_all, g_ffn_w_all, loss_all = _unpack(summed[n_rep_rows:], extra_shapes)
    loss = loss_all[0, 0]
    dmod_all = gathered[:, n_sum_rows:].reshape(N_DEV, -1)[:, :bsz * n_mod * d].reshape(n_batch, n_mod * d)

    rep_out = _adamw("adamw_small", _pack(rep_w), _pack(rep_m), _pack(rep_v), grad=summed[:n_rep_rows])
    rep_out = [_unpack(o, rep_shapes) for o in rep_out]
    rep = {name: tuple(rep_out[q][p] for q in range(4)) for p, name in enumerate(rep_names)}

    g_conv_w_me = lax.dynamic_slice(g_conv_w_all, (0, me * dc_sh), (kc, dc_sh))
    cw_out = _adamw("adamw_conv_w", conv_w[0], m_conv_w[0], v_conv_w[0], grad=g_conv_w_me)
    g_ffn_w_me = lax.dynamic_slice(g_ffn_w_all, (0, me * cw_sh), (kf, cw_sh))
    fw_out = _adamw("adamw_ffn_conv_w", ffn_conv_w[0], m_ffn_conv_w[0], v_ffn_conv_w[0], grad=g_ffn_w_me)

    g_b_ada = _col_sums("grad_b_ada", dmod_all)
    bada_out = _adamw("adamw_b_ada", b_ada, m_b_ada, v_b_ada, grad=g_b_ada)
    dmod_sh = lax.dynamic_slice(dmod_all, (0, me * ada_sh), (n_batch, ada_sh)).astype(BF16)
    wada_out = _adamw("adamw_w_ada", w_ada[0], m_w_ada[0], v_w_ada[0], ca_t=c_act.T, dmod=dmod_sh)

    win_out = big("w_in", x_in, wada_out[0], w_in[0], m_w_in[0], v_w_in[0])
    wup_out = tuple(jnp.swapaxes(o, 0, 1) for o in wup_out)

    def lead(outs4):
        return tuple(o[None] for o in outs4)

    results = {
        "w_ada": lead(wada_out), "b_ada": bada_out, "w_in": lead(win_out), "conv_w": lead(cw_out),
        "w_out": lead(wout_out), "w_up": lead(wup_out), "ffn_conv_w": lead(fw_out), "w_down": lead(wdown_out),
    }
    results.update(rep)
    order = ["w_ada", "b_ada", "g_mix", "w_in", "conv_w", "conv_b", "conv_ln_g", "conv_ln_b", "sgu_ln_g", "sgu_ln_b",
             "sgu_w", "sgu_b", "out_g_conv", "out_g_sgu", "w_out", "g_ffn", "w_up", "ffn_conv_w", "ffn_conv_b", "w_down", "g_final"]
    out = [loss, grad_x.reshape(bsz, seq, d)]
    for q in range(4):
        out += [results[n][q] for n in order]
    return tuple(out)
```

```python
import functools

import jax
import jax.numpy as jnp
from jax import lax
from jax.experimental import pallas as pl
from jax.experimental.pallas import tpu as pltpu

EPS = 1e-6
N_DEV = 8
MESH_ID = pl.DeviceIdType.MESH
V7X_VMEM_BYTES = 64 * 1024 * 1024
VMEM_LIMIT = V7X_VMEM_BYTES - 8 * 1024 * 1024
LANES = 128
SUBLANES = 8
HALO = 16
BLOB_ALIGN = SUBLANES * LANES

ADAM_LR = 0.001
ADAM_B1 = 0.9
ADAM_B2 = 0.999
ADAM_EPS = 1e-08
ADAM_WD = 0.01
ADAM_STEP = 10

F32 = jnp.float32
BF16 = jnp.bfloat16


def _cp(sem=None, **kw):
    return pltpu.CompilerParams(dimension_semantics=sem, vmem_limit_bytes=VMEM_LIMIT, **kw)


def _blk(n, pref):
    return pref if n % pref == 0 else n


def _rows_tile(rows, cols, budget=3 << 19):
    best = None
    for t in range(SUBLANES, rows + 1, SUBLANES):
        if rows % t == 0 and t * cols * 4 <= budget:
            best = t
    return best if best is not None else rows


def _sigmoid(z):
    return 1.0 / (1.0 + jnp.exp(-z))


def _gelu(z):
    return z * (lax.erf(z * 0.7071067811865476) + 1.0) * 0.5


def _gelu_grad(z):
    return 0.5 * (1.0 + lax.erf(z * 0.7071067811865476)) + z * jnp.exp(-0.5 * z * z) * 0.3989422804014327


def _mean(z):
    return jnp.mean(z, axis=-1, keepdims=True)


def _colsum(z):
    return jnp.sum(z, axis=0, keepdims=True)


def _my_pos():
    return lax.axis_index("x"), lax.axis_index("y"), lax.axis_index("c")


def _allgather(name, shard, in_vmem):
    r, cdim = shard.shape

    def body(x_ref, out_ref, send_sems, recv_sems, local_sem):
        x, y, c = _my_pos()
        me, sibling = (x, y, c), (x, y, 1 - c)
        chips = [(1 - x, y), (x, 1 - y), (1 - x, 1 - y)]

        def slot(px, py, pc):
            return out_ref.at[4 * px + 2 * py + pc]

        def copy(k, block, to, src=None):
            return pltpu.make_async_remote_copy(
                src_ref=slot(*block) if src is None else src, dst_ref=slot(*block),
                send_sem=send_sems.at[k], recv_sem=recv_sems.at[k], device_id=to, device_id_type=MESH_ID)

        mine = pltpu.make_async_copy(x_ref, slot(*me), local_sem)
        mine.start()
        first = [copy(0, me, sibling, src=x_ref)]
        first += [copy(1 + j, me, (*chip, c), src=x_ref) for j, chip in enumerate(chips)]
        for cp in first:
            cp.start()
        passed = [copy(4 + j, (*chip, c), sibling) for j, chip in enumerate(chips)]
        for j, chip in enumerate(chips):
            copy(1 + j, (*chip, c), me).wait_recv()
            passed[j].start()
        copy(0, sibling, me).wait_recv()
        for j, chip in enumerate(chips):
            copy(4 + j, (*chip, 1 - c), me).wait_recv()
        for cp in first + passed:
            cp.wait_send()
        mine.wait()

    space = pltpu.VMEM if in_vmem else pl.ANY
    return pl.pallas_call(
        body, name=name, out_shape=jax.ShapeDtypeStruct((N_DEV, r, cdim), shard.dtype),
        in_specs=[pl.BlockSpec(memory_space=space)], out_specs=pl.BlockSpec(memory_space=space),
        scratch_shapes=[pltpu.SemaphoreType.DMA((7,)), pltpu.SemaphoreType.DMA((7,)), pltpu.SemaphoreType.DMA],
        compiler_params=pltpu.CompilerParams(vmem_limit_bytes=VMEM_LIMIT),
    )(shard)


HBM_SPEC = pl.BlockSpec(memory_space=pltpu.HBM)
SEM_SPEC = pl.BlockSpec(memory_space=pltpu.SEMAPHORE)
ANY_SPEC = pl.BlockSpec(memory_space=pl.ANY)
SPLIT_EFFECT = pltpu.SideEffectType.DATAFLOW_SIDE_EFFECTING
TOKEN_SHAPE = jax.ShapeDtypeStruct((SUBLANES, LANES), F32)


def _hbm(a):
    return pltpu.with_memory_space_constraint(a, pltpu.HBM)


def _peers():
    x, y, c = _my_pos()
    out = []
    for k in range(1, N_DEV):
        px = 1 - x if k & 4 else x
        py = 1 - y if k & 2 else y
        pc = 1 - c if k & 1 else c
        out.append((k, (px, py, pc), 4 * px + 2 * py + pc))
    return out


def _alltoall_start(name, blocks):
    def body(x_ref, land_ref, send_sems, recv_sems, x_thru, land_thru, token):
        del x_thru, land_thru
        x, y, c = _my_pos()
        me = 4 * x + 2 * y + c
        for k, peer, slot in _peers():
            pltpu.make_async_remote_copy(
                src_ref=x_ref.at[slot], dst_ref=land_ref.at[me], send_sem=send_sems.at[k - 1],
                recv_sem=recv_sems.at[k - 1], device_id=peer, device_id_type=MESH_ID).start()
        token[...] = jnp.zeros_like(token)

    sems = pltpu.SemaphoreType.DMA((N_DEV - 1,))
    buf = pltpu.HBM(blocks.shape, blocks.dtype)
    return pl.pallas_call(
        body, name=name, out_shape=(sems, sems, buf, buf, TOKEN_SHAPE), in_specs=(HBM_SPEC, HBM_SPEC),
        out_specs=(SEM_SPEC, SEM_SPEC, HBM_SPEC, HBM_SPEC, pl.BlockSpec(memory_space=pltpu.VMEM)),
        input_output_aliases={0: 2, 1: 3}, compiler_params=pltpu.CompilerParams(has_side_effects=SPLIT_EFFECT),
    )(_hbm(blocks), _hbm(lax.empty(blocks.shape, blocks.dtype)))


def _alltoall_wait(name, send_sems, recv_sems, x_thru, land_thru, after):
    def body(x_ref, land_ref, send_sems, recv_sems, after_ref, x_dead, got_ref):
        del after_ref, x_dead, got_ref
        for k, peer, slot in _peers():
            cp = pltpu.make_async_remote_copy(
                src_ref=x_ref.at[slot], dst_ref=land_ref.at[slot], send_sem=send_sems.at[k - 1],
                recv_sem=recv_sems.at[k - 1], device_id=peer, device_id_type=MESH_ID)
            cp.wait_send()
            cp.wait_recv()

    buf = pltpu.HBM(x_thru.shape, x_thru.dtype)
    return pl.pallas_call(
        body, name=name, out_shape=(buf, buf), in_specs=(HBM_SPEC, HBM_SPEC, SEM_SPEC, SEM_SPEC, ANY_SPEC),
        out_specs=(HBM_SPEC, HBM_SPEC), input_output_aliases={0: 0, 1: 1},
        compiler_params=pltpu.CompilerParams(has_side_effects=SPLIT_EFFECT),
    )(x_thru, land_thru, send_sems, recv_sems, after)


def _gather_start(name, shard, land):
    def body(x_ref, land_ref, sib_send, sib_recv, ici_send, ici_recv, x_thru, land_thru, token):
        del x_thru, land_thru
        x, y, c = _my_pos()
        me = 4 * x + 2 * y + c
        pltpu.make_async_remote_copy(
            src_ref=x_ref, dst_ref=land_ref.at[me], send_sem=sib_send.at[0], recv_sem=sib_recv.at[0],
            device_id=(x, y, 1 - c), device_id_type=MESH_ID).start()
        for k, peer in enumerate([(1 - x, y, c), (x, 1 - y, c), (1 - x, 1 - y, c)]):
            pltpu.make_async_remote_copy(
                src_ref=x_ref, dst_ref=land_ref.at[me], send_sem=ici_send.at[k], recv_sem=ici_recv.at[k],
                device_id=peer, device_id_type=MESH_ID).start()
        token[...] = jnp.zeros_like(token)

    sib, ici = pltpu.SemaphoreType.DMA((1,)), pltpu.SemaphoreType.DMA((3,))
    return pl.pallas_call(
        body, name=name,
        out_shape=(sib, sib, ici, ici, pltpu.HBM(shard.shape, shard.dtype), pltpu.HBM(land.shape, land.dtype), TOKEN_SHAPE),
        in_specs=(HBM_SPEC, HBM_SPEC),
        out_specs=(SEM_SPEC, SEM_SPEC, SEM_SPEC, SEM_SPEC, HBM_SPEC, HBM_SPEC, pl.BlockSpec(memory_space=pltpu.VMEM)),
        input_output_aliases={0: 4, 1: 5}, compiler_params=pltpu.CompilerParams(has_side_effects=SPLIT_EFFECT),
    )(_hbm(shard), _hbm(land))


def _gather_wait(name, send_sems, recv_sems, x_thru, land_thru, after, sibling):
    def body(x_ref, land_ref, send_sems, recv_sems, after_ref, x_dead, got_ref):
        del after_ref, x_dead, got_ref
        x, y, c = _my_pos()
        sources = [(x, y, 1 - c)] if sibling else [(1 - x, y, c), (x, 1 - y, c), (1 - x, 1 - y, c)]
        for k, (px, py, pc) in enumerate(sources):
            cp = pltpu.make_async_remote_copy(
                src_ref=x_ref, dst_ref=land_ref.at[4 * px + 2 * py + pc], send_sem=send_sems.at[k],
                recv_sem=recv_sems.at[k], device_id=(px, py, pc), device_id_type=MESH_ID)
            cp.wait_send()
            cp.wait_recv()

    return pl.pallas_call(
        body, name=name, out_shape=(pltpu.HBM(x_thru.shape, x_thru.dtype), pltpu.HBM(land_thru.shape, land_thru.dtype)),
        in_specs=(HBM_SPEC, HBM_SPEC, SEM_SPEC, SEM_SPEC, ANY_SPEC), out_specs=(HBM_SPEC, HBM_SPEC),
        input_output_aliases={0: 0, 1: 1}, compiler_params=pltpu.CompilerParams(has_side_effects=SPLIT_EFFECT),
    )(x_thru, land_thru, send_sems, recv_sems, after)


def _forward_start(name, land):
    def body(land_ref, send_sems, recv_sems, land_thru, token):
        del land_thru
        x, y, c = _my_pos()
        for j, (px, py) in enumerate([(1 - x, y), (x, 1 - y), (1 - x, 1 - y)]):
            mine = 4 * px + 2 * py + c
            pltpu.make_async_remote_copy(
                src_ref=land_ref.at[mine], dst_ref=land_ref.at[mine], send_sem=send_sems.at[j], recv_sem=recv_sems.at[j],
                device_id=(x, y, 1 - c), device_id_type=MESH_ID).start()
        token[...] = jnp.zeros_like(token)

    sems = pltpu.SemaphoreType.DMA((3,))
    return pl.pallas_call(
        body, name=name, out_shape=(sems, sems, pltpu.HBM(land.shape, land.dtype), TOKEN_SHAPE), in_specs=(HBM_SPEC,),
        out_specs=(SEM_SPEC, SEM_SPEC, HBM_SPEC, pl.BlockSpec(memory_space=pltpu.VMEM)), input_output_aliases={0: 2},
        compiler_params=pltpu.CompilerParams(has_side_effects=SPLIT_EFFECT),
    )(_hbm(land))


def _forward_wait(name, send_sems, recv_sems, land_thru, after):
    def body(land_ref, send_sems, recv_sems, after_ref, got_ref):
        del after_ref, got_ref
        x, y, c = _my_pos()
        for j, (px, py) in enumerate([(1 - x, y), (x, 1 - y), (1 - x, 1 - y)]):
            mine, theirs = 4 * px + 2 * py + c, 4 * px + 2 * py + (1 - c)
            cp = pltpu.make_async_remote_copy(
                src_ref=land_ref.at[mine], dst_ref=land_ref.at[theirs], send_sem=send_sems.at[j], recv_sem=recv_sems.at[j],
                device_id=(x, y, 1 - c), device_id_type=MESH_ID)
            cp.wait_send()
            cp.wait_recv()

    return pl.pallas_call(
        body, name=name, out_shape=pltpu.HBM(land_thru.shape, land_thru.dtype),
        in_specs=(HBM_SPEC, SEM_SPEC, SEM_SPEC, ANY_SPEC), out_specs=HBM_SPEC, input_output_aliases={0: 0},
        compiler_params=pltpu.CompilerParams(has_side_effects=SPLIT_EFFECT),
    )(land_thru, send_sems, recv_sems, after)


def _gather_forward(name, land):
    def body(x_ref, out_ref, send_sems, recv_sems):
        x, y, c = _my_pos()
        chips = [(1 - x, y), (x, 1 - y), (1 - x, 1 - y)]
        sends = []
        for j, (px, py) in enumerate(chips):
            mine = 4 * px + 2 * py + c
            cp = pltpu.make_async_remote_copy(
                src_ref=x_ref.at[mine], dst_ref=out_ref.at[mine], send_sem=send_sems.at[j], recv_sem=recv_sems.at[j],
                device_id=(x, y, 1 - c), device_id_type=MESH_ID)
            cp.start()
            sends.append(cp)
        for j, (px, py) in enumerate(chips):
            theirs = 4 * px + 2 * py + (1 - c)
            pltpu.make_async_remote_copy(
                src_ref=x_ref.at[theirs], dst_ref=out_ref.at[theirs], send_sem=send_sems.at[j], recv_sem=recv_sems.at[j],
                device_id=(x, y, 1 - c), device_id_type=MESH_ID).wait_recv()
        for cp in sends:
            cp.wait_send()

    return pl.pallas_call(
        body, name=name, out_shape=jax.ShapeDtypeStruct(land.shape, land.dtype), in_specs=[ANY_SPEC], out_specs=ANY_SPEC,
        scratch_shapes=[pltpu.SemaphoreType.DMA((3,)), pltpu.SemaphoreType.DMA((3,))], input_output_aliases={0: 0},
    )(land)


def _pack(arrays):
    parts = []
    for a in arrays:
        flat = a.reshape(-1).astype(F32)
        pad = (-flat.shape[0]) % BLOB_ALIGN
        parts.append(jnp.pad(flat, (0, pad)) if pad else flat)
    return jnp.concatenate(parts).reshape(-1, LANES)


def _unpack(blob, shapes):
    flat = blob.reshape(-1)
    out, off = [], 0
    for shp in shapes:
        n = 1
        for s in shp:
            n *= s
        out.append(flat[off:off + n].reshape(shp))
        off += n + (-n) % BLOB_ALIGN
    return out


def _cast_bf16(name, w, after):
    r, cdim = w.shape
    tr = _rows_tile(r, cdim)

    def body(w_ref, after_ref, o_ref):
        del after_ref
        o_ref[...] = w_ref[...].astype(BF16)

    return pl.pallas_call(
        body, name=name, grid=(r // tr,), out_shape=jax.ShapeDtypeStruct(w.shape, BF16),
        in_specs=[pl.BlockSpec((tr, cdim), lambda i: (i, 0)), ANY_SPEC], out_specs=pl.BlockSpec((tr, cdim), lambda i: (i, 0)),
        compiler_params=_cp(("parallel",)))(w, after)


def _adam_math(w, g, m, v):
    m = ADAM_B1 * m + (1.0 - ADAM_B1) * g
    v = ADAM_B2 * v + (1.0 - ADAM_B2) * (g * g)
    m_hat = m / (1.0 - ADAM_B1 ** ADAM_STEP)
    v_hat = v / (1.0 - ADAM_B2 ** ADAM_STEP)
    delta = -ADAM_LR * (m_hat / (jnp.sqrt(v_hat) + ADAM_EPS) + ADAM_WD * w)
    return delta, m, v


def _adamw(name, w, m, v, *, grad=None, slots=None, own=None, me=None, ca_t=None, dmod=None):
    r, cdim = w.shape
    tr = _rows_tile(r, cdim)
    row = pl.BlockSpec((tr, cdim), lambda i, *_: (i, 0))
    prefetch = []
    if grad is not None:
        srcs, src_specs = [grad], [row]
    elif slots is not None:
        prefetch = [jnp.reshape(me, (1,)).astype(jnp.int32)]
        srcs = [slots, own]
        src_specs = [pl.BlockSpec((N_DEV, tr, cdim), lambda i, me_ref: (0, i, 0)),
                     pl.BlockSpec((None, tr, cdim), lambda i, me_ref: (me_ref[0], i, 0))]
    else:
        srcs = [ca_t, dmod]
        src_specs = [pl.BlockSpec((tr, ca_t.shape[1]), lambda i: (i, 0)), pl.BlockSpec(dmod.shape, lambda i: (0, 0))]
    n_src = len(srcs)
    n_pre = len(prefetch)

    def body(*refs):
        pre, refs = refs[:n_pre], refs[n_pre:]
        w_ref, m_ref, v_ref = refs[n_src:n_src + 3]
        g_ref, d_ref, nm_ref, nv_ref = refs[n_src + 3:]
        if grad is not None:
            g = refs[0][...]
        elif slots is not None:
            mine = pre[0][0]
            own_f = refs[1][...].astype(F32)
            g = jnp.where(mine == 0, own_f, refs[0][0].astype(F32))
            for s in range(1, N_DEV):
                g = g + jnp.where(mine == s, own_f, refs[0][s].astype(F32))
        else:
            g = jnp.dot(refs[0][...], refs[1][...], preferred_element_type=F32)
        delta, nm, nv = _adam_math(w_ref[...], g, m_ref[...], v_ref[...])
        g_ref[...] = g
        d_ref[...] = delta
        nm_ref[...] = nm
        nv_ref[...] = nv

    shp = jax.ShapeDtypeStruct(w.shape, F32)
    grid_spec = pltpu.PrefetchScalarGridSpec(
        num_scalar_prefetch=n_pre, grid=(r // tr,), in_specs=[*src_specs, row, row, row], out_specs=(row,) * 4)
    return pl.pallas_call(
        body, name=name, grid_spec=grid_spec, out_shape=(shp,) * 4, compiler_params=_cp(("parallel",)))(*prefetch, *srcs, w, m, v)


def _sum_slots(name, gathered):
    _, r, cdim = gathered.shape
    tr = _rows_tile(r, cdim * N_DEV)

    def body(g_ref, o_ref):
        acc = g_ref[0]
        for s in range(1, N_DEV):
            acc = acc + g_ref[s]
        o_ref[...] = acc

    return pl.pallas_call(
        body, name=name, grid=(r // tr,), out_shape=jax.ShapeDtypeStruct((r, cdim), F32),
        in_specs=[pl.BlockSpec((N_DEV, tr, cdim), lambda i: (0, i, 0))], out_specs=pl.BlockSpec((tr, cdim), lambda i: (i, 0)),
        compiler_params=_cp(("parallel",)))(gathered)


def _col_sums(name, a):
    r, cdim = a.shape

    def body(a_ref, o_ref):
        o_ref[...] = _colsum(a_ref[...])

    return pl.pallas_call(
        body, name=name, out_shape=jax.ShapeDtypeStruct((1, cdim), F32),
        in_specs=[pl.BlockSpec(memory_space=pltpu.VMEM)], out_specs=pl.BlockSpec(memory_space=pltpu.VMEM),
        compiler_params=_cp())(a)


def _mm(name, a, b, *, mode, grid, a_spec, b_spec, acc_shape, outs, out_specs, epilogue, extra=(), extra_specs=(), after=(),
        ids=None, prev=None):
    nk = grid[-1]
    dims = {"nn": ((1,), (0,)), "nt": ((1,), (1,)), "tn": ((0,), (0,))}[mode]
    after = tuple(after) + ((prev,) if prev is not None else ())
    n_extra, n_out, n_after = len(extra), len(outs), len(after)
    n_pre = 0 if ids is None else 1

    def body(*refs):
        refs = refs[n_pre:]
        a_ref, b_ref = refs[0], refs[1]
        extra_refs = refs[2:2 + n_extra]
        out_refs = refs[2 + n_extra + n_after:2 + n_extra + n_after + n_out]
        def part():
            return lax.dot_general(a_ref[...], b_ref[...], (dims, ((), ())), preferred_element_type=F32)

        if nk == 1:
            epilogue(part(), extra_refs, out_refs)
        else:
            acc = refs[-1]
            k = pl.program_id(len(grid) - 1)

            @pl.when(k == 0)
            def _():
                acc[...] = part()

            @pl.when(jnp.logical_and(k > 0, k < nk - 1))
            def _():
                acc[...] += part()

            @pl.when(k == nk - 1)
            def _():
                epilogue(acc[...] + part(), extra_refs, out_refs)

    scratch = [pltpu.VMEM(acc_shape, F32)] if nk > 1 else []
    sem = ("parallel",) * (len(grid) - 1) + ("arbitrary",)
    grid_spec = pltpu.PrefetchScalarGridSpec(
        num_scalar_prefetch=n_pre, grid=grid, in_specs=[a_spec, b_spec, *extra_specs, *([ANY_SPEC] * n_after)],
        out_specs=out_specs, scratch_shapes=scratch)
    aliases = {n_pre + 2 + n_extra + n_after - 1: 0} if prev is not None else {}
    pre = () if ids is None else (ids,)
    return pl.pallas_call(
        body, name=name, grid_spec=grid_spec, out_shape=outs, input_output_aliases=aliases,
        compiler_params=_cp(sem))(*pre, a, b, *extra, *after)


def _store_bf16(acc, extra_refs, out_refs):
    out_refs[0][...] = acc.astype(BF16)


def _store_f32(acc, extra_refs, out_refs):
    out_refs[0][...] = acc


def _residual_epilogue(acc, extra_refs, out_refs):
    x_ref, gate_ref = extra_refs
    out_refs[0][...] = acc
    out_refs[1][...] = x_ref[...] + gate_ref[...] * acc


def kernel(x, c, w_ada, b_ada, g_mix, w_in, conv_w, conv_b, conv_ln_g, conv_ln_b, sgu_ln_g, sgu_ln_b, sgu_w, sgu_b, out_g_conv, out_g_sgu, w_out, g_ffn, w_up, ffn_conv_w, ffn_conv_b, w_down, g_final, loss_target, m_w_ada, m_b_ada, m_g_mix, m_w_in, m_conv_w, m_conv_b, m_conv_ln_g, m_conv_ln_b, m_sgu_ln_g, m_sgu_ln_b, m_sgu_w, m_sgu_b, m_out_g_conv, m_out_g_sgu, m_w_out, m_g_ffn, m_w_up, m_ffn_conv_w, m_ffn_conv_b, m_w_down, m_g_final, v_w_ada, v_b_ada, v_g_mix, v_w_in, v_conv_w, v_conv_b, v_conv_ln_g, v_conv_ln_b, v_sgu_ln_g, v_sgu_ln_b, v_sgu_w, v_sgu_b, v_out_g_conv, v_out_g_sgu, v_w_out, v_g_ffn, v_w_up, v_ffn_conv_w, v_ffn_conv_b, v_w_down, v_g_final):
    bsz, seq, d = x.shape
    t = bsz * seq
    n_batch = bsz * N_DEV
    ada_sh = w_ada.shape[2]
    n_mod = ada_sh * N_DEV // d
    win_sh = w_in.shape[2]
    kc = conv_w.shape[1]
    dc_sh = conv_w.shape[2]
    dc = dc_sh * N_DEV
    heads, ch = sgu_w.shape[1], sgu_w.shape[2]
    hd = dc // heads
    wout_sh = w_out.shape[1]
    cup = w_up.shape[2]
    kf = ffn_conv_w.shape[1]
    cw_sh = ffn_conv_w.shape[2]
    dff = cw_sh * N_DEV
    n_fs = dff // cup
    assert win_sh * N_DEV == 4 * dc and hd == LANES and kc // 2 < HALO and kf == 3 and 2 * n_fs == N_DEV
    assert w_down.shape[1] * 2 == cup and seq % ch == 0

    xi, yi, ci = _my_pos()
    me = 4 * xi + 2 * yi + ci

    tb = min(256, seq)
    tps = seq // tb
    nt = t // tb
    tbd = min(128, seq)
    tpd = seq // tbd
    ntd = t // tbd
    tm = min(512, seq)
    nm = t // tm

    xf = x.reshape(t, d)
    tgt = loss_target.reshape(t, d)

    small_shapes = [(bsz, d), (kc, dc_sh), (kf, cw_sh)]
    blob0 = _allgather("ag_small_in", _pack([c, conv_w[0], ffn_conv_w[0]]), True)
    per_dev = [_unpack(blob0[s], small_shapes) for s in range(N_DEV)]
    c_all = jnp.concatenate([p[0] for p in per_dev], axis=0)
    conv_w_full = jnp.concatenate([p[1] for p in per_dev], axis=1)
    ffn_w_full = jnp.concatenate([p[2] for p in per_dev], axis=1)
    ffn_w_s = ffn_w_full.reshape(kf, n_fs, cup).transpose(1, 0, 2)
    ffn_b_s = ffn_conv_b.reshape(n_fs, 1, cup)

    b_ada_sh = lax.dynamic_slice(b_ada, (0, me * ada_sh), (1, ada_sh))
    tn_ada = _blk(ada_sh, 512)

    def ada_body(c_ref, w_ref, b_ref, mod_ref, ca_ref):
        cc = c_ref[...]
        ca = (cc * _sigmoid(cc)).astype(BF16)
        ca_ref[...] = ca
        mod_ref[...] = jnp.dot(ca, w_ref[...].astype(BF16), preferred_element_type=F32) + b_ref[...]

    mod_sh, c_act = pl.pallas_call(
        ada_body, name="ada_fwd", grid=(ada_sh // tn_ada,),
        out_shape=(jax.ShapeDtypeStruct((n_batch, ada_sh), F32), jax.ShapeDtypeStruct((n_batch, d), BF16)),
        in_specs=[pl.BlockSpec((n_batch, d), lambda j: (0, 0)), pl.BlockSpec((d, tn_ada), lambda j: (0, j)),
                  pl.BlockSpec((1, tn_ada), lambda j: (0, j))],
        out_specs=(pl.BlockSpec((n_batch, tn_ada), lambda j: (0, j)), pl.BlockSpec((n_batch, d), lambda j: (0, 0))),
        compiler_params=_cp(("arbitrary",)))(c_all, w_ada[0], b_ada_sh)
    mod_all = _allgather("ag_mod", mod_sh, True)
    mod_me = lax.dynamic_slice(mod_all, (0, me * bsz, 0), (N_DEV, bsz, ada_sh))
    mod_me = mod_me.transpose(1, 0, 2).reshape(bsz, n_mod, 1, d)
    shift_m, scale_m, gate_m = mod_me[:, 0], mod_me[:, 1], mod_me[:, 2]
    shift_f, scale_f, gate_f = mod_me[:, 3], mod_me[:, 4], mod_me[:, 5]

    def gather_begin(name, w2d):
        shard = _cast_bf16("cast_" + name, w2d, mod_all)
        land = lax.dynamic_update_slice(lax.empty((N_DEV,) + shard.shape, BF16), shard[None], (me, 0, 0))
        return _gather_start("ag1_" + name, shard, land)

    def gather_end(name, started, after):
        sib_send, sib_recv, ici_send, ici_recv, x_thru, land_thru, _ = started
        x_thru, land = _gather_wait("ag1s_" + name, sib_send, sib_recv, x_thru, land_thru, after, True)
        _, land = _gather_wait("ag1w_" + name, ici_send, ici_recv, x_thru, land, after, False)
        return _gather_forward("ag2_" + name, land)

    def gather_mid(name, started, after):
        sib_send, sib_recv, ici_send, ici_recv, x_thru, land_thru, _ = started
        x_thru, land = _gather_wait("ag1s_" + name, sib_send, sib_recv, x_thru, land_thru, after, True)
        _, land = _gather_wait("ag1w_" + name, ici_send, ici_recv, x_thru, land, after, False)
        fwd_send, fwd_recv, land, token = _forward_start("ag2s_" + name, land)
        return (fwd_send, fwd_recv, land), token[0, 0]

    def gather_finish(name, mid, after):
        fwd_send, fwd_recv, land = mid
        return _forward_wait("ag2w_" + name, fwd_send, fwd_recv, land, after)

    sib = me + 1 - 2 * ci
    flips = [(1 - xi, yi), (xi, 1 - yi), (1 - xi, 1 - yi)]
    ids_near = jnp.stack([me, sib]).astype(jnp.int32)
    ids_ici = jnp.stack([4 * px + 2 * py + ci for px, py in flips]).astype(jnp.int32)
    ids_fwd = jnp.stack([4 * px + 2 * py + 1 - ci for px, py in flips]).astype(jnp.int32)

    def staged_product(name, started, after, part):
        sib_send, sib_recv, ici_send, ici_recv, x_thru, land_thru, _ = started
        x_thru, land = _gather_wait("ag1s_" + name, sib_send, sib_recv, x_thru, land_thru, after, True)
        out = part("near", land, ids_near, None)
        _, land = _gather_wait("ag1w_" + name, ici_send, ici_recv, x_thru, land, out, False)
        fwd_send, fwd_recv, land, _ = _forward_start("ag2s_" + name, land)
        out = part("ici", land, ids_ici, out)
        land = _forward_wait("ag2w_" + name, fwd_send, fwd_recv, land, out)
        out = part("fwd", land, ids_fwd, out)
        return land, out

    w_up_t, m_w_up_t, v_w_up_t = (jnp.swapaxes(a[0], 0, 1) for a in (w_up, m_w_up, v_w_up))
    ag_in = gather_begin("w_in", w_in[0])
    ag_out = gather_begin("w_out", w_out[0])
    ag_up = gather_begin("w_up", w_up_t)
    ag_down = gather_begin("w_down", w_down[0])
    started = ag_in[6][0, 0] + ag_out[6][0, 0] + ag_up[6][0, 0] + ag_down[6][0, 0]

    row_d = pl.BlockSpec((tbd, d), lambda i: (i, 0))
    vec_d = pl.BlockSpec((1, d), lambda i: (0, 0))
    seq_d = pl.BlockSpec((None, 1, d), lambda i: (i // tpd, 0, 0))

    def modulate(name, xin, g, shift, scale):
        def body(x_ref, g_ref, sh_ref, sc_ref, h_ref):
            xx = x_ref[...]
            yy = xx * lax.rsqrt(_mean(xx * xx) + EPS)
            h_ref[...] = ((yy * g_ref[...]) * (1.0 + sc_ref[...]) + sh_ref[...]).astype(BF16)

        return pl.pallas_call(
            body, name=name, grid=(ntd,), out_shape=jax.ShapeDtypeStruct((t, d), BF16),
            in_specs=[row_d, vec_d, seq_d, seq_d], out_specs=row_d, compiler_params=_cp(("parallel",)))(xin, g, shift, scale)

    h1 = modulate("mod1_fwd", xf, g_mix + started, shift_m, scale_m)

    tk_d = _blk(d, 1024)
    def proj_part(tag, land, ids, prev):
        return _mm(
            "mm_proj_" + tag, h1, land, mode="nn", grid=(nm, ids.shape[0], 1),
            a_spec=pl.BlockSpec((tm, d), lambda i, j, k, s: (i, 0)),
            b_spec=pl.BlockSpec((None, d, win_sh), lambda i, j, k, s: (s[j], 0, 0)),
            acc_shape=(tm, win_sh), outs=(jax.ShapeDtypeStruct((t, N_DEV * win_sh), BF16),),
            out_specs=(pl.BlockSpec((tm, win_sh), lambda i, j, k, s: (i, s[j])),), epilogue=_store_bf16,
            ids=ids, prev=prev)[0]

    w_in_g, proj = staged_product("w_in", ag_in, h1, proj_part)

    hb = tb // HALO
    n_hb = t // HALO

    def main_col(col):
        return pl.BlockSpec((tb, dc), lambda i, col=col: (i, col))

    def chunk_col(col):
        return pl.BlockSpec((ch, dc), lambda i, col=col: (i, col))

    def prev_col(col):
        return pl.BlockSpec((HALO, dc), lambda i, col=col: (jnp.maximum(i * hb - 1, 0), col))

    def next_col(col):
        return pl.BlockSpec((HALO, dc), lambda i, col=col: (jnp.minimum((i + 1) * hb, n_hb - 1), col))

    vec_c = pl.BlockSpec((1, dc), lambda i: (0, 0))
    row_c = pl.BlockSpec((tb, dc), lambda i: (i, 0))
    convw_spec = pl.BlockSpec((kc, dc), lambda i: (0, 0))
    sguw_spec = pl.BlockSpec((heads, ch, ch), lambda i: (0, 0, 0))
    bias_spec = pl.BlockSpec((ch, dc), lambda i: (0, 0))
    n_chunk = tb // ch

    sgu_w_bf = sgu_w[0].astype(BF16)
    sgu_wt_bf = jnp.swapaxes(sgu_w[0], 1, 2).astype(BF16)
    sgu_bias = jnp.repeat(sgu_b[0].T, hd, axis=1)

    CL = 2 * LANES
    conv_pieces = [pl.ds(l0, CL) for l0 in range(0, dc, CL)]
    stage_rows = tb + 2 * HALO
    sh_shape = (SUBLANES, stage_rows + SUBLANES, CL)
    padc = kc // 2
    assert dc % CL == 0 and tb % HALO == 0

    def shift_copies(sh):
        sh[0, pl.ds(stage_rows, SUBLANES), :] = jnp.zeros((SUBLANES, CL), F32)
        for r in range(1, SUBLANES):
            sh[r, pl.ds(0, stage_rows), :] = sh[0, pl.ds(r, stage_rows), :]

    def tap_piece(sh, off, row0):
        r = off % SUBLANES
        return sh[r, pl.ds(off - r + row0, SUBLANES), :]

    def stage_glu(sh, i, cols, pv, pg, pvp, pgp, pvn, pgn):
        first = (i % tps) == 0
        last = (i % tps) == tps - 1

        def glu(v_ref, g_ref, rows):
            return v_ref[rows, cols].astype(F32) * _sigmoid(g_ref[rows, cols].astype(F32))

        whole = pl.ds(0, HALO)
        sh[0, pl.ds(0, HALO), :] = jnp.where(first, 0.0, glu(pvp, pgp, whole))
        for rr in range(tb // HALO):
            sh[0, pl.ds(HALO + rr * HALO, HALO), :] = glu(pv, pg, pl.ds(rr * HALO, HALO))
        sh[0, pl.ds(HALO + tb, HALO), :] = jnp.where(last, 0.0, glu(pvn, pgn, whole))
        shift_copies(sh)

    def stage_taps(wb, cw, cols):
        for k in range(kc):
            wb[k] = jnp.broadcast_to(cw[pl.ds(k, 1), cols], (SUBLANES, CL))

    def layer_norm_stats(z):
        mu = _mean(z)
        zc = z - mu
        rstd = lax.rsqrt(_mean(zc * zc) + EPS)
        return zc * rstd, rstd

    def sgu_mix(vs_ref, w_ref, vnb, bias_ref, n_chunks):
        for cc in range(n_chunks):
            for h in range(heads):
                blk = jnp.dot(w_ref[h], vnb[cc * ch:(cc + 1) * ch, h * hd:(h + 1) * hd], preferred_element_type=F32)
                vs_ref[pl.ds(cc * ch, ch), pl.ds(h * hd, hd)] = blk + bias_ref[:, pl.ds(h * hd, hd)]

    def mix_fwd_body(pv, pg, pu, pw, pvp, pgp, pvn, pgn, cw, cb, lg, lb, og, slg, slb, sw, sbias, sog, y_ref, a1_ref, sh_e, wb, vs_ref):
        i = pl.program_id(0)
        for cols in conv_pieces:
            stage_glu(sh_e, i, cols, pv, pg, pvp, pgp, pvn, pgn)
            stage_taps(wb, cw, cols)
            bias = jnp.broadcast_to(cb[:, cols], (SUBLANES, CL))
            for row0 in range(0, tb, SUBLANES):
                acc = bias
                for k in range(kc):
                    acc = acc + wb[k] * tap_piece(sh_e, HALO - padc + k, row0)
                a1_ref[pl.ds(row0, SUBLANES), cols] = acc
        xh, _ = layer_norm_stats(a1_ref[...])
        a2 = xh * lg[...] + lb[...]
        a3 = a2 * _sigmoid(a2)
        ya = a3 * lax.rsqrt(_mean(a3 * a3) + EPS) * og[...]
        y_ref[:, pl.ds(0, dc)] = ya.astype(BF16)

        u = _gelu(pu[...].astype(F32))
        vv = _gelu(pw[...].astype(F32))
        xhv, _ = layer_norm_stats(vv)
        vn = xhv * slg[...] + slb[...]
        sgu_mix(vs_ref, sw, vn.astype(BF16), sbias, n_chunk)
        bg = u * vs_ref[...]
        yb = bg * lax.rsqrt(_mean(bg * bg) + EPS) * sog[...]
        y_ref[:, pl.ds(dc, dc)] = yb.astype(BF16)

    out_mid, out_token = gather_mid("w_out", ag_out, proj)
    y, a1 = pl.pallas_call(
        mix_fwd_body, name="mix_fwd", grid=(nt,),
        out_shape=(jax.ShapeDtypeStruct((t, 2 * dc), BF16), jax.ShapeDtypeStruct((t, dc), F32)),
        in_specs=[main_col(0), main_col(1), main_col(2), main_col(3), prev_col(0), prev_col(1), next_col(0), next_col(1),
                  convw_spec, vec_c, vec_c, vec_c, vec_c, vec_c, vec_c, sguw_spec, bias_spec, vec_c],
        out_specs=(pl.BlockSpec((tb, 2 * dc), lambda i: (i, 0)), row_c),
        scratch_shapes=[pltpu.VMEM(sh_shape, F32), pltpu.VMEM((kc, SUBLANES, CL), F32), pltpu.VMEM((tb, dc), F32)],
        compiler_params=_cp(("parallel",)),
    )(proj, proj, proj, proj, proj, proj, proj, proj, conv_w_full, conv_b + out_token, conv_ln_g, conv_ln_b, out_g_conv,
      sgu_ln_g, sgu_ln_b, sgu_w_bf, sgu_bias, out_g_sgu)

    tn_d = _blk(d, 1024)
    gate_spec3 = pl.BlockSpec((None, 1, tn_d), lambda i, j, k: (i * tm // seq, 0, j))
    res_spec3 = pl.BlockSpec((tm, tn_d), lambda i, j, k: (i, j))
    td_shape = jax.ShapeDtypeStruct((t, d), F32)

    w_out_full = gather_finish("w_out", out_mid, y).reshape(d, d)
    o_mix, x1 = _mm(
        "mm_out", y, w_out_full, mode="nn", grid=(nm, d // tn_d, 1),
        a_spec=pl.BlockSpec((tm, d), lambda i, j, k: (i, 0)), b_spec=pl.BlockSpec((d, tn_d), lambda i, j, k: (0, j)),
        acc_shape=(tm, tn_d), outs=(td_shape, td_shape), out_specs=(res_spec3, res_spec3),
        epilogue=_residual_epilogue, extra=(xf, gate_m), extra_specs=(res_spec3, gate_spec3))

    h2 = modulate("mod2_fwd", x1, g_ffn, shift_f, scale_f)

    tm_up = min(1024, seq)
    nk_d = d // tk_d
    def up_part(tag, land, ids, prev):
        return _mm(
            "mm_up_" + tag, h2, land, mode="nt", grid=(t // tm_up, ids.shape[0], nk_d),
            a_spec=pl.BlockSpec((tm_up, tk_d), lambda i, j, k, s: (i, k)),
            b_spec=pl.BlockSpec((None, cup, tk_d), lambda i, j, k, s: (s[j], 0, k)),
            acc_shape=(tm_up, cup), outs=(jax.ShapeDtypeStruct((N_DEV, t, cup), BF16),),
            out_specs=(pl.BlockSpec((None, tm_up, cup), lambda i, j, k, s: (s[j], i, 0)),), epilogue=_store_bf16,
            ids=ids, prev=prev)[0]

    w_up_g, up = staged_product("w_up", ag_up, h2, up_part)
    up4 = up.reshape(2, n_fs, t, cup)

    def ffn_halo(which, lo):
        if lo:
            return pl.BlockSpec((None, None, HALO, cup), lambda j, i: (which, j, jnp.maximum(i * hb - 1, 0), 0))
        return pl.BlockSpec((None, None, HALO, cup), lambda j, i: (which, j, jnp.minimum((i + 1) * hb, n_hb - 1), 0))

    pair_spec = pl.BlockSpec((2, None, tb, cup), lambda j, i: (0, j, i, 0))
    fw_spec = pl.BlockSpec((None, kf, cup), lambda j, i: (j, 0, 0))
    fb_spec = pl.BlockSpec((None, 1, cup), lambda j, i: (j, 0, 0))
    act_spec = pl.BlockSpec((None, tb, cup), lambda j, i: (j, i, 0))

    RC = HALO
    LC = 2 * LANES
    pieces = [(l0, min(LC, cup - l0)) for l0 in range(0, cup, LC)]
    n_rc = tb // RC
    pad = SUBLANES

    def fill_gate_ext(ext, i, cols, gate_ref, prev_ref, next_ref):
        first = (i % tps) == 0
        last = (i % tps) == tps - 1
        lw = cols.size
        ext[pl.ds(0, pad), cols] = jnp.zeros((pad, lw), F32)
        ext[pl.ds(pad, HALO), cols] = jnp.where(first, 0.0, prev_ref[:, cols].astype(F32))
        for rr in range(n_rc):
            ext[pl.ds(pad + HALO + rr * RC, RC), cols] = gate_ref[0, pl.ds(rr * RC, RC), cols].astype(F32)
        ext[pl.ds(pad + HALO + tb, HALO), cols] = jnp.where(last, 0.0, next_ref[:, cols].astype(F32))
        ext[pl.ds(pad + 2 * HALO + tb, pad), cols] = jnp.zeros((pad, lw), F32)

    def tap_rows(fw, fb, cols):
        lw = cols.size
        return [jnp.broadcast_to(fw[pl.ds(k, 1), cols], (RC, lw)) for k in range(kf)] + [jnp.broadcast_to(fb[:, cols], (RC, lw))]

    def ffn_fwd_body(pair, gp, gn, fw, fb, act_ref, ext):
        i = pl.program_id(1)
        for l0, lw in pieces:
            cols = pl.ds(l0, lw)
            fill_gate_ext(ext, i, cols, pair, gp, gn)
            w0, w1, w2, bb = tap_rows(fw, fb, cols)
            for rr in range(n_rc):
                e0 = pad + HALO + rr * RC
                gte = bb + w0 * ext[pl.ds(e0 - 1, RC), cols]
                gte = gte + w1 * ext[pl.ds(e0, RC), cols]
                gte = gte + w2 * ext[pl.ds(e0 + 1, RC), cols]
                val = pair[1, pl.ds(rr * RC, RC), cols].astype(F32)
                act_ref[pl.ds(rr * RC, RC), cols] = (gte * _sigmoid(gte) * val).astype(BF16)

    ext_rows = tb + 2 * HALO + 2 * pad
    down_mid, down_token = gather_mid("w_down", ag_down, up)
    act = pl.pallas_call(
        ffn_fwd_body, name="ffn_fwd", grid=(n_fs, nt), out_shape=jax.ShapeDtypeStruct((n_fs, t, cup), BF16),
        in_specs=[pair_spec, ffn_halo(0, True), ffn_halo(0, False), fw_spec, fb_spec], out_specs=act_spec,
        scratch_shapes=[pltpu.VMEM((ext_rows, cup), F32)], compiler_params=_cp(("parallel", "parallel")),
    )(up4, up4, up4, ffn_w_s, ffn_b_s + down_token)

    w_down_s = gather_finish("w_down", down_mid, act).reshape(n_fs, cup, d)
    gate_spec_f = pl.BlockSpec((None, 1, tn_d), lambda i, j, k: (i * tm_up // seq, 0, j))
    res_spec_f = pl.BlockSpec((tm_up, tn_d), lambda i, j, k: (i, j))
    dn, x2 = _mm(
        "mm_down", act, w_down_s, mode="nn", grid=(t // tm_up, d // tn_d, n_fs),
        a_spec=pl.BlockSpec((None, tm_up, cup), lambda i, j, k: (k, i, 0)),
        b_spec=pl.BlockSpec((None, cup, tn_d), lambda i, j, k: (k, 0, j)),
        acc_shape=(tm_up, tn_d), outs=(td_shape, td_shape), out_specs=(res_spec_f, res_spec_f),
        epilogue=_residual_epilogue, extra=(x1, gate_f), extra_specs=(res_spec_f, gate_spec_f))

    acc_d = pl.BlockSpec((1, d), lambda i: (0, 0))
    seq_acc = pl.BlockSpec((None, 1, d), lambda i: (i // tpd, 0, 0))

    def head_body(x_ref, t_ref, g_ref, dn_ref, gate_ref, dx_ref, ddn_ref, loss_ref, dg_ref, dgate_ref):
        i = pl.program_id(0)
        xx = x_ref[...]
        rr = lax.rsqrt(_mean(xx * xx) + EPS)
        xn = xx * rr
        err = xn * g_ref[...] - t_ref[...]
        dyf = err * (1.0 / d)
        dxn = dyf * g_ref[...]
        dx = rr * (dxn - xn * _mean(dxn * xn))
        dx_ref[...] = dx
        ddn_ref[...] = (gate_ref[...] * dx).astype(BF16)
        part = 0.5 * jnp.sum(_mean(err * err), axis=0, keepdims=True)

        @pl.when(i == 0)
        def _():
            loss_ref[...] = jnp.zeros_like(loss_ref)
            dg_ref[...] = jnp.zeros_like(dg_ref)

        @pl.when(i % tpd == 0)
        def _():
            dgate_ref[...] = jnp.zeros_like(dgate_ref)

        loss_ref[...] += jnp.broadcast_to(part, loss_ref.shape)
        dg_ref[...] += _colsum(dyf * xn)
        dgate_ref[...] += _colsum(dx * dn_ref[...])

    seq_shape = jax.ShapeDtypeStruct((bsz, 1, d), F32)
    vec_shape = jax.ShapeDtypeStruct((1, d), F32)
    dx2, ddn, loss_part, dg_final, dgate_f = pl.pallas_call(
        head_body, name="loss_head", grid=(ntd,),
        out_shape=(td_shape, jax.ShapeDtypeStruct((t, d), BF16), jax.ShapeDtypeStruct((1, LANES), F32), vec_shape, seq_shape),
        in_specs=[row_d, row_d, vec_d, row_d, seq_d],
        out_specs=(row_d, row_d, pl.BlockSpec((1, LANES), lambda i: (0, 0)), acc_d, seq_acc),
        compiler_params=_cp(("arbitrary",)))(x2, tgt, g_final.reshape(1, d), dn, gate_f)

    tk_t = min(1024, t)
    nk_t = t // tk_t
    tn_gw = _blk(d, 1024)
    gw_down = _mm(
        "mm_gw_down", act, ddn, mode="tn", grid=(n_fs, d // tn_gw, nk_t),
        a_spec=pl.BlockSpec((None, tk_t, cup), lambda j, n, k: (j, k, 0)),
        b_spec=pl.BlockSpec((tk_t, tn_gw), lambda j, n, k: (k, n)),
        acc_shape=(cup, tn_gw), outs=(jax.ShapeDtypeStruct((n_fs, cup, d), BF16),),
        out_specs=(pl.BlockSpec((None, cup, tn_gw), lambda j, n, k: (j, 0, n)),), epilogue=_store_bf16)[0]

    def exchange_begin(name, gw):
        return _alltoall_start("a2a_" + name, gw.reshape(N_DEV, gw.size // (N_DEV * gw.shape[-1]), gw.shape[-1]))

    x_down = exchange_begin("w_down", gw_down)
    dact = _mm(
        "mm_dact", ddn, w_down_s, mode="nt", grid=(t // tm_up, n_fs, nk_d),
        a_spec=pl.BlockSpec((tm_up, tk_d), lambda i, j, k: (i, k)),
        b_spec=pl.BlockSpec((None, cup, tk_d), lambda i, j, k: (j, 0, k)),
        acc_shape=(tm_up, cup), outs=(jax.ShapeDtypeStruct((n_fs, t, cup), BF16),),
        out_specs=(pl.BlockSpec((None, tm_up, cup), lambda i, j, k: (j, i, 0)),), epilogue=_store_bf16, after=(x_down[4],))[0]


    def ffn_bwd_body(pair, gp, gn, vp, vn, da, dap, dan, fw, fb, dup_ref, dw_ref, db_ref, ext, dg_e):
        i = pl.program_id(1)
        first = (i % tps) == 0
        last = (i % tps) == tps - 1

        @pl.when(i == 0)
        def _():
            dw_ref[...] = jnp.zeros_like(dw_ref)
            db_ref[...] = jnp.zeros_like(db_ref)

        for l0, lw in pieces:
            cols = pl.ds(l0, lw)
            fill_gate_ext(ext, i, cols, pair, gp, gn)
            w0, w1, w2, bb = tap_rows(fw, fb, cols)
            sums = [jnp.zeros((RC, lw), F32) for _ in range(kf + 1)]
            for cc in range(n_rc + 2):
                e0 = pad + cc * RC
                taps = [ext[pl.ds(e0 - 1, RC), cols], ext[pl.ds(e0, RC), cols], ext[pl.ds(e0 + 1, RC), cols]]
                gte = bb + w0 * taps[0]
                gte = gte + w1 * taps[1]
                gte = gte + w2 * taps[2]
                sg = _sigmoid(gte)
                if cc == 0:
                    val, dact_v = vp[:, cols], dap[:, cols]
                elif cc == n_rc + 1:
                    val, dact_v = vn[:, cols], dan[:, cols]
                else:
                    rows = pl.ds((cc - 1) * RC, RC)
                    val, dact_v = pair[1, rows, cols], da[rows, cols]
                val, dact_v = val.astype(F32), dact_v.astype(F32)
                dgte = dact_v * val * (sg * (1.0 + gte * (1.0 - sg)))
                if cc == 0:
                    dgte = jnp.where(first, 0.0, dgte)
                elif cc == n_rc + 1:
                    dgte = jnp.where(last, 0.0, dgte)
                else:
                    dup_ref[1, rows, cols] = (dact_v * (gte * sg)).astype(BF16)
                    for k in range(kf):
                        sums[k] = sums[k] + dgte * taps[k]
                    sums[kf] = sums[kf] + dgte
                dg_e[pl.ds(cc * RC, RC), cols] = dgte
            for k in range(kf):
                dw_ref[pl.ds(k, 1), cols] += _colsum(sums[k])
            db_ref[:, cols] += _colsum(sums[kf])
            for rr in range(n_rc):
                e0 = HALO + rr * RC
                dgate = w0 * dg_e[pl.ds(e0 + 1, RC), cols] + w1 * dg_e[pl.ds(e0, RC), cols] + w2 * dg_e[pl.ds(e0 - 1, RC), cols]
                dup_ref[0, pl.ds(rr * RC, RC), cols] = dgate.astype(BF16)

    def act_halo(lo):
        if lo:
            return pl.BlockSpec((None, HALO, cup), lambda j, i: (j, jnp.maximum(i * hb - 1, 0), 0))
        return pl.BlockSpec((None, HALO, cup), lambda j, i: (j, jnp.minimum((i + 1) * hb, n_hb - 1), 0))

    dup4, g_ffn_w_s, g_ffn_b_s = pl.pallas_call(
        ffn_bwd_body, name="ffn_bwd", grid=(n_fs, nt),
        out_shape=(jax.ShapeDtypeStruct((2, n_fs, t, cup), BF16), jax.ShapeDtypeStruct((n_fs, kf, cup), F32),
                   jax.ShapeDtypeStruct((n_fs, 1, cup), F32)),
        in_specs=[pair_spec, ffn_halo(0, True), ffn_halo(0, False), ffn_halo(1, True), ffn_halo(1, False),
                  act_spec, act_halo(True), act_halo(False), fw_spec, fb_spec],
        out_specs=(pair_spec, fw_spec, fb_spec),
        scratch_shapes=[pltpu.VMEM((ext_rows, cup), F32), pltpu.VMEM((tb + 2 * HALO, cup), F32)],
        compiler_params=_cp(("parallel", "arbitrary")),
    )(up4, up4, up4, up4, up4, dact, dact, dact, ffn_w_s, ffn_b_s)
    dup = dup4.reshape(N_DEV, t, cup)

    tm_w = _blk(d, 1024)
    gw_up = _mm(
        "mm_gw_up", dup, h2, mode="tn", grid=(N_DEV, d // tn_gw, nk_t),
        a_spec=pl.BlockSpec((None, tk_t, cup), lambda j, n, k: (j, k, 0)),
        b_spec=pl.BlockSpec((tk_t, tn_gw), lambda j, n, k: (k, n)),
        acc_shape=(cup, tn_gw), outs=(jax.ShapeDtypeStruct((N_DEV, cup, d), BF16),),
        out_specs=(pl.BlockSpec((None, cup, tn_gw), lambda j, n, k: (j, 0, n)),), epilogue=_store_bf16)[0]

    x_up = exchange_begin("w_up", gw_up)
    dh2 = _mm(
        "mm_dh2", dup, w_up_g, mode="nn", grid=(t // tm_up, d // tn_d, N_DEV),
        a_spec=pl.BlockSpec((None, tm_up, cup), lambda i, j, k: (k, i, 0)),
        b_spec=pl.BlockSpec((None, cup, tn_d), lambda i, j, k: (k, 0, j)),
        acc_shape=(tm_up, tn_d), outs=(td_shape,), out_specs=(pl.BlockSpec((tm_up, tn_d), lambda i, j, k: (i, j)),),
        epilogue=_store_f32, after=(x_up[4],))[0]

    def modulate_bwd(name, xin, dh, dres, g, scale, gate=None, branch=None):
        gated = gate is not None

        def body(*refs):
            x_ref, dh_ref, dres_ref, g_ref, sc_ref = refs[:5]
            rest = refs[5:]
            if gated:
                gate_ref, br_ref, dx_ref, dsh_ref, dsc_ref, dg_ref, do_ref, dgate_ref = rest
            else:
                dx_ref, dsh_ref, dsc_ref, dg_ref = rest
            i = pl.program_id(0)
            xx = x_ref[...]
            rr = lax.rsqrt(_mean(xx * xx) + EPS)
            xn = xx * rr
            s1 = 1.0 + sc_ref[...]
            dhh = dh_ref[...]
            dxn = dhh * g_ref[...] * s1
            dx = dres_ref[...] + rr * (dxn - xn * _mean(dxn * xn))
            dx_ref[...] = dx

            @pl.when(i == 0)
            def _():
                dg_ref[...] = jnp.zeros_like(dg_ref)

            @pl.when(i % tpd == 0)
            def _():
                dsh_ref[...] = jnp.zeros_like(dsh_ref)
                dsc_ref[...] = jnp.zeros_like(dsc_ref)
                if gated:
                    dgate_ref[...] = jnp.zeros_like(dgate_ref)

            dsh_ref[...] += _colsum(dhh)
            dsc_ref[...] += _colsum(dhh * (xn * g_ref[...]))
            dg_ref[...] += _colsum(dhh * s1 * xn)
            if gated:
                do_ref[...] = (gate_ref[...] * dx).astype(BF16)
                dgate_ref[...] += _colsum(dx * br_ref[...])

        ins = [xin, dh, dres, g, scale]
        in_specs = [row_d, row_d, row_d, vec_d, seq_d]
        outs = [td_shape, seq_shape, seq_shape, vec_shape]
        out_specs = [row_d, seq_acc, seq_acc, acc_d]
        if gated:
            ins += [gate, branch]
            in_specs += [seq_d, row_d]
            outs += [jax.ShapeDtypeStruct((t, d), BF16), seq_shape]
            out_specs += [row_d, seq_acc]
        return pl.pallas_call(
            body, name=name, grid=(ntd,), out_shape=tuple(outs), in_specs=in_specs, out_specs=tuple(out_specs),
            compiler_params=_cp(("arbitrary",)))(*ins)

    dx1, dshift_f, dscale_f, dg_ffn, d_o, dgate_m = modulate_bwd("mod2_bwd", x1, dh2, dx2, g_ffn, scale_f, gate_m, o_mix)

    tn_w = _blk(d, 2048)
    gw_out = _mm(
        "mm_gw_out", y, d_o, mode="tn", grid=(d // tm_w, d // tn_w, nk_t),
        a_spec=pl.BlockSpec((tk_t, tm_w), lambda i, j, k: (k, i)), b_spec=pl.BlockSpec((tk_t, tn_w), lambda i, j, k: (k, j)),
        acc_shape=(tm_w, tn_w), outs=(jax.ShapeDtypeStruct((d, d), BF16),),
        out_specs=(pl.BlockSpec((tm_w, tn_w), lambda i, j, k: (i, j)),), epilogue=_store_bf16)[0]

    x_out = exchange_begin("w_out", gw_out)
    dy = _mm(
        "mm_dy", d_o, w_out_full, mode="nt", grid=(nm, d // tn_d, 1),
        a_spec=pl.BlockSpec((tm, d), lambda i, j, k: (i, 0)), b_spec=pl.BlockSpec((tn_d, d), lambda i, j, k: (j, 0)),
        acc_shape=(tm, tn_d), outs=(jax.ShapeDtypeStruct((t, d), BF16),),
        out_specs=(pl.BlockSpec((tm, tn_d), lambda i, j, k: (i, j)),), epilogue=_store_bf16, after=(x_out[4],))[0]

    def conv_norm_bwd_body(dy_ref, a1_ref, lg, lb, og, da1_ref, dog_ref, dlg_ref, dlb_ref):
        i = pl.program_id(0)
        xh, rstd = layer_norm_stats(a1_ref[...])
        a2 = xh * lg[...] + lb[...]
        sg = _sigmoid(a2)
        a3 = a2 * sg
        r3 = lax.rsqrt(_mean(a3 * a3) + EPS)
        n3 = a3 * r3
        dya = dy_ref[...].astype(F32)
        dn3 = dya * og[...]
        da3 = r3 * (dn3 - n3 * _mean(dn3 * n3))
        da2 = da3 * (sg * (1.0 + a2 * (1.0 - sg)))
        dxh = da2 * lg[...]
        da1_ref[...] = rstd * (dxh - _mean(dxh) - xh * _mean(dxh * xh))

        @pl.when(i == 0)
        def _():
            dog_ref[...] = jnp.zeros_like(dog_ref)
            dlg_ref[...] = jnp.zeros_like(dlg_ref)
            dlb_ref[...] = jnp.zeros_like(dlb_ref)

        dog_ref[...] += _colsum(dya * n3)
        dlg_ref[...] += _colsum(da2 * xh)
        dlb_ref[...] += _colsum(da2)

    vecc_shape = jax.ShapeDtypeStruct((1, dc), F32)
    da1, g_og_conv, g_cln_g, g_cln_b = pl.pallas_call(
        conv_norm_bwd_body, name="conv_norm_bwd", grid=(nt,),
        out_shape=(jax.ShapeDtypeStruct((t, dc), F32), vecc_shape, vecc_shape, vecc_shape),
        in_specs=[main_col(0), row_c, vec_c, vec_c, vec_c], out_specs=(row_c, vec_c, vec_c, vec_c),
        compiler_params=_cp(("arbitrary",)))(dy, a1, conv_ln_g, conv_ln_b, out_g_conv)

    def sgu_bwd_body(dy_ref, pu, pw, slg, slb, sw, swt, sbias, sog, dproj_ref, dsog_ref, dslg_ref, dslb_ref, dsw_ref, dsb_ref, vs_ref, dvn_ref):
        i = pl.program_id(0)
        zu = pu[...].astype(F32)
        zv = pw[...].astype(F32)
        u = _gelu(zu)
        vv = _gelu(zv)
        xhv, rstd = layer_norm_stats(vv)
        vnb = (xhv * slg[...] + slb[...]).astype(BF16)
        sgu_mix(vs_ref, sw, vnb, sbias, 1)
        vs = vs_ref[...]
        bg = u * vs
        rb = lax.rsqrt(_mean(bg * bg) + EPS)
        nb = bg * rb
        dyb = dy_ref[...].astype(F32)
        dnb = dyb * sog[...]
        dbg = rb * (dnb - nb * _mean(dnb * nb))
        du = dbg * vs
        dvs = dbg * u
        dvsb = dvs.astype(BF16)

        @pl.when(i == 0)
        def _():
            dsog_ref[...] = jnp.zeros_like(dsog_ref)
            dslg_ref[...] = jnp.zeros_like(dslg_ref)
            dslb_ref[...] = jnp.zeros_like(dslb_ref)
            dsw_ref[...] = jnp.zeros_like(dsw_ref)
            dsb_ref[...] = jnp.zeros_like(dsb_ref)

        dsog_ref[...] += _colsum(dyb * nb)
        dsb_ref[...] += dvs
        for h in range(heads):
            dblk = dvsb[:, h * hd:(h + 1) * hd]
            vblk = vnb[:, h * hd:(h + 1) * hd]
            dsw_ref[h] += lax.dot_general(dblk, vblk, (((1,), (1,)), ((), ())), preferred_element_type=F32)
            dvn_ref[:, pl.ds(h * hd, hd)] = jnp.dot(swt[h], dblk, preferred_element_type=F32)
        dvn = dvn_ref[...]
        dslg_ref[...] += _colsum(dvn * xhv)
        dslb_ref[...] += _colsum(dvn)
        dxh = dvn * slg[...]
        dvv = rstd * (dxh - _mean(dxh) - xhv * _mean(dxh * xhv))
        dproj_ref[:, pl.ds(0, dc)] = (du * _gelu_grad(zu)).astype(BF16)
        dproj_ref[:, pl.ds(dc, dc)] = (dvv * _gelu_grad(zv)).astype(BF16)

    dproj, g_og_sgu, g_sln_g, g_sln_b, g_sgu_w, g_sgu_bias = pl.pallas_call(
        sgu_bwd_body, name="sgu_bwd", grid=(t // ch,),
        out_shape=(jax.ShapeDtypeStruct((t, 4 * dc), BF16), vecc_shape, vecc_shape, vecc_shape,
                   jax.ShapeDtypeStruct((heads, ch, ch), F32), jax.ShapeDtypeStruct((ch, dc), F32)),
        in_specs=[chunk_col(1), chunk_col(2), chunk_col(3), vec_c, vec_c, sguw_spec, sguw_spec, bias_spec, vec_c],
        out_specs=(pl.BlockSpec((ch, 2 * dc), lambda i: (i, 1)), vec_c, vec_c, vec_c, sguw_spec, bias_spec),
        scratch_shapes=[pltpu.VMEM((ch, dc), F32), pltpu.VMEM((ch, dc), F32)],
        compiler_params=_cp(("arbitrary",)),
    )(dy, proj, proj, sgu_ln_g, sgu_ln_b, sgu_w_bf, sgu_wt_bf, sgu_bias, out_g_sgu)

    def sgu_bias_reduce_body(b_ref, o_ref):
        lane = lax.broadcasted_iota(jnp.int32, (ch, LANES), 1)
        res = jnp.zeros((ch, LANES), F32)
        for h in range(heads):
            res = jnp.where(lane == h, jnp.sum(b_ref[:, pl.ds(h * hd, hd)], axis=1, keepdims=True), res)
        o_ref[...] = res

    g_sgu_b_t = pl.pallas_call(
        sgu_bias_reduce_body, name="sgu_bias_reduce", out_shape=jax.ShapeDtypeStruct((ch, LANES), F32),
        in_specs=[pl.BlockSpec(memory_space=pltpu.VMEM)], out_specs=pl.BlockSpec(memory_space=pltpu.VMEM),
        compiler_params=_cp())(g_sgu_bias)
    g_sgu_b = g_sgu_b_t[:, :heads].T

    def conv_bwd_body(da, dap, dan, pv, pg, pvp, pgp, pvn, pgn, cw, dproj_in, dproj_ref, dcw_ref, dcb_ref, sh_e, sh_d, wb):
        del dproj_in
        i = pl.program_id(0)
        first = (i % tps) == 0
        last = (i % tps) == tps - 1

        @pl.when(i == 0)
        def _():
            dcw_ref[...] = jnp.zeros_like(dcw_ref)
            dcb_ref[...] = jnp.zeros_like(dcb_ref)

        for cols in conv_pieces:
            stage_glu(sh_e, i, cols, pv, pg, pvp, pgp, pvn, pgn)
            sh_d[0, pl.ds(0, HALO), :] = jnp.where(first, 0.0, dap[:, cols])
            for rr in range(tb // HALO):
                sh_d[0, pl.ds(HALO + rr * HALO, HALO), :] = da[pl.ds(rr * HALO, HALO), cols]
            sh_d[0, pl.ds(HALO + tb, HALO), :] = jnp.where(last, 0.0, dan[:, cols])
            shift_copies(sh_d)
            stage_taps(wb, cw, cols)
            for rr in range(tb // HALO):
                halves = []
                for row0 in (rr * HALO, rr * HALO + SUBLANES):
                    acc = jnp.zeros((SUBLANES, CL), F32)
                    for k in range(kc):
                        acc = acc + wb[k] * tap_piece(sh_d, HALO + padc - k, row0)
                    halves.append(acc)
                da0 = jnp.concatenate(halves, axis=0)
                rows = pl.ds(rr * HALO, HALO)
                vv = pv[rows, cols].astype(F32)
                sg = _sigmoid(pg[rows, cols].astype(F32))
                dproj_ref[rows, cols] = (da0 * sg).astype(BF16)
                dproj_ref[rows, pl.ds(dc + cols.start, CL)] = (da0 * vv * sg * (1.0 - sg)).astype(BF16)
            for k in range(kc):
                acc = jnp.zeros((SUBLANES, CL), F32)
                for row0 in range(0, tb, SUBLANES):
                    acc = acc + da[pl.ds(row0, SUBLANES), cols] * tap_piece(sh_e, HALO - padc + k, row0)
                dcw_ref[pl.ds(k, 1), cols] += _colsum(acc)
            acc = jnp.zeros((SUBLANES, CL), F32)
            for row0 in range(0, tb, SUBLANES):
                acc = acc + da[pl.ds(row0, SUBLANES), cols]
            dcb_ref[:, cols] += _colsum(acc)

    def halo_rows(lo):
        if lo:
            return pl.BlockSpec((HALO, dc), lambda i: (jnp.maximum(i * hb - 1, 0), 0))
        return pl.BlockSpec((HALO, dc), lambda i: (jnp.minimum((i + 1) * hb, n_hb - 1), 0))

    dproj, g_conv_w, g_conv_b = pl.pallas_call(
        conv_bwd_body, name="conv_bwd", grid=(nt,),
        out_shape=(jax.ShapeDtypeStruct((t, 4 * dc), BF16), jax.ShapeDtypeStruct((kc, dc), F32), vecc_shape),
        in_specs=[row_c, halo_rows(True), halo_rows(False), main_col(0), main_col(1), prev_col(0), prev_col(1),
                  next_col(0), next_col(1), convw_spec, pl.BlockSpec(memory_space=pl.ANY)],
        out_specs=(pl.BlockSpec((tb, 2 * dc), lambda i: (i, 0)), convw_spec, vec_c),
        scratch_shapes=[pltpu.VMEM(sh_shape, F32), pltpu.VMEM(sh_shape, F32), pltpu.VMEM((kc, SUBLANES, CL), F32)],
        input_output_aliases={10: 0}, compiler_params=_cp(("arbitrary",)),
    )(da1, da1, da1, proj, proj, proj, proj, proj, proj, conv_w_full, dproj)

    tm_wi = _blk(d, 2048)
    gw_in = _mm(
        "mm_gw_in", h1, dproj, mode="tn", grid=(N_DEV, d // tm_wi, nk_t),
        a_spec=pl.BlockSpec((tk_t, tm_wi), lambda j, i, k: (k, i)),
        b_spec=pl.BlockSpec((tk_t, win_sh), lambda j, i, k: (k, j)),
        acc_shape=(tm_wi, win_sh), outs=(jax.ShapeDtypeStruct((N_DEV, d, win_sh), BF16),),
        out_specs=(pl.BlockSpec((None, tm_wi, win_sh), lambda j, i, k: (j, i, 0)),), epilogue=_store_bf16)[0]

    x_in = exchange_begin("w_in", gw_in)
    tn_h = _blk(d, 2048)
    dh1 = _mm(
        "mm_dh1", dproj, w_in_g, mode="nt", grid=(t // tm_up, d // tn_h, N_DEV),
        a_spec=pl.BlockSpec((tm_up, win_sh), lambda i, j, k: (i, k)),
        b_spec=pl.BlockSpec((None, tn_h, win_sh), lambda i, j, k: (k, j, 0)),
        acc_shape=(tm_up, tn_h), outs=(td_shape,), out_specs=(pl.BlockSpec((tm_up, tn_h), lambda i, j, k: (i, j)),),
        epilogue=_store_f32, after=(x_in[4],))[0]

    grad_x, dshift_m, dscale_m, dg_mix = modulate_bwd("mod1_bwd", xf, dh1, dx1, g_mix, scale_m)

    dmod = jnp.concatenate([dshift_m, dscale_m, dgate_m, dshift_f, dscale_f, dgate_f], axis=1)
    g_ffn_w_full = g_ffn_w_s.transpose(1, 0, 2).reshape(kf, dff)
    rep_names = ["g_mix", "conv_b", "conv_ln_g", "conv_ln_b", "sgu_ln_g", "sgu_ln_b", "sgu_w", "sgu_b",
                 "out_g_conv", "out_g_sgu", "g_ffn", "ffn_conv_b", "g_final"]
    rep_w = [g_mix, conv_b, conv_ln_g, conv_ln_b, sgu_ln_g, sgu_ln_b, sgu_w, sgu_b, out_g_conv, out_g_sgu, g_ffn, ffn_conv_b, g_final]
    rep_m = [m_g_mix, m_conv_b, m_conv_ln_g, m_conv_ln_b, m_sgu_ln_g, m_sgu_ln_b, m_sgu_w, m_sgu_b, m_out_g_conv, m_out_g_sgu,
             m_g_ffn, m_ffn_conv_b, m_g_final]
    rep_v = [v_g_mix, v_conv_b, v_conv_ln_g, v_conv_ln_b, v_sgu_ln_g, v_sgu_ln_b, v_sgu_w, v_sgu_b, v_out_g_conv, v_out_g_sgu,
             v_g_ffn, v_ffn_conv_b, v_g_final]
    rep_g = [dg_mix, g_conv_b, g_cln_g, g_cln_b, g_sln_g, g_sln_b, g_sgu_w, g_sgu_b, g_og_conv, g_og_sgu, dg_ffn,
             g_ffn_b_s, dg_final]
    rep_shapes = [w.shape for w in rep_w]
    extra_g = [g_conv_w, g_ffn_w_full, loss_part[:, :1]]
    extra_shapes = [(kc, dc), (kf, dff), (1, 1)]
    sum_blob = _pack(rep_g + extra_g)
    n_sum_rows = sum_blob.shape[0]
    blob1 = jnp.concatenate([sum_blob, _pack([dmod])], axis=0)
    small_land = lax.dynamic_update_slice(lax.empty((N_DEV,) + blob1.shape, F32), blob1[None], (me, 0, 0))
    small_started = _gather_start("ag1_small_grads", blob1, small_land)

    def big(name, started, after, w2d, m2d, v2d):
        send_sems, recv_sems, x_thru, land_thru, _ = started
        mine, recv = _alltoall_wait("a2aw_" + name, send_sems, recv_sems, x_thru, land_thru, after)
        return _adamw("adamw_" + name, w2d, m2d, v2d, slots=recv, own=mine, me=me)

    wdown_out = big("w_down", x_down, small_started[6], w_down[0], m_w_down[0], v_w_down[0])
    wup_out = big("w_up", x_up, wdown_out[0], w_up_t, m_w_up_t, v_w_up_t)
    wout_out = big("w_out", x_out, wup_out[0], w_out[0], m_w_out[0], v_w_out[0])

    gathered = gather_end("small_grads", small_started, wout_out[0])
    summed = _sum_slots("sum_small_grads", gathered[:, :n_sum_rows])
    n_rep_rows = _pack(rep_g).shape[0]
    g_conv_w_all, g_ffn_w_all, loss_all = _unpack(summed[n_rep_rows:], extra_shapes)
    loss = loss_all[0, 0]
    dmod_all = gathered[:, n_sum_rows:].reshape(N_DEV, -1)[:, :bsz * n_mod * d].reshape(n_batch, n_mod * d)

    rep_out = _adamw("adamw_small", _pack(rep_w), _pack(rep_m), _pack(rep_v), grad=summed[:n_rep_rows])
    rep_out = [_unpack(o, rep_shapes) for o in rep_out]
    rep = {name: tuple(rep_out[q][p] for q in range(4)) for p, name in enumerate(rep_names)}

    g_conv_w_me = lax.dynamic_slice(g_conv_w_all, (0, me * dc_sh), (kc, dc_sh))
    cw_out = _adamw("adamw_conv_w", conv_w[0], m_conv_w[0], v_conv_w[0], grad=g_conv_w_me)
    g_ffn_w_me = lax.dynamic_slice(g_ffn_w_all, (0, me * cw_sh), (kf, cw_sh))
    fw_out = _adamw("adamw_ffn_conv_w", ffn_conv_w[0], m_ffn_conv_w[0], v_ffn_conv_w[0], grad=g_ffn_w_me)

    g_b_ada = _col_sums("grad_b_ada", dmod_all)
    bada_out = _adamw("adamw_b_ada", b_ada, m_b_ada, v_b_ada, grad=g_b_ada)
    dmod_sh = lax.dynamic_slice(dmod_all, (0, me * ada_sh), (n_batch, ada_sh)).astype(BF16)
    wada_out = _adamw("adamw_w_ada", w_ada[0], m_w_ada[0], v_w_ada[0], ca_t=c_act.T, dmod=dmod_sh)

    win_out = big("w_in", x_in, wada_out[0], w_in[0], m_w_in[0], v_w_in[0])
    wup_out = tuple(jnp.swapaxes(o, 0, 1) for o in wup_out)

    def lead(outs4):
        return tuple(o[None] for o in outs4)

    results = {
        "w_ada": lead(wada_out), "b_ada": bada_out, "w_in": lead(win_out), "conv_w": lead(cw_out),
        "w_out": lead(wout_out), "w_up": lead(wup_out), "ffn_conv_w": lead(fw_out), "w_down": lead(wdown_out),
    }
    results.update(rep)
    order = ["w_ada", "b_ada", "g_mix", "w_in", "conv_w", "conv_b", "conv_ln_g", "conv_ln_b", "sgu_ln_g", "sgu_ln_b",
             "sgu_w", "sgu_b", "out_g_conv", "out_g_sgu", "w_out", "g_ffn", "w_up", "ffn_conv_w", "ffn_conv_b", "w_down", "g_final"]
    out = [loss, grad_x.reshape(bsz, seq, d)]
    for q in range(4):
        out += [results[n][q] for n in order]
    return tuple(out)
```
